```python
import math
import jax, jax.numpy as jnp
from jax import lax
import numpy as np

D_MODEL = 1024
BATCH = 2
SEQ = 8192
DEPTH = 1

HEAD_DIM = 64
RWKV_WIDTH = D_MODEL // 2
MOBA_WIDTH = D_MODEL - RWKV_WIDTH
RWKV_HEADS = RWKV_WIDTH // HEAD_DIM
MOBA_HEADS = MOBA_WIDTH // HEAD_DIM
DECAY_RANK = 32
AAA_RANK = 32
GATE_RANK = 96
GN_EPS = 64e-5
L2_EPS = 1e-12
MOBA_BLOCK = 256
MOBA_TOPK = 3
Q_CHUNK = 128
N_GROUPS = 4
EXPERTS_PER_GROUP = 8
D_EXPERT = 256
EXPERT_TOPK = 2
LN_EPS = 1e-5
DEEPNORM_ALPHA = float((2 * DEPTH) ** 0.25)
DEEPNORM_BETA = float((8 * DEPTH) ** -0.25)
NEG_INF = -1e30

RWKV_SPLITS = (RWKV_WIDTH, RWKV_WIDTH, RWKV_WIDTH, DECAY_RANK, AAA_RANK, GATE_RANK)
MOBA_SPLITS = (MOBA_WIDTH, MOBA_WIDTH, MOBA_WIDTH)
RWKV_COLS = sum(RWKV_SPLITS)
IN_COLS = RWKV_COLS + sum(MOBA_SPLITS)

kernel_name = "rwkv7_moba_hier_moe_deepnorm"


def _split(p, sizes):
    idx = [int(i) for i in np.cumsum(sizes)[:-1]]
    return jnp.split(p, idx, axis=-1)


def _layer_norm(x, g, b):
    xf = x.astype(jnp.float32)
    mu = jnp.mean(xf, axis=-1, keepdims=True)
    var = jnp.mean(jnp.square(xf - mu), axis=-1, keepdims=True)
    y = (xf - mu) * lax.rsqrt(var + LN_EPS) * g.astype(jnp.float32) + b.astype(jnp.float32)
    return y.astype(x.dtype)


def _rwkv7_group(p_r, p_k, p_v, p_wd, p_ad, p_gd, w0, w_lora_up, a0, a_lora_up,
                 g_lora_up, k_k, k_a, r_k, gn_w, gn_b):
    f32 = jnp.float32
    B, T, C = p_r.shape
    H, N = RWKV_HEADS, HEAD_DIM
    r, k, v = p_r.astype(f32), p_k.astype(f32), p_v.astype(f32)
    w_log = -jax.nn.softplus(-(w0.astype(f32) + jnp.tanh(p_wd.astype(f32)) @ w_lora_up.astype(f32))) - 0.5
    decay = jnp.exp(-jnp.exp(w_log))
    a = jax.nn.sigmoid(a0.astype(f32) + p_ad.astype(f32) @ a_lora_up.astype(f32))
    g = jax.nn.sigmoid(p_gd.astype(f32)) @ g_lora_up.astype(f32)
    kk = (k * k_k.astype(f32)).reshape(B, T, H, N)
    kk = kk / jnp.maximum(jnp.linalg.norm(kk, axis=-1, keepdims=True), L2_EPS)
    k = k * (1.0 + (a - 1.0) * k_a.astype(f32))
    hd = lambda t: t.reshape(B, T, H, N)
    r_h, w_h, k_h, v_h, a_h = hd(r), hd(decay), hd(k), hd(v), hd(a)
    seq = tuple(t.transpose(1, 0, 2, 3) for t in (r_h, w_h, k_h, v_h, -kk, kk * a_h))

    def step(S, inp):
        rt, wt, kt, vt, at, bt = inp
        sa = jnp.einsum('bhij,bhj->bhi', S, at)
        S = S * wt[:, :, None, :] + sa[..., None] * bt[:, :, None, :] + vt[..., None] * kt[:, :, None, :]
        yt = jnp.einsum('bhij,bhj->bhi', S, rt)
        return S, yt

    S0 = jnp.zeros((B, H, N, N), f32)
    _, y = lax.scan(step, S0, seq)
    y = y.transpose(1, 0, 2, 3)
    mu = jnp.mean(y, axis=-1, keepdims=True)
    var = jnp.mean(jnp.square(y - mu), axis=-1, keepdims=True)
    y = ((y - mu) * lax.rsqrt(var + GN_EPS)).reshape(B, T, C) * gn_w.astype(f32) + gn_b.astype(f32)
    bonus = jnp.sum(r_h * k_h * r_k.astype(f32), axis=-1, keepdims=True) * v_h
    y = (y + bonus.reshape(B, T, C)) * g
    return y.astype(p_r.dtype)


def _moba_group(p_q, p_k, p_v):
    f32 = jnp.float32
    B, T, _ = p_q.shape
    H, Dh, L = MOBA_HEADS, HEAD_DIM, MOBA_BLOCK
    NB = T // L
    NC = T // Q_CHUNK
    K_EFF = min(MOBA_TOPK, NB)
    scale = 1.0 / math.sqrt(Dh)
    q = p_q.reshape(B, T, H, Dh).transpose(0, 2, 1, 3)
    Kb = p_k.reshape(B, T, H, Dh).transpose(0, 2, 1, 3).reshape(B, H, NB, L, Dh)
    Vb = p_v.reshape(B, T, H, Dh).transpose(0, 2, 1, 3).reshape(B, H, NB, L, Dh)
    kmean = jnp.mean(Kb.astype(f32), axis=3)
    slopes = 2.0 ** (-8.0 * (jnp.arange(H, dtype=f32) + 1.0) / H)
    gather = jax.vmap(jax.vmap(lambda kb, i: kb[i]))

    def chunk(c):
        start = c * Q_CHUNK
        q_c = lax.dynamic_slice_in_dim(q, start, Q_CHUNK, axis=2)
        t_pos = start + jnp.arange(Q_CHUNK)
        blk = start // L
        gate = jnp.einsum('bhcd,bhnd->bhcn', q_c.astype(f32), kmean)
        gate = jnp.where(jnp.arange(NB) < blk, gate, NEG_INF)
        _, idx = lax.top_k(gate, K_EFF)
        valid = jnp.arange(K_EFF) < blk
        K_sel = gather(Kb, idx)
        V_sel = gather(Vb, idx)
        s_sel = jnp.einsum('bhcd,bhckld->bhckl', q_c, K_sel).astype(f32) * scale
        key_pos = idx[..., None] * L + jnp.arange(L)
        dist = (t_pos[None, None, :, None, None] - key_pos).astype(f32)
        s_sel = s_sel - slopes[None, :, None, None, None] * dist
        s_sel = jnp.where(valid[None, None, None, :, None], s_sel, NEG_INF)
        s_sel = s_sel.reshape(B, H, Q_CHUNK, K_EFF * L)
        K_own = lax.dynamic_index_in_dim(Kb, blk, axis=2, keepdims=False)
        V_own = lax.dynamic_index_in_dim(Vb, blk, axis=2, keepdims=False)
        own_pos = blk * L + jnp.arange(L)
        d_own = (t_pos[:, None] - own_pos[None, :]).astype(f32)
        s_own = jnp.einsum('bhcd,bhld->bhcl', q_c, K_own).astype(f32) * scale
        s_own = s_own - slopes[None, :, None, None] * d_own[None, None]
        s_own = jnp.where((d_own >= 0)[None, None], s_own, NEG_INF)
        p = jax.nn.softmax(jnp.concatenate([s_sel, s_own], axis=-1), axis=-1)
        p_sel = p[..., :K_EFF * L].reshape(B, H, Q_CHUNK, K_EFF, L).astype(V_sel.dtype)
        p_own = p[..., K_EFF * L:].astype(V_own.dtype)
        out = (jnp.einsum('bhckl,bhckld->bhcd', p_sel, V_sel)
               + jnp.einsum('bhcl,bhld->bhcd', p_own, V_own))
        return out

    out = lax.map(chunk, jnp.arange(NC))
    out = out.transpose(1, 0, 3, 2, 4).reshape(B, T, H * Dh)
    return out


def _hier_moe(h, w_group, b_group, w_expert, b_expert, w1_exp, w3_exp, w2_exp):
    f32 = jnp.float32
    B, T, D = h.shape
    tok = h.reshape(B * T, D)
    g_logits = (tok @ w_group).astype(f32) + b_group.astype(f32)
    g_prob = jax.nn.softmax(g_logits, axis=-1)
    p_g, g_idx = lax.top_k(g_prob, 1)
    g_onehot = jax.nn.one_hot(g_idx[:, 0], N_GROUPS, dtype=f32)
    e_logits = ((tok @ w_expert).astype(f32) + b_expert.astype(f32)).reshape(-1, N_GROUPS, EXPERTS_PER_GROUP)
    e_in_group = jnp.einsum('ng,nge->ne', g_onehot, e_logits)
    e_prob = jax.nn.softmax(e_in_group, axis=-1)
    e_val, e_idx = lax.top_k(e_prob, EXPERT_TOPK)
    e_w = e_val / jnp.sum(e_val, axis=-1, keepdims=True) * p_g
    within = jnp.sum(jax.nn.one_hot(e_idx, EXPERTS_PER_GROUP, dtype=f32) * e_w[..., None], axis=1)
    gates = g_onehot[:, :, None] * within[:, None, :]
    gates = gates.astype(tok.dtype)
    out = jnp.zeros_like(tok)
    for gi in range(N_GROUPS):
        hid = jax.nn.silu(jnp.einsum('nd,edf->nef', tok, w1_exp[gi])) * jnp.einsum('nd,edf->nef', tok, w3_exp[gi])
        out = out + jnp.einsum('nef,efd->nd', hid * gates[:, gi, :, None], w2_exp[gi])
    return out.reshape(B, T, D)


def setup_inputs(seed: int = 0) -> dict:
    key = jax.random.key(seed)
    ks = jax.random.split(key, 32)
    f32 = jnp.float32
    D = D_MODEL
    nrm = lambda k, shape, s: jax.random.normal(k, shape, f32) * s
    return {
        "x": nrm(ks[0], (BATCH, SEQ, D), 1.0),
        "w_in": nrm(ks[1], (D, IN_COLS), D ** -0.5),
        "mu_shift": jax.random.uniform(ks[2], (RWKV_COLS,), f32),
        "w0": nrm(ks[3], (RWKV_WIDTH,), 0.5),
        "w_lora_up": nrm(ks[4], (DECAY_RANK, RWKV_WIDTH), 0.1 * DECAY_RANK ** -0.5),
        "a0": nrm(ks[5], (RWKV_WIDTH,), 0.1),
        "a_lora_up": nrm(ks[6], (AAA_RANK, RWKV_WIDTH), 0.1 * AAA_RANK ** -0.5),
        "g_lora_up": nrm(ks[7], (GATE_RANK, RWKV_WIDTH), GATE_RANK ** -0.5),
        "k_k": 0.85 + nrm(ks[8], (RWKV_WIDTH,), 0.05),
        "k_a": 1.0 + nrm(ks[9], (RWKV_WIDTH,), 0.05),
        "r_k": nrm(ks[10], (RWKV_HEADS, HEAD_DIM), 0.1),
        "gn_w": 1.0 + nrm(ks[11], (RWKV_WIDTH,), 0.01),
        "gn_b": nrm(ks[12], (RWKV_WIDTH,), 0.01),
        "w_out": nrm(ks[13], (D, D), DEEPNORM_BETA * D ** -0.5),
        "ln1_g": 1.0 + nrm(ks[14], (D,), 0.01),
        "ln1_b": nrm(ks[15], (D,), 0.01),
        "w_group": nrm(ks[16], (D, N_GROUPS), D ** -0.5),
        "b_group": nrm(ks[17], (N_GROUPS,), 0.01),
        "w_expert": nrm(ks[18], (D, N_GROUPS * EXPERTS_PER_GROUP), D ** -0.5),
        "b_expert": nrm(ks[19], (N_GROUPS * EXPERTS_PER_GROUP,), 0.01),
        "w1_exp": nrm(ks[20], (N_GROUPS, EXPERTS_PER_GROUP, D, D_EXPERT), D ** -0.5),
        "w3_exp": nrm(ks[21], (N_GROUPS, EXPERTS_PER_GROUP, D, D_EXPERT), D ** -0.5),
        "w2_exp": nrm(ks[22], (N_GROUPS, EXPERTS_PER_GROUP, D_EXPERT, D), DEEPNORM_BETA * D_EXPERT ** -0.5),
        "ln2_g": 1.0 + nrm(ks[23], (D,), 0.01),
        "ln2_b": nrm(ks[24], (D,), 0.01),
    }


def reference(x, w_in, mu_shift, w0, w_lora_up, a0, a_lora_up, g_lora_up, k_k, k_a, r_k,
              gn_w, gn_b, w_out, ln1_g, ln1_b, w_group, b_group, w_expert, b_expert,
              w1_exp, w3_exp, w2_exp, ln2_g, ln2_b):
    h = x
    for _ in range(DEPTH):
        p = h @ w_in
        p_rwkv, p_moba = p[..., :RWKV_COLS], p[..., RWKV_COLS:]
        p_prev = jnp.pad(p_rwkv[:, :-1], ((0, 0), (1, 0), (0, 0)))
        p_rwkv = p_rwkv + (p_prev - p_rwkv) * mu_shift
        p_r, p_k, p_v, p_wd, p_ad, p_gd = _split(p_rwkv, RWKV_SPLITS)
        y_a = _rwkv7_group(p_r, p_k, p_v, p_wd, p_ad, p_gd, w0, w_lora_up, a0, a_lora_up,
                           g_lora_up, k_k, k_a, r_k, gn_w, gn_b)
        q_m, k_m, v_m = _split(p_moba, MOBA_SPLITS)
        y_b = _moba_group(q_m, k_m, v_m)
        mix = jnp.concatenate([y_a, y_b.astype(y_a.dtype)], axis=-1) @ w_out
        h = _layer_norm(DEEPNORM_ALPHA * h + mix, ln1_g, ln1_b)
        ffn = _hier_moe(h, w_group, b_group, w_expert, b_expert, w1_exp, w3_exp, w2_exp)
        h = _layer_norm(DEEPNORM_ALPHA * h + ffn, ln2_g, ln2_b)
    return h
```

```python
import functools
import math

import jax
import jax.numpy as jnp
from jax import lax
from jax.experimental import pallas as pl
from jax.experimental.pallas import tpu as pltpu

F32 = jnp.float32
BF16 = jnp.bfloat16

D_MODEL = 1024
HEAD_DIM = 64
RWKV_WIDTH = 512
MOBA_WIDTH = 512
DECAY_RANK = 32
AAA_RANK = 32
GATE_RANK = 96
GN_EPS = 64e-5
L2_EPS = 1e-12
MOBA_BLOCK = 256
MOBA_TOPK = 3
N_GROUPS = 4
EXPERTS_PER_GROUP = 8
N_EXPERTS = N_GROUPS * EXPERTS_PER_GROUP
D_EXPERT = 256
LN_EPS = 1e-5
DEEPNORM_ALPHA = float(2.0 ** 0.25)
NEG_INF = -1e30
F32_LOWEST = -3.0e38

LANES = 128
PAIR = 2 * HEAD_DIM
N_PAIRS = RWKV_WIDTH // PAIR
LORA_PAD = LANES
RWKV_COLS_PAD = 3 * RWKV_WIDTH + 3 * LORA_PAD
IN_COLS_PAD = RWKV_COLS_PAD + 3 * MOBA_WIDTH
VMEM_LIMIT = 56 * 1024 * 1024

INPROJ_TM = 512
INPROJ_TN = 384
RWKV_CHUNK = 64
OUTPROJ_TM = 512
MOE_TM = 1024
MOE_EXPERTS_PER_STEP = 4
ROUTER_PAD = LANES
GROUP_LANE0 = N_EXPERTS

NN = (((1,), (0,)), ((), ()))
NT = (((1,), (1,)), ((), ()))


def _dot(a, b, dims=NN, precision=None):
    return lax.dot_general(a, b, dims, precision=precision, preferred_element_type=F32)


def _split_bf16(x, parts):
    out = []
    rem = x
    for i in range(parts):
        p = rem.astype(BF16)
        out.append(p)
        if i + 1 < parts:
            rem = rem - p.astype(F32)
    return out


def _mm(a, b, dims=NN, passes=3):
    if passes == 1:
        return _dot(a.astype(BF16), b.astype(BF16), dims)
    if passes == 6:
        return _dot(a, b, dims, precision=lax.Precision.HIGHEST)
    a_hi, a_lo = _split_bf16(a, 2)
    b_hi, b_lo = _split_bf16(b, 2)
    return _dot(a_hi, b_hi, dims) + (_dot(a_hi, b_lo, dims) + _dot(a_lo, b_hi, dims))


def _mm_exact_lhs(a_bf16, b, dims=NN):
    b1, b2, b3 = _split_bf16(b, 3)
    return _dot(a_bf16, b1, dims) + (_dot(a_bf16, b2, dims) + _dot(a_bf16, b3, dims))


def _mm_exact_rhs(a, b_bf16, dims=NN):
    a1, a2, a3 = _split_bf16(a, 3)
    return _dot(a1, b_bf16, dims) + (_dot(a2, b_bf16, dims) + _dot(a3, b_bf16, dims))


def _inproj_kernel(x_ref, w_ref, mu_ref, prkv_ref, plora_ref, pm_ref, carry_ref, *, tiles_per_seq):
    tm = x_ref.shape[0]
    xb = x_ref[...].astype(BF16)
    seq_start = (pl.program_id(0) % tiles_per_seq) == 0
    row0 = lax.broadcasted_iota(jnp.int32, (tm, INPROJ_TN), 0) == 0
    n_shift_tiles = RWKV_COLS_PAD // INPROJ_TN
    for j in range(n_shift_tiles):
        c0 = j * INPROJ_TN
        acc = _dot(xb, w_ref[:, c0:c0 + INPROJ_TN])
        prev_last = jnp.where(seq_start, 0.0, carry_ref[0:1, c0:c0 + INPROJ_TN])
        shifted = jnp.where(row0, prev_last, pltpu.roll(acc, 1, 0))
        carry_ref[0:1, c0:c0 + INPROJ_TN] = acc[tm - 1:tm, :]
        out = acc + (shifted - acc) * mu_ref[:, c0:c0 + INPROJ_TN]
        if c0 < 3 * RWKV_WIDTH:
            prkv_ref[:, c0:c0 + INPROJ_TN] = out
        else:
            plora_ref[:, c0 - 3 * RWKV_WIDTH:c0 - 3 * RWKV_WIDTH + INPROJ_TN] = out
    for j in range(3 * MOBA_WIDTH // INPROJ_TN):
        c0 = j * INPROJ_TN
        acc = _dot(xb, w_ref[:, RWKV_COLS_PAD + c0:RWKV_COLS_PAD + c0 + INPROJ_TN])
        pm_ref[:, c0:c0 + INPROJ_TN] = acc.astype(BF16)


def _inproj_call(x2, w_cat, mu_cat, seq_len):
    n = x2.shape[0]
    tm = INPROJ_TM
    assert seq_len % tm == 0 and (3 * RWKV_WIDTH) % INPROJ_TN == 0
    return pl.pallas_call(
        functools.partial(_inproj_kernel, tiles_per_seq=seq_len // tm),
        grid=(n // tm,),
        in_specs=[
            pl.BlockSpec((tm, D_MODEL), lambda i: (i, 0)),
            pl.BlockSpec((D_MODEL, IN_COLS_PAD), lambda i: (0, 0)),
            pl.BlockSpec((1, RWKV_COLS_PAD), lambda i: (0, 0)),
        ],
        out_specs=[
            pl.BlockSpec((tm, 3 * RWKV_WIDTH), lambda i: (i, 0)),
            pl.BlockSpec((tm, 3 * LORA_PAD), lambda i: (i, 0)),
            pl.BlockSpec((tm, 3 * MOBA_WIDTH), lambda i: (i, 0)),
        ],
        out_shape=[
            jax.ShapeDtypeStruct((n, 3 * RWKV_WIDTH), F32),
            jax.ShapeDtypeStruct((n, 3 * LORA_PAD), F32),
            jax.ShapeDtypeStruct((n, 3 * MOBA_WIDTH), BF16),
        ],
        scratch_shapes=[pltpu.VMEM((8, RWKV_COLS_PAD), F32)],
        compiler_params=pltpu.CompilerParams(
            dimension_semantics=("arbitrary",), vmem_limit_bytes=VMEM_LIMIT),
        name="inproj_shift",
    )(x2, w_cat, mu_cat)


def _softplus(z):
    return jnp.maximum(z, 0.0) + jnp.log(1.0 + jnp.exp(-jnp.abs(z)))


def _sigmoid(z):
    return 1.0 / (1.0 + jnp.exp(-z))


def _rwkv_pair(rt, kt, at, bt, v, d_last, s_prev, passes):
    c = rt.shape[0]
    row = lax.broadcasted_iota(jnp.int32, (c, c), 0)
    col = lax.broadcasted_iota(jnp.int32, (c, c), 1)
    strict = row > col
    incl = row >= col
    eye_c = (row == col).astype(F32)
    lane = lax.broadcasted_iota(jnp.int32, (1, PAIR), 1)
    head0 = lane < HEAD_DIM
    prow = lax.broadcasted_iota(jnp.int32, (PAIR, PAIR), 0)
    pcol = lax.broadcasted_iota(jnp.int32, (PAIR, PAIR), 1)
    same_head = (prow < HEAD_DIM) == (pcol < HEAD_DIM)
    eye_p = (prow == pcol).astype(F32)

    w_h, u0_h, q_h, y1_h = [], [], [], []
    for h in (0, 1):
        mh = head0 if h == 0 else jnp.logical_not(head0)
        x = jnp.concatenate([jnp.where(mh, at, 0.0), jnp.where(mh, rt, 0.0)], axis=0)
        z_b = _mm(x, bt, NT, passes)
        z_k = _mm(x, kt, NT, passes)
        l_ab = jnp.where(strict, z_b[:c], 0.0)
        m_rb = jnp.where(incl, z_b[c:], 0.0)
        l_ak = jnp.where(strict, z_k[:c], 0.0)
        m_rk = jnp.where(incl, z_k[c:], 0.0)
        pw = l_ab
        t_inv = eye_c + l_ab
        for _ in range(int(math.log2(c)) - 1):
            pw = _mm(pw, pw, NN, passes)
            t_inv = t_inv + _mm(pw, t_inv, NN, passes)
        lv = _mm(l_ak, v, NN, passes)
        wu = _mm(t_inv, jnp.concatenate([at, lv], axis=1), NN, passes)
        qy = _mm(m_rb, wu, NN, passes)
        mv = _mm(m_rk, v, NN, passes)
        w_h.append(wu[:, :PAIR])
        u0_h.append(wu[:, PAIR:])
        q_h.append(rt + qy[:, :PAIR])
        y1_h.append(qy[:, PAIR:] + mv)
    w = jnp.where(head0, w_h[0], w_h[1])
    u0 = jnp.where(head0, u0_h[0], u0_h[1])
    qeff = jnp.where(head0, q_h[0], q_h[1])
    y1 = jnp.where(head0, y1_h[0], y1_h[1])

    phi = jnp.where(same_head, (eye_p + _mm(w.T, bt, NN, passes)) * d_last, 0.0)
    uv_t = jnp.concatenate([u0, v], axis=0).T
    bk = jnp.concatenate([bt, kt], axis=0)
    psi = jnp.where(same_head, _mm(uv_t, bk, NN, passes) * d_last, 0.0)
    y = _mm(qeff, s_prev, NT, passes) + y1
    s_next = _mm(s_prev, phi, NN, passes) + psi
    return y, s_next


def _rwkv_kernel(prkv_ref, plora_ref, vec_ref, wl_ref, al_ref, gl_ref, bd_ref, y_ref, s_ref,
                 *, passes):
    c = prkv_ref.shape[0]
    width = RWKV_WIDTH

    @pl.when(pl.program_id(1) == 0)
    def _():
        s_ref[...] = jnp.zeros_like(s_ref)

    r = prkv_ref[:, 0:width]
    k_raw = prkv_ref[:, width:2 * width]
    v = prkv_ref[:, 2 * width:3 * width]
    p_wd = plora_ref[:, 0:LORA_PAD]
    p_ad = plora_ref[:, LORA_PAD:2 * LORA_PAD]
    p_gd = plora_ref[:, 2 * LORA_PAD:3 * LORA_PAD]
    w0 = vec_ref[0:1, :]
    a0 = vec_ref[1:2, :]
    k_k = vec_ref[2:3, :]
    k_a = vec_ref[3:4, :]
    r_k = vec_ref[4:5, :]
    gn_w = vec_ref[5:6, :]
    gn_b = vec_ref[6:7, :]
    bd = bd_ref[...]

    def seg_sum(z):
        return _mm_exact_rhs(z, bd)

    w_log = -_softplus(-(w0 + _mm(jnp.tanh(p_wd), wl_ref[...], NN, 6))) - 0.5
    log_w = -jnp.exp(w_log)
    a = _sigmoid(a0 + _mm(p_ad, al_ref[...], NN, 6))
    g = _mm(_sigmoid(p_gd), gl_ref[...], NN, 6)
    kk = k_raw * k_k
    kk = kk / jnp.maximum(jnp.sqrt(seg_sum(kk * kk)), L2_EPS)
    k = k_raw * (1.0 + (a - 1.0) * k_a)

    row = lax.broadcasted_iota(jnp.int32, (c, c), 0)
    col = lax.broadcasted_iota(jnp.int32, (c, c), 1)
    tri = (row >= col).astype(BF16)
    cum = _mm_exact_lhs(tri, log_w)
    d_incl = jnp.exp(cum)
    d_inv = jnp.exp(-cum)
    d_excl = jnp.exp(cum - log_w)
    rt = r * d_incl
    kt = k * d_inv
    at = -kk * d_excl
    bt = kk * a * d_inv
    d_last = d_incl[c - 1:c, :]

    ys = []
    for p in range(N_PAIRS):
        sl = slice(p * PAIR, (p + 1) * PAIR)
        y_p, s_next = _rwkv_pair(rt[:, sl], kt[:, sl], at[:, sl], bt[:, sl], v[:, sl],
                                 d_last[:, sl], s_ref[p], passes)
        s_ref[p] = s_next
        ys.append(y_p)
    y = jnp.concatenate(ys, axis=1)

    inv_n = 1.0 / HEAD_DIM
    mu = seg_sum(y) * inv_n
    yc = y - mu
    var = seg_sum(yc * yc) * inv_n
    yn = yc * lax.rsqrt(var + GN_EPS) * gn_w + gn_b
    bonus = seg_sum(r * k * r_k) * v
    y_ref[...] = ((yn + bonus) * g).astype(y_ref.dtype)


def _rwkv_call(p_rkv, p_lora, vecs, wl, al, gl, bd, batch, seq_len, passes):
    n = p_rkv.shape[0]
    c = RWKV_CHUNK
    nc = seq_len // c
    row_map = lambda b, i: (b * nc + i, 0)
    const = lambda b, i: (0, 0)
    return pl.pallas_call(
        functools.partial(_rwkv_kernel, passes=passes),
        grid=(batch, nc),
        in_specs=[
            pl.BlockSpec((c, 3 * RWKV_WIDTH), row_map),
            pl.BlockSpec((c, 3 * LORA_PAD), row_map),
            pl.BlockSpec((8, RWKV_WIDTH), const),
            pl.BlockSpec((LORA_PAD, RWKV_WIDTH), const),
            pl.BlockSpec((LORA_PAD, RWKV_WIDTH), const),
            pl.BlockSpec((LORA_PAD, RWKV_WIDTH), const),
            pl.BlockSpec((RWKV_WIDTH, RWKV_WIDTH), const),
        ],
        out_specs=pl.BlockSpec((c, RWKV_WIDTH), row_map),
        out_shape=jax.ShapeDtypeStruct((n, RWKV_WIDTH), BF16),
        scratch_shapes=[pltpu.VMEM((N_PAIRS, PAIR, PAIR), F32)],
        compiler_params=pltpu.CompilerParams(
            dimension_semantics=("arbitrary", "arbitrary"), vmem_limit_bytes=VMEM_LIMIT),
        name="rwkv7_chunked",
    )(p_rkv, p_lora, vecs, wl, al, gl, bd)


def _moba_kernel(q_ref, k_ref, v_ref, kaug_ref, o_ref, kmean_ref, m_ref, l_ref, acc_ref, *, n_blocks):
    blk = MOBA_BLOCK
    i = pl.program_id(2)
    scale = 1.0 / math.sqrt(HEAD_DIM)

    @pl.when(i == 0)
    def _():
        kmean_ref[...] = jnp.zeros_like(kmean_ref)

        def body(n, carry):
            off = pl.multiple_of(n * blk, blk)
            kb = k_ref[pl.ds(off, blk), :].astype(F32)
            kmean_ref[pl.ds(n, 1), :] = jnp.sum(kb, axis=0, keepdims=True) * (1.0 / blk)
            return carry
        lax.fori_loop(0, n_blocks, body, 0)

    q = q_ref[...]
    lane = lax.broadcasted_iota(jnp.int32, (1, PAIR), 1)
    head0 = lane < HEAD_DIM
    lane_q = lax.broadcasted_iota(jnp.int32, (blk, LANES), 1)
    past = lane_q < i
    ones_lanes = (lane_q == n_blocks) | (lane_q == n_blocks + 1)
    kmean = kmean_ref[...]

    qa = []
    for h in (0, 1):
        mh = head0 if h == 0 else jnp.logical_not(head0)
        qh = jnp.where(mh, q, jnp.zeros_like(q))
        gate = _dot(qh.astype(F32), kmean, NT, precision=lax.Precision.HIGHEST)
        gate = jnp.where(past, gate, F32_LOWEST)
        sel = jnp.zeros(gate.shape, jnp.bool_)
        for _ in range(MOBA_TOPK):
            mx = jnp.max(gate, axis=-1, keepdims=True)
            first = jnp.min(jnp.where(gate == mx, lane_q, LANES), axis=-1, keepdims=True)
            pick = (lane_q == first) & (mx > F32_LOWEST)
            sel = sel | pick
            gate = jnp.where(pick, F32_LOWEST, gate)
        aug = jnp.where(ones_lanes, 1.0, jnp.where(past & jnp.logical_not(sel), NEG_INF, 0.0))
        qa.append(jnp.concatenate([qh * scale, aug.astype(BF16)], axis=1))

    own = pl.multiple_of(i * blk, blk)
    tq = lax.broadcasted_iota(jnp.int32, (blk, blk), 0)
    tk = lax.broadcasted_iota(jnp.int32, (blk, blk), 1)
    causal = tk <= tq
    k_own = k_ref[pl.ds(own, blk), :]
    v_own = v_ref[pl.ds(own, blk), :]
    for h in (0, 1):
        ka = jnp.concatenate([k_own, kaug_ref[h, pl.ds(own, blk), :]], axis=1)
        s = jnp.where(causal, _dot(qa[h], ka, NT), NEG_INF)
        m = jnp.max(s, axis=-1, keepdims=True)
        p = jnp.exp(s - m)
        m_ref[h] = m
        l_ref[h] = jnp.sum(p, axis=-1, keepdims=True)
        acc_ref[h] = _dot(p.astype(BF16), v_own)

    def kv_step(n, carry):
        off = pl.multiple_of(n * blk, blk)
        k_n = k_ref[pl.ds(off, blk), :]
        v_n = v_ref[pl.ds(off, blk), :]
        for h in (0, 1):
            ka = jnp.concatenate([k_n, kaug_ref[h, pl.ds(off, blk), :]], axis=1)
            s = _dot(qa[h], ka, NT)
            m_old = m_ref[h]
            m_new = jnp.maximum(m_old, jnp.max(s, axis=-1, keepdims=True))
            alpha = jnp.exp(m_old - m_new)
            p = jnp.exp(s - m_new)
            l_ref[h] = alpha * l_ref[h] + jnp.sum(p, axis=-1, keepdims=True)
            acc_ref[h] = alpha * acc_ref[h] + _dot(p.astype(BF16), v_n)
            m_ref[h] = m_new
        return carry
    lax.fori_loop(0, i, kv_step, 0)

    out0 = acc_ref[0] / l_ref[0]
    out1 = acc_ref[1] / l_ref[1]
    o_ref[...] = jnp.where(head0, out0, out1).astype(o_ref.dtype)


def _moba_call(qkv, kaug, batch, seq_len):
    n = qkv.shape[0]
    blk = MOBA_BLOCK
    nb = seq_len // blk
    assert nb + 2 <= LANES
    lane_groups = MOBA_WIDTH // LANES
    return pl.pallas_call(
        functools.partial(_moba_kernel, n_blocks=nb),
        grid=(batch, N_PAIRS, nb),
        in_specs=[
            pl.BlockSpec((blk, PAIR), lambda b, p, i: (b * nb + i, p)),
            pl.BlockSpec((seq_len, PAIR), lambda b, p, i: (b, lane_groups + p)),
            pl.BlockSpec((seq_len, PAIR), lambda b, p, i: (b, 2 * lane_groups + p)),
            pl.BlockSpec((2, seq_len, LANES), lambda b, p, i: (p, 0, 0)),
        ],
        out_specs=pl.BlockSpec((blk, PAIR), lambda b, p, i: (b * nb + i, p)),
        out_shape=jax.ShapeDtypeStruct((n, MOBA_WIDTH), BF16),
        scratch_shapes=[
            pltpu.VMEM((LANES, PAIR), F32),
            pltpu.VMEM((2, blk, 1), F32),
            pltpu.VMEM((2, blk, 1), F32),
            pltpu.VMEM((2, blk, PAIR), F32),
        ],
        compiler_params=pltpu.CompilerParams(
            dimension_semantics=("arbitrary", "arbitrary", "arbitrary"),
            vmem_limit_bytes=VMEM_LIMIT),
        name="moba_attention",
    )(qkv, qkv, qkv, kaug)


def _moba_key_aug(seq_len):
    nb = seq_len // MOBA_BLOCK
    heads = MOBA_WIDTH // HEAD_DIM
    pos = jnp.arange(seq_len, dtype=jnp.int32)
    blk_id = pos // MOBA_BLOCK
    slopes = 2.0 ** (-8.0 * (jnp.arange(heads, dtype=F32) + 1.0) / heads)
    onehot = (blk_id[:, None] == jnp.arange(LANES)[None, :]).astype(F32)
    lane = jnp.arange(LANES)[None, None, :]
    lo = slopes[:, None, None] * (pos % MOBA_BLOCK).astype(F32)[None, :, None]
    hi = slopes[:, None, None] * (blk_id * MOBA_BLOCK).astype(F32)[None, :, None]
    aug = jnp.where(lane == nb, lo, jnp.where(lane == nb + 1, hi, onehot[None]))
    return aug.astype(BF16)


def _layer_norm(z, g, b):
    mu = jnp.mean(z, axis=-1, keepdims=True)
    zc = z - mu
    var = jnp.mean(zc * zc, axis=-1, keepdims=True)
    return zc * lax.rsqrt(var + LN_EPS) * g + b


def _outproj_kernel(ya_ref, yb_ref, x_ref, wa_ref, wb_ref, g_ref, b_ref, wr_ref, br_ref,
                    h_ref, hb_ref, lg_ref):
    mix = _dot(ya_ref[...], wa_ref[...]) + _dot(yb_ref[...], wb_ref[...])
    h = _layer_norm(DEEPNORM_ALPHA * x_ref[...] + mix, g_ref[...], b_ref[...])
    h_ref[...] = h
    hb_ref[...] = h.astype(BF16)
    lg_ref[...] = _dot(h, wr_ref[...], NN, precision=lax.Precision.HIGHEST) + br_ref[...]


def _outproj_call(y_a, y_b, x2, wa, wb, ln_g, ln_b, w_router, b_router):
    n = x2.shape[0]
    tm = OUTPROJ_TM
    row = lambda i: (i, 0)
    const = lambda i: (0, 0)
    return pl.pallas_call(
        _outproj_kernel,
        grid=(n // tm,),
        in_specs=[
            pl.BlockSpec((tm, RWKV_WIDTH), row),
            pl.BlockSpec((tm, MOBA_WIDTH), row),
            pl.BlockSpec((tm, D_MODEL), row),
            pl.BlockSpec((RWKV_WIDTH, D_MODEL), const),
            pl.BlockSpec((MOBA_WIDTH, D_MODEL), const),
            pl.BlockSpec((1, D_MODEL), const),
            pl.BlockSpec((1, D_MODEL), const),
            pl.BlockSpec((D_MODEL, ROUTER_PAD), const),
            pl.BlockSpec((1, ROUTER_PAD), const),
        ],
        out_specs=[
            pl.BlockSpec((tm, D_MODEL), row),
            pl.BlockSpec((tm, D_MODEL), row),
            pl.BlockSpec((tm, ROUTER_PAD), row),
        ],
        out_shape=[
            jax.ShapeDtypeStruct((n, D_MODEL), F32),
            jax.ShapeDtypeStruct((n, D_MODEL), BF16),
            jax.ShapeDtypeStruct((n, ROUTER_PAD), F32),
        ],
        compiler_params=pltpu.CompilerParams(
            dimension_semantics=("arbitrary",), vmem_limit_bytes=VMEM_LIMIT),
        name="outproj_ln_router",
    )(y_a, y_b, x2, wa, wb, ln_g, ln_b, w_router, b_router)


def _route(logits):
    lane = lax.broadcasted_iota(jnp.int32, logits.shape, 1)
    is_group = (lane >= GROUP_LANE0) & (lane < GROUP_LANE0 + N_GROUPS)
    gl = jnp.where(is_group, logits, F32_LOWEST)
    g_max = jnp.max(gl, axis=-1, keepdims=True)
    g_first = jnp.min(jnp.where(gl == g_max, lane, LANES), axis=-1, keepdims=True)
    g_exp = jnp.where(is_group, jnp.exp(gl - g_max), 0.0)
    p_g = 1.0 / jnp.sum(g_exp, axis=-1, keepdims=True)
    g_idx = g_first - GROUP_LANE0
    in_group = (lane >= g_idx * EXPERTS_PER_GROUP) & (lane < (g_idx + 1) * EXPERTS_PER_GROUP)
    el = jnp.where(in_group, logits, F32_LOWEST)
    e_max = jnp.max(el, axis=-1, keepdims=True)
    e_exp = jnp.where(in_group, jnp.exp(el - e_max), 0.0)
    e_prob = e_exp / jnp.sum(e_exp, axis=-1, keepdims=True)
    cand = jnp.where(in_group, e_prob, -1.0)
    v1 = jnp.max(cand, axis=-1, keepdims=True)
    i1 = jnp.min(jnp.where(cand == v1, lane, LANES), axis=-1, keepdims=True)
    pick1 = lane == i1
    cand2 = jnp.where(pick1, -1.0, cand)
    v2 = jnp.max(cand2, axis=-1, keepdims=True)
    i2 = jnp.min(jnp.where(cand2 == v2, lane, LANES), axis=-1, keepdims=True)
    pick2 = lane == i2
    denom = v1 + v2
    return jnp.where(pick1, v1 / denom * p_g, jnp.where(pick2, v2 / denom * p_g, 0.0))


def _moe_kernel(hb_ref, h_ref, lg_ref, w1_ref, w3_ref, w2_ref, g_ref, b_ref, o_ref, gates_ref, acc_ref):
    step = pl.program_id(1)

    @pl.when(step == 0)
    def _():
        gates_ref[...] = _route(lg_ref[...])
        acc_ref[...] = jnp.zeros_like(acc_ref)

    tok = hb_ref[...]
    gates = gates_ref[...]
    lane = lax.broadcasted_iota(jnp.int32, gates.shape, 1)
    acc = acc_ref[...]
    for e in range(MOE_EXPERTS_PER_STEP):
        gate_e = jnp.sum(jnp.where(lane == step * MOE_EXPERTS_PER_STEP + e, gates, 0.0),
                         axis=-1, keepdims=True)
        a1 = _dot(tok, w1_ref[e])
        a3 = _dot(tok, w3_ref[e])
        hid = (a1 * _sigmoid(a1)) * a3 * gate_e
        acc = acc + _dot(hid.astype(BF16), w2_ref[e])
    acc_ref[...] = acc

    @pl.when(step == N_EXPERTS // MOE_EXPERTS_PER_STEP - 1)
    def _():
        o_ref[...] = _layer_norm(DEEPNORM_ALPHA * h_ref[...] + acc_ref[...], g_ref[...], b_ref[...])


def _moe_call(h_bf16, h_f32, logits, w1, w3, w2, ln_g, ln_b):
    n = h_f32.shape[0]
    tm = MOE_TM
    row = lambda t, g: (t, 0)
    const = lambda t, g: (0, 0)
    wmap = lambda t, g: (g, 0, 0)
    eps = MOE_EXPERTS_PER_STEP
    return pl.pallas_call(
        _moe_kernel,
        grid=(n // tm, N_EXPERTS // eps),
        in_specs=[
            pl.BlockSpec((tm, D_MODEL), row),
            pl.BlockSpec((tm, D_MODEL), row),
            pl.BlockSpec((tm, ROUTER_PAD), row),
            pl.BlockSpec((eps, D_MODEL, D_EXPERT), wmap),
            pl.BlockSpec((eps, D_MODEL, D_EXPERT), wmap),
            pl.BlockSpec((eps, D_EXPERT, D_MODEL), wmap),
            pl.BlockSpec((1, D_MODEL), const),
            pl.BlockSpec((1, D_MODEL), const),
        ],
        out_specs=pl.BlockSpec((tm, D_MODEL), row),
        out_shape=jax.ShapeDtypeStruct((n, D_MODEL), F32),
        scratch_shapes=[pltpu.VMEM((tm, ROUTER_PAD), F32), pltpu.VMEM((tm, D_MODEL), F32)],
        compiler_params=pltpu.CompilerParams(
            dimension_semantics=("arbitrary", "arbitrary"), vmem_limit_bytes=VMEM_LIMIT),
        name="hier_moe_ln",
    )(h_bf16, h_f32, logits, w1, w3, w2, ln_g, ln_b)


def _pad_cols(w, width):
    return jnp.pad(w, ((0, 0), (0, width - w.shape[1])))


def _pad_rows(w, height):
    return jnp.pad(w, ((0, height - w.shape[0]), (0, 0)))


RWKV_PASSES = 3


def kernel(x, w_in, mu_shift, w0, w_lora_up, a0, a_lora_up, g_lora_up, k_k, k_a, r_k, gn_w, gn_b, w_out, ln1_g, ln1_b, w_group, b_group, w_expert, b_expert, w1_exp, w3_exp, w2_exp, ln2_g, ln2_b):
    batch, seq_len, d = x.shape
    assert d == D_MODEL
    n = batch * seq_len
    x2 = x.reshape(n, d)

    c_rkv = 3 * RWKV_WIDTH
    c_wd = c_rkv + DECAY_RANK
    c_ad = c_wd + AAA_RANK
    c_gd = c_ad + GATE_RANK
    w_cat = jnp.concatenate([
        w_in[:, :c_rkv],
        _pad_cols(w_in[:, c_rkv:c_wd], LORA_PAD),
        _pad_cols(w_in[:, c_wd:c_ad], LORA_PAD),
        _pad_cols(w_in[:, c_ad:c_gd], LORA_PAD),
        w_in[:, c_gd:],
    ], axis=1).astype(BF16)
    mu2 = mu_shift[None, :]
    mu_cat = jnp.concatenate([
        mu2[:, :c_rkv],
        _pad_cols(mu2[:, c_rkv:c_wd], LORA_PAD),
        _pad_cols(mu2[:, c_wd:c_ad], LORA_PAD),
        _pad_cols(mu2[:, c_ad:c_gd], LORA_PAD),
    ], axis=1)
    p_rkv, p_lora, p_moba = _inproj_call(x2, w_cat, mu_cat, seq_len)

    vecs = jnp.stack([w0, a0, k_k, k_a, r_k.reshape(-1), gn_w, gn_b, jnp.zeros_like(w0)], axis=0)
    head_id = jnp.arange(RWKV_WIDTH) // HEAD_DIM
    bd = (head_id[:, None] == head_id[None, :]).astype(BF16)
    y_a = _rwkv_call(p_rkv, p_lora, vecs,
                     _pad_rows(w_lora_up, LORA_PAD), _pad_rows(a_lora_up, LORA_PAD),
                     _pad_rows(g_lora_up, LORA_PAD), bd, batch, seq_len, RWKV_PASSES)

    y_b = _moba_call(p_moba, _moba_key_aug(seq_len), batch, seq_len)

    w_out_b = w_out.astype(BF16)
    w_router = _pad_cols(jnp.concatenate([w_expert, w_group], axis=1), ROUTER_PAD)
    b_router = _pad_cols(jnp.concatenate([b_expert, b_group])[None, :], ROUTER_PAD)
    h1, h1_b, logits = _outproj_call(y_a, y_b, x2, w_out_b[:RWKV_WIDTH], w_out_b[RWKV_WIDTH:],
                                     ln1_g[None, :], ln1_b[None, :], w_router, b_router)

    flat = lambda w: w.astype(BF16).reshape((N_EXPERTS,) + w.shape[2:])
    out = _moe_call(h1_b, h1, logits, flat(w1_exp), flat(w3_exp), flat(w2_exp),
                    ln2_g[None, :], ln2_b[None, :])
    return out.reshape(batch, seq_len, d)
```

```python
import functools
import math

import jax
import jax.numpy as jnp
from jax import lax
from jax.experimental import pallas as pl
from jax.experimental.pallas import tpu as pltpu

F32 = jnp.float32
BF16 = jnp.bfloat16

D_MODEL = 1024
HEAD_DIM = 64
RWKV_WIDTH = 512
MOBA_WIDTH = 512
DECAY_RANK = 32
AAA_RANK = 32
GATE_RANK = 96
GN_EPS = 64e-5
L2_EPS = 1e-12
MOBA_BLOCK = 256
MOBA_TOPK = 3
N_GROUPS = 4
EXPERTS_PER_GROUP = 8
N_EXPERTS = N_GROUPS * EXPERTS_PER_GROUP
D_EXPERT = 256
LN_EPS = 1e-5
DEEPNORM_ALPHA = float(2.0 ** 0.25)
NEG_INF = -1e30
F32_LOWEST = -3.0e38

LANES = 128
PAIR = 2 * HEAD_DIM
N_PAIRS = RWKV_WIDTH // PAIR
LORA_PAD = LANES
RWKV_COLS_PAD = 3 * RWKV_WIDTH + 3 * LORA_PAD
IN_COLS_PAD = RWKV_COLS_PAD + 3 * MOBA_WIDTH
VMEM_LIMIT = 56 * 1024 * 1024

INPROJ_TM = 512
INPROJ_TN = 384
RWKV_CHUNK = 64
OUTPROJ_TM = 512
MOE_TM = 1024
MOE_EXPERTS_PER_STEP = 4
MOBA_KV_TILE = 512
ROUTER_PAD = LANES
GROUP_LANE0 = N_EXPERTS

NN = (((1,), (0,)), ((), ()))
NT = (((1,), (1,)), ((), ()))


def _dot(a, b, dims=NN, precision=None):
    return lax.dot_general(a, b, dims, precision=precision, preferred_element_type=F32)


def _split_bf16(x, parts):
    out = []
    rem = x
    for i in range(parts):
        p = rem.astype(BF16)
        out.append(p)
        if i + 1 < parts:
            rem = rem - p.astype(F32)
    return out


def _mm(a, b, dims=NN, passes=3):
    if passes == 1:
        return _dot(a.astype(BF16), b.astype(BF16), dims)
    if passes == 6:
        return _dot(a, b, dims, precision=lax.Precision.HIGHEST)
    a_hi, a_lo = _split_bf16(a, 2)
    b_hi, b_lo = _split_bf16(b, 2)
    return _dot(a_hi, b_hi, dims) + (_dot(a_hi, b_lo, dims) + _dot(a_lo, b_hi, dims))


def _mm_exact_lhs(a_bf16, b, dims=NN):
    b1, b2, b3 = _split_bf16(b, 3)
    return _dot(a_bf16, b1, dims) + (_dot(a_bf16, b2, dims) + _dot(a_bf16, b3, dims))


def _mm_exact_rhs(a, b_bf16, dims=NN):
    a1, a2, a3 = _split_bf16(a, 3)
    return _dot(a1, b_bf16, dims) + (_dot(a2, b_bf16, dims) + _dot(a3, b_bf16, dims))


def _inproj_kernel(x_ref, w_ref, mu_ref, prkv_ref, plora_ref, pm_ref, carry_ref, *, tiles_per_seq):
    tm = x_ref.shape[0]
    xb = x_ref[...].astype(BF16)
    seq_start = (pl.program_id(0) % tiles_per_seq) == 0
    row0 = lax.broadcasted_iota(jnp.int32, (tm, INPROJ_TN), 0) == 0
    n_shift_tiles = RWKV_COLS_PAD // INPROJ_TN
    for j in range(n_shift_tiles):
        c0 = j * INPROJ_TN
        acc = _dot(xb, w_ref[:, c0:c0 + INPROJ_TN])
        prev_last = jnp.where(seq_start, 0.0, carry_ref[0:1, c0:c0 + INPROJ_TN])
        shifted = jnp.where(row0, prev_last, pltpu.roll(acc, 1, 0))
        carry_ref[0:1, c0:c0 + INPROJ_TN] = acc[tm - 1:tm, :]
        out = acc + (shifted - acc) * mu_ref[:, c0:c0 + INPROJ_TN]
        if c0 < 3 * RWKV_WIDTH:
            prkv_ref[:, c0:c0 + INPROJ_TN] = out
        else:
            plora_ref[:, c0 - 3 * RWKV_WIDTH:c0 - 3 * RWKV_WIDTH + INPROJ_TN] = out
    for j in range(3 * MOBA_WIDTH // INPROJ_TN):
        c0 = j * INPROJ_TN
        acc = _dot(xb, w_ref[:, RWKV_COLS_PAD + c0:RWKV_COLS_PAD + c0 + INPROJ_TN])
        pm_ref[:, c0:c0 + INPROJ_TN] = acc.astype(BF16)


def _inproj_call(x2, w_cat, mu_cat, seq_len):
    n = x2.shape[0]
    tm = INPROJ_TM
    assert seq_len % tm == 0 and (3 * RWKV_WIDTH) % INPROJ_TN == 0
    return pl.pallas_call(
        functools.partial(_inproj_kernel, tiles_per_seq=seq_len // tm),
        grid=(n // tm,),
        in_specs=[
            pl.BlockSpec((tm, D_MODEL), lambda i: (i, 0)),
            pl.BlockSpec((D_MODEL, IN_COLS_PAD), lambda i: (0, 0)),
            pl.BlockSpec((1, RWKV_COLS_PAD), lambda i: (0, 0)),
        ],
        out_specs=[
            pl.BlockSpec((tm, 3 * RWKV_WIDTH), lambda i: (i, 0)),
            pl.BlockSpec((tm, 3 * LORA_PAD), lambda i: (i, 0)),
            pl.BlockSpec((tm, 3 * MOBA_WIDTH), lambda i: (i, 0)),
        ],
        out_shape=[
            jax.ShapeDtypeStruct((n, 3 * RWKV_WIDTH), F32),
            jax.ShapeDtypeStruct((n, 3 * LORA_PAD), F32),
            jax.ShapeDtypeStruct((n, 3 * MOBA_WIDTH), BF16),
        ],
        scratch_shapes=[pltpu.VMEM((8, RWKV_COLS_PAD), F32)],
        compiler_params=pltpu.CompilerParams(
            dimension_semantics=("arbitrary",), vmem_limit_bytes=VMEM_LIMIT),
        name="inproj_shift",
    )(x2, w_cat, mu_cat)


def _softplus(z):
    return jnp.maximum(z, 0.0) + jnp.log(1.0 + jnp.exp(-jnp.abs(z)))


def _sigmoid(z):
    return 1.0 / (1.0 + jnp.exp(-z))


def _rwkv_chunk(rt, kt, at, bt, v, d_last, s_prev, passes, state_passes):
    c = rt.shape[0]
    n_pairs = len(s_prev)
    row = lax.broadcasted_iota(jnp.int32, (c, c), 0)
    col = lax.broadcasted_iota(jnp.int32, (c, c), 1)
    strict = row > col
    incl = row >= col
    eye_c = (row == col).astype(F32)
    lane = lax.broadcasted_iota(jnp.int32, (1, PAIR), 1)
    head0 = lane < HEAD_DIM
    head_mask = (head0, jnp.logical_not(head0))
    prow = lax.broadcasted_iota(jnp.int32, (PAIR, PAIR), 0)
    pcol = lax.broadcasted_iota(jnp.int32, (PAIR, PAIR), 1)
    same_head = (prow < HEAD_DIM) == (pcol < HEAD_DIM)
    eye_p = (prow == pcol).astype(F32)
    heads = [(p, h) for p in range(n_pairs) for h in (0, 1)]
    sl = [slice(p * PAIR, (p + 1) * PAIR) for p in range(n_pairs)]
    bt_t = bt.T
    kt_t = kt.T

    x = {(p, h): jnp.concatenate([jnp.where(head_mask[h], at[:, sl[p]], 0.0),
                                  jnp.where(head_mask[h], rt[:, sl[p]], 0.0)], axis=0)
         for p, h in heads}
    z_b = {(p, h): _mm(x[p, h], bt_t[sl[p]], NN, passes) for p, h in heads}
    z_k = {(p, h): _mm(x[p, h], kt_t[sl[p]], NN, passes) for p, h in heads}
    l_ab = {k_: jnp.where(strict, z[:c], 0.0) for k_, z in z_b.items()}
    m_rb = {k_: jnp.where(incl, z[c:], 0.0) for k_, z in z_b.items()}
    l_ak = {k_: jnp.where(strict, z[:c], 0.0) for k_, z in z_k.items()}
    m_rk = {k_: jnp.where(incl, z[c:], 0.0) for k_, z in z_k.items()}
    pw = l_ab
    t_inv = {k_: eye_c + l for k_, l in l_ab.items()}
    for _ in range(int(math.log2(c)) - 1):
        pw = {k_: _mm(m, m, NN, passes) for k_, m in pw.items()}
        t_inv = {k_: t_inv[k_] + _mm(pw[k_], t_inv[k_], NN, passes) for k_ in heads}
    lv = {(p, h): _mm(l_ak[p, h], v[:, sl[p]], NN, passes) for p, h in heads}
    wu = {(p, h): _mm(t_inv[p, h], jnp.concatenate([at[:, sl[p]], lv[p, h]], axis=1), NN, passes)
          for p, h in heads}
    qy = {k_: _mm(m_rb[k_], wu[k_], NN, passes) for k_ in heads}
    mv = {(p, h): _mm(m_rk[p, h], v[:, sl[p]], NN, passes) for p, h in heads}

    ys, s_next = [], []
    for p in range(n_pairs):
        pick = lambda t0, t1: jnp.where(head0, t0, t1)
        w = pick(wu[p, 0][:, :PAIR], wu[p, 1][:, :PAIR])
        u0 = pick(wu[p, 0][:, PAIR:], wu[p, 1][:, PAIR:])
        qeff = rt[:, sl[p]] + pick(qy[p, 0][:, :PAIR], qy[p, 1][:, :PAIR])
        y1 = pick(qy[p, 0][:, PAIR:] + mv[p, 0], qy[p, 1][:, PAIR:] + mv[p, 1])
        d_p = d_last[:, sl[p]]
        phi = jnp.where(same_head, (eye_p + _mm(w.T, bt[:, sl[p]], NN, passes)) * d_p, 0.0)
        uv_t = jnp.concatenate([u0, v[:, sl[p]]], axis=0).T
        bk = jnp.concatenate([bt[:, sl[p]], kt[:, sl[p]]], axis=0)
        psi = jnp.where(same_head, _mm(uv_t, bk, NN, passes) * d_p, 0.0)
        ys.append(_mm(qeff, s_prev[p].T, NN, state_passes) + y1)
        s_next.append(_mm(s_prev[p], phi, NN, state_passes) + psi)
    return jnp.concatenate(ys, axis=1), s_next


def _rwkv_kernel(prkv_ref, plora_ref, vec_ref, wl_ref, al_ref, gl_ref, bd_ref, y_ref, s_ref,
                 *, passes):
    c = prkv_ref.shape[0]
    width = RWKV_WIDTH

    @pl.when(pl.program_id(1) == 0)
    def _():
        s_ref[...] = jnp.zeros_like(s_ref)

    r = prkv_ref[:, 0:width]
    k_raw = prkv_ref[:, width:2 * width]
    v = prkv_ref[:, 2 * width:3 * width]
    p_wd = plora_ref[:, 0:LORA_PAD]
    p_ad = plora_ref[:, LORA_PAD:2 * LORA_PAD]
    p_gd = plora_ref[:, 2 * LORA_PAD:3 * LORA_PAD]
    w0 = vec_ref[0:1, :]
    a0 = vec_ref[1:2, :]
    k_k = vec_ref[2:3, :]
    k_a = vec_ref[3:4, :]
    r_k = vec_ref[4:5, :]
    gn_w = vec_ref[5:6, :]
    gn_b = vec_ref[6:7, :]
    bd = bd_ref[...]

    def seg_sum(z):
        return _mm_exact_rhs(z, bd)

    w_log = -_softplus(-(w0 + _mm(jnp.tanh(p_wd), wl_ref[...], NN, 6))) - 0.5
    log_w = -jnp.exp(w_log)
    a = _sigmoid(a0 + _mm(p_ad, al_ref[...], NN, 6))
    g = _mm(_sigmoid(p_gd), gl_ref[...], NN, 6)
    kk = k_raw * k_k
    kk = kk / jnp.maximum(jnp.sqrt(seg_sum(kk * kk)), L2_EPS)
    k = k_raw * (1.0 + (a - 1.0) * k_a)

    row = lax.broadcasted_iota(jnp.int32, (c, c), 0)
    col = lax.broadcasted_iota(jnp.int32, (c, c), 1)
    tri = (row >= col).astype(BF16)
    cum = _mm_exact_lhs(tri, log_w)
    d_incl = jnp.exp(cum)
    d_inv = jnp.exp(-cum)
    d_excl = jnp.exp(cum - log_w)
    rt = r * d_incl
    kt = k * d_inv
    at = -kk * d_excl
    bt = kk * a * d_inv
    d_last = d_incl[c - 1:c, :]

    y, s_next = _rwkv_chunk(rt, kt, at, bt, v, d_last, [s_ref[p] for p in range(N_PAIRS)],
                            passes, RWKV_STATE_PASSES)
    for p in range(N_PAIRS):
        s_ref[p] = s_next[p]

    inv_n = 1.0 / HEAD_DIM
    mu = seg_sum(y) * inv_n
    yc = y - mu
    var = seg_sum(yc * yc) * inv_n
    yn = yc * lax.rsqrt(var + GN_EPS) * gn_w + gn_b
    bonus = seg_sum(r * k * r_k) * v
    y_ref[...] = ((yn + bonus) * g).astype(y_ref.dtype)


def _rwkv_call(p_rkv, p_lora, vecs, wl, al, gl, bd, batch, seq_len, passes):
    n = p_rkv.shape[0]
    c = RWKV_CHUNK
    nc = seq_len // c
    row_map = lambda b, i: (b * nc + i, 0)
    const = lambda b, i: (0, 0)
    return pl.pallas_call(
        functools.partial(_rwkv_kernel, passes=passes),
        grid=(batch, nc),
        in_specs=[
            pl.BlockSpec((c, 3 * RWKV_WIDTH), row_map),
            pl.BlockSpec((c, 3 * LORA_PAD), row_map),
            pl.BlockSpec((8, RWKV_WIDTH), const),
            pl.BlockSpec((LORA_PAD, RWKV_WIDTH), const),
            pl.BlockSpec((LORA_PAD, RWKV_WIDTH), const),
            pl.BlockSpec((LORA_PAD, RWKV_WIDTH), const),
            pl.BlockSpec((RWKV_WIDTH, RWKV_WIDTH), const),
        ],
        out_specs=pl.BlockSpec((c, RWKV_WIDTH), row_map),
        out_shape=jax.ShapeDtypeStruct((n, RWKV_WIDTH), BF16),
        scratch_shapes=[pltpu.VMEM((N_PAIRS, PAIR, PAIR), F32)],
        compiler_params=pltpu.CompilerParams(
            dimension_semantics=("arbitrary", "arbitrary"), vmem_limit_bytes=VMEM_LIMIT),
        name="rwkv7_chunked",
    )(p_rkv, p_lora, vecs, wl, al, gl, bd)


def _moba_kernel(q_ref, k_ref, v_ref, kaug_ref, o_ref, kmean_ref, vt_ref,
                 m0_ref, m1_ref, l0_ref, l1_ref, acc0_ref, acc1_ref, *, n_blocks):
    blk = MOBA_BLOCK
    tk = MOBA_KV_TILE
    i = pl.program_id(2)
    scale = 1.0 / math.sqrt(HEAD_DIM)
    nb_pad = kmean_ref.shape[0]
    m_refs, l_refs, acc_refs = (m0_ref, m1_ref), (l0_ref, l1_ref), (acc0_ref, acc1_ref)

    @pl.when(i == 0)
    def _():
        kmean_ref[...] = jnp.zeros_like(kmean_ref)

        def mean_body(n, carry):
            off = pl.multiple_of(n * blk, blk)
            kb = k_ref[pl.ds(off, blk), :].astype(F32)
            kmean_ref[pl.ds(n, 1), :] = jnp.sum(kb, axis=0, keepdims=True) * (1.0 / blk)
            return carry
        lax.fori_loop(0, n_blocks, mean_body, 0)

        def vt_body(j, carry):
            off = pl.multiple_of(j * tk, tk)
            vt_ref[j] = v_ref[pl.ds(off, tk), :].astype(F32).T.astype(BF16)
            return carry
        lax.fori_loop(0, vt_ref.shape[0], vt_body, 0)

    q_t = q_ref[...].astype(F32).T * scale
    chan = lax.broadcasted_iota(jnp.int32, (PAIR, blk), 0)
    blk_row = lax.broadcasted_iota(jnp.int32, (nb_pad, blk), 0)
    past = blk_row < i
    aug_row = lax.broadcasted_iota(jnp.int32, (LANES, blk), 0)
    ones_rows = (aug_row == n_blocks) | (aug_row == n_blocks + 1)
    kmean = kmean_ref[...] * math.sqrt(HEAD_DIM)

    qa_t = []
    for h in (0, 1):
        qh_t = jnp.where((chan < HEAD_DIM) == (h == 0), q_t, 0.0)
        gate = _dot(kmean, qh_t, NN, precision=lax.Precision.HIGHEST)
        gate = jnp.where(past, gate, F32_LOWEST)
        sel = jnp.zeros(gate.shape, jnp.bool_)
        for _ in range(MOBA_TOPK):
            mx = jnp.max(gate, axis=0, keepdims=True)
            first = jnp.min(jnp.where(gate == mx, blk_row, nb_pad), axis=0, keepdims=True)
            pick = (blk_row == first) & (mx > F32_LOWEST)
            sel = sel | pick
            gate = jnp.where(pick, F32_LOWEST, gate)
        sel_bias = jnp.where(past & jnp.logical_not(sel), NEG_INF, 0.0)
        aug_t = jnp.concatenate([sel_bias, jnp.zeros((LANES - nb_pad, blk), F32)], axis=0)
        aug_t = jnp.where(ones_rows, 1.0, aug_t)
        qa_t.append(jnp.concatenate([qh_t, aug_t], axis=0).astype(BF16))

    def tile_scores(j):
        off = pl.multiple_of(j * tk, tk)
        k_t = k_ref[pl.ds(off, tk), :]
        return [_dot(jnp.concatenate([k_t, kaug_ref[h, pl.ds(off, tk), :]], axis=1), qa_t[h])
                for h in (0, 1)]

    def tile_update(j, s, first):
        m_new, alpha, p = [], [], []
        for h in (0, 1):
            mx = jnp.max(s[h], axis=0, keepdims=True)
            if first:
                m_new.append(mx)
            else:
                m_old = m_refs[h][...]
                m_new.append(jnp.maximum(m_old, mx))
                alpha.append(jnp.exp(m_old - m_new[h]))
            p.append(jnp.exp(s[h] - m_new[h]))
        for h in (0, 1):
            pv = _dot(vt_ref[j, h * HEAD_DIM:(h + 1) * HEAD_DIM, :], p[h].astype(BF16))
            p_sum = jnp.sum(p[h], axis=0, keepdims=True)
            if first:
                l_refs[h][...] = p_sum
                acc_refs[h][...] = pv
            else:
                l_refs[h][...] = alpha[h] * l_refs[h][...] + p_sum
                acc_refs[h][...] = alpha[h] * acc_refs[h][...] + pv
            m_refs[h][...] = m_new[h]

    j_own = (i * blk) // tk
    key_pos = j_own * tk + lax.broadcasted_iota(jnp.int32, (tk, blk), 0)
    query_pos = i * blk + lax.broadcasted_iota(jnp.int32, (tk, blk), 1)
    causal = key_pos <= query_pos
    tile_update(j_own, [jnp.where(causal, s_h, NEG_INF) for s_h in tile_scores(j_own)], True)

    def kv_step(j, carry):
        tile_update(j, tile_scores(j), False)
        return carry
    lax.fori_loop(0, j_own, kv_step, 0)

    out_t = jnp.concatenate([acc0_ref[...] / l0_ref[...], acc1_ref[...] / l1_ref[...]], axis=0)
    o_ref[...] = out_t.T.astype(o_ref.dtype)


def _moba_call(qkv, kaug, batch, seq_len):
    n = qkv.shape[0]
    blk = MOBA_BLOCK
    nb = seq_len // blk
    assert nb + 2 <= LANES
    lane_groups = MOBA_WIDTH // LANES
    return pl.pallas_call(
        functools.partial(_moba_kernel, n_blocks=nb),
        grid=(batch, N_PAIRS, nb),
        in_specs=[
            pl.BlockSpec((blk, PAIR), lambda b, p, i: (b * nb + i, p)),
            pl.BlockSpec((seq_len, PAIR), lambda b, p, i: (b, lane_groups + p)),
            pl.BlockSpec((seq_len, PAIR), lambda b, p, i: (b, 2 * lane_groups + p)),
            pl.BlockSpec((2, seq_len, LANES), lambda b, p, i: (p, 0, 0)),
        ],
        out_specs=pl.BlockSpec((blk, PAIR), lambda b, p, i: (b * nb + i, p)),
        out_shape=jax.ShapeDtypeStruct((n, MOBA_WIDTH), BF16),
        scratch_shapes=[
            pltpu.VMEM((-(-nb // 8) * 8, PAIR), F32),
            pltpu.VMEM((seq_len // MOBA_KV_TILE, PAIR, MOBA_KV_TILE), BF16),
            pltpu.VMEM((1, blk), F32), pltpu.VMEM((1, blk), F32),
            pltpu.VMEM((1, blk), F32), pltpu.VMEM((1, blk), F32),
            pltpu.VMEM((HEAD_DIM, blk), F32), pltpu.VMEM((HEAD_DIM, blk), F32),
        ],
        compiler_params=pltpu.CompilerParams(
            dimension_semantics=("arbitrary", "arbitrary", "arbitrary"),
            vmem_limit_bytes=VMEM_LIMIT),
        name="moba_attention",
    )(qkv, qkv, qkv, kaug)


def _moba_key_aug(seq_len):
    nb = seq_len // MOBA_BLOCK
    heads = MOBA_WIDTH // HEAD_DIM
    pos = jnp.arange(seq_len, dtype=jnp.int32)
    blk_id = pos // MOBA_BLOCK
    slopes = 2.0 ** (-8.0 * (jnp.arange(heads, dtype=F32) + 1.0) / heads)
    onehot = (blk_id[:, None] == jnp.arange(LANES)[None, :]).astype(F32)
    lane = jnp.arange(LANES)[None, None, :]
    lo = slopes[:, None, None] * (pos % MOBA_BLOCK).astype(F32)[None, :, None]
    hi = slopes[:, None, None] * (blk_id * MOBA_BLOCK).astype(F32)[None, :, None]
    aug = jnp.where(lane == nb, lo, jnp.where(lane == nb + 1, hi, onehot[None]))
    return aug.astype(BF16)


def _layer_norm(z, g, b):
    mu = jnp.mean(z, axis=-1, keepdims=True)
    zc = z - mu
    var = jnp.mean(zc * zc, axis=-1, keepdims=True)
    return zc * lax.rsqrt(var + LN_EPS) * g + b


def _outproj_kernel(ya_ref, yb_ref, x_ref, wa_ref, wb_ref, g_ref, b_ref, wr_ref, br_ref,
                    h_ref, hb_ref, lg_ref):
    mix = _dot(ya_ref[...], wa_ref[...]) + _dot(yb_ref[...], wb_ref[...])
    h = _layer_norm(DEEPNORM_ALPHA * x_ref[...] + mix, g_ref[...], b_ref[...])
    h_ref[...] = h
    hb_ref[...] = h.astype(BF16)
    lg_ref[...] = _dot(h, wr_ref[...], NN, precision=lax.Precision.HIGHEST) + br_ref[...]


def _outproj_call(y_a, y_b, x2, wa, wb, ln_g, ln_b, w_router, b_router):
    n = x2.shape[0]
    tm = OUTPROJ_TM
    row = lambda i: (i, 0)
    const = lambda i: (0, 0)
    return pl.pallas_call(
        _outproj_kernel,
        grid=(n // tm,),
        in_specs=[
            pl.BlockSpec((tm, RWKV_WIDTH), row),
            pl.BlockSpec((tm, MOBA_WIDTH), row),
            pl.BlockSpec((tm, D_MODEL), row),
            pl.BlockSpec((RWKV_WIDTH, D_MODEL), const),
            pl.BlockSpec((MOBA_WIDTH, D_MODEL), const),
            pl.BlockSpec((1, D_MODEL), const),
            pl.BlockSpec((1, D_MODEL), const),
            pl.BlockSpec((D_MODEL, ROUTER_PAD), const),
            pl.BlockSpec((1, ROUTER_PAD), const),
        ],
        out_specs=[
            pl.BlockSpec((tm, D_MODEL), row),
            pl.BlockSpec((tm, D_MODEL), row),
            pl.BlockSpec((tm, ROUTER_PAD), row),
        ],
        out_shape=[
            jax.ShapeDtypeStruct((n, D_MODEL), F32),
            jax.ShapeDtypeStruct((n, D_MODEL), BF16),
            jax.ShapeDtypeStruct((n, ROUTER_PAD), F32),
        ],
        compiler_params=pltpu.CompilerParams(
            dimension_semantics=("arbitrary",), vmem_limit_bytes=VMEM_LIMIT),
        name="outproj_ln_router",
    )(y_a, y_b, x2, wa, wb, ln_g, ln_b, w_router, b_router)


def _route(logits):
    lane = lax.broadcasted_iota(jnp.int32, logits.shape, 1)
    is_group = (lane >= GROUP_LANE0) & (lane < GROUP_LANE0 + N_GROUPS)
    gl = jnp.where(is_group, logits, F32_LOWEST)
    g_max = jnp.max(gl, axis=-1, keepdims=True)
    g_first = jnp.min(jnp.where(gl == g_max, lane, LANES), axis=-1, keepdims=True)
    g_exp = jnp.where(is_group, jnp.exp(gl - g_max), 0.0)
    p_g = 1.0 / jnp.sum(g_exp, axis=-1, keepdims=True)
    g_idx = g_first - GROUP_LANE0
    in_group = (lane >= g_idx * EXPERTS_PER_GROUP) & (lane < (g_idx + 1) * EXPERTS_PER_GROUP)
    el = jnp.where(in_group, logits, F32_LOWEST)
    e_max = jnp.max(el, axis=-1, keepdims=True)
    e_exp = jnp.where(in_group, jnp.exp(el - e_max), 0.0)
    e_prob = e_exp / jnp.sum(e_exp, axis=-1, keepdims=True)
    cand = jnp.where(in_group, e_prob, -1.0)
    v1 = jnp.max(cand, axis=-1, keepdims=True)
    i1 = jnp.min(jnp.where(cand == v1, lane, LANES), axis=-1, keepdims=True)
    pick1 = lane == i1
    cand2 = jnp.where(pick1, -1.0, cand)
    v2 = jnp.max(cand2, axis=-1, keepdims=True)
    i2 = jnp.min(jnp.where(cand2 == v2, lane, LANES), axis=-1, keepdims=True)
    pick2 = lane == i2
    denom = v1 + v2
    return jnp.where(pick1, v1 / denom * p_g, jnp.where(pick2, v2 / denom * p_g, 0.0))


def _moe_kernel(hb_ref, h_ref, lg_ref, w1_ref, w3_ref, w2_ref, g_ref, b_ref, o_ref, gates_ref, acc_ref):
    step = pl.program_id(1)

    @pl.when(step == 0)
    def _():
        gates_ref[...] = _route(lg_ref[...])
        acc_ref[...] = jnp.zeros_like(acc_ref)

    tok = hb_ref[...]
    gates = gates_ref[...]
    lane = lax.broadcasted_iota(jnp.int32, gates.shape, 1)
    acc = acc_ref[...]
    for e in range(MOE_EXPERTS_PER_STEP):
        gate_e = jnp.sum(jnp.where(lane == step * MOE_EXPERTS_PER_STEP + e, gates, 0.0),
                         axis=-1, keepdims=True)
        a1 = _dot(tok, w1_ref[e])
        a3 = _dot(tok, w3_ref[e])
        hid = (a1 * _sigmoid(a1)) * a3 * gate_e
        acc = acc + _dot(hid.astype(BF16), w2_ref[e])
    acc_ref[...] = acc

    @pl.when(step == N_EXPERTS // MOE_EXPERTS_PER_STEP - 1)
    def _():
        o_ref[...] = _layer_norm(DEEPNORM_ALPHA * h_ref[...] + acc_ref[...], g_ref[...], b_ref[...])


def _moe_call(h_bf16, h_f32, logits, w1, w3, w2, ln_g, ln_b):
    n = h_f32.shape[0]
    tm = MOE_TM
    row = lambda t, g: (t, 0)
    const = lambda t, g: (0, 0)
    wmap = lambda t, g: (g, 0, 0)
    eps = MOE_EXPERTS_PER_STEP
    return pl.pallas_call(
        _moe_kernel,
        grid=(n // tm, N_EXPERTS // eps),
        in_specs=[
            pl.BlockSpec((tm, D_MODEL), row),
            pl.BlockSpec((tm, D_MODEL), row),
            pl.BlockSpec((tm, ROUTER_PAD), row),
            pl.BlockSpec((eps, D_MODEL, D_EXPERT), wmap),
            pl.BlockSpec((eps, D_MODEL, D_EXPERT), wmap),
            pl.BlockSpec((eps, D_EXPERT, D_MODEL), wmap),
            pl.BlockSpec((1, D_MODEL), const),
            pl.BlockSpec((1, D_MODEL), const),
        ],
        out_specs=pl.BlockSpec((tm, D_MODEL), row),
        out_shape=jax.ShapeDtypeStruct((n, D_MODEL), F32),
        scratch_shapes=[pltpu.VMEM((tm, ROUTER_PAD), F32), pltpu.VMEM((tm, D_MODEL), F32)],
        compiler_params=pltpu.CompilerParams(
            dimension_semantics=("arbitrary", "arbitrary"), vmem_limit_bytes=VMEM_LIMIT),
        name="hier_moe_ln",
    )(h_bf16, h_f32, logits, w1, w3, w2, ln_g, ln_b)


def _pad_cols(w, width):
    return jnp.pad(w, ((0, 0), (0, width - w.shape[1])))


def _pad_rows(w, height):
    return jnp.pad(w, ((0, height - w.shape[0]), (0, 0)))


RWKV_PASSES = 1
RWKV_STATE_PASSES = 3


def kernel(x, w_in, mu_shift, w0, w_lora_up, a0, a_lora_up, g_lora_up, k_k, k_a, r_k, gn_w, gn_b, w_out, ln1_g, ln1_b, w_group, b_group, w_expert, b_expert, w1_exp, w3_exp, w2_exp, ln2_g, ln2_b):
    batch, seq_len, d = x.shape
    assert d == D_MODEL
    n = batch * seq_len
    x2 = x.reshape(n, d)

    c_rkv = 3 * RWKV_WIDTH
    c_wd = c_rkv + DECAY_RANK
    c_ad = c_wd + AAA_RANK
    c_gd = c_ad + GATE_RANK
    w_cat = jnp.concatenate([
        w_in[:, :c_rkv],
        _pad_cols(w_in[:, c_rkv:c_wd], LORA_PAD),
        _pad_cols(w_in[:, c_wd:c_ad], LORA_PAD),
        _pad_cols(w_in[:, c_ad:c_gd], LORA_PAD),
        w_in[:, c_gd:],
    ], axis=1).astype(BF16)
    mu2 = mu_shift[None, :]
    mu_cat = jnp.concatenate([
        mu2[:, :c_rkv],
        _pad_cols(mu2[:, c_rkv:c_wd], LORA_PAD),
        _pad_cols(mu2[:, c_wd:c_ad], LORA_PAD),
        _pad_cols(mu2[:, c_ad:c_gd], LORA_PAD),
    ], axis=1)
    p_rkv, p_lora, p_moba = _inproj_call(x2, w_cat, mu_cat, seq_len)

    vecs = jnp.stack([w0, a0, k_k, k_a, r_k.reshape(-1), gn_w, gn_b, jnp.zeros_like(w0)], axis=0)
    head_id = jnp.arange(RWKV_WIDTH) // HEAD_DIM
    bd = (head_id[:, None] == head_id[None, :]).astype(BF16)
    y_a = _rwkv_call(p_rkv, p_lora, vecs,
                     _pad_rows(w_lora_up, LORA_PAD), _pad_rows(a_lora_up, LORA_PAD),
                     _pad_rows(g_lora_up, LORA_PAD), bd, batch, seq_len, RWKV_PASSES)

    y_b = _moba_call(p_moba, _moba_key_aug(seq_len), batch, seq_len)

    w_out_b = w_out.astype(BF16)
    w_router = _pad_cols(jnp.concatenate([w_expert, w_group], axis=1), ROUTER_PAD)
    b_router = _pad_cols(jnp.concatenate([b_expert, b_group])[None, :], ROUTER_PAD)
    h1, h1_b, logits = _outproj_call(y_a, y_b, x2, w_out_b[:RWKV_WIDTH], w_out_b[RWKV_WIDTH:],
                                     ln1_g[None, :], ln1_b[None, :], w_router, b_router)

    flat = lambda w: w.astype(BF16).reshape((N_EXPERTS,) + w.shape[2:])
    out = _moe_call(h1_b, h1, logits, flat(w1_exp), flat(w3_exp), flat(w2_exp),
                    ln2_g[None, :], ln2_b[None, :])
    return out.reshape(batch, seq_len, d)
```

```python
import functools
import math

import jax
import jax.numpy as jnp
from jax import lax
from jax.experimental import pallas as pl
from jax.experimental.pallas import tpu as pltpu

F32 = jnp.float32
BF16 = jnp.bfloat16

D_MODEL = 1024
HEAD_DIM = 64
RWKV_WIDTH = 512
MOBA_WIDTH = 512
DECAY_RANK = 32
AAA_RANK = 32
GATE_RANK = 96
GN_EPS = 64e-5
L2_EPS = 1e-12
MOBA_BLOCK = 256
MOBA_TOPK = 3
N_GROUPS = 4
EXPERTS_PER_GROUP = 8
N_EXPERTS = N_GROUPS * EXPERTS_PER_GROUP
D_EXPERT = 256
LN_EPS = 1e-5
DEEPNORM_ALPHA = float(2.0 ** 0.25)
NEG_INF = -1e30
F32_LOWEST = -3.0e38

LANES = 128
PAIR = 2 * HEAD_DIM
N_PAIRS = RWKV_WIDTH // PAIR
LORA_PAD = LANES
RWKV_COLS_PAD = 3 * RWKV_WIDTH + 3 * LORA_PAD
IN_COLS_PAD = RWKV_COLS_PAD + 3 * MOBA_WIDTH
VMEM_LIMIT = 56 * 1024 * 1024

INPROJ_TM = 512
INPROJ_TN = 384
RWKV_CHUNK = 64
OUTPROJ_TM = 512
MOE_TM = 1024
MOE_EXPERTS_PER_STEP = 4
MOBA_KV_TILE = 512
MOBA_Q_TILE = 512
MOBA_V_ROWS = HEAD_DIM + 16
MOBA_ALIBI_PARTS = 3
LOG2_E = 1.4426950408889634
ROUTER_PAD = LANES
GROUP_LANE0 = N_EXPERTS

NN = (((1,), (0,)), ((), ()))
NT = (((1,), (1,)), ((), ()))


def _dot(a, b, dims=NN, precision=None):
    return lax.dot_general(a, b, dims, precision=precision, preferred_element_type=F32)


def _split_bf16(x, parts):
    out = []
    rem = x
    for i in range(parts):
        p = rem.astype(BF16)
        out.append(p)
        if i + 1 < parts:
            rem = rem - p.astype(F32)
    return out


def _mm(a, b, dims=NN, passes=3):
    if passes == 1:
        return _dot(a.astype(BF16), b.astype(BF16), dims)
    if passes == 6:
        return _dot(a, b, dims, precision=lax.Precision.HIGHEST)
    a_hi, a_lo = _split_bf16(a, 2)
    b_hi, b_lo = _split_bf16(b, 2)
    return _dot(a_hi, b_hi, dims) + (_dot(a_hi, b_lo, dims) + _dot(a_lo, b_hi, dims))


def _mm_exact_lhs(a_bf16, b, dims=NN):
    b1, b2, b3 = _split_bf16(b, 3)
    return _dot(a_bf16, b1, dims) + (_dot(a_bf16, b2, dims) + _dot(a_bf16, b3, dims))


def _mm_exact_rhs(a, b_bf16, dims=NN):
    a1, a2, a3 = _split_bf16(a, 3)
    return _dot(a1, b_bf16, dims) + (_dot(a2, b_bf16, dims) + _dot(a3, b_bf16, dims))


def _inproj_kernel(x_ref, w_ref, mu_ref, prkv_ref, plora_ref, pm_ref, carry_ref, *, tiles_per_seq):
    tm = x_ref.shape[0]
    xb = x_ref[...].astype(BF16)
    seq_start = (pl.program_id(0) % tiles_per_seq) == 0
    row0 = lax.broadcasted_iota(jnp.int32, (tm, INPROJ_TN), 0) == 0
    n_shift_tiles = RWKV_COLS_PAD // INPROJ_TN
    for j in range(n_shift_tiles):
        c0 = j * INPROJ_TN
        acc = _dot(xb, w_ref[:, c0:c0 + INPROJ_TN])
        prev_last = jnp.where(seq_start, 0.0, carry_ref[0:1, c0:c0 + INPROJ_TN])
        shifted = jnp.where(row0, prev_last, pltpu.roll(acc, 1, 0))
        carry_ref[0:1, c0:c0 + INPROJ_TN] = acc[tm - 1:tm, :]
        out = acc + (shifted - acc) * mu_ref[:, c0:c0 + INPROJ_TN]
        if c0 < 3 * RWKV_WIDTH:
            prkv_ref[:, c0:c0 + INPROJ_TN] = out
        else:
            plora_ref[:, c0 - 3 * RWKV_WIDTH:c0 - 3 * RWKV_WIDTH + INPROJ_TN] = out
    for j in range(3 * MOBA_WIDTH // INPROJ_TN):
        c0 = j * INPROJ_TN
        acc = _dot(xb, w_ref[:, RWKV_COLS_PAD + c0:RWKV_COLS_PAD + c0 + INPROJ_TN])
        pm_ref[:, c0:c0 + INPROJ_TN] = acc.astype(BF16)


def _inproj_call(x2, w_cat, mu_cat, seq_len):
    n = x2.shape[0]
    tm = INPROJ_TM
    assert seq_len % tm == 0 and (3 * RWKV_WIDTH) % INPROJ_TN == 0
    return pl.pallas_call(
        functools.partial(_inproj_kernel, tiles_per_seq=seq_len // tm),
        grid=(n // tm,),
        in_specs=[
            pl.BlockSpec((tm, D_MODEL), lambda i: (i, 0)),
            pl.BlockSpec((D_MODEL, IN_COLS_PAD), lambda i: (0, 0)),
            pl.BlockSpec((1, RWKV_COLS_PAD), lambda i: (0, 0)),
        ],
        out_specs=[
            pl.BlockSpec((tm, 3 * RWKV_WIDTH), lambda i: (i, 0)),
            pl.BlockSpec((tm, 3 * LORA_PAD), lambda i: (i, 0)),
            pl.BlockSpec((tm, 3 * MOBA_WIDTH), lambda i: (i, 0)),
        ],
        out_shape=[
            jax.ShapeDtypeStruct((n, 3 * RWKV_WIDTH), F32),
            jax.ShapeDtypeStruct((n, 3 * LORA_PAD), F32),
            jax.ShapeDtypeStruct((n, 3 * MOBA_WIDTH), BF16),
        ],
        scratch_shapes=[pltpu.VMEM((8, RWKV_COLS_PAD), F32)],
        compiler_params=pltpu.CompilerParams(
            dimension_semantics=("arbitrary",), vmem_limit_bytes=VMEM_LIMIT),
        name="inproj_shift",
    )(x2, w_cat, mu_cat)


def _softplus(z):
    return jnp.maximum(z, 0.0) + jnp.log(1.0 + jnp.exp(-jnp.abs(z)))


def _sigmoid(z):
    return 1.0 / (1.0 + jnp.exp(-z))


def _rwkv_chunk(rt, kt, at, bt, v, d_last, s_prev, passes, state_passes):
    c = rt.shape[0]
    n_pairs = len(s_prev)
    row = lax.broadcasted_iota(jnp.int32, (c, c), 0)
    col = lax.broadcasted_iota(jnp.int32, (c, c), 1)
    strict = row > col
    incl = row >= col
    eye_c = (row == col).astype(F32)
    lane = lax.broadcasted_iota(jnp.int32, (1, PAIR), 1)
    head0 = lane < HEAD_DIM
    head_mask = (head0, jnp.logical_not(head0))
    prow = lax.broadcasted_iota(jnp.int32, (PAIR, PAIR), 0)
    pcol = lax.broadcasted_iota(jnp.int32, (PAIR, PAIR), 1)
    same_head = (prow < HEAD_DIM) == (pcol < HEAD_DIM)
    eye_p = (prow == pcol).astype(F32)
    heads = [(p, h) for p in range(n_pairs) for h in (0, 1)]
    sl = [slice(p * PAIR, (p + 1) * PAIR) for p in range(n_pairs)]
    bt_t = bt.T
    kt_t = kt.T

    x = {(p, h): jnp.concatenate([jnp.where(head_mask[h], at[:, sl[p]], 0.0),
                                  jnp.where(head_mask[h], rt[:, sl[p]], 0.0)], axis=0)
         for p, h in heads}
    z_b = {(p, h): _mm(x[p, h], bt_t[sl[p]], NN, passes) for p, h in heads}
    z_k = {(p, h): _mm(x[p, h], kt_t[sl[p]], NN, passes) for p, h in heads}
    l_ab = {k_: jnp.where(strict, z[:c], 0.0) for k_, z in z_b.items()}
    m_rb = {k_: jnp.where(incl, z[c:], 0.0) for k_, z in z_b.items()}
    l_ak = {k_: jnp.where(strict, z[:c], 0.0) for k_, z in z_k.items()}
    m_rk = {k_: jnp.where(incl, z[c:], 0.0) for k_, z in z_k.items()}
    pw = l_ab
    t_inv = {k_: eye_c + l for k_, l in l_ab.items()}
    for _ in range(int(math.log2(c)) - 1):
        pw = {k_: _mm(m, m, NN, passes) for k_, m in pw.items()}
        t_inv = {k_: t_inv[k_] + _mm(pw[k_], t_inv[k_], NN, passes) for k_ in heads}
    lv = {(p, h): _mm(l_ak[p, h], v[:, sl[p]], NN, passes) for p, h in heads}
    wu = {(p, h): _mm(t_inv[p, h], jnp.concatenate([at[:, sl[p]], lv[p, h]], axis=1), NN, passes)
          for p, h in heads}
    qy = {k_: _mm(m_rb[k_], wu[k_], NN, passes) for k_ in heads}
    mv = {(p, h): _mm(m_rk[p, h], v[:, sl[p]], NN, passes) for p, h in heads}

    ys, s_next = [], []
    for p in range(n_pairs):
        pick = lambda t0, t1: jnp.where(head0, t0, t1)
        w = pick(wu[p, 0][:, :PAIR], wu[p, 1][:, :PAIR])
        u0 = pick(wu[p, 0][:, PAIR:], wu[p, 1][:, PAIR:])
        qeff = rt[:, sl[p]] + pick(qy[p, 0][:, :PAIR], qy[p, 1][:, :PAIR])
        y1 = pick(qy[p, 0][:, PAIR:] + mv[p, 0], qy[p, 1][:, PAIR:] + mv[p, 1])
        d_p = d_last[:, sl[p]]
        phi = jnp.where(same_head, (eye_p + _mm(w.T, bt[:, sl[p]], NN, passes)) * d_p, 0.0)
        uv_t = jnp.concatenate([u0, v[:, sl[p]]], axis=0).T
        bk = jnp.concatenate([bt[:, sl[p]], kt[:, sl[p]]], axis=0)
        psi = jnp.where(same_head, _mm(uv_t, bk, NN, passes) * d_p, 0.0)
        ys.append(_mm(qeff, s_prev[p].T, NN, state_passes) + y1)
        s_next.append(_mm(s_prev[p], phi, NN, state_passes) + psi)
    return jnp.concatenate(ys, axis=1), s_next


def _rwkv_kernel(prkv_ref, plora_ref, vec_ref, wl_ref, al_ref, gl_ref, bd_ref, y_ref, s_ref,
                 *, passes):
    c = prkv_ref.shape[0]
    width = RWKV_WIDTH

    @pl.when(pl.program_id(1) == 0)
    def _():
        s_ref[...] = jnp.zeros_like(s_ref)

    r = prkv_ref[:, 0:width]
    k_raw = prkv_ref[:, width:2 * width]
    v = prkv_ref[:, 2 * width:3 * width]
    p_wd = plora_ref[:, 0:LORA_PAD]
    p_ad = plora_ref[:, LORA_PAD:2 * LORA_PAD]
    p_gd = plora_ref[:, 2 * LORA_PAD:3 * LORA_PAD]
    w0 = vec_ref[0:1, :]
    a0 = vec_ref[1:2, :]
    k_k = vec_ref[2:3, :]
    k_a = vec_ref[3:4, :]
    r_k = vec_ref[4:5, :]
    gn_w = vec_ref[5:6, :]
    gn_b = vec_ref[6:7, :]
    bd = bd_ref[...]

    def seg_sum(z):
        return _mm_exact_rhs(z, bd)

    w_log = -_softplus(-(w0 + _mm(jnp.tanh(p_wd), wl_ref[...], NN, 6))) - 0.5
    log_w = -jnp.exp(w_log)
    a = _sigmoid(a0 + _mm(p_ad, al_ref[...], NN, 6))
    g = _mm(_sigmoid(p_gd), gl_ref[...], NN, 6)
    kk = k_raw * k_k
    kk = kk / jnp.maximum(jnp.sqrt(seg_sum(kk * kk)), L2_EPS)
    k = k_raw * (1.0 + (a - 1.0) * k_a)

    row = lax.broadcasted_iota(jnp.int32, (c, c), 0)
    col = lax.broadcasted_iota(jnp.int32, (c, c), 1)
    tri = (row >= col).astype(BF16)
    cum = _mm_exact_lhs(tri, log_w)
    d_incl = jnp.exp(cum)
    d_inv = jnp.exp(-cum)
    d_excl = jnp.exp(cum - log_w)
    rt = r * d_incl
    kt = k * d_inv
    at = -kk * d_excl
    bt = kk * a * d_inv
    d_last = d_incl[c - 1:c, :]

    y, s_next = _rwkv_chunk(rt, kt, at, bt, v, d_last, [s_ref[p] for p in range(N_PAIRS)],
                            passes, RWKV_STATE_PASSES)
    for p in range(N_PAIRS):
        s_ref[p] = s_next[p]

    inv_n = 1.0 / HEAD_DIM
    mu = seg_sum(y) * inv_n
    yc = y - mu
    var = seg_sum(yc * yc) * inv_n
    yn = yc * lax.rsqrt(var + GN_EPS) * gn_w + gn_b
    bonus = seg_sum(r * k * r_k) * v
    y_ref[...] = ((yn + bonus) * g).astype(y_ref.dtype)


def _rwkv_call(p_rkv, p_lora, vecs, wl, al, gl, bd, batch, seq_len, passes):
    n = p_rkv.shape[0]
    c = RWKV_CHUNK
    nc = seq_len // c
    row_map = lambda b, i: (b * nc + i, 0)
    const = lambda b, i: (0, 0)
    return pl.pallas_call(
        functools.partial(_rwkv_kernel, passes=passes),
        grid=(batch, nc),
        in_specs=[
            pl.BlockSpec((c, 3 * RWKV_WIDTH), row_map),
            pl.BlockSpec((c, 3 * LORA_PAD), row_map),
            pl.BlockSpec((8, RWKV_WIDTH), const),
            pl.BlockSpec((LORA_PAD, RWKV_WIDTH), const),
            pl.BlockSpec((LORA_PAD, RWKV_WIDTH), const),
            pl.BlockSpec((LORA_PAD, RWKV_WIDTH), const),
            pl.BlockSpec((RWKV_WIDTH, RWKV_WIDTH), const),
        ],
        out_specs=pl.BlockSpec((c, RWKV_WIDTH), row_map),
        out_shape=jax.ShapeDtypeStruct((n, RWKV_WIDTH), BF16),
        scratch_shapes=[pltpu.VMEM((N_PAIRS, PAIR, PAIR), F32)],
        compiler_params=pltpu.CompilerParams(
            dimension_semantics=("arbitrary", "arbitrary"), vmem_limit_bytes=VMEM_LIMIT),
        name="rwkv7_chunked",
    )(p_rkv, p_lora, vecs, wl, al, gl, bd)


def _moba_kernel(q_ref, k_ref, v_ref, kaug_ref, o_ref, kmean_ref, vt_ref,
                 m0_ref, m1_ref, acc0_ref, acc1_ref, s_even_ref, s_odd_ref,
                 smax_even_ref, smax_odd_ref, *, n_blocks):
    blk = MOBA_BLOCK
    tk = MOBA_KV_TILE
    tq = q_ref.shape[0]
    i = pl.program_id(2)
    nb_pad = kmean_ref.shape[0]
    m_refs, acc_refs = (m0_ref, m1_ref), (acc0_ref, acc1_ref)

    @pl.when(i == 0)
    def _():
        kmean_ref[...] = jnp.zeros_like(kmean_ref)

        def mean_body(n, carry):
            off = pl.multiple_of(n * blk, blk)
            kb = k_ref[pl.ds(off, blk), :].astype(F32)
            kmean_ref[pl.ds(n, 1), :] = jnp.sum(kb, axis=0, keepdims=True) * (1.0 / blk)
            return carry
        lax.fori_loop(0, n_blocks, mean_body, 0)

        ones = jnp.ones((MOBA_V_ROWS - HEAD_DIM, tk), BF16)

        def vt_body(j, carry):
            off = pl.multiple_of(j * tk, tk)
            v_t = v_ref[pl.ds(off, tk), :].astype(F32).T.astype(BF16)
            for h in (0, 1):
                vt_ref[j, h] = jnp.concatenate([v_t[h * HEAD_DIM:(h + 1) * HEAD_DIM], ones], axis=0)
            return carry
        lax.fori_loop(0, vt_ref.shape[0], vt_body, 0)

    q_t = q_ref[...].astype(F32).T
    chan = lax.broadcasted_iota(jnp.int32, (PAIR, tq), 0)
    blk_row = lax.broadcasted_iota(jnp.int32, (nb_pad, tq), 0)
    own_blk = (i * tq + lax.broadcasted_iota(jnp.int32, (nb_pad, tq), 1)) // blk
    past = blk_row < own_blk
    aug_row = lax.broadcasted_iota(jnp.int32, (LANES, tq), 0)
    ones_rows = (aug_row >= n_blocks) & (aug_row < n_blocks + MOBA_ALIBI_PARTS)
    kmean = kmean_ref[...]

    qa_t = []
    for h in (0, 1):
        qh_t = jnp.where((chan < HEAD_DIM) == (h == 0), q_t, 0.0)
        gate = _dot(kmean, qh_t, NN, precision=lax.Precision.HIGHEST)
        gate = jnp.where(past, gate, F32_LOWEST)
        sel = jnp.zeros(gate.shape, jnp.bool_)
        for _ in range(MOBA_TOPK):
            mx = jnp.max(gate, axis=0, keepdims=True)
            first = jnp.min(jnp.where(gate == mx, blk_row, nb_pad), axis=0, keepdims=True)
            pick = (blk_row == first) & (mx > F32_LOWEST)
            sel = sel | pick
            gate = jnp.where(pick, F32_LOWEST, gate)
        sel_bias = jnp.where(past & jnp.logical_not(sel), NEG_INF, 0.0)
        aug_t = jnp.concatenate([sel_bias, jnp.zeros((LANES - nb_pad, tq), F32)], axis=0)
        aug_t = jnp.where(ones_rows, 1.0, aug_t)
        qa_t.append(jnp.concatenate([qh_t * (LOG2_E / math.sqrt(HEAD_DIM)), aug_t], axis=0).astype(BF16))

    def tile_scores(j):
        off = pl.multiple_of(j * tk, tk)
        k_t = k_ref[pl.ds(off, tk), :]
        return [_dot(jnp.concatenate([k_t, kaug_ref[h, pl.ds(off, tk), :]], axis=1), qa_t[h])
                for h in (0, 1)]

    def tile_update(j, buf):
        s_buf, smax_buf = buf
        for h in (0, 1):
            m_old = m_refs[h][...]
            m_new = jnp.maximum(m_old, smax_buf[h])
            p = jnp.exp2(s_buf[h] - m_new).astype(BF16)
            pv = _dot(vt_ref[j, h], p)
            acc_refs[h][...] = jnp.exp2(m_old - m_new) * acc_refs[h][...] + pv
            m_refs[h][...] = m_new

    for h in (0, 1):
        m_refs[h][...] = jnp.full(m_refs[h].shape, F32_LOWEST, F32)
        acc_refs[h][...] = jnp.zeros(acc_refs[h].shape, F32)

    even = (s_even_ref, smax_even_ref)
    odd = (s_odd_ref, smax_odd_ref)

    def put_scores(buf, s):
        for h in (0, 1):
            buf[0][h] = s[h]
            buf[1][h] = jnp.max(s[h], axis=0, keepdims=True)

    j_own = (i * tq) // tk
    key_pos = j_own * tk + lax.broadcasted_iota(jnp.int32, (tk, tq), 0)
    query_pos = i * tq + lax.broadcasted_iota(jnp.int32, (tk, tq), 1)
    causal = key_pos <= query_pos
    put_scores(even, [jnp.where(causal, s_h, NEG_INF) for s_h in tile_scores(j_own)])

    def previous_tile(j):
        return jnp.where(j == 0, j_own, j - 1)

    def pipelined_step(j, src, dst):
        put_scores(dst, tile_scores(j))
        tile_update(previous_tile(j), src)

    def kv_pair_step(u, carry):
        pipelined_step(2 * u, even, odd)
        pipelined_step(2 * u + 1, odd, even)
        return carry
    lax.fori_loop(0, j_own // 2, kv_pair_step, 0)

    @pl.when(j_own % 2 == 1)
    def _():
        pipelined_step(j_own - 1, even, odd)
        tile_update(j_own - 1, odd)

    @pl.when(j_own % 2 == 0)
    def _():
        tile_update(previous_tile(j_own), even)

    out_t = jnp.concatenate([acc_refs[h][0:HEAD_DIM, :] / acc_refs[h][HEAD_DIM:HEAD_DIM + 1, :]
                             for h in (0, 1)], axis=0)
    o_ref[...] = out_t.T.astype(o_ref.dtype)


def _moba_call(qkv, kaug, batch, seq_len):
    n = qkv.shape[0]
    blk = MOBA_BLOCK
    tq = MOBA_Q_TILE
    nb = seq_len // blk
    nq = seq_len // tq
    assert nb + MOBA_ALIBI_PARTS <= LANES and seq_len % MOBA_KV_TILE == 0
    assert MOBA_KV_TILE % tq == 0 and tq % blk == 0
    lane_groups = MOBA_WIDTH // LANES
    return pl.pallas_call(
        functools.partial(_moba_kernel, n_blocks=nb),
        grid=(batch, N_PAIRS, nq),
        in_specs=[
            pl.BlockSpec((tq, PAIR), lambda b, p, i: (b * nq + i, p)),
            pl.BlockSpec((seq_len, PAIR), lambda b, p, i: (b, lane_groups + p)),
            pl.BlockSpec((seq_len, PAIR), lambda b, p, i: (b, 2 * lane_groups + p)),
            pl.BlockSpec((2, seq_len, LANES), lambda b, p, i: (p, 0, 0)),
        ],
        out_specs=pl.BlockSpec((tq, PAIR), lambda b, p, i: (b * nq + i, p)),
        out_shape=jax.ShapeDtypeStruct((n, MOBA_WIDTH), BF16),
        scratch_shapes=[
            pltpu.VMEM((-(-nb // 8) * 8, PAIR), F32),
            pltpu.VMEM((seq_len // MOBA_KV_TILE, 2, MOBA_V_ROWS, MOBA_KV_TILE), BF16),
            pltpu.VMEM((1, tq), F32), pltpu.VMEM((1, tq), F32),
            pltpu.VMEM((MOBA_V_ROWS, tq), F32), pltpu.VMEM((MOBA_V_ROWS, tq), F32),
            pltpu.VMEM((2, MOBA_KV_TILE, tq), F32), pltpu.VMEM((2, MOBA_KV_TILE, tq), F32),
            pltpu.VMEM((2, 1, tq), F32), pltpu.VMEM((2, 1, tq), F32),
        ],
        compiler_params=pltpu.CompilerParams(
            dimension_semantics=("arbitrary", "arbitrary", "arbitrary"),
            vmem_limit_bytes=VMEM_LIMIT),
        name="moba_attention",
    )(qkv, qkv, qkv, kaug)


def _moba_key_aug(seq_len):
    nb = seq_len // MOBA_BLOCK
    heads = MOBA_WIDTH // HEAD_DIM
    pos = jnp.arange(seq_len, dtype=jnp.int32)
    slopes = 2.0 ** (-8.0 * (jnp.arange(heads, dtype=F32) + 1.0) / heads)
    onehot = ((pos // MOBA_BLOCK)[:, None] == jnp.arange(LANES)[None, :]).astype(F32)
    bias = (LOG2_E * slopes)[:, None] * pos.astype(F32)[None, :]
    lane = jnp.arange(LANES)[None, None, :]
    aug = jnp.broadcast_to(onehot[None], (heads, seq_len, LANES))
    rem = bias
    for part in range(MOBA_ALIBI_PARTS):
        bits = lax.bitcast_convert_type(rem, jnp.uint32) & jnp.uint32(0xFFFF0000)
        piece = lax.bitcast_convert_type(bits, F32)
        aug = jnp.where(lane == nb + part, piece[:, :, None], aug)
        rem = rem - piece
    return aug.astype(BF16)


def _layer_norm(z, g, b):
    mu = jnp.mean(z, axis=-1, keepdims=True)
    zc = z - mu
    var = jnp.mean(zc * zc, axis=-1, keepdims=True)
    return zc * lax.rsqrt(var + LN_EPS) * g + b


def _outproj_kernel(ya_ref, yb_ref, x_ref, wa_ref, wb_ref, g_ref, b_ref, wr_ref, br_ref,
                    h_ref, hb_ref, lg_ref):
    mix = _dot(ya_ref[...], wa_ref[...]) + _dot(yb_ref[...], wb_ref[...])
    h = _layer_norm(DEEPNORM_ALPHA * x_ref[...] + mix, g_ref[...], b_ref[...])
    h_ref[...] = h
    hb_ref[...] = h.astype(BF16)
    lg_ref[...] = _dot(h, wr_ref[...], NN, precision=lax.Precision.HIGHEST) + br_ref[...]


def _outproj_call(y_a, y_b, x2, wa, wb, ln_g, ln_b, w_router, b_router):
    n = x2.shape[0]
    tm = OUTPROJ_TM
    row = lambda i: (i, 0)
    const = lambda i: (0, 0)
    return pl.pallas_call(
        _outproj_kernel,
        grid=(n // tm,),
        in_specs=[
            pl.BlockSpec((tm, RWKV_WIDTH), row),
            pl.BlockSpec((tm, MOBA_WIDTH), row),
            pl.BlockSpec((tm, D_MODEL), row),
            pl.BlockSpec((RWKV_WIDTH, D_MODEL), const),
            pl.BlockSpec((MOBA_WIDTH, D_MODEL), const),
            pl.BlockSpec((1, D_MODEL), const),
            pl.BlockSpec((1, D_MODEL), const),
            pl.BlockSpec((D_MODEL, ROUTER_PAD), const),
            pl.BlockSpec((1, ROUTER_PAD), const),
        ],
        out_specs=[
            pl.BlockSpec((tm, D_MODEL), row),
            pl.BlockSpec((tm, D_MODEL), row),
            pl.BlockSpec((tm, ROUTER_PAD), row),
        ],
        out_shape=[
            jax.ShapeDtypeStruct((n, D_MODEL), F32),
            jax.ShapeDtypeStruct((n, D_MODEL), BF16),
            jax.ShapeDtypeStruct((n, ROUTER_PAD), F32),
        ],
        compiler_params=pltpu.CompilerParams(
            dimension_semantics=("arbitrary",), vmem_limit_bytes=VMEM_LIMIT),
        name="outproj_ln_router",
    )(y_a, y_b, x2, wa, wb, ln_g, ln_b, w_router, b_router)


def _route(logits):
    lane = lax.broadcasted_iota(jnp.int32, logits.shape, 1)
    is_group = (lane >= GROUP_LANE0) & (lane < GROUP_LANE0 + N_GROUPS)
    gl = jnp.where(is_group, logits, F32_LOWEST)
    g_max = jnp.max(gl, axis=-1, keepdims=True)
    g_first = jnp.min(jnp.where(gl == g_max, lane, LANES), axis=-1, keepdims=True)
    g_exp = jnp.where(is_group, jnp.exp(gl - g_max), 0.0)
    p_g = 1.0 / jnp.sum(g_exp, axis=-1, keepdims=True)
    g_idx = g_first - GROUP_LANE0
    in_group = (lane >= g_idx * EXPERTS_PER_GROUP) & (lane < (g_idx + 1) * EXPERTS_PER_GROUP)
    el = jnp.where(in_group, logits, F32_LOWEST)
    e_max = jnp.max(el, axis=-1, keepdims=True)
    e_exp = jnp.where(in_group, jnp.exp(el - e_max), 0.0)
    e_prob = e_exp / jnp.sum(e_exp, axis=-1, keepdims=True)
    cand = jnp.where(in_group, e_prob, -1.0)
    v1 = jnp.max(cand, axis=-1, keepdims=True)
    i1 = jnp.min(jnp.where(cand == v1, lane, LANES), axis=-1, keepdims=True)
    pick1 = lane == i1
    cand2 = jnp.where(pick1, -1.0, cand)
    v2 = jnp.max(cand2, axis=-1, keepdims=True)
    i2 = jnp.min(jnp.where(cand2 == v2, lane, LANES), axis=-1, keepdims=True)
    pick2 = lane == i2
    denom = v1 + v2
    return jnp.where(pick1, v1 / denom * p_g, jnp.where(pick2, v2 / denom * p_g, 0.0))


def _moe_kernel(hb_ref, h_ref, lg_ref, w1_ref, w3_ref, w2_ref, g_ref, b_ref, o_ref, gates_ref, acc_ref):
    step = pl.program_id(1)

    @pl.when(step == 0)
    def _():
        gates_ref[...] = _route(lg_ref[...])
        acc_ref[...] = jnp.zeros_like(acc_ref)

    tok = hb_ref[...]
    gates = gates_ref[...]
    lane = lax.broadcasted_iota(jnp.int32, gates.shape, 1)
    acc = acc_ref[...]
    for e in range(MOE_EXPERTS_PER_STEP):
        gate_e = jnp.sum(jnp.where(lane == step * MOE_EXPERTS_PER_STEP + e, gates, 0.0),
                         axis=-1, keepdims=True)
        a1 = _dot(tok, w1_ref[e])
        a3 = _dot(tok, w3_ref[e])
        hid = (a1 * _sigmoid(a1)) * a3 * gate_e
        acc = acc + _dot(hid.astype(BF16), w2_ref[e])
    acc_ref[...] = acc

    @pl.when(step == N_EXPERTS // MOE_EXPERTS_PER_STEP - 1)
    def _():
        o_ref[...] = _layer_norm(DEEPNORM_ALPHA * h_ref[...] + acc_ref[...], g_ref[...], b_ref[...])


def _moe_call(h_bf16, h_f32, logits, w1, w3, w2, ln_g, ln_b):
    n = h_f32.shape[0]
    tm = MOE_TM
    row = lambda t, g: (t, 0)
    const = lambda t, g: (0, 0)
    wmap = lambda t, g: (g, 0, 0)
    eps = MOE_EXPERTS_PER_STEP
    return pl.pallas_call(
        _moe_kernel,
        grid=(n // tm, N_EXPERTS // eps),
        in_specs=[
            pl.BlockSpec((tm, D_MODEL), row),
            pl.BlockSpec((tm, D_MODEL), row),
            pl.BlockSpec((tm, ROUTER_PAD), row),
            pl.BlockSpec((eps, D_MODEL, D_EXPERT), wmap),
            pl.BlockSpec((eps, D_MODEL, D_EXPERT), wmap),
            pl.BlockSpec((eps, D_EXPERT, D_MODEL), wmap),
            pl.BlockSpec((1, D_MODEL), const),
            pl.BlockSpec((1, D_MODEL), const),
        ],
        out_specs=pl.BlockSpec((tm, D_MODEL), row),
        out_shape=jax.ShapeDtypeStruct((n, D_MODEL), F32),
        scratch_shapes=[pltpu.VMEM((tm, ROUTER_PAD), F32), pltpu.VMEM((tm, D_MODEL), F32)],
        compiler_params=pltpu.CompilerParams(
            dimension_semantics=("arbitrary", "arbitrary"), vmem_limit_bytes=VMEM_LIMIT),
        name="hier_moe_ln",
    )(h_bf16, h_f32, logits, w1, w3, w2, ln_g, ln_b)


def _pad_cols(w, width):
    return jnp.pad(w, ((0, 0), (0, width - w.shape[1])))


def _pad_rows(w, height):
    return jnp.pad(w, ((0, height - w.shape[0]), (0, 0)))


RWKV_PASSES = 1
RWKV_STATE_PASSES = 3


def kernel(x, w_in, mu_shift, w0, w_lora_up, a0, a_lora_up, g_lora_up, k_k, k_a, r_k, gn_w, gn_b, w_out, ln1_g, ln1_b, w_group, b_group, w_expert, b_expert, w1_exp, w3_exp, w2_exp, ln2_g, ln2_b):
    batch, seq_len, d = x.shape
    assert d == D_MODEL
    n = batch * seq_len
    x2 = x.reshape(n, d)

    c_rkv = 3 * RWKV_WIDTH
    c_wd = c_rkv + DECAY_RANK
    c_ad = c_wd + AAA_RANK
    c_gd = c_ad + GATE_RANK
    w_cat = jnp.concatenate([
        w_in[:, :c_rkv],
        _pad_cols(w_in[:, c_rkv:c_wd], LORA_PAD),
        _pad_cols(w_in[:, c_wd:c_ad], LORA_PAD),
        _pad_cols(w_in[:, c_ad:c_gd], LORA_PAD),
        w_in[:, c_gd:],
    ], axis=1).astype(BF16)
    mu2 = mu_shift[None, :]
    mu_cat = jnp.concatenate([
        mu2[:, :c_rkv],
        _pad_cols(mu2[:, c_rkv:c_wd], LORA_PAD),
        _pad_cols(mu2[:, c_wd:c_ad], LORA_PAD),
        _pad_cols(mu2[:, c_ad:c_gd], LORA_PAD),
    ], axis=1)
    p_rkv, p_lora, p_moba = _inproj_call(x2, w_cat, mu_cat, seq_len)

    vecs = jnp.stack([w0, a0, k_k, k_a, r_k.reshape(-1), gn_w, gn_b, jnp.zeros_like(w0)], axis=0)
    head_id = jnp.arange(RWKV_WIDTH) // HEAD_DIM
    bd = (head_id[:, None] == head_id[None, :]).astype(BF16)
    y_a = _rwkv_call(p_rkv, p_lora, vecs,
                     _pad_rows(w_lora_up, LORA_PAD), _pad_rows(a_lora_up, LORA_PAD),
                     _pad_rows(g_lora_up, LORA_PAD), bd, batch, seq_len, RWKV_PASSES)

    y_b = _moba_call(p_moba, _moba_key_aug(seq_len), batch, seq_len)

    w_out_b = w_out.astype(BF16)
    w_router = _pad_cols(jnp.concatenate([w_expert, w_group], axis=1), ROUTER_PAD)
    b_router = _pad_cols(jnp.concatenate([b_expert, b_group])[None, :], ROUTER_PAD)
    h1, h1_b, logits = _outproj_call(y_a, y_b, x2, w_out_b[:RWKV_WIDTH], w_out_b[RWKV_WIDTH:],
                                     ln1_g[None, :], ln1_b[None, :], w_router, b_router)

    flat = lambda w: w.astype(BF16).reshape((N_EXPERTS,) + w.shape[2:])
    out = _moe_call(h1_b, h1, logits, flat(w1_exp), flat(w3_exp), flat(w2_exp),
                    ln2_g[None, :], ln2_b[None, :])
    return out.reshape(batch, seq_len, d)
```

```python
import functools
import math

import jax
import jax.numpy as jnp
from jax import lax
from jax.experimental import pallas as pl
from jax.experimental.pallas import tpu as pltpu

F32 = jnp.float32
BF16 = jnp.bfloat16

D_MODEL = 1024
HEAD_DIM = 64
RWKV_WIDTH = 512
MOBA_WIDTH = 512
DECAY_RANK = 32
AAA_RANK = 32
GATE_RANK = 96
GN_EPS = 64e-5
L2_EPS = 1e-12
MOBA_BLOCK = 256
MOBA_TOPK = 3
N_GROUPS = 4
EXPERTS_PER_GROUP = 8
N_EXPERTS = N_GROUPS * EXPERTS_PER_GROUP
D_EXPERT = 256
LN_EPS = 1e-5
DEEPNORM_ALPHA = float(2.0 ** 0.25)
NEG_INF = -1e30
F32_LOWEST = -3.0e38

LANES = 128
PAIR = 2 * HEAD_DIM
N_PAIRS = RWKV_WIDTH // PAIR
LORA_PAD = LANES
RWKV_COLS_PAD = 3 * RWKV_WIDTH + 3 * LORA_PAD
IN_COLS_PAD = RWKV_COLS_PAD + 3 * MOBA_WIDTH
VMEM_LIMIT = 56 * 1024 * 1024

INPROJ_TM = 512
INPROJ_TN = 384
RWKV_CHUNK = 64
RWKV_CHUNKS_PER_STEP = 4
RWKV_PASSES = 1
RWKV_STATE_PASSES = 1
OUTPROJ_TM = 512
MOE_TM = 1024
MOE_EXPERTS_PER_STEP = 4
MOBA_KV_TILE = 512
MOBA_Q_TILE = 512
MOBA_V_ROWS = HEAD_DIM + 16
MOBA_ALIBI_PARTS = 3
LOG2_E = 1.4426950408889634
ROUTER_PAD = LANES
GROUP_LANE0 = N_EXPERTS

NN = (((1,), (0,)), ((), ()))
NT = (((1,), (1,)), ((), ()))


def _dot(a, b, dims=NN, precision=None):
    return lax.dot_general(a, b, dims, precision=precision, preferred_element_type=F32)


def _split_bf16(x, parts):
    out = []
    rem = x
    for i in range(parts):
        p = rem.astype(BF16)
        out.append(p)
        if i + 1 < parts:
            rem = rem - p.astype(F32)
    return out


def _mm(a, b, dims=NN, passes=3):
    if passes == 1:
        return _dot(a.astype(BF16), b.astype(BF16), dims)
    if passes == 6:
        return _dot(a, b, dims, precision=lax.Precision.HIGHEST)
    a_hi, a_lo = _split_bf16(a, 2)
    b_hi, b_lo = _split_bf16(b, 2)
    return _dot(a_hi, b_hi, dims) + (_dot(a_hi, b_lo, dims) + _dot(a_lo, b_hi, dims))


def _mm_exact_lhs(a_bf16, b, dims=NN):
    b1, b2, b3 = _split_bf16(b, 3)
    return _dot(a_bf16, b1, dims) + (_dot(a_bf16, b2, dims) + _dot(a_bf16, b3, dims))


def _mm_exact_rhs(a, b_bf16, dims=NN):
    a1, a2, a3 = _split_bf16(a, 3)
    return _dot(a1, b_bf16, dims) + (_dot(a2, b_bf16, dims) + _dot(a3, b_bf16, dims))


def _inproj_kernel(x_ref, w_ref, mu_ref, prkv_ref, plora_ref, pm_ref, carry_ref, *, tiles_per_seq):
    tm = x_ref.shape[0]
    xb = x_ref[...].astype(BF16)
    seq_start = (pl.program_id(0) % tiles_per_seq) == 0
    row0 = lax.broadcasted_iota(jnp.int32, (tm, INPROJ_TN), 0) == 0
    n_shift_tiles = RWKV_COLS_PAD // INPROJ_TN
    for j in range(n_shift_tiles):
        c0 = j * INPROJ_TN
        acc = _dot(xb, w_ref[:, c0:c0 + INPROJ_TN])
        prev_last = jnp.where(seq_start, 0.0, carry_ref[0:1, c0:c0 + INPROJ_TN])
        shifted = jnp.where(row0, prev_last, pltpu.roll(acc, 1, 0))
        carry_ref[0:1, c0:c0 + INPROJ_TN] = acc[tm - 1:tm, :]
        out = acc + (shifted - acc) * mu_ref[:, c0:c0 + INPROJ_TN]
        if c0 < 3 * RWKV_WIDTH:
            prkv_ref[:, c0:c0 + INPROJ_TN] = out
        else:
            plora_ref[:, c0 - 3 * RWKV_WIDTH:c0 - 3 * RWKV_WIDTH + INPROJ_TN] = out
    for j in range(3 * MOBA_WIDTH // INPROJ_TN):
        c0 = j * INPROJ_TN
        acc = _dot(xb, w_ref[:, RWKV_COLS_PAD + c0:RWKV_COLS_PAD + c0 + INPROJ_TN])
        pm_ref[:, c0:c0 + INPROJ_TN] = acc.astype(BF16)


def _inproj_call(x2, w_cat, mu_cat, seq_len):
    n = x2.shape[0]
    tm = INPROJ_TM
    assert seq_len % tm == 0 and (3 * RWKV_WIDTH) % INPROJ_TN == 0
    return pl.pallas_call(
        functools.partial(_inproj_kernel, tiles_per_seq=seq_len // tm),
        grid=(n // tm,),
        in_specs=[
            pl.BlockSpec((tm, D_MODEL), lambda i: (i, 0)),
            pl.BlockSpec((D_MODEL, IN_COLS_PAD), lambda i: (0, 0)),
            pl.BlockSpec((1, RWKV_COLS_PAD), lambda i: (0, 0)),
        ],
        out_specs=[
            pl.BlockSpec((tm, 3 * RWKV_WIDTH), lambda i: (i, 0)),
            pl.BlockSpec((tm, 3 * LORA_PAD), lambda i: (i, 0)),
            pl.BlockSpec((tm, 3 * MOBA_WIDTH), lambda i: (i, 0)),
        ],
        out_shape=[
            jax.ShapeDtypeStruct((n, 3 * RWKV_WIDTH), F32),
            jax.ShapeDtypeStruct((n, 3 * LORA_PAD), F32),
            jax.ShapeDtypeStruct((n, 3 * MOBA_WIDTH), BF16),
        ],
        scratch_shapes=[pltpu.VMEM((8, RWKV_COLS_PAD), F32)],
        compiler_params=pltpu.CompilerParams(
            dimension_semantics=("arbitrary",), vmem_limit_bytes=VMEM_LIMIT),
        name="inproj_shift",
    )(x2, w_cat, mu_cat)


def _softplus(z):
    return jnp.maximum(z, 0.0) + jnp.log(1.0 + jnp.exp(-jnp.abs(z)))


def _sigmoid(z):
    return 1.0 / (1.0 + jnp.exp(-z))


def _rwkv_chunks(rt, kt, at, bt, v, d_incl, s_prev, passes, state_passes):
    c = RWKV_CHUNK
    n_chunks = rt.shape[0] // c
    n_pairs = len(s_prev)
    row = lax.broadcasted_iota(jnp.int32, (c, c), 0)
    col = lax.broadcasted_iota(jnp.int32, (c, c), 1)
    strict = row > col
    incl = row >= col
    eye_c = (row == col).astype(F32)
    lane = lax.broadcasted_iota(jnp.int32, (1, PAIR), 1)
    head0 = lane < HEAD_DIM
    head_mask = (head0, jnp.logical_not(head0))
    prow = lax.broadcasted_iota(jnp.int32, (PAIR, PAIR), 0)
    pcol = lax.broadcasted_iota(jnp.int32, (PAIR, PAIR), 1)
    same_head = (prow < HEAD_DIM) == (pcol < HEAD_DIM)
    eye_p = (prow == pcol).astype(F32)
    pick = lambda t0, t1: jnp.where(head0, t0, t1)
    rows = [slice(ci * c, (ci + 1) * c) for ci in range(n_chunks)]
    sl = [slice(p * PAIR, (p + 1) * PAIR) for p in range(n_pairs)]
    pairs = [(ci, p) for ci in range(n_chunks) for p in range(n_pairs)]
    heads = [(ci, p, h) for ci, p in pairs for h in (0, 1)]
    cut = lambda t, ci, p: t[rows[ci], sl[p]]
    bt_t = [bt[rows[ci]].T for ci in range(n_chunks)]
    kt_t = [kt[rows[ci]].T for ci in range(n_chunks)]

    x = {(ci, p, h): jnp.concatenate([jnp.where(head_mask[h], cut(at, ci, p), 0.0),
                                      jnp.where(head_mask[h], cut(rt, ci, p), 0.0)], axis=0)
         for ci, p, h in heads}
    z_b = {(ci, p, h): _mm(x[ci, p, h], bt_t[ci][sl[p]], NN, passes) for ci, p, h in heads}
    z_k = {(ci, p, h): _mm(x[ci, p, h], kt_t[ci][sl[p]], NN, passes) for ci, p, h in heads}
    l_ab = {k_: jnp.where(strict, z[:c], 0.0) for k_, z in z_b.items()}
    m_rb = {k_: jnp.where(incl, z[c:], 0.0) for k_, z in z_b.items()}
    l_ak = {k_: jnp.where(strict, z[:c], 0.0) for k_, z in z_k.items()}
    m_rk = {k_: jnp.where(incl, z[c:], 0.0) for k_, z in z_k.items()}
    pw = l_ab
    t_inv = {k_: eye_c + l for k_, l in l_ab.items()}
    for _ in range(int(math.log2(c)) - 1):
        pw = {k_: _mm(m, m, NN, passes) for k_, m in pw.items()}
        t_inv = {k_: t_inv[k_] + _mm(pw[k_], t_inv[k_], NN, passes) for k_ in heads}
    lv = {(ci, p, h): _mm(l_ak[ci, p, h], cut(v, ci, p), NN, passes) for ci, p, h in heads}
    wu = {(ci, p, h): _mm(t_inv[ci, p, h], jnp.concatenate([cut(at, ci, p), lv[ci, p, h]], axis=1),
                          NN, passes) for ci, p, h in heads}
    qy = {k_: _mm(m_rb[k_], wu[k_], NN, passes) for k_ in heads}
    mv = {(ci, p, h): _mm(m_rk[ci, p, h], cut(v, ci, p), NN, passes) for ci, p, h in heads}

    qeff, y1, phi, psi = {}, {}, {}, {}
    for ci, p in pairs:
        w = pick(wu[ci, p, 0][:, :PAIR], wu[ci, p, 1][:, :PAIR])
        u0 = pick(wu[ci, p, 0][:, PAIR:], wu[ci, p, 1][:, PAIR:])
        qeff[ci, p] = cut(rt, ci, p) + pick(qy[ci, p, 0][:, :PAIR], qy[ci, p, 1][:, :PAIR])
        y1[ci, p] = pick(qy[ci, p, 0][:, PAIR:] + mv[ci, p, 0], qy[ci, p, 1][:, PAIR:] + mv[ci, p, 1])
        d_p = d_incl[(ci + 1) * c - 1:(ci + 1) * c, sl[p]]
        phi[ci, p] = jnp.where(same_head, (eye_p + _mm(w.T, cut(bt, ci, p), NN, passes)) * d_p, 0.0)
        uv_t = jnp.concatenate([u0, cut(v, ci, p)], axis=0).T
        bk = jnp.concatenate([cut(bt, ci, p), cut(kt, ci, p)], axis=0)
        psi[ci, p] = jnp.where(same_head, _mm(uv_t, bk, NN, passes) * d_p, 0.0)

    state = list(s_prev)
    ys = [[None] * n_pairs for _ in range(n_chunks)]
    for ci in range(n_chunks):
        for p in range(n_pairs):
            ys[ci][p] = _mm(qeff[ci, p], state[p].T, NN, state_passes) + y1[ci, p]
            state[p] = _mm(state[p], phi[ci, p], NN, state_passes) + psi[ci, p]
    y = jnp.concatenate([jnp.concatenate(ys[ci], axis=1) for ci in range(n_chunks)], axis=0)
    return y, state


def _rwkv_kernel(prkv_ref, plora_ref, vec_ref, wl_ref, al_ref, gl_ref, bd_ref, y_ref, s_ref):
    rows = prkv_ref.shape[0]
    c = RWKV_CHUNK
    width = RWKV_WIDTH

    @pl.when(pl.program_id(1) == 0)
    def _():
        s_ref[...] = jnp.zeros_like(s_ref)

    r = prkv_ref[:, 0:width]
    k_raw = prkv_ref[:, width:2 * width]
    v = prkv_ref[:, 2 * width:3 * width]
    p_wd = plora_ref[:, 0:LORA_PAD]
    p_ad = plora_ref[:, LORA_PAD:2 * LORA_PAD]
    p_gd = plora_ref[:, 2 * LORA_PAD:3 * LORA_PAD]
    w0 = vec_ref[0:1, :]
    a0 = vec_ref[1:2, :]
    k_k = vec_ref[2:3, :]
    k_a = vec_ref[3:4, :]
    r_k = vec_ref[4:5, :]
    gn_w = vec_ref[5:6, :]
    gn_b = vec_ref[6:7, :]
    bd = bd_ref[...]

    def seg_sum(z):
        halves = []
        for c0 in range(0, width, bd.shape[0]):
            z_hi, z_lo = _split_bf16(z[:, c0:c0 + bd.shape[0]], 2)
            halves.append(_dot(z_hi, bd) + _dot(z_lo, bd))
        return jnp.concatenate(halves, axis=1)

    w_log = -_softplus(-(w0 + _mm(jnp.tanh(p_wd), wl_ref[...], NN, 3))) - 0.5
    log_w = -jnp.exp(w_log)
    a = _sigmoid(a0 + _mm(p_ad, al_ref[...], NN, RWKV_PASSES))
    g = _mm(_sigmoid(p_gd), gl_ref[...], NN, RWKV_PASSES)
    kk = k_raw * k_k
    kk = kk / jnp.maximum(jnp.sqrt(seg_sum(kk * kk)), L2_EPS)
    k = k_raw * (1.0 + (a - 1.0) * k_a)

    row = lax.broadcasted_iota(jnp.int32, (rows, rows), 0)
    col = lax.broadcasted_iota(jnp.int32, (rows, rows), 1)
    tri = ((row >= col) & (row // c == col // c)).astype(BF16)
    cum = _mm_exact_lhs(tri, log_w)
    d_incl = jnp.exp(cum)
    d_inv = jnp.exp(-cum)
    d_excl = jnp.exp(cum - log_w)
    rt = r * d_incl
    kt = k * d_inv
    at = -kk * d_excl
    bt = kk * a * d_inv

    y, s_next = _rwkv_chunks(rt, kt, at, bt, v, d_incl, [s_ref[p] for p in range(N_PAIRS)],
                             RWKV_PASSES, RWKV_STATE_PASSES)
    for p in range(N_PAIRS):
        s_ref[p] = s_next[p]

    inv_n = 1.0 / HEAD_DIM
    mu = seg_sum(y) * inv_n
    yc = y - mu
    var = seg_sum(yc * yc) * inv_n
    yn = yc * lax.rsqrt(var + GN_EPS) * gn_w + gn_b
    bonus = seg_sum(r * k * r_k) * v
    y_ref[...] = ((yn + bonus) * g).astype(y_ref.dtype)


def _rwkv_call(p_rkv, p_lora, vecs, wl, al, gl, bd, batch, seq_len):
    n = p_rkv.shape[0]
    rows = RWKV_CHUNK * RWKV_CHUNKS_PER_STEP
    assert seq_len % rows == 0
    steps = seq_len // rows
    row_map = lambda b, i: (b * steps + i, 0)
    const = lambda b, i: (0, 0)
    return pl.pallas_call(
        _rwkv_kernel,
        grid=(batch, steps),
        in_specs=[
            pl.BlockSpec((rows, 3 * RWKV_WIDTH), row_map),
            pl.BlockSpec((rows, 3 * LORA_PAD), row_map),
            pl.BlockSpec((8, RWKV_WIDTH), const),
            pl.BlockSpec((LORA_PAD, RWKV_WIDTH), const),
            pl.BlockSpec((LORA_PAD, RWKV_WIDTH), const),
            pl.BlockSpec((LORA_PAD, RWKV_WIDTH), const),
            pl.BlockSpec((2 * PAIR, 2 * PAIR), const),
        ],
        out_specs=pl.BlockSpec((rows, RWKV_WIDTH), row_map),
        out_shape=jax.ShapeDtypeStruct((n, RWKV_WIDTH), BF16),
        scratch_shapes=[pltpu.VMEM((N_PAIRS, PAIR, PAIR), F32)],
        compiler_params=pltpu.CompilerParams(
            dimension_semantics=("arbitrary", "arbitrary"), vmem_limit_bytes=VMEM_LIMIT),
        name="rwkv7_chunked",
    )(p_rkv, p_lora, vecs, wl, al, gl, bd)


def _moba_kernel(q_ref, k_ref, v_ref, kaug_ref, o_ref, kmean_ref, vt_ref,
                 m0_ref, m1_ref, acc0_ref, acc1_ref, s_even_ref, s_odd_ref,
                 smax_even_ref, smax_odd_ref, *, n_blocks):
    blk = MOBA_BLOCK
    tk = MOBA_KV_TILE
    tq = q_ref.shape[0]
    i = pl.program_id(2)
    nb_pad = kmean_ref.shape[0]
    m_refs, acc_refs = (m0_ref, m1_ref), (acc0_ref, acc1_ref)

    @pl.when(i == 0)
    def _():
        kmean_ref[...] = jnp.zeros_like(kmean_ref)

        def mean_body(n, carry):
            off = pl.multiple_of(n * blk, blk)
            kb = k_ref[pl.ds(off, blk), :].astype(F32)
            kmean_ref[pl.ds(n, 1), :] = jnp.sum(kb, axis=0, keepdims=True) * (1.0 / blk)
            return carry
        lax.fori_loop(0, n_blocks, mean_body, 0)

        ones = jnp.ones((MOBA_V_ROWS - HEAD_DIM, tk), BF16)

        def vt_body(j, carry):
            off = pl.multiple_of(j * tk, tk)
            v_t = v_ref[pl.ds(off, tk), :].astype(F32).T.astype(BF16)
            for h in (0, 1):
                vt_ref[j, h] = jnp.concatenate([v_t[h * HEAD_DIM:(h + 1) * HEAD_DIM], ones], axis=0)
            return carry
        lax.fori_loop(0, vt_ref.shape[0], vt_body, 0)

    q_t = q_ref[...].astype(F32).T
    chan = lax.broadcasted_iota(jnp.int32, (PAIR, tq), 0)
    blk_row = lax.broadcasted_iota(jnp.int32, (nb_pad, tq), 0)
    own_blk = (i * tq + lax.broadcasted_iota(jnp.int32, (nb_pad, tq), 1)) // blk
    past = blk_row < own_blk
    aug_row = lax.broadcasted_iota(jnp.int32, (LANES, tq), 0)
    ones_rows = (aug_row >= n_blocks) & (aug_row < n_blocks + MOBA_ALIBI_PARTS)
    kmean = kmean_ref[...]

    qa_t = []
    for h in (0, 1):
        qh_t = jnp.where((chan < HEAD_DIM) == (h == 0), q_t, 0.0)
        gate = _dot(kmean, qh_t, NN, precision=lax.Precision.HIGHEST)
        gate = jnp.where(past, gate, F32_LOWEST)
        sel = jnp.zeros(gate.shape, jnp.bool_)
        for _ in range(MOBA_TOPK):
            mx = jnp.max(gate, axis=0, keepdims=True)
            first = jnp.min(jnp.where(gate == mx, blk_row, nb_pad), axis=0, keepdims=True)
            pick = (blk_row == first) & (mx > F32_LOWEST)
            sel = sel | pick
            gate = jnp.where(pick, F32_LOWEST, gate)
        sel_bias = jnp.where(past & jnp.logical_not(sel), NEG_INF, 0.0)
        aug_t = jnp.concatenate([sel_bias, jnp.zeros((LANES - nb_pad, tq), F32)], axis=0)
        aug_t = jnp.where(ones_rows, 1.0, aug_t)
        qa_t.append(jnp.concatenate([qh_t * (LOG2_E / math.sqrt(HEAD_DIM)), aug_t], axis=0).astype(BF16))

    def tile_scores(j):
        off = pl.multiple_of(j * tk, tk)
        k_t = k_ref[pl.ds(off, tk), :]
        return [_dot(jnp.concatenate([k_t, kaug_ref[h, pl.ds(off, tk), :]], axis=1), qa_t[h])
                for h in (0, 1)]

    def tile_update(j, buf):
        s_buf, smax_buf = buf
        for h in (0, 1):
            m_old = m_refs[h][...]
            m_new = jnp.maximum(m_old, smax_buf[h])
            p = jnp.exp2(s_buf[h] - m_new).astype(BF16)
            pv = _dot(vt_ref[j, h], p)
            acc_refs[h][...] = jnp.exp2(m_old - m_new) * acc_refs[h][...] + pv
            m_refs[h][...] = m_new

    for h in (0, 1):
        m_refs[h][...] = jnp.full(m_refs[h].shape, F32_LOWEST, F32)
        acc_refs[h][...] = jnp.zeros(acc_refs[h].shape, F32)

    even = (s_even_ref, smax_even_ref)
    odd = (s_odd_ref, smax_odd_ref)

    def put_scores(buf, s):
        for h in (0, 1):
            buf[0][h] = s[h]
            buf[1][h] = jnp.max(s[h], axis=0, keepdims=True)

    j_own = (i * tq) // tk
    key_pos = j_own * tk + lax.broadcasted_iota(jnp.int32, (tk, tq), 0)
    query_pos = i * tq + lax.broadcasted_iota(jnp.int32, (tk, tq), 1)
    causal = key_pos <= query_pos
    put_scores(even, [jnp.where(causal, s_h, NEG_INF) for s_h in tile_scores(j_own)])

    def previous_tile(j):
        return jnp.where(j == 0, j_own, j - 1)

    def pipelined_step(j, src, dst):
        put_scores(dst, tile_scores(j))
        tile_update(previous_tile(j), src)

    def kv_pair_step(u, carry):
        pipelined_step(2 * u, even, odd)
        pipelined_step(2 * u + 1, odd, even)
        return carry
    lax.fori_loop(0, j_own // 2, kv_pair_step, 0)

    @pl.when(j_own % 2 == 1)
    def _():
        pipelined_step(j_own - 1, even, odd)
        tile_update(j_own - 1, odd)

    @pl.when(j_own % 2 == 0)
    def _():
        tile_update(previous_tile(j_own), even)

    out_t = jnp.concatenate([acc_refs[h][0:HEAD_DIM, :] / acc_refs[h][HEAD_DIM:HEAD_DIM + 1, :]
                             for h in (0, 1)], axis=0)
    o_ref[...] = out_t.T.astype(o_ref.dtype)


def _moba_call(qkv, kaug, batch, seq_len):
    n = qkv.shape[0]
    blk = MOBA_BLOCK
    tq = MOBA_Q_TILE
    nb = seq_len // blk
    nq = seq_len // tq
    assert nb + MOBA_ALIBI_PARTS <= LANES and seq_len % MOBA_KV_TILE == 0
    assert MOBA_KV_TILE % tq == 0 and tq % blk == 0
    lane_groups = MOBA_WIDTH // LANES
    return pl.pallas_call(
        functools.partial(_moba_kernel, n_blocks=nb),
        grid=(batch, N_PAIRS, nq),
        in_specs=[
            pl.BlockSpec((tq, PAIR), lambda b, p, i: (b * nq + i, p)),
            pl.BlockSpec((seq_len, PAIR), lambda b, p, i: (b, lane_groups + p)),
            pl.BlockSpec((seq_len, PAIR), lambda b, p, i: (b, 2 * lane_groups + p)),
            pl.BlockSpec((2, seq_len, LANES), lambda b, p, i: (p, 0, 0)),
        ],
        out_specs=pl.BlockSpec((tq, PAIR), lambda b, p, i: (b * nq + i, p)),
        out_shape=jax.ShapeDtypeStruct((n, MOBA_WIDTH), BF16),
        scratch_shapes=[
            pltpu.VMEM((-(-nb // 8) * 8, PAIR), F32),
            pltpu.VMEM((seq_len // MOBA_KV_TILE, 2, MOBA_V_ROWS, MOBA_KV_TILE), BF16),
            pltpu.VMEM((1, tq), F32), pltpu.VMEM((1, tq), F32),
            pltpu.VMEM((MOBA_V_ROWS, tq), F32), pltpu.VMEM((MOBA_V_ROWS, tq), F32),
            pltpu.VMEM((2, MOBA_KV_TILE, tq), F32), pltpu.VMEM((2, MOBA_KV_TILE, tq), F32),
            pltpu.VMEM((2, 1, tq), F32), pltpu.VMEM((2, 1, tq), F32),
        ],
        compiler_params=pltpu.CompilerParams(
            dimension_semantics=("arbitrary", "arbitrary", "arbitrary"),
            vmem_limit_bytes=VMEM_LIMIT),
        name="moba_attention",
    )(qkv, qkv, qkv, kaug)


def _moba_key_aug(seq_len):
    nb = seq_len // MOBA_BLOCK
    heads = MOBA_WIDTH // HEAD_DIM
    pos = jnp.arange(seq_len, dtype=jnp.int32)
    slopes = 2.0 ** (-8.0 * (jnp.arange(heads, dtype=F32) + 1.0) / heads)
    onehot = ((pos // MOBA_BLOCK)[:, None] == jnp.arange(LANES)[None, :]).astype(F32)
    bias = (LOG2_E * slopes)[:, None] * pos.astype(F32)[None, :]
    lane = jnp.arange(LANES)[None, None, :]
    aug = jnp.broadcast_to(onehot[None], (heads, seq_len, LANES))
    rem = bias
    for part in range(MOBA_ALIBI_PARTS):
        bits = lax.bitcast_convert_type(rem, jnp.uint32) & jnp.uint32(0xFFFF0000)
        piece = lax.bitcast_convert_type(bits, F32)
        aug = jnp.where(lane == nb + part, piece[:, :, None], aug)
        rem = rem - piece
    return aug.astype(BF16)


def _layer_norm(z, g, b):
    mu = jnp.mean(z, axis=-1, keepdims=True)
    zc = z - mu
    var = jnp.mean(zc * zc, axis=-1, keepdims=True)
    return zc * lax.rsqrt(var + LN_EPS) * g + b


def _outproj_kernel(ya_ref, yb_ref, x_ref, wa_ref, wb_ref, g_ref, b_ref, wr_ref, br_ref,
                    h_ref, hb_ref, lg_ref):
    mix = _dot(ya_ref[...], wa_ref[...]) + _dot(yb_ref[...], wb_ref[...])
    h = _layer_norm(DEEPNORM_ALPHA * x_ref[...] + mix, g_ref[...], b_ref[...])
    h_ref[...] = h
    hb_ref[...] = h.astype(BF16)
    lg_ref[...] = _dot(h, wr_ref[...], NN, precision=lax.Precision.HIGHEST) + br_ref[...]


def _outproj_call(y_a, y_b, x2, wa, wb, ln_g, ln_b, w_router, b_router):
    n = x2.shape[0]
    tm = OUTPROJ_TM
    row = lambda i: (i, 0)
    const = lambda i: (0, 0)
    return pl.pallas_call(
        _outproj_kernel,
        grid=(n // tm,),
        in_specs=[
            pl.BlockSpec((tm, RWKV_WIDTH), row),
            pl.BlockSpec((tm, MOBA_WIDTH), row),
            pl.BlockSpec((tm, D_MODEL), row),
            pl.BlockSpec((RWKV_WIDTH, D_MODEL), const),
            pl.BlockSpec((MOBA_WIDTH, D_MODEL), const),
            pl.BlockSpec((1, D_MODEL), const),
            pl.BlockSpec((1, D_MODEL), const),
            pl.BlockSpec((D_MODEL, ROUTER_PAD), const),
            pl.BlockSpec((1, ROUTER_PAD), const),
        ],
        out_specs=[
            pl.BlockSpec((tm, D_MODEL), row),
            pl.BlockSpec((tm, D_MODEL), row),
            pl.BlockSpec((tm, ROUTER_PAD), row),
        ],
        out_shape=[
            jax.ShapeDtypeStruct((n, D_MODEL), F32),
            jax.ShapeDtypeStruct((n, D_MODEL), BF16),
            jax.ShapeDtypeStruct((n, ROUTER_PAD), F32),
        ],
        compiler_params=pltpu.CompilerParams(
            dimension_semantics=("arbitrary",), vmem_limit_bytes=VMEM_LIMIT),
        name="outproj_ln_router",
    )(y_a, y_b, x2, wa, wb, ln_g, ln_b, w_router, b_router)


def _route(logits):
    lane = lax.broadcasted_iota(jnp.int32, logits.shape, 1)
    is_group = (lane >= GROUP_LANE0) & (lane < GROUP_LANE0 + N_GROUPS)
    gl = jnp.where(is_group, logits, F32_LOWEST)
    g_max = jnp.max(gl, axis=-1, keepdims=True)
    g_first = jnp.min(jnp.where(gl == g_max, lane, LANES), axis=-1, keepdims=True)
    g_exp = jnp.where(is_group, jnp.exp(gl - g_max), 0.0)
    p_g = 1.0 / jnp.sum(g_exp, axis=-1, keepdims=True)
    g_idx = g_first - GROUP_LANE0
    in_group = (lane >= g_idx * EXPERTS_PER_GROUP) & (lane < (g_idx + 1) * EXPERTS_PER_GROUP)
    el = jnp.where(in_group, logits, F32_LOWEST)
    e_max = jnp.max(el, axis=-1, keepdims=True)
    e_exp = jnp.where(in_group, jnp.exp(el - e_max), 0.0)
    e_prob = e_exp / jnp.sum(e_exp, axis=-1, keepdims=True)
    cand = jnp.where(in_group, e_prob, -1.0)
    v1 = jnp.max(cand, axis=-1, keepdims=True)
    i1 = jnp.min(jnp.where(cand == v1, lane, LANES), axis=-1, keepdims=True)
    pick1 = lane == i1
    cand2 = jnp.where(pick1, -1.0, cand)
    v2 = jnp.max(cand2, axis=-1, keepdims=True)
    i2 = jnp.min(jnp.where(cand2 == v2, lane, LANES), axis=-1, keepdims=True)
    pick2 = lane == i2
    denom = v1 + v2
    return jnp.where(pick1, v1 / denom * p_g, jnp.where(pick2, v2 / denom * p_g, 0.0))


def _moe_kernel(hb_ref, h_ref, lg_ref, w1_ref, w3_ref, w2_ref, g_ref, b_ref, o_ref, gates_ref, acc_ref):
    step = pl.program_id(1)

    @pl.when(step == 0)
    def _():
        gates_ref[...] = _route(lg_ref[...])
        acc_ref[...] = jnp.zeros_like(acc_ref)

    tok = hb_ref[...]
    gates = gates_ref[...]
    lane = lax.broadcasted_iota(jnp.int32, gates.shape, 1)
    acc = acc_ref[...]
    for e in range(MOE_EXPERTS_PER_STEP):
        gate_e = jnp.sum(jnp.where(lane == step * MOE_EXPERTS_PER_STEP + e, gates, 0.0),
                         axis=-1, keepdims=True)
        a1 = _dot(tok, w1_ref[e])
        a3 = _dot(tok, w3_ref[e])
        hid = (a1 * _sigmoid(a1)) * a3 * gate_e
        acc = acc + _dot(hid.astype(BF16), w2_ref[e])
    acc_ref[...] = acc

    @pl.when(step == N_EXPERTS // MOE_EXPERTS_PER_STEP - 1)
    def _():
        o_ref[...] = _layer_norm(DEEPNORM_ALPHA * h_ref[...] + acc_ref[...], g_ref[...], b_ref[...])


def _moe_call(h_bf16, h_f32, logits, w1, w3, w2, ln_g, ln_b):
    n = h_f32.shape[0]
    tm = MOE_TM
    row = lambda t, g: (t, 0)
    const = lambda t, g: (0, 0)
    wmap = lambda t, g: (g, 0, 0)
    eps = MOE_EXPERTS_PER_STEP
    return pl.pallas_call(
        _moe_kernel,
        grid=(n // tm, N_EXPERTS // eps),
        in_specs=[
            pl.BlockSpec((tm, D_MODEL), row),
            pl.BlockSpec((tm, D_MODEL), row),
            pl.BlockSpec((tm, ROUTER_PAD), row),
            pl.BlockSpec((eps, D_MODEL, D_EXPERT), wmap),
            pl.BlockSpec((eps, D_MODEL, D_EXPERT), wmap),
            pl.BlockSpec((eps, D_EXPERT, D_MODEL), wmap),
            pl.BlockSpec((1, D_MODEL), const),
            pl.BlockSpec((1, D_MODEL), const),
        ],
        out_specs=pl.BlockSpec((tm, D_MODEL), row),
        out_shape=jax.ShapeDtypeStruct((n, D_MODEL), F32),
        scratch_shapes=[pltpu.VMEM((tm, ROUTER_PAD), F32), pltpu.VMEM((tm, D_MODEL), F32)],
        compiler_params=pltpu.CompilerParams(
            dimension_semantics=("arbitrary", "arbitrary"), vmem_limit_bytes=VMEM_LIMIT),
        name="hier_moe_ln",
    )(h_bf16, h_f32, logits, w1, w3, w2, ln_g, ln_b)


def _pad_cols(w, width):
    return jnp.pad(w, ((0, 0), (0, width - w.shape[1])))


def _pad_rows(w, height):
    return jnp.pad(w, ((0, height - w.shape[0]), (0, 0)))


def kernel(x, w_in, mu_shift, w0, w_lora_up, a0, a_lora_up, g_lora_up, k_k, k_a, r_k, gn_w, gn_b, w_out, ln1_g, ln1_b, w_group, b_group, w_expert, b_expert, w1_exp, w3_exp, w2_exp, ln2_g, ln2_b):
    batch, seq_len, d = x.shape
    assert d == D_MODEL
    n = batch * seq_len
    x2 = x.reshape(n, d)

    c_rkv = 3 * RWKV_WIDTH
    c_wd = c_rkv + DECAY_RANK
    c_ad = c_wd + AAA_RANK
    c_gd = c_ad + GATE_RANK
    w_cat = jnp.concatenate([
        w_in[:, :c_rkv],
        _pad_cols(w_in[:, c_rkv:c_wd], LORA_PAD),
        _pad_cols(w_in[:, c_wd:c_ad], LORA_PAD),
        _pad_cols(w_in[:, c_ad:c_gd], LORA_PAD),
        w_in[:, c_gd:],
    ], axis=1).astype(BF16)
    mu2 = mu_shift[None, :]
    mu_cat = jnp.concatenate([
        mu2[:, :c_rkv],
        _pad_cols(mu2[:, c_rkv:c_wd], LORA_PAD),
        _pad_cols(mu2[:, c_wd:c_ad], LORA_PAD),
        _pad_cols(mu2[:, c_ad:c_gd], LORA_PAD),
    ], axis=1)
    p_rkv, p_lora, p_moba = _inproj_call(x2, w_cat, mu_cat, seq_len)

    vecs = jnp.stack([w0, a0, k_k, k_a, r_k.reshape(-1), gn_w, gn_b, jnp.zeros_like(w0)], axis=0)
    head_id = jnp.arange(2 * PAIR) // HEAD_DIM
    bd = (head_id[:, None] == head_id[None, :]).astype(BF16)
    y_a = _rwkv_call(p_rkv, p_lora, vecs,
                     _pad_rows(w_lora_up, LORA_PAD), _pad_rows(a_lora_up, LORA_PAD),
                     _pad_rows(g_lora_up, LORA_PAD), bd, batch, seq_len)

    y_b = _moba_call(p_moba, _moba_key_aug(seq_len), batch, seq_len)

    w_out_b = w_out.astype(BF16)
    w_router = _pad_cols(jnp.concatenate([w_expert, w_group], axis=1), ROUTER_PAD)
    b_router = _pad_cols(jnp.concatenate([b_expert, b_group])[None, :], ROUTER_PAD)
    h1, h1_b, logits = _outproj_call(y_a, y_b, x2, w_out_b[:RWKV_WIDTH], w_out_b[RWKV_WIDTH:],
                                     ln1_g[None, :], ln1_b[None, :], w_router, b_router)

    flat = lambda w: w.astype(BF16).reshape((N_EXPERTS,) + w.shape[2:])
    out = _moe_call(h1_b, h1, logits, flat(w1_exp), flat(w3_exp), flat(w2_exp),
                    ln2_g[None, :], ln2_b[None, :])
    return out.reshape(batch, seq_len, d)
```

```python
import functools
import math

import jax
import jax.numpy as jnp
import numpy as np
from jax import lax
from jax.experimental import pallas as pl
from jax.experimental.pallas import tpu as pltpu

F32 = jnp.float32
BF16 = jnp.bfloat16

D_MODEL = 1024
HEAD_DIM = 64
RWKV_WIDTH = 512
MOBA_WIDTH = 512
DECAY_RANK = 32
AAA_RANK = 32
GATE_RANK = 96
GN_EPS = 64e-5
L2_EPS = 1e-12
MOBA_BLOCK = 256
MOBA_TOPK = 3
N_GROUPS = 4
EXPERTS_PER_GROUP = 8
N_EXPERTS = N_GROUPS * EXPERTS_PER_GROUP
D_EXPERT = 256
LN_EPS = 1e-5
DEEPNORM_ALPHA = float(2.0 ** 0.25)
NEG_INF = -1e30
F32_LOWEST = -3.0e38

LANES = 128
PAIR = 2 * HEAD_DIM
N_PAIRS = RWKV_WIDTH // PAIR
LORA_PAD = LANES
RWKV_COLS_PAD = 3 * RWKV_WIDTH + 3 * LORA_PAD
IN_COLS_PAD = RWKV_COLS_PAD + 3 * MOBA_WIDTH
VMEM_LIMIT = 56 * 1024 * 1024

INPROJ_TM = 512
INPROJ_TN = 384
RWKV_CHUNK = 64
RWKV_CHUNKS_PER_STEP = 4
RWKV_PASSES = 1
RWKV_STATE_PASSES = 1
OUTPROJ_TM = 512
MOE_TM = 1024
MOE_EXPERTS_PER_STEP = 4
MOE_SUB = 256
MOBA_KV_TILE = 512
MOBA_Q_TILE = 512
MOBA_V_ROWS = HEAD_DIM + 16
MOBA_ALIBI_PARTS = 3
LOG2_E = 1.4426950408889634
ROUTER_PAD = LANES
GROUP_LANE0 = N_EXPERTS

NN = (((1,), (0,)), ((), ()))
NT = (((1,), (1,)), ((), ()))


def _dot(a, b, dims=NN, precision=None):
    return lax.dot_general(a, b, dims, precision=precision, preferred_element_type=F32)


def _split_bf16(x, parts):
    out = []
    rem = x
    for i in range(parts):
        p = rem.astype(BF16)
        out.append(p)
        if i + 1 < parts:
            rem = rem - p.astype(F32)
    return out


def _mm(a, b, dims=NN, passes=3):
    if passes == 1:
        return _dot(a.astype(BF16), b.astype(BF16), dims)
    if passes == 6:
        return _dot(a, b, dims, precision=lax.Precision.HIGHEST)
    a_hi, a_lo = _split_bf16(a, 2)
    b_hi, b_lo = _split_bf16(b, 2)
    return _dot(a_hi, b_hi, dims) + (_dot(a_hi, b_lo, dims) + _dot(a_lo, b_hi, dims))


def _mm_exact_lhs(a_bf16, b, dims=NN):
    b1, b2, b3 = _split_bf16(b, 3)
    return _dot(a_bf16, b1, dims) + (_dot(a_bf16, b2, dims) + _dot(a_bf16, b3, dims))


def _mm_exact_rhs(a, b_bf16, dims=NN):
    a1, a2, a3 = _split_bf16(a, 3)
    return _dot(a1, b_bf16, dims) + (_dot(a2, b_bf16, dims) + _dot(a3, b_bf16, dims))


def _inproj_kernel(x_ref, w_ref, mu_ref, prkv_ref, plora_ref, pm_ref, carry_ref, *, tiles_per_seq):
    tm = x_ref.shape[0]
    xb = x_ref[...].astype(BF16)
    seq_start = (pl.program_id(0) % tiles_per_seq) == 0
    row0 = lax.broadcasted_iota(jnp.int32, (tm, INPROJ_TN), 0) == 0
    n_shift_tiles = RWKV_COLS_PAD // INPROJ_TN
    for j in range(n_shift_tiles):
        c0 = j * INPROJ_TN
        acc = _dot(xb, w_ref[:, c0:c0 + INPROJ_TN])
        prev_last = jnp.where(seq_start, 0.0, carry_ref[0:1, c0:c0 + INPROJ_TN])
        shifted = jnp.where(row0, prev_last, pltpu.roll(acc, 1, 0))
        carry_ref[0:1, c0:c0 + INPROJ_TN] = acc[tm - 1:tm, :]
        out = acc + (shifted - acc) * mu_ref[:, c0:c0 + INPROJ_TN]
        if c0 < 3 * RWKV_WIDTH:
            prkv_ref[:, c0:c0 + INPROJ_TN] = out
        else:
            plora_ref[:, c0 - 3 * RWKV_WIDTH:c0 - 3 * RWKV_WIDTH + INPROJ_TN] = out
    for j in range(3 * MOBA_WIDTH // INPROJ_TN):
        c0 = j * INPROJ_TN
        acc = _dot(xb, w_ref[:, RWKV_COLS_PAD + c0:RWKV_COLS_PAD + c0 + INPROJ_TN])
        pm_ref[:, c0:c0 + INPROJ_TN] = acc.astype(BF16)


def _inproj_call(x2, w_cat, mu_cat, seq_len):
    n = x2.shape[0]
    tm = INPROJ_TM
    assert seq_len % tm == 0 and (3 * RWKV_WIDTH) % INPROJ_TN == 0
    return pl.pallas_call(
        functools.partial(_inproj_kernel, tiles_per_seq=seq_len // tm),
        grid=(n // tm,),
        in_specs=[
            pl.BlockSpec((tm, D_MODEL), lambda i: (i, 0)),
            pl.BlockSpec((D_MODEL, IN_COLS_PAD), lambda i: (0, 0)),
            pl.BlockSpec((1, RWKV_COLS_PAD), lambda i: (0, 0)),
        ],
        out_specs=[
            pl.BlockSpec((tm, 3 * RWKV_WIDTH), lambda i: (i, 0)),
            pl.BlockSpec((tm, 3 * LORA_PAD), lambda i: (i, 0)),
            pl.BlockSpec((tm, 3 * MOBA_WIDTH), lambda i: (i, 0)),
        ],
        out_shape=[
            jax.ShapeDtypeStruct((n, 3 * RWKV_WIDTH), F32),
            jax.ShapeDtypeStruct((n, 3 * LORA_PAD), F32),
            jax.ShapeDtypeStruct((n, 3 * MOBA_WIDTH), BF16),
        ],
        scratch_shapes=[pltpu.VMEM((8, RWKV_COLS_PAD), F32)],
        compiler_params=pltpu.CompilerParams(
            dimension_semantics=("arbitrary",), vmem_limit_bytes=VMEM_LIMIT),
        name="inproj_shift",
    )(x2, w_cat, mu_cat)


def _softplus(z):
    return jnp.maximum(z, 0.0) + jnp.log(1.0 + jnp.exp(-jnp.abs(z)))


def _sigmoid(z):
    return 1.0 / (1.0 + jnp.exp(-z))


def _rwkv_chunks(rt, kt, at, bt, v, d_incl, s_prev, passes, state_passes):
    c = RWKV_CHUNK
    n_chunks = rt.shape[0] // c
    n_pairs = len(s_prev)
    row = lax.broadcasted_iota(jnp.int32, (c, c), 0)
    col = lax.broadcasted_iota(jnp.int32, (c, c), 1)
    strict = row > col
    incl = row >= col
    eye_c = (row == col).astype(F32)
    lane = lax.broadcasted_iota(jnp.int32, (1, PAIR), 1)
    head0 = lane < HEAD_DIM
    head_mask = (head0, jnp.logical_not(head0))
    prow = lax.broadcasted_iota(jnp.int32, (PAIR, PAIR), 0)
    pcol = lax.broadcasted_iota(jnp.int32, (PAIR, PAIR), 1)
    same_head = (prow < HEAD_DIM) == (pcol < HEAD_DIM)
    eye_p = (prow == pcol).astype(F32)
    pick = lambda t0, t1: jnp.where(head0, t0, t1)
    rows = [slice(ci * c, (ci + 1) * c) for ci in range(n_chunks)]
    sl = [slice(p * PAIR, (p + 1) * PAIR) for p in range(n_pairs)]
    pairs = [(ci, p) for ci in range(n_chunks) for p in range(n_pairs)]
    heads = [(ci, p, h) for ci, p in pairs for h in (0, 1)]
    cut = lambda t, ci, p: t[rows[ci], sl[p]]
    bt_t = [bt[rows[ci]].T for ci in range(n_chunks)]
    kt_t = [kt[rows[ci]].T for ci in range(n_chunks)]

    x = {(ci, p, h): jnp.concatenate([jnp.where(head_mask[h], cut(at, ci, p), 0.0),
                                      jnp.where(head_mask[h], cut(rt, ci, p), 0.0)], axis=0)
         for ci, p, h in heads}
    z_b = {(ci, p, h): _mm(x[ci, p, h], bt_t[ci][sl[p]], NN, passes) for ci, p, h in heads}
    z_k = {(ci, p, h): _mm(x[ci, p, h], kt_t[ci][sl[p]], NN, passes) for ci, p, h in heads}
    l_ab = {k_: jnp.where(strict, z[:c], 0.0) for k_, z in z_b.items()}
    m_rb = {k_: jnp.where(incl, z[c:], 0.0) for k_, z in z_b.items()}
    l_ak = {k_: jnp.where(strict, z[:c], 0.0) for k_, z in z_k.items()}
    m_rk = {k_: jnp.where(incl, z[c:], 0.0) for k_, z in z_k.items()}
    pw = l_ab
    t_inv = {k_: eye_c + l for k_, l in l_ab.items()}
    for _ in range(int(math.log2(c)) - 1):
        pw = {k_: _mm(m, m, NN, passes) for k_, m in pw.items()}
        t_inv = {k_: t_inv[k_] + _mm(pw[k_], t_inv[k_], NN, passes) for k_ in heads}
    lv = {(ci, p, h): _mm(l_ak[ci, p, h], cut(v, ci, p), NN, passes) for ci, p, h in heads}
    wu = {(ci, p, h): _mm(t_inv[ci, p, h], jnp.concatenate([cut(at, ci, p), lv[ci, p, h]], axis=1),
                          NN, passes) for ci, p, h in heads}
    qy = {k_: _mm(m_rb[k_], wu[k_], NN, passes) for k_ in heads}
    mv = {(ci, p, h): _mm(m_rk[ci, p, h], cut(v, ci, p), NN, passes) for ci, p, h in heads}

    qeff, y1, phi, psi = {}, {}, {}, {}
    for ci, p in pairs:
        w = pick(wu[ci, p, 0][:, :PAIR], wu[ci, p, 1][:, :PAIR])
        u0 = pick(wu[ci, p, 0][:, PAIR:], wu[ci, p, 1][:, PAIR:])
        qeff[ci, p] = cut(rt, ci, p) + pick(qy[ci, p, 0][:, :PAIR], qy[ci, p, 1][:, :PAIR])
        y1[ci, p] = pick(qy[ci, p, 0][:, PAIR:] + mv[ci, p, 0], qy[ci, p, 1][:, PAIR:] + mv[ci, p, 1])
        d_p = d_incl[(ci + 1) * c - 1:(ci + 1) * c, sl[p]]
        phi[ci, p] = jnp.where(same_head, (eye_p + _mm(w.T, cut(bt, ci, p), NN, passes)) * d_p, 0.0)
        uv_t = jnp.concatenate([u0, cut(v, ci, p)], axis=0).T
        bk = jnp.concatenate([cut(bt, ci, p), cut(kt, ci, p)], axis=0)
        psi[ci, p] = jnp.where(same_head, _mm(uv_t, bk, NN, passes) * d_p, 0.0)

    state = list(s_prev)
    ys = [[None] * n_pairs for _ in range(n_chunks)]
    for ci in range(n_chunks):
        for p in range(n_pairs):
            ys[ci][p] = _mm(qeff[ci, p], state[p].T, NN, state_passes) + y1[ci, p]
            state[p] = _mm(state[p], phi[ci, p], NN, state_passes) + psi[ci, p]
    y = jnp.concatenate([jnp.concatenate(ys[ci], axis=1) for ci in range(n_chunks)], axis=0)
    return y, state


def _rwkv_kernel(prkv_ref, plora_ref, vec_ref, wl_ref, al_ref, gl_ref, bd_ref, y_ref, s_ref):
    rows = prkv_ref.shape[0]
    c = RWKV_CHUNK
    width = RWKV_WIDTH

    @pl.when(pl.program_id(1) == 0)
    def _():
        s_ref[...] = jnp.zeros_like(s_ref)

    r = prkv_ref[:, 0:width]
    k_raw = prkv_ref[:, width:2 * width]
    v = prkv_ref[:, 2 * width:3 * width]
    p_wd = plora_ref[:, 0:LORA_PAD]
    p_ad = plora_ref[:, LORA_PAD:2 * LORA_PAD]
    p_gd = plora_ref[:, 2 * LORA_PAD:3 * LORA_PAD]
    w0 = vec_ref[0:1, :]
    a0 = vec_ref[1:2, :]
    k_k = vec_ref[2:3, :]
    k_a = vec_ref[3:4, :]
    r_k = vec_ref[4:5, :]
    gn_w = vec_ref[5:6, :]
    gn_b = vec_ref[6:7, :]
    bd = bd_ref[...]

    def seg_sum(z):
        halves = []
        for c0 in range(0, width, bd.shape[0]):
            z_hi, z_lo = _split_bf16(z[:, c0:c0 + bd.shape[0]], 2)
            halves.append(_dot(z_hi, bd) + _dot(z_lo, bd))
        return jnp.concatenate(halves, axis=1)

    w_log = -_softplus(-(w0 + _mm(jnp.tanh(p_wd), wl_ref[...], NN, 3))) - 0.5
    log_w = -jnp.exp(w_log)
    a = _sigmoid(a0 + _mm(p_ad, al_ref[...], NN, RWKV_PASSES))
    g = _mm(_sigmoid(p_gd), gl_ref[...], NN, RWKV_PASSES)
    kk = k_raw * k_k
    kk = kk / jnp.maximum(jnp.sqrt(seg_sum(kk * kk)), L2_EPS)
    k = k_raw * (1.0 + (a - 1.0) * k_a)

    row = lax.broadcasted_iota(jnp.int32, (rows, rows), 0)
    col = lax.broadcasted_iota(jnp.int32, (rows, rows), 1)
    tri = ((row >= col) & (row // c == col // c)).astype(BF16)
    cum = _mm_exact_lhs(tri, log_w)
    d_incl = jnp.exp(cum)
    d_inv = jnp.exp(-cum)
    d_excl = jnp.exp(cum - log_w)
    rt = r * d_incl
    kt = k * d_inv
    at = -kk * d_excl
    bt = kk * a * d_inv

    y, s_next = _rwkv_chunks(rt, kt, at, bt, v, d_incl, [s_ref[p] for p in range(N_PAIRS)],
                             RWKV_PASSES, RWKV_STATE_PASSES)
    for p in range(N_PAIRS):
        s_ref[p] = s_next[p]

    inv_n = 1.0 / HEAD_DIM
    mu = seg_sum(y) * inv_n
    yc = y - mu
    var = seg_sum(yc * yc) * inv_n
    yn = yc * lax.rsqrt(var + GN_EPS) * gn_w + gn_b
    bonus = seg_sum(r * k * r_k) * v
    y_ref[...] = ((yn + bonus) * g).astype(y_ref.dtype)


def _rwkv_call(p_rkv, p_lora, vecs, wl, al, gl, bd, batch, seq_len):
    n = p_rkv.shape[0]
    rows = RWKV_CHUNK * RWKV_CHUNKS_PER_STEP
    assert seq_len % rows == 0
    steps = seq_len // rows
    row_map = lambda b, i: (b * steps + i, 0)
    const = lambda b, i: (0, 0)
    return pl.pallas_call(
        _rwkv_kernel,
        grid=(batch, steps),
        in_specs=[
            pl.BlockSpec((rows, 3 * RWKV_WIDTH), row_map),
            pl.BlockSpec((rows, 3 * LORA_PAD), row_map),
            pl.BlockSpec((8, RWKV_WIDTH), const),
            pl.BlockSpec((LORA_PAD, RWKV_WIDTH), const),
            pl.BlockSpec((LORA_PAD, RWKV_WIDTH), const),
            pl.BlockSpec((LORA_PAD, RWKV_WIDTH), const),
            pl.BlockSpec((2 * PAIR, 2 * PAIR), const),
        ],
        out_specs=pl.BlockSpec((rows, RWKV_WIDTH), row_map),
        out_shape=jax.ShapeDtypeStruct((n, RWKV_WIDTH), BF16),
        scratch_shapes=[pltpu.VMEM((N_PAIRS, PAIR, PAIR), F32)],
        compiler_params=pltpu.CompilerParams(
            dimension_semantics=("arbitrary", "arbitrary"), vmem_limit_bytes=VMEM_LIMIT),
        name="rwkv7_chunked",
    )(p_rkv, p_lora, vecs, wl, al, gl, bd)


def _moba_kernel(q_ref, k_ref, v_ref, kaug_ref, o_ref, kmean_ref, vt_ref,
                 m0_ref, m1_ref, acc0_ref, acc1_ref, s_even_ref, s_odd_ref,
                 smax_even_ref, smax_odd_ref, *, n_blocks):
    blk = MOBA_BLOCK
    tk = MOBA_KV_TILE
    tq = q_ref.shape[0]
    i = pl.program_id(2)
    nb_pad = kmean_ref.shape[0]
    m_refs, acc_refs = (m0_ref, m1_ref), (acc0_ref, acc1_ref)

    @pl.when(i == 0)
    def _():
        kmean_ref[...] = jnp.zeros_like(kmean_ref)

        def mean_body(n, carry):
            off = pl.multiple_of(n * blk, blk)
            kb = k_ref[pl.ds(off, blk), :].astype(F32)
            kmean_ref[pl.ds(n, 1), :] = jnp.sum(kb, axis=0, keepdims=True) * (1.0 / blk)
            return carry
        lax.fori_loop(0, n_blocks, mean_body, 0)

        ones = jnp.ones((MOBA_V_ROWS - HEAD_DIM, tk), BF16)

        def vt_body(j, carry):
            off = pl.multiple_of(j * tk, tk)
            v_t = v_ref[pl.ds(off, tk), :].astype(F32).T.astype(BF16)
            for h in (0, 1):
                vt_ref[j, h] = jnp.concatenate([v_t[h * HEAD_DIM:(h + 1) * HEAD_DIM], ones], axis=0)
            return carry
        lax.fori_loop(0, vt_ref.shape[0], vt_body, 0)

    q_t = q_ref[...].astype(F32).T
    chan = lax.broadcasted_iota(jnp.int32, (PAIR, tq), 0)
    blk_row = lax.broadcasted_iota(jnp.int32, (nb_pad, tq), 0)
    own_blk = (i * tq + lax.broadcasted_iota(jnp.int32, (nb_pad, tq), 1)) // blk
    past = blk_row < own_blk
    aug_row = lax.broadcasted_iota(jnp.int32, (LANES, tq), 0)
    ones_rows = (aug_row >= n_blocks) & (aug_row < n_blocks + MOBA_ALIBI_PARTS)
    kmean = kmean_ref[...]

    qa_t = []
    for h in (0, 1):
        qh_t = jnp.where((chan < HEAD_DIM) == (h == 0), q_t, 0.0)
        gate = _dot(kmean, qh_t, NN, precision=lax.Precision.HIGHEST)
        gate = jnp.where(past, gate, F32_LOWEST)
        sel = jnp.zeros(gate.shape, jnp.bool_)
        for _ in range(MOBA_TOPK):
            mx = jnp.max(gate, axis=0, keepdims=True)
            first = jnp.min(jnp.where(gate == mx, blk_row, nb_pad), axis=0, keepdims=True)
            pick = (blk_row == first) & (mx > F32_LOWEST)
            sel = sel | pick
            gate = jnp.where(pick, F32_LOWEST, gate)
        sel_bias = jnp.where(past & jnp.logical_not(sel), NEG_INF, 0.0)
        aug_t = jnp.concatenate([sel_bias, jnp.zeros((LANES - nb_pad, tq), F32)], axis=0)
        aug_t = jnp.where(ones_rows, 1.0, aug_t)
        qa_t.append(jnp.concatenate([qh_t * (LOG2_E / math.sqrt(HEAD_DIM)), aug_t], axis=0).astype(BF16))

    def tile_scores(j):
        off = pl.multiple_of(j * tk, tk)
        k_t = k_ref[pl.ds(off, tk), :]
        return [_dot(jnp.concatenate([k_t, kaug_ref[h, pl.ds(off, tk), :]], axis=1), qa_t[h])
                for h in (0, 1)]

    def tile_update(j, buf):
        s_buf, smax_buf = buf
        for h in (0, 1):
            m_old = m_refs[h][...]
            m_new = jnp.maximum(m_old, smax_buf[h])
            p = jnp.exp2(s_buf[h] - m_new).astype(BF16)
            pv = _dot(vt_ref[j, h], p)
            acc_refs[h][...] = jnp.exp2(m_old - m_new) * acc_refs[h][...] + pv
            m_refs[h][...] = m_new

    for h in (0, 1):
        m_refs[h][...] = jnp.full(m_refs[h].shape, F32_LOWEST, F32)
        acc_refs[h][...] = jnp.zeros(acc_refs[h].shape, F32)

    even = (s_even_ref, smax_even_ref)
    odd = (s_odd_ref, smax_odd_ref)

    def put_scores(buf, s):
        for h in (0, 1):
            buf[0][h] = s[h]
            buf[1][h] = jnp.max(s[h], axis=0, keepdims=True)

    j_own = (i * tq) // tk
    key_pos = j_own * tk + lax.broadcasted_iota(jnp.int32, (tk, tq), 0)
    query_pos = i * tq + lax.broadcasted_iota(jnp.int32, (tk, tq), 1)
    causal = key_pos <= query_pos
    put_scores(even, [jnp.where(causal, s_h, NEG_INF) for s_h in tile_scores(j_own)])

    def previous_tile(j):
        return jnp.where(j == 0, j_own, j - 1)

    def pipelined_step(j, src, dst):
        put_scores(dst, tile_scores(j))
        tile_update(previous_tile(j), src)

    def kv_pair_step(u, carry):
        pipelined_step(2 * u, even, odd)
        pipelined_step(2 * u + 1, odd, even)
        return carry
    lax.fori_loop(0, j_own // 2, kv_pair_step, 0)

    @pl.when(j_own % 2 == 1)
    def _():
        pipelined_step(j_own - 1, even, odd)
        tile_update(j_own - 1, odd)

    @pl.when(j_own % 2 == 0)
    def _():
        tile_update(previous_tile(j_own), even)

    out_t = jnp.concatenate([acc_refs[h][0:HEAD_DIM, :] / acc_refs[h][HEAD_DIM:HEAD_DIM + 1, :]
                             for h in (0, 1)], axis=0)
    o_ref[...] = out_t.T.astype(o_ref.dtype)


def _moba_call(qkv, kaug, batch, seq_len):
    n = qkv.shape[0]
    blk = MOBA_BLOCK
    tq = MOBA_Q_TILE
    nb = seq_len // blk
    nq = seq_len // tq
    assert nb + MOBA_ALIBI_PARTS <= LANES and seq_len % MOBA_KV_TILE == 0
    assert MOBA_KV_TILE % tq == 0 and tq % blk == 0
    lane_groups = MOBA_WIDTH // LANES
    return pl.pallas_call(
        functools.partial(_moba_kernel, n_blocks=nb),
        grid=(batch, N_PAIRS, nq),
        in_specs=[
            pl.BlockSpec((tq, PAIR), lambda b, p, i: (b * nq + i, p)),
            pl.BlockSpec((seq_len, PAIR), lambda b, p, i: (b, lane_groups + p)),
            pl.BlockSpec((seq_len, PAIR), lambda b, p, i: (b, 2 * lane_groups + p)),
            pl.BlockSpec((2, seq_len, LANES), lambda b, p, i: (p, 0, 0)),
        ],
        out_specs=pl.BlockSpec((tq, PAIR), lambda b, p, i: (b * nq + i, p)),
        out_shape=jax.ShapeDtypeStruct((n, MOBA_WIDTH), BF16),
        scratch_shapes=[
            pltpu.VMEM((-(-nb // 8) * 8, PAIR), F32),
            pltpu.VMEM((seq_len // MOBA_KV_TILE, 2, MOBA_V_ROWS, MOBA_KV_TILE), BF16),
            pltpu.VMEM((1, tq), F32), pltpu.VMEM((1, tq), F32),
            pltpu.VMEM((MOBA_V_ROWS, tq), F32), pltpu.VMEM((MOBA_V_ROWS, tq), F32),
            pltpu.VMEM((2, MOBA_KV_TILE, tq), F32), pltpu.VMEM((2, MOBA_KV_TILE, tq), F32),
            pltpu.VMEM((2, 1, tq), F32), pltpu.VMEM((2, 1, tq), F32),
        ],
        compiler_params=pltpu.CompilerParams(
            dimension_semantics=("arbitrary", "arbitrary", "arbitrary"),
            vmem_limit_bytes=VMEM_LIMIT),
        name="moba_attention",
    )(qkv, qkv, qkv, kaug)


def _moba_key_aug(seq_len):
    nb = seq_len // MOBA_BLOCK
    heads = MOBA_WIDTH // HEAD_DIM
    pos = np.arange(seq_len, dtype=np.int32)
    slopes = (2.0 ** (-8.0 * (np.arange(heads, dtype=np.float32) + 1.0) / heads)).astype(np.float32)
    aug = np.zeros((heads, seq_len, LANES), np.float32)
    aug[:, pos, pos // MOBA_BLOCK] = 1.0
    rem = (np.float32(LOG2_E) * slopes)[:, None] * pos.astype(np.float32)[None, :]
    for part in range(MOBA_ALIBI_PARTS):
        piece = (rem.view(np.uint32) & np.uint32(0xFFFF0000)).view(np.float32)
        aug[:, :, nb + part] = piece
        rem = rem - piece
    return jnp.asarray(aug.astype(BF16))


def _layer_norm(z, g, b):
    mu = jnp.mean(z, axis=-1, keepdims=True)
    zc = z - mu
    var = jnp.mean(zc * zc, axis=-1, keepdims=True)
    return zc * lax.rsqrt(var + LN_EPS) * g + b


def _outproj_kernel(ya_ref, yb_ref, x_ref, wa_ref, wb_ref, g_ref, b_ref, wr_ref, br_ref,
                    h_ref, hb_ref, lg_ref):
    mix = _dot(ya_ref[...], wa_ref[...]) + _dot(yb_ref[...], wb_ref[...])
    h = _layer_norm(DEEPNORM_ALPHA * x_ref[...] + mix, g_ref[...], b_ref[...])
    h_ref[...] = h
    hb_ref[...] = h.astype(BF16)
    lg_ref[...] = _mm(h, wr_ref[...], NN, 3) + br_ref[...]


def _outproj_call(y_a, y_b, x2, wa, wb, ln_g, ln_b, w_router, b_router):
    n = x2.shape[0]
    tm = OUTPROJ_TM
    row = lambda i: (i, 0)
    const = lambda i: (0, 0)
    return pl.pallas_call(
        _outproj_kernel,
        grid=(n // tm,),
        in_specs=[
            pl.BlockSpec((tm, RWKV_WIDTH), row),
            pl.BlockSpec((tm, MOBA_WIDTH), row),
            pl.BlockSpec((tm, D_MODEL), row),
            pl.BlockSpec((RWKV_WIDTH, D_MODEL), const),
            pl.BlockSpec((MOBA_WIDTH, D_MODEL), const),
            pl.BlockSpec((1, D_MODEL), const),
            pl.BlockSpec((1, D_MODEL), const),
            pl.BlockSpec((D_MODEL, ROUTER_PAD), const),
            pl.BlockSpec((1, ROUTER_PAD), const),
        ],
        out_specs=[
            pl.BlockSpec((tm, D_MODEL), row),
            pl.BlockSpec((tm, D_MODEL), row),
            pl.BlockSpec((tm, ROUTER_PAD), row),
        ],
        out_shape=[
            jax.ShapeDtypeStruct((n, D_MODEL), F32),
            jax.ShapeDtypeStruct((n, D_MODEL), BF16),
            jax.ShapeDtypeStruct((n, ROUTER_PAD), F32),
        ],
        compiler_params=pltpu.CompilerParams(
            dimension_semantics=("arbitrary",), vmem_limit_bytes=VMEM_LIMIT),
        name="outproj_ln_router",
    )(y_a, y_b, x2, wa, wb, ln_g, ln_b, w_router, b_router)


def _route(logits):
    lane = lax.broadcasted_iota(jnp.int32, logits.shape, 1)
    is_group = (lane >= GROUP_LANE0) & (lane < GROUP_LANE0 + N_GROUPS)
    gl = jnp.where(is_group, logits, F32_LOWEST)
    g_max = jnp.max(gl, axis=-1, keepdims=True)
    g_first = jnp.min(jnp.where(gl == g_max, lane, LANES), axis=-1, keepdims=True)
    g_exp = jnp.where(is_group, jnp.exp(gl - g_max), 0.0)
    p_g = 1.0 / jnp.sum(g_exp, axis=-1, keepdims=True)
    g_idx = g_first - GROUP_LANE0
    in_group = (lane >= g_idx * EXPERTS_PER_GROUP) & (lane < (g_idx + 1) * EXPERTS_PER_GROUP)
    el = jnp.where(in_group, logits, F32_LOWEST)
    e_max = jnp.max(el, axis=-1, keepdims=True)
    e_exp = jnp.where(in_group, jnp.exp(el - e_max), 0.0)
    e_prob = e_exp / jnp.sum(e_exp, axis=-1, keepdims=True)
    cand = jnp.where(in_group, e_prob, -1.0)
    v1 = jnp.max(cand, axis=-1, keepdims=True)
    i1 = jnp.min(jnp.where(cand == v1, lane, LANES), axis=-1, keepdims=True)
    pick1 = lane == i1
    cand2 = jnp.where(pick1, -1.0, cand)
    v2 = jnp.max(cand2, axis=-1, keepdims=True)
    i2 = jnp.min(jnp.where(cand2 == v2, lane, LANES), axis=-1, keepdims=True)
    pick2 = lane == i2
    denom = v1 + v2
    gates = jnp.where(pick1, v1 / denom * p_g, jnp.where(pick2, v2 / denom * p_g, 0.0))
    return gates, g_idx


def _route_sort_kernel(lg_ref, gates_ref, pos_ref, flags_ref):
    tm = lg_ref.shape[0]
    gates, g_idx = _route(lg_ref[...])
    lane = lax.broadcasted_iota(jnp.int32, (tm, LANES), 1)
    in_own = lane == g_idx
    onehot = jnp.where(in_own, 1.0, 0.0)
    row = lax.broadcasted_iota(jnp.int32, (tm, tm), 0)
    col = lax.broadcasted_iota(jnp.int32, (tm, tm), 1)
    earlier = jnp.where(row > col, 1.0, 0.0).astype(BF16)
    rank = _dot(earlier, onehot.astype(BF16))
    count = jnp.sum(onehot, axis=0, keepdims=True)
    r128 = lax.broadcasted_iota(jnp.int32, (LANES, LANES), 0)
    c128 = lax.broadcasted_iota(jnp.int32, (LANES, LANES), 1)
    below = jnp.where(r128 < c128, 1.0, 0.0)
    start = _dot(jnp.broadcast_to(count, (8, LANES)), below, NN,
                 precision=lax.Precision.HIGHEST)[0:1]
    pos = jnp.sum(jnp.where(in_own, rank + start, 0.0), axis=-1, keepdims=True)
    gates_ref[...] = gates
    pos_ref[...] = jnp.broadcast_to(pos, (tm, LANES))
    start_c = jnp.broadcast_to(start, (LANES, LANES)).T
    count_c = jnp.broadcast_to(count, (LANES, LANES)).T
    sub_lo = c128 * MOE_SUB
    hit = (count_c > 0.0) & (start_c < (sub_lo + MOE_SUB).astype(F32)) & (start_c + count_c > sub_lo.astype(F32))
    flags_ref[0] = jnp.where(hit, 1, 0)[0:8].astype(jnp.int32)


def _route_sort_call(logits):
    n = logits.shape[0]
    tm = MOE_TM
    row = lambda t: (t, 0)
    return pl.pallas_call(
        _route_sort_kernel,
        grid=(n // tm,),
        in_specs=[pl.BlockSpec((tm, ROUTER_PAD), row)],
        out_specs=[
            pl.BlockSpec((tm, ROUTER_PAD), row),
            pl.BlockSpec((tm, LANES), row),
            pl.BlockSpec((1, 8, LANES), lambda t: (t, 0, 0)),
        ],
        out_shape=[
            jax.ShapeDtypeStruct((n, ROUTER_PAD), F32),
            jax.ShapeDtypeStruct((n, LANES), F32),
            jax.ShapeDtypeStruct((n // tm, 8, LANES), jnp.int32),
        ],
        compiler_params=pltpu.CompilerParams(
            dimension_semantics=("arbitrary",), vmem_limit_bytes=VMEM_LIMIT),
        name="route_sort",
    )(logits)


def _moe_kernel(flags_ref, hb_ref, h_ref, gates_ref, pos_ref, w1_ref, w3_ref, w2_ref, g_ref, b_ref,
                o_ref, xs_ref, gs_ref, acc_ref, pt_ref):
    tile = pl.program_id(0)
    step = pl.program_id(1)
    tm = hb_ref.shape[0]
    sub = MOE_SUB
    n_sub = tm // sub
    eps = MOE_EXPERTS_PER_STEP

    @pl.when(step == 0)
    def _():
        pos_b = pos_ref[...]
        pos_row = pos_b.T[0:1, :]
        g_hi, g_lo = _split_bf16(gates_ref[...], 2)
        hb = hb_ref[...]
        for c0 in range(0, tm, sub):
            slot = (lax.broadcasted_iota(jnp.int32, (sub, tm), 0) + c0).astype(F32)
            p_c = jnp.where(slot == pos_row, 1.0, 0.0).astype(BF16)
            xs_ref[c0:c0 + sub, :] = _dot(p_c, hb).astype(BF16)
            gs_ref[c0:c0 + sub, :] = _dot(p_c, g_hi) + _dot(p_c, g_lo)
        for c0 in range(0, tm, LANES):
            slot = (lax.broadcasted_iota(jnp.int32, (tm, LANES), 1) + c0).astype(F32)
            pt_ref[:, c0:c0 + LANES] = jnp.where(pos_b == slot, 1.0, 0.0).astype(BF16)
        acc_ref[...] = jnp.zeros_like(acc_ref)

    group = step // (EXPERTS_PER_GROUP // eps)

    def visit(r):
        rows = slice(r * sub, (r + 1) * sub)
        x_r = xs_ref[rows, :]
        g_r = gs_ref[rows, :]
        lane = lax.broadcasted_iota(jnp.int32, g_r.shape, 1)
        acc = acc_ref[rows, :]
        for e in range(eps):
            gate_e = jnp.sum(jnp.where(lane == step * eps + e, g_r, 0.0), axis=-1, keepdims=True)
            a1 = _dot(x_r, w1_ref[e])
            a3 = _dot(x_r, w3_ref[e])
            hid = (a1 * _sigmoid(a1)) * a3 * gate_e
            acc = acc + _dot(hid.astype(BF16), w2_ref[e])
        acc_ref[rows, :] = acc

    for r in range(n_sub):
        pl.when(flags_ref[(tile * N_GROUPS + group) * n_sub + r] != 0)(functools.partial(visit, r))

    @pl.when(step == N_EXPERTS // eps - 1)
    def _():
        a_hi, a_lo = _split_bf16(acc_ref[...], 2)
        pt = pt_ref[...]
        ffn = _dot(pt, a_hi) + _dot(pt, a_lo)
        o_ref[...] = _layer_norm(DEEPNORM_ALPHA * h_ref[...] + ffn, g_ref[...], b_ref[...])


def _moe_call(flags, h_bf16, h_f32, gates, pos, w1, w3, w2, ln_g, ln_b):
    n = h_f32.shape[0]
    tm = MOE_TM
    row = lambda t, s, f: (t, 0)
    const = lambda t, s, f: (0, 0)
    wmap = lambda t, s, f: (s, 0, 0)
    eps = MOE_EXPERTS_PER_STEP
    assert EXPERTS_PER_GROUP % eps == 0 and tm % MOE_SUB == 0
    grid_spec = pltpu.PrefetchScalarGridSpec(
        num_scalar_prefetch=1,
        grid=(n // tm, N_EXPERTS // eps),
        in_specs=[
            pl.BlockSpec((tm, D_MODEL), row),
            pl.BlockSpec((tm, D_MODEL), row),
            pl.BlockSpec((tm, ROUTER_PAD), row),
            pl.BlockSpec((tm, LANES), row),
            pl.BlockSpec((eps, D_MODEL, D_EXPERT), wmap),
            pl.BlockSpec((eps, D_MODEL, D_EXPERT), wmap),
            pl.BlockSpec((eps, D_EXPERT, D_MODEL), wmap),
            pl.BlockSpec((1, D_MODEL), const),
            pl.BlockSpec((1, D_MODEL), const),
        ],
        out_specs=pl.BlockSpec((tm, D_MODEL), row),
        scratch_shapes=[
            pltpu.VMEM((tm, D_MODEL), BF16),
            pltpu.VMEM((tm, ROUTER_PAD), F32),
            pltpu.VMEM((tm, D_MODEL), F32),
            pltpu.VMEM((tm, tm), BF16),
        ],
    )
    return pl.pallas_call(
        _moe_kernel,
        grid_spec=grid_spec,
        out_shape=jax.ShapeDtypeStruct((n, D_MODEL), F32),
        compiler_params=pltpu.CompilerParams(
            dimension_semantics=("arbitrary", "arbitrary"), vmem_limit_bytes=VMEM_LIMIT),
        name="hier_moe_ln",
    )(flags, h_bf16, h_f32, gates, pos, w1, w3, w2, ln_g, ln_b)


def _pad_cols(w, width):
    return jnp.pad(w, ((0, 0), (0, width - w.shape[1])))


def _pad_rows(w, height):
    return jnp.pad(w, ((0, height - w.shape[0]), (0, 0)))


def kernel(x, w_in, mu_shift, w0, w_lora_up, a0, a_lora_up, g_lora_up, k_k, k_a, r_k, gn_w, gn_b, w_out, ln1_g, ln1_b, w_group, b_group, w_expert, b_expert, w1_exp, w3_exp, w2_exp, ln2_g, ln2_b):
    batch, seq_len, d = x.shape
    assert d == D_MODEL
    n = batch * seq_len
    x2 = x.reshape(n, d)

    c_rkv = 3 * RWKV_WIDTH
    c_wd = c_rkv + DECAY_RANK
    c_ad = c_wd + AAA_RANK
    c_gd = c_ad + GATE_RANK
    w_cat = jnp.concatenate([
        w_in[:, :c_rkv],
        _pad_cols(w_in[:, c_rkv:c_wd], LORA_PAD),
        _pad_cols(w_in[:, c_wd:c_ad], LORA_PAD),
        _pad_cols(w_in[:, c_ad:c_gd], LORA_PAD),
        w_in[:, c_gd:],
    ], axis=1).astype(BF16)
    mu2 = mu_shift[None, :]
    mu_cat = jnp.concatenate([
        mu2[:, :c_rkv],
        _pad_cols(mu2[:, c_rkv:c_wd], LORA_PAD),
        _pad_cols(mu2[:, c_wd:c_ad], LORA_PAD),
        _pad_cols(mu2[:, c_ad:c_gd], LORA_PAD),
    ], axis=1)
    p_rkv, p_lora, p_moba = _inproj_call(x2, w_cat, mu_cat, seq_len)

    vecs = jnp.stack([w0, a0, k_k, k_a, r_k.reshape(-1), gn_w, gn_b, jnp.zeros_like(w0)], axis=0)
    head_id = jnp.arange(2 * PAIR) // HEAD_DIM
    bd = (head_id[:, None] == head_id[None, :]).astype(BF16)
    y_a = _rwkv_call(p_rkv, p_lora, vecs,
                     _pad_rows(w_lora_up, LORA_PAD), _pad_rows(a_lora_up, LORA_PAD),
                     _pad_rows(g_lora_up, LORA_PAD), bd, batch, seq_len)

    y_b = _moba_call(p_moba, _moba_key_aug(seq_len), batch, seq_len)

    w_out_b = w_out.astype(BF16)
    w_router = _pad_cols(jnp.concatenate([w_expert, w_group], axis=1), ROUTER_PAD)
    b_router = _pad_cols(jnp.concatenate([b_expert, b_group])[None, :], ROUTER_PAD)
    h1, h1_b, logits = _outproj_call(y_a, y_b, x2, w_out_b[:RWKV_WIDTH], w_out_b[RWKV_WIDTH:],
                                     ln1_g[None, :], ln1_b[None, :], w_router, b_router)

    flat = lambda w: w.astype(BF16).reshape((N_EXPERTS,) + w.shape[2:])
    gates, pos, flags = _route_sort_call(logits)
    flags = flags[:, :N_GROUPS, :MOE_TM // MOE_SUB].reshape(-1)
    out = _moe_call(flags, h1_b, h1, gates, pos, flat(w1_exp), flat(w3_exp), flat(w2_exp),
                    ln2_g[None, :], ln2_b[None, :])
    return out.reshape(batch, seq_len, d)
```

```python
import functools
import math

import jax
import jax.numpy as jnp
import numpy as np
from jax import lax
from jax.experimental import pallas as pl
from jax.experimental.pallas import tpu as pltpu

F32 = jnp.float32
BF16 = jnp.bfloat16

D_MODEL = 1024
HEAD_DIM = 64
RWKV_WIDTH = 512
MOBA_WIDTH = 512
DECAY_RANK = 32
AAA_RANK = 32
GATE_RANK = 96
GN_EPS = 64e-5
L2_EPS = 1e-12
MOBA_BLOCK = 256
MOBA_TOPK = 3
N_GROUPS = 4
EXPERTS_PER_GROUP = 8
N_EXPERTS = N_GROUPS * EXPERTS_PER_GROUP
D_EXPERT = 256
LN_EPS = 1e-5
DEEPNORM_ALPHA = float(2.0 ** 0.25)
NEG_INF = -1e30
F32_LOWEST = -3.0e38

LANES = 128
PAIR = 2 * HEAD_DIM
N_PAIRS = RWKV_WIDTH // PAIR
LORA_PAD = 2 * LANES
RWKV_COLS_PAD = 3 * RWKV_WIDTH + LORA_PAD
IN_COLS_PAD = RWKV_COLS_PAD + 3 * MOBA_WIDTH
VMEM_LIMIT = 56 * 1024 * 1024

INPROJ_TM = 512
INPROJ_TN = 256
RWKV_CHUNK = 64
RWKV_CHUNKS_PER_STEP = 4
RWKV_PASSES = 1
RWKV_STATE_PASSES = 1
OUTPROJ_TM = 512
MOE_TM = 1024
MOE_EXPERTS_PER_STEP = 4
MOE_SUB = 256
MOBA_KV_TILE = 512
MOBA_Q_TILE = 512
MOBA_V_ROWS = HEAD_DIM + 16
MOBA_ALIBI_PARTS = 3
LOG2_E = 1.4426950408889634
ROUTER_PAD = LANES
GROUP_LANE0 = N_EXPERTS

NN = (((1,), (0,)), ((), ()))
NT = (((1,), (1,)), ((), ()))


def _dot(a, b, dims=NN, precision=None):
    return lax.dot_general(a, b, dims, precision=precision, preferred_element_type=F32)


def _split_bf16(x, parts):
    out = []
    rem = x
    for i in range(parts):
        p = rem.astype(BF16)
        out.append(p)
        if i + 1 < parts:
            rem = rem - p.astype(F32)
    return out


def _mm(a, b, dims=NN, passes=3):
    if passes == 1:
        return _dot(a.astype(BF16), b.astype(BF16), dims)
    if passes == 6:
        return _dot(a, b, dims, precision=lax.Precision.HIGHEST)
    a_hi, a_lo = _split_bf16(a, 2)
    b_hi, b_lo = _split_bf16(b, 2)
    return _dot(a_hi, b_hi, dims) + (_dot(a_hi, b_lo, dims) + _dot(a_lo, b_hi, dims))


def _mm_exact_lhs(a_bf16, b, dims=NN):
    b1, b2, b3 = _split_bf16(b, 3)
    return _dot(a_bf16, b1, dims) + (_dot(a_bf16, b2, dims) + _dot(a_bf16, b3, dims))


def _mm_exact_rhs(a, b_bf16, dims=NN):
    a1, a2, a3 = _split_bf16(a, 3)
    return _dot(a1, b_bf16, dims) + (_dot(a2, b_bf16, dims) + _dot(a3, b_bf16, dims))


def _inproj_kernel(x_ref, w_ref, mu_ref, prkv_ref, plora_ref, pm_ref, carry_ref, *, tiles_per_seq):
    tm = x_ref.shape[0]
    xb = x_ref[...].astype(BF16)
    seq_start = (pl.program_id(0) % tiles_per_seq) == 0
    row0 = lax.broadcasted_iota(jnp.int32, (tm, INPROJ_TN), 0) == 0
    n_shift_tiles = RWKV_COLS_PAD // INPROJ_TN
    for j in range(n_shift_tiles):
        c0 = j * INPROJ_TN
        acc = _dot(xb, w_ref[:, c0:c0 + INPROJ_TN])
        prev_last = jnp.where(seq_start, 0.0, carry_ref[0:1, c0:c0 + INPROJ_TN])
        shifted = jnp.where(row0, prev_last, pltpu.roll(acc, 1, 0))
        carry_ref[0:1, c0:c0 + INPROJ_TN] = acc[tm - 1:tm, :]
        out = acc + (shifted - acc) * mu_ref[:, c0:c0 + INPROJ_TN]
        if c0 < 3 * RWKV_WIDTH:
            prkv_ref[:, c0:c0 + INPROJ_TN] = out
        else:
            plora_ref[:, c0 - 3 * RWKV_WIDTH:c0 - 3 * RWKV_WIDTH + INPROJ_TN] = out
    for j in range(3 * MOBA_WIDTH // INPROJ_TN):
        c0 = j * INPROJ_TN
        acc = _dot(xb, w_ref[:, RWKV_COLS_PAD + c0:RWKV_COLS_PAD + c0 + INPROJ_TN])
        pm_ref[:, c0:c0 + INPROJ_TN] = acc.astype(BF16)


def _inproj_call(x2, w_cat, mu_cat, seq_len):
    n = x2.shape[0]
    tm = INPROJ_TM
    assert seq_len % tm == 0 and (3 * RWKV_WIDTH) % INPROJ_TN == 0
    return pl.pallas_call(
        functools.partial(_inproj_kernel, tiles_per_seq=seq_len // tm),
        grid=(n // tm,),
        in_specs=[
            pl.BlockSpec((tm, D_MODEL), lambda i: (i, 0)),
            pl.BlockSpec((D_MODEL, IN_COLS_PAD), lambda i: (0, 0)),
            pl.BlockSpec((1, RWKV_COLS_PAD), lambda i: (0, 0)),
        ],
        out_specs=[
            pl.BlockSpec((tm, 3 * RWKV_WIDTH), lambda i: (i, 0)),
            pl.BlockSpec((tm, LORA_PAD), lambda i: (i, 0)),
            pl.BlockSpec((tm, 3 * MOBA_WIDTH), lambda i: (i, 0)),
        ],
        out_shape=[
            jax.ShapeDtypeStruct((n, 3 * RWKV_WIDTH), F32),
            jax.ShapeDtypeStruct((n, LORA_PAD), F32),
            jax.ShapeDtypeStruct((n, 3 * MOBA_WIDTH), BF16),
        ],
        scratch_shapes=[pltpu.VMEM((8, RWKV_COLS_PAD), F32)],
        compiler_params=pltpu.CompilerParams(
            dimension_semantics=("arbitrary",), vmem_limit_bytes=VMEM_LIMIT),
        name="inproj_shift",
    )(x2, w_cat, mu_cat)


def _softplus(z):
    return jnp.maximum(z, 0.0) + jnp.log(1.0 + jnp.exp(-jnp.abs(z)))


def _sigmoid(z):
    return 1.0 / (1.0 + jnp.exp(-z))


def _rwkv_chunks(rt, kt, at, bt, v, d_incl, s_prev, passes, state_passes):
    c = RWKV_CHUNK
    n_chunks = rt.shape[0] // c
    n_pairs = len(s_prev)
    row = lax.broadcasted_iota(jnp.int32, (c, c), 0)
    col = lax.broadcasted_iota(jnp.int32, (c, c), 1)
    strict = row > col
    incl = row >= col
    eye_c = (row == col).astype(F32)
    lane = lax.broadcasted_iota(jnp.int32, (1, PAIR), 1)
    head0 = lane < HEAD_DIM
    head_mask = (head0, jnp.logical_not(head0))
    prow = lax.broadcasted_iota(jnp.int32, (PAIR, PAIR), 0)
    pcol = lax.broadcasted_iota(jnp.int32, (PAIR, PAIR), 1)
    same_head = (prow < HEAD_DIM) == (pcol < HEAD_DIM)
    eye_p = (prow == pcol).astype(F32)
    pick = lambda t0, t1: jnp.where(head0, t0, t1)
    rows = [slice(ci * c, (ci + 1) * c) for ci in range(n_chunks)]
    sl = [slice(p * PAIR, (p + 1) * PAIR) for p in range(n_pairs)]
    pairs = [(ci, p) for ci in range(n_chunks) for p in range(n_pairs)]
    heads = [(ci, p, h) for ci, p in pairs for h in (0, 1)]
    cut = lambda t, ci, p: t[rows[ci], sl[p]]
    bt_t = [bt[rows[ci]].T for ci in range(n_chunks)]
    kt_t = [kt[rows[ci]].T for ci in range(n_chunks)]

    x = {(ci, p, h): jnp.concatenate([jnp.where(head_mask[h], cut(at, ci, p), 0.0),
                                      jnp.where(head_mask[h], cut(rt, ci, p), 0.0)], axis=0)
         for ci, p, h in heads}
    z_b = {(ci, p, h): _mm(x[ci, p, h], bt_t[ci][sl[p]], NN, passes) for ci, p, h in heads}
    z_k = {(ci, p, h): _mm(x[ci, p, h], kt_t[ci][sl[p]], NN, passes) for ci, p, h in heads}
    l_ab = {k_: jnp.where(strict, z[:c], 0.0) for k_, z in z_b.items()}
    m_rb = {k_: jnp.where(incl, z[c:], 0.0) for k_, z in z_b.items()}
    l_ak = {k_: jnp.where(strict, z[:c], 0.0) for k_, z in z_k.items()}
    m_rk = {k_: jnp.where(incl, z[c:], 0.0) for k_, z in z_k.items()}
    pw = l_ab
    t_inv = {k_: eye_c + l for k_, l in l_ab.items()}
    for _ in range(int(math.log2(c)) - 1):
        pw = {k_: _mm(m, m, NN, passes) for k_, m in pw.items()}
        t_inv = {k_: t_inv[k_] + _mm(pw[k_], t_inv[k_], NN, passes) for k_ in heads}
    lv = {(ci, p, h): _mm(l_ak[ci, p, h], cut(v, ci, p), NN, passes) for ci, p, h in heads}
    wu = {(ci, p, h): _mm(t_inv[ci, p, h], jnp.concatenate([cut(at, ci, p), lv[ci, p, h]], axis=1),
                          NN, passes) for ci, p, h in heads}
    qy = {k_: _mm(m_rb[k_], wu[k_], NN, passes) for k_ in heads}
    mv = {(ci, p, h): _mm(m_rk[ci, p, h], cut(v, ci, p), NN, passes) for ci, p, h in heads}

    qeff, y1, phi, psi = {}, {}, {}, {}
    for ci, p in pairs:
        w = pick(wu[ci, p, 0][:, :PAIR], wu[ci, p, 1][:, :PAIR])
        u0 = pick(wu[ci, p, 0][:, PAIR:], wu[ci, p, 1][:, PAIR:])
        qeff[ci, p] = cut(rt, ci, p) + pick(qy[ci, p, 0][:, :PAIR], qy[ci, p, 1][:, :PAIR])
        y1[ci, p] = pick(qy[ci, p, 0][:, PAIR:] + mv[ci, p, 0], qy[ci, p, 1][:, PAIR:] + mv[ci, p, 1])
        d_p = d_incl[(ci + 1) * c - 1:(ci + 1) * c, sl[p]]
        phi[ci, p] = jnp.where(same_head, (eye_p + _mm(w.T, cut(bt, ci, p), NN, passes)) * d_p, 0.0)
        uv_t = jnp.concatenate([u0, cut(v, ci, p)], axis=0).T
        bk = jnp.concatenate([cut(bt, ci, p), cut(kt, ci, p)], axis=0)
        psi[ci, p] = jnp.where(same_head, _mm(uv_t, bk, NN, passes) * d_p, 0.0)

    state = list(s_prev)
    ys = [[None] * n_pairs for _ in range(n_chunks)]
    for ci in range(n_chunks):
        for p in range(n_pairs):
            ys[ci][p] = _mm(qeff[ci, p], state[p].T, NN, state_passes) + y1[ci, p]
            state[p] = _mm(state[p], phi[ci, p], NN, state_passes) + psi[ci, p]
    y = jnp.concatenate([jnp.concatenate(ys[ci], axis=1) for ci in range(n_chunks)], axis=0)
    return y, state


def _rwkv_kernel(prkv_ref, plora_ref, vec_ref, wl_ref, al_ref, gl_ref, bd_ref, y_ref, s_ref):
    rows = prkv_ref.shape[0]
    c = RWKV_CHUNK
    width = RWKV_WIDTH

    @pl.when(pl.program_id(1) == 0)
    def _():
        s_ref[...] = jnp.zeros_like(s_ref)

    r = prkv_ref[:, 0:width]
    k_raw = prkv_ref[:, width:2 * width]
    v = prkv_ref[:, 2 * width:3 * width]
    p_wd = p_ad = p_gd = plora_ref[...]
    w0 = vec_ref[0:1, :]
    a0 = vec_ref[1:2, :]
    k_k = vec_ref[2:3, :]
    k_a = vec_ref[3:4, :]
    r_k = vec_ref[4:5, :]
    gn_w = vec_ref[5:6, :]
    gn_b = vec_ref[6:7, :]
    bd = bd_ref[...]

    def seg_sum(z):
        halves = []
        for c0 in range(0, width, bd.shape[0]):
            z_hi, z_lo = _split_bf16(z[:, c0:c0 + bd.shape[0]], 2)
            halves.append(_dot(z_hi, bd) + _dot(z_lo, bd))
        return jnp.concatenate(halves, axis=1)

    w_log = -_softplus(-(w0 + _mm(jnp.tanh(p_wd), wl_ref[...], NN, 3))) - 0.5
    log_w = -jnp.exp(w_log)
    a = _sigmoid(a0 + _mm(p_ad, al_ref[...], NN, RWKV_PASSES))
    g = _mm(_sigmoid(p_gd), gl_ref[...], NN, RWKV_PASSES)
    kk = k_raw * k_k
    kk = kk / jnp.maximum(jnp.sqrt(seg_sum(kk * kk)), L2_EPS)
    k = k_raw * (1.0 + (a - 1.0) * k_a)

    row = lax.broadcasted_iota(jnp.int32, (rows, rows), 0)
    col = lax.broadcasted_iota(jnp.int32, (rows, rows), 1)
    tri = ((row >= col) & (row // c == col // c)).astype(BF16)
    cum = _mm_exact_lhs(tri, log_w)
    d_incl = jnp.exp(cum)
    d_inv = jnp.exp(-cum)
    d_excl = jnp.exp(cum - log_w)
    rt = r * d_incl
    kt = k * d_inv
    at = -kk * d_excl
    bt = kk * a * d_inv

    y, s_next = _rwkv_chunks(rt, kt, at, bt, v, d_incl, [s_ref[p] for p in range(N_PAIRS)],
                             RWKV_PASSES, RWKV_STATE_PASSES)
    for p in range(N_PAIRS):
        s_ref[p] = s_next[p]

    inv_n = 1.0 / HEAD_DIM
    mu = seg_sum(y) * inv_n
    yc = y - mu
    var = seg_sum(yc * yc) * inv_n
    yn = yc * lax.rsqrt(var + GN_EPS) * gn_w + gn_b
    bonus = seg_sum(r * k * r_k) * v
    y_ref[...] = ((yn + bonus) * g).astype(y_ref.dtype)


def _rwkv_call(p_rkv, p_lora, vecs, wl, al, gl, bd, batch, seq_len):
    n = p_rkv.shape[0]
    rows = RWKV_CHUNK * RWKV_CHUNKS_PER_STEP
    assert seq_len % rows == 0
    steps = seq_len // rows
    row_map = lambda b, i: (b * steps + i, 0)
    const = lambda b, i: (0, 0)
    return pl.pallas_call(
        _rwkv_kernel,
        grid=(batch, steps),
        in_specs=[
            pl.BlockSpec((rows, 3 * RWKV_WIDTH), row_map),
            pl.BlockSpec((rows, LORA_PAD), row_map),
            pl.BlockSpec((8, RWKV_WIDTH), const),
            pl.BlockSpec((LORA_PAD, RWKV_WIDTH), const),
            pl.BlockSpec((LORA_PAD, RWKV_WIDTH), const),
            pl.BlockSpec((LORA_PAD, RWKV_WIDTH), const),
            pl.BlockSpec((2 * PAIR, 2 * PAIR), const),
        ],
        out_specs=pl.BlockSpec((rows, RWKV_WIDTH), row_map),
        out_shape=jax.ShapeDtypeStruct((n, RWKV_WIDTH), BF16),
        scratch_shapes=[pltpu.VMEM((N_PAIRS, PAIR, PAIR), F32)],
        compiler_params=pltpu.CompilerParams(
            dimension_semantics=("arbitrary", "arbitrary"), vmem_limit_bytes=VMEM_LIMIT),
        name="rwkv7_chunked",
    )(p_rkv, p_lora, vecs, wl, al, gl, bd)


def _moba_kernel(q_ref, k_ref, v_ref, kaug_ref, o_ref, kmean_ref, vt_ref,
                 m0_ref, m1_ref, acc0_ref, acc1_ref, s_even_ref, s_odd_ref,
                 smax_even_ref, smax_odd_ref, *, n_blocks):
    blk = MOBA_BLOCK
    tk = MOBA_KV_TILE
    tq = q_ref.shape[0]
    i = pl.program_id(2)
    nb_pad = kmean_ref.shape[0]
    m_refs, acc_refs = (m0_ref, m1_ref), (acc0_ref, acc1_ref)

    @pl.when(i == 0)
    def _():
        kmean_ref[...] = jnp.zeros_like(kmean_ref)

        def mean_body(n, carry):
            off = pl.multiple_of(n * blk, blk)
            kb = k_ref[pl.ds(off, blk), :].astype(F32)
            kmean_ref[pl.ds(n, 1), :] = jnp.sum(kb, axis=0, keepdims=True) * (1.0 / blk)
            return carry
        lax.fori_loop(0, n_blocks, mean_body, 0)

        ones = jnp.ones((MOBA_V_ROWS - HEAD_DIM, tk), BF16)

        def vt_body(j, carry):
            off = pl.multiple_of(j * tk, tk)
            v_t = v_ref[pl.ds(off, tk), :].astype(F32).T.astype(BF16)
            for h in (0, 1):
                vt_ref[j, h] = jnp.concatenate([v_t[h * HEAD_DIM:(h + 1) * HEAD_DIM], ones], axis=0)
            return carry
        lax.fori_loop(0, vt_ref.shape[0], vt_body, 0)

    q_t = q_ref[...].astype(F32).T
    chan = lax.broadcasted_iota(jnp.int32, (PAIR, tq), 0)
    blk_row = lax.broadcasted_iota(jnp.int32, (nb_pad, tq), 0)
    own_blk = (i * tq + lax.broadcasted_iota(jnp.int32, (nb_pad, tq), 1)) // blk
    past = blk_row < own_blk
    aug_row = lax.broadcasted_iota(jnp.int32, (LANES, tq), 0)
    ones_rows = (aug_row >= n_blocks) & (aug_row < n_blocks + MOBA_ALIBI_PARTS)
    kmean = kmean_ref[...]

    qa_t = []
    for h in (0, 1):
        qh_t = jnp.where((chan < HEAD_DIM) == (h == 0), q_t, 0.0)
        gate = _dot(kmean, qh_t, NN, precision=lax.Precision.HIGHEST)
        gate = jnp.where(past, gate, F32_LOWEST)
        sel = jnp.zeros(gate.shape, jnp.bool_)
        for _ in range(MOBA_TOPK):
            mx = jnp.max(gate, axis=0, keepdims=True)
            first = jnp.min(jnp.where(gate == mx, blk_row, nb_pad), axis=0, keepdims=True)
            pick = (blk_row == first) & (mx > F32_LOWEST)
            sel = sel | pick
            gate = jnp.where(pick, F32_LOWEST, gate)
        sel_bias = jnp.where(past & jnp.logical_not(sel), NEG_INF, 0.0)
        aug_t = jnp.concatenate([sel_bias, jnp.zeros((LANES - nb_pad, tq), F32)], axis=0)
        aug_t = jnp.where(ones_rows, 1.0, aug_t)
        qa_t.append(jnp.concatenate([qh_t * (LOG2_E / math.sqrt(HEAD_DIM)), aug_t], axis=0).astype(BF16))

    def tile_scores(j):
        off = pl.multiple_of(j * tk, tk)
        k_t = k_ref[pl.ds(off, tk), :]
        return [_dot(jnp.concatenate([k_t, kaug_ref[h, pl.ds(off, tk), :]], axis=1), qa_t[h])
                for h in (0, 1)]

    def tile_update(j, buf):
        s_buf, smax_buf = buf
        for h in (0, 1):
            m_old = m_refs[h][...]
            m_new = jnp.maximum(m_old, smax_buf[h])
            p = jnp.exp2(s_buf[h] - m_new).astype(BF16)
            pv = _dot(vt_ref[j, h], p)
            acc_refs[h][...] = jnp.exp2(m_old - m_new) * acc_refs[h][...] + pv
            m_refs[h][...] = m_new

    for h in (0, 1):
        m_refs[h][...] = jnp.full(m_refs[h].shape, F32_LOWEST, F32)
        acc_refs[h][...] = jnp.zeros(acc_refs[h].shape, F32)

    even = (s_even_ref, smax_even_ref)
    odd = (s_odd_ref, smax_odd_ref)

    def put_scores(buf, s):
        for h in (0, 1):
            buf[0][h] = s[h]
            buf[1][h] = jnp.max(s[h], axis=0, keepdims=True)

    j_own = (i * tq) // tk
    key_pos = j_own * tk + lax.broadcasted_iota(jnp.int32, (tk, tq), 0)
    query_pos = i * tq + lax.broadcasted_iota(jnp.int32, (tk, tq), 1)
    causal = key_pos <= query_pos
    put_scores(even, [jnp.where(causal, s_h, NEG_INF) for s_h in tile_scores(j_own)])

    def previous_tile(j):
        return jnp.where(j == 0, j_own, j - 1)

    def pipelined_step(j, src, dst):
        put_scores(dst, tile_scores(j))
        tile_update(previous_tile(j), src)

    def kv_pair_step(u, carry):
        pipelined_step(2 * u, even, odd)
        pipelined_step(2 * u + 1, odd, even)
        return carry
    lax.fori_loop(0, j_own // 2, kv_pair_step, 0)

    @pl.when(j_own % 2 == 1)
    def _():
        pipelined_step(j_own - 1, even, odd)
        tile_update(j_own - 1, odd)

    @pl.when(j_own % 2 == 0)
    def _():
        tile_update(previous_tile(j_own), even)

    out_t = jnp.concatenate([acc_refs[h][0:HEAD_DIM, :] / acc_refs[h][HEAD_DIM:HEAD_DIM + 1, :]
                             for h in (0, 1)], axis=0)
    o_ref[...] = out_t.T.astype(o_ref.dtype)


def _moba_call(qkv, kaug, batch, seq_len):
    n = qkv.shape[0]
    blk = MOBA_BLOCK
    tq = MOBA_Q_TILE
    nb = seq_len // blk
    nq = seq_len // tq
    assert nb + MOBA_ALIBI_PARTS <= LANES and seq_len % MOBA_KV_TILE == 0
    assert MOBA_KV_TILE % tq == 0 and tq % blk == 0
    lane_groups = MOBA_WIDTH // LANES
    return pl.pallas_call(
        functools.partial(_moba_kernel, n_blocks=nb),
        grid=(batch, N_PAIRS, nq),
        in_specs=[
            pl.BlockSpec((tq, PAIR), lambda b, p, i: (b * nq + i, p)),
            pl.BlockSpec((seq_len, PAIR), lambda b, p, i: (b, lane_groups + p)),
            pl.BlockSpec((seq_len, PAIR), lambda b, p, i: (b, 2 * lane_groups + p)),
            pl.BlockSpec((2, seq_len, LANES), lambda b, p, i: (p, 0, 0)),
        ],
        out_specs=pl.BlockSpec((tq, PAIR), lambda b, p, i: (b * nq + i, p)),
        out_shape=jax.ShapeDtypeStruct((n, MOBA_WIDTH), BF16),
        scratch_shapes=[
            pltpu.VMEM((-(-nb // 8) * 8, PAIR), F32),
            pltpu.VMEM((seq_len // MOBA_KV_TILE, 2, MOBA_V_ROWS, MOBA_KV_TILE), BF16),
            pltpu.VMEM((1, tq), F32), pltpu.VMEM((1, tq), F32),
            pltpu.VMEM((MOBA_V_ROWS, tq), F32), pltpu.VMEM((MOBA_V_ROWS, tq), F32),
            pltpu.VMEM((2, MOBA_KV_TILE, tq), F32), pltpu.VMEM((2, MOBA_KV_TILE, tq), F32),
            pltpu.VMEM((2, 1, tq), F32), pltpu.VMEM((2, 1, tq), F32),
        ],
        compiler_params=pltpu.CompilerParams(
            dimension_semantics=("arbitrary", "arbitrary", "arbitrary"),
            vmem_limit_bytes=VMEM_LIMIT),
        name="moba_attention",
    )(qkv, qkv, qkv, kaug)


def _moba_key_aug(seq_len):
    nb = seq_len // MOBA_BLOCK
    heads = MOBA_WIDTH // HEAD_DIM
    pos = np.arange(seq_len, dtype=np.int32)
    slopes = (2.0 ** (-8.0 * (np.arange(heads, dtype=np.float32) + 1.0) / heads)).astype(np.float32)
    aug = np.zeros((heads, seq_len, LANES), np.float32)
    aug[:, pos, pos // MOBA_BLOCK] = 1.0
    rem = (np.float32(LOG2_E) * slopes)[:, None] * pos.astype(np.float32)[None, :]
    for part in range(MOBA_ALIBI_PARTS):
        piece = (rem.view(np.uint32) & np.uint32(0xFFFF0000)).view(np.float32)
        aug[:, :, nb + part] = piece
        rem = rem - piece
    return jnp.asarray(aug.astype(BF16))


def _layer_norm(z, g, b):
    mu = jnp.mean(z, axis=-1, keepdims=True)
    zc = z - mu
    var = jnp.mean(zc * zc, axis=-1, keepdims=True)
    return zc * lax.rsqrt(var + LN_EPS) * g + b


def _outproj_kernel(ya_ref, yb_ref, x_ref, wa_ref, wb_ref, g_ref, b_ref, wr_ref, br_ref,
                    h_ref, hb_ref, lg_ref):
    mix = _dot(ya_ref[...], wa_ref[...]) + _dot(yb_ref[...], wb_ref[...])
    h = _layer_norm(DEEPNORM_ALPHA * x_ref[...] + mix, g_ref[...], b_ref[...])
    h_ref[...] = h
    hb_ref[...] = h.astype(BF16)
    lg_ref[...] = _mm(h, wr_ref[...], NN, 3) + br_ref[...]


def _outproj_call(y_a, y_b, x2, wa, wb, ln_g, ln_b, w_router, b_router):
    n = x2.shape[0]
    tm = OUTPROJ_TM
    row = lambda i: (i, 0)
    const = lambda i: (0, 0)
    return pl.pallas_call(
        _outproj_kernel,
        grid=(n // tm,),
        in_specs=[
            pl.BlockSpec((tm, RWKV_WIDTH), row),
            pl.BlockSpec((tm, MOBA_WIDTH), row),
            pl.BlockSpec((tm, D_MODEL), row),
            pl.BlockSpec((RWKV_WIDTH, D_MODEL), const),
            pl.BlockSpec((MOBA_WIDTH, D_MODEL), const),
            pl.BlockSpec((1, D_MODEL), const),
            pl.BlockSpec((1, D_MODEL), const),
            pl.BlockSpec((D_MODEL, ROUTER_PAD), const),
            pl.BlockSpec((1, ROUTER_PAD), const),
        ],
        out_specs=[
            pl.BlockSpec((tm, D_MODEL), row),
            pl.BlockSpec((tm, D_MODEL), row),
            pl.BlockSpec((tm, ROUTER_PAD), row),
        ],
        out_shape=[
            jax.ShapeDtypeStruct((n, D_MODEL), F32),
            jax.ShapeDtypeStruct((n, D_MODEL), BF16),
            jax.ShapeDtypeStruct((n, ROUTER_PAD), F32),
        ],
        compiler_params=pltpu.CompilerParams(
            dimension_semantics=("arbitrary",), vmem_limit_bytes=VMEM_LIMIT),
        name="outproj_ln_router",
    )(y_a, y_b, x2, wa, wb, ln_g, ln_b, w_router, b_router)


def _route(logits):
    lane = lax.broadcasted_iota(jnp.int32, logits.shape, 1)
    is_group = (lane >= GROUP_LANE0) & (lane < GROUP_LANE0 + N_GROUPS)
    gl = jnp.where(is_group, logits, F32_LOWEST)
    g_max = jnp.max(gl, axis=-1, keepdims=True)
    g_first = jnp.min(jnp.where(gl == g_max, lane, LANES), axis=-1, keepdims=True)
    g_exp = jnp.where(is_group, jnp.exp(gl - g_max), 0.0)
    p_g = 1.0 / jnp.sum(g_exp, axis=-1, keepdims=True)
    g_idx = g_first - GROUP_LANE0
    in_group = (lane >= g_idx * EXPERTS_PER_GROUP) & (lane < (g_idx + 1) * EXPERTS_PER_GROUP)
    el = jnp.where(in_group, logits, F32_LOWEST)
    e_max = jnp.max(el, axis=-1, keepdims=True)
    e_exp = jnp.where(in_group, jnp.exp(el - e_max), 0.0)
    e_prob = e_exp / jnp.sum(e_exp, axis=-1, keepdims=True)
    cand = jnp.where(in_group, e_prob, -1.0)
    v1 = jnp.max(cand, axis=-1, keepdims=True)
    i1 = jnp.min(jnp.where(cand == v1, lane, LANES), axis=-1, keepdims=True)
    pick1 = lane == i1
    cand2 = jnp.where(pick1, -1.0, cand)
    v2 = jnp.max(cand2, axis=-1, keepdims=True)
    i2 = jnp.min(jnp.where(cand2 == v2, lane, LANES), axis=-1, keepdims=True)
    pick2 = lane == i2
    denom = v1 + v2
    gates = jnp.where(pick1, v1 / denom * p_g, jnp.where(pick2, v2 / denom * p_g, 0.0))
    return gates, g_idx


def _route_sort_kernel(lg_ref, gates_ref, pos_ref, flags_ref):
    tm = lg_ref.shape[0]
    gates, g_idx = _route(lg_ref[...])
    lane = lax.broadcasted_iota(jnp.int32, (tm, LANES), 1)
    in_own = lane == g_idx
    onehot = jnp.where(in_own, 1.0, 0.0)
    row = lax.broadcasted_iota(jnp.int32, (tm, tm), 0)
    col = lax.broadcasted_iota(jnp.int32, (tm, tm), 1)
    earlier = jnp.where(row > col, 1.0, 0.0).astype(BF16)
    rank = _dot(earlier, onehot.astype(BF16))
    count = jnp.sum(onehot, axis=0, keepdims=True)
    r128 = lax.broadcasted_iota(jnp.int32, (LANES, LANES), 0)
    c128 = lax.broadcasted_iota(jnp.int32, (LANES, LANES), 1)
    below = jnp.where(r128 < c128, 1.0, 0.0)
    start = _dot(jnp.broadcast_to(count, (8, LANES)), below, NN,
                 precision=lax.Precision.HIGHEST)[0:1]
    pos = jnp.sum(jnp.where(in_own, rank + start, 0.0), axis=-1, keepdims=True)
    gates_ref[...] = gates
    pos_ref[...] = jnp.broadcast_to(pos, (tm, LANES))
    start_c = jnp.broadcast_to(start, (LANES, LANES)).T
    count_c = jnp.broadcast_to(count, (LANES, LANES)).T
    sub_lo = c128 * MOE_SUB
    hit = (count_c > 0.0) & (start_c < (sub_lo + MOE_SUB).astype(F32)) & (start_c + count_c > sub_lo.astype(F32))
    flags_ref[0] = jnp.where(hit, 1, 0)[0:8].astype(jnp.int32)


def _route_sort_call(logits):
    n = logits.shape[0]
    tm = MOE_TM
    row = lambda t: (t, 0)
    return pl.pallas_call(
        _route_sort_kernel,
        grid=(n // tm,),
        in_specs=[pl.BlockSpec((tm, ROUTER_PAD), row)],
        out_specs=[
            pl.BlockSpec((tm, ROUTER_PAD), row),
            pl.BlockSpec((tm, LANES), row),
            pl.BlockSpec((1, 8, LANES), lambda t: (t, 0, 0)),
        ],
        out_shape=[
            jax.ShapeDtypeStruct((n, ROUTER_PAD), F32),
            jax.ShapeDtypeStruct((n, LANES), F32),
            jax.ShapeDtypeStruct((n // tm, 8, LANES), jnp.int32),
        ],
        compiler_params=pltpu.CompilerParams(
            dimension_semantics=("arbitrary",), vmem_limit_bytes=VMEM_LIMIT),
        name="route_sort",
    )(logits)


def _moe_kernel(flags_ref, hb_ref, h_ref, gates_ref, pos_ref, w1_ref, w3_ref, w2_ref, g_ref, b_ref,
                o_ref, xs_ref, gs_ref, acc_ref, pt_ref):
    tile = pl.program_id(0)
    step = pl.program_id(1)
    tm = hb_ref.shape[0]
    sub = MOE_SUB
    n_sub = tm // sub
    eps = MOE_EXPERTS_PER_STEP

    @pl.when(step == 0)
    def _():
        pos_b = pos_ref[...]
        pos_row = pos_b.T[0:1, :]
        g_hi, g_lo = _split_bf16(gates_ref[...], 2)
        src = jnp.concatenate([hb_ref[...], g_hi, g_lo], axis=1)
        for c0 in range(0, tm, sub):
            slot = (lax.broadcasted_iota(jnp.int32, (sub, tm), 0) + c0).astype(F32)
            p_c = jnp.where(slot == pos_row, 1.0, 0.0).astype(BF16)
            moved = _dot(p_c, src)
            xs_ref[c0:c0 + sub, :] = moved[:, :D_MODEL].astype(BF16)
            gs_ref[c0:c0 + sub, :] = moved[:, D_MODEL:D_MODEL + LANES] + moved[:, D_MODEL + LANES:]
        for c0 in range(0, tm, LANES):
            slot = (lax.broadcasted_iota(jnp.int32, (tm, LANES), 1) + c0).astype(F32)
            pt_ref[:, c0:c0 + LANES] = jnp.where(pos_b == slot, 1.0, 0.0).astype(BF16)
        acc_ref[...] = jnp.zeros_like(acc_ref)

    group = step // (EXPERTS_PER_GROUP // eps)

    def visit(r):
        rows = slice(r * sub, (r + 1) * sub)
        x_r = xs_ref[rows, :]
        g_r = gs_ref[rows, :]
        lane = lax.broadcasted_iota(jnp.int32, g_r.shape, 1)
        acc = acc_ref[rows, :]
        for e in range(eps):
            gate_e = jnp.sum(jnp.where(lane == step * eps + e, g_r, 0.0), axis=-1, keepdims=True)
            a1 = _dot(x_r, w1_ref[e])
            a3 = _dot(x_r, w3_ref[e])
            hid = (a1 * _sigmoid(a1)) * a3 * gate_e
            acc = acc + _dot(hid.astype(BF16), w2_ref[e])
        acc_ref[rows, :] = acc

    for r in range(n_sub):
        pl.when(flags_ref[(tile * N_GROUPS + group) * n_sub + r] != 0)(functools.partial(visit, r))

    @pl.when(step == N_EXPERTS // eps - 1)
    def _():
        ffn = _dot(pt_ref[...], acc_ref[...].astype(BF16))
        o_ref[...] = _layer_norm(DEEPNORM_ALPHA * h_ref[...] + ffn, g_ref[...], b_ref[...])


def _moe_call(flags, h_bf16, h_f32, gates, pos, w1, w3, w2, ln_g, ln_b):
    n = h_f32.shape[0]
    tm = MOE_TM
    row = lambda t, s, f: (t, 0)
    const = lambda t, s, f: (0, 0)
    wmap = lambda t, s, f: (s, 0, 0)
    eps = MOE_EXPERTS_PER_STEP
    assert EXPERTS_PER_GROUP % eps == 0 and tm % MOE_SUB == 0
    grid_spec = pltpu.PrefetchScalarGridSpec(
        num_scalar_prefetch=1,
        grid=(n // tm, N_EXPERTS // eps),
        in_specs=[
            pl.BlockSpec((tm, D_MODEL), row),
            pl.BlockSpec((tm, D_MODEL), row),
            pl.BlockSpec((tm, ROUTER_PAD), row),
            pl.BlockSpec((tm, LANES), row),
            pl.BlockSpec((eps, D_MODEL, D_EXPERT), wmap),
            pl.BlockSpec((eps, D_MODEL, D_EXPERT), wmap),
            pl.BlockSpec((eps, D_EXPERT, D_MODEL), wmap),
            pl.BlockSpec((1, D_MODEL), const),
            pl.BlockSpec((1, D_MODEL), const),
        ],
        out_specs=pl.BlockSpec((tm, D_MODEL), row),
        scratch_shapes=[
            pltpu.VMEM((tm, D_MODEL), BF16),
            pltpu.VMEM((tm, ROUTER_PAD), F32),
            pltpu.VMEM((tm, D_MODEL), F32),
            pltpu.VMEM((tm, tm), BF16),
        ],
    )
    return pl.pallas_call(
        _moe_kernel,
        grid_spec=grid_spec,
        out_shape=jax.ShapeDtypeStruct((n, D_MODEL), F32),
        compiler_params=pltpu.CompilerParams(
            dimension_semantics=("arbitrary", "arbitrary"), vmem_limit_bytes=VMEM_LIMIT),
        name="hier_moe_ln",
    )(flags, h_bf16, h_f32, gates, pos, w1, w3, w2, ln_g, ln_b)


def _pad_cols(w, width):
    return jnp.pad(w, ((0, 0), (0, width - w.shape[1])))


def _pad_rows(w, height):
    return jnp.pad(w, ((0, height - w.shape[0]), (0, 0)))


def kernel(x, w_in, mu_shift, w0, w_lora_up, a0, a_lora_up, g_lora_up, k_k, k_a, r_k, gn_w, gn_b, w_out, ln1_g, ln1_b, w_group, b_group, w_expert, b_expert, w1_exp, w3_exp, w2_exp, ln2_g, ln2_b):
    batch, seq_len, d = x.shape
    assert d == D_MODEL
    n = batch * seq_len
    x2 = x.reshape(n, d)

    c_rkv = 3 * RWKV_WIDTH
    c_wd = c_rkv + DECAY_RANK
    c_ad = c_wd + AAA_RANK
    c_gd = c_ad + GATE_RANK
    w_cat = jnp.concatenate([
        w_in[:, :c_rkv], _pad_cols(w_in[:, c_rkv:c_gd], LORA_PAD), w_in[:, c_gd:],
    ], axis=1).astype(BF16)
    mu2 = mu_shift[None, :]
    mu_cat = jnp.concatenate([mu2[:, :c_rkv], _pad_cols(mu2[:, c_rkv:c_gd], LORA_PAD)], axis=1)
    p_rkv, p_lora, p_moba = _inproj_call(x2, w_cat, mu_cat, seq_len)

    place = lambda w, first: jnp.pad(w, ((first, LORA_PAD - first - w.shape[0]), (0, 0)))
    lora_up = (place(w_lora_up, 0), place(a_lora_up, DECAY_RANK), place(g_lora_up, DECAY_RANK + AAA_RANK))

    vecs = jnp.stack([w0, a0, k_k, k_a, r_k.reshape(-1), gn_w, gn_b, jnp.zeros_like(w0)], axis=0)
    head_id = jnp.arange(2 * PAIR) // HEAD_DIM
    bd = (head_id[:, None] == head_id[None, :]).astype(BF16)
    y_a = _rwkv_call(p_rkv, p_lora, vecs, *lora_up, bd, batch, seq_len)

    y_b = _moba_call(p_moba, _moba_key_aug(seq_len), batch, seq_len)

    w_out_b = w_out.astype(BF16)
    w_router = _pad_cols(jnp.concatenate([w_expert, w_group], axis=1), ROUTER_PAD)
    b_router = _pad_cols(jnp.concatenate([b_expert, b_group])[None, :], ROUTER_PAD)
    h1, h1_b, logits = _outproj_call(y_a, y_b, x2, w_out_b[:RWKV_WIDTH], w_out_b[RWKV_WIDTH:],
                                     ln1_g[None, :], ln1_b[None, :], w_router, b_router)

    flat = lambda w: w.astype(BF16).reshape((N_EXPERTS,) + w.shape[2:])
    gates, pos, flags = _route_sort_call(logits)
    flags = flags[:, :N_GROUPS, :MOE_TM // MOE_SUB].reshape(-1)
    out = _moe_call(flags, h1_b, h1, gates, pos, flat(w1_exp), flat(w3_exp), flat(w2_exp),
                    ln2_g[None, :], ln2_b[None, :])
    return out.reshape(batch, seq_len, d)
```

```python
import functools
import math

import jax
import jax.numpy as jnp
import numpy as np
from jax import lax
from jax.experimental import pallas as pl
from jax.experimental.pallas import tpu as pltpu

F32 = jnp.float32
BF16 = jnp.bfloat16

D_MODEL = 1024
HEAD_DIM = 64
RWKV_WIDTH = 512
MOBA_WIDTH = 512
DECAY_RANK = 32
AAA_RANK = 32
GATE_RANK = 96
GN_EPS = 64e-5
L2_EPS = 1e-12
MOBA_BLOCK = 256
MOBA_TOPK = 3
N_GROUPS = 4
EXPERTS_PER_GROUP = 8
N_EXPERTS = N_GROUPS * EXPERTS_PER_GROUP
D_EXPERT = 256
LN_EPS = 1e-5
DEEPNORM_ALPHA = float(2.0 ** 0.25)
NEG_INF = -1e30
F32_LOWEST = -3.0e38

LANES = 128
PAIR = 2 * HEAD_DIM
N_PAIRS = RWKV_WIDTH // PAIR
LORA_PAD = 2 * LANES
RWKV_COLS_PAD = 3 * RWKV_WIDTH + LORA_PAD
IN_COLS_PAD = RWKV_COLS_PAD + 3 * MOBA_WIDTH
VMEM_LIMIT = 56 * 1024 * 1024

INPROJ_TM = 512
INPROJ_TN = 256
RWKV_CHUNK = 64
RWKV_CHUNKS_PER_STEP = 4
RWKV_PASSES = 1
RWKV_STATE_PASSES = 1
OUTPROJ_TM = 512
MOE_TM = 1024
MOE_EXPERTS_PER_STEP = 4
MOE_SUB = 256
MOBA_KV_TILE = 512
MOBA_Q_TILE = 512
MOBA_V_ROWS = HEAD_DIM + 16
MOBA_ALIBI_PARTS = 3
LOG2_E = 1.4426950408889634
ROUTER_PAD = LANES
GROUP_LANE0 = N_EXPERTS

NN = (((1,), (0,)), ((), ()))
NT = (((1,), (1,)), ((), ()))


def _dot(a, b, dims=NN, precision=None):
    return lax.dot_general(a, b, dims, precision=precision, preferred_element_type=F32)


def _split_bf16(x, parts):
    out = []
    rem = x
    for i in range(parts):
        p = rem.astype(BF16)
        out.append(p)
        if i + 1 < parts:
            rem = rem - p.astype(F32)
    return out


def _mm(a, b, dims=NN, passes=3):
    if passes == 1:
        return _dot(a.astype(BF16), b.astype(BF16), dims)
    if passes == 6:
        return _dot(a, b, dims, precision=lax.Precision.HIGHEST)
    a_hi, a_lo = _split_bf16(a, 2)
    b_hi, b_lo = _split_bf16(b, 2)
    return _dot(a_hi, b_hi, dims) + (_dot(a_hi, b_lo, dims) + _dot(a_lo, b_hi, dims))


def _mm_exact_lhs(a_bf16, b, dims=NN):
    b1, b2, b3 = _split_bf16(b, 3)
    return _dot(a_bf16, b1, dims) + (_dot(a_bf16, b2, dims) + _dot(a_bf16, b3, dims))


def _mm_exact_rhs(a, b_bf16, dims=NN):
    a1, a2, a3 = _split_bf16(a, 3)
    return _dot(a1, b_bf16, dims) + (_dot(a2, b_bf16, dims) + _dot(a3, b_bf16, dims))


def _inproj_kernel(x_ref, w_ref, mu_ref, prkv_ref, plora_ref, pm_ref, carry_ref, *, tiles_per_seq):
    tm = x_ref.shape[0]
    xb = x_ref[...].astype(BF16)
    seq_start = (pl.program_id(0) % tiles_per_seq) == 0
    row0 = lax.broadcasted_iota(jnp.int32, (tm, INPROJ_TN), 0) == 0
    n_shift_tiles = RWKV_COLS_PAD // INPROJ_TN
    for j in range(n_shift_tiles):
        c0 = j * INPROJ_TN
        acc = _dot(xb, w_ref[:, c0:c0 + INPROJ_TN])
        prev_last = jnp.where(seq_start, 0.0, carry_ref[0:1, c0:c0 + INPROJ_TN])
        shifted = jnp.where(row0, prev_last, pltpu.roll(acc, 1, 0))
        carry_ref[0:1, c0:c0 + INPROJ_TN] = acc[tm - 1:tm, :]
        out = acc + (shifted - acc) * mu_ref[:, c0:c0 + INPROJ_TN]
        if c0 < 3 * RWKV_WIDTH:
            prkv_ref[:, c0:c0 + INPROJ_TN] = out
        else:
            plora_ref[:, c0 - 3 * RWKV_WIDTH:c0 - 3 * RWKV_WIDTH + INPROJ_TN] = out
    for j in range(3 * MOBA_WIDTH // INPROJ_TN):
        c0 = j * INPROJ_TN
        acc = _dot(xb, w_ref[:, RWKV_COLS_PAD + c0:RWKV_COLS_PAD + c0 + INPROJ_TN])
        pm_ref[:, c0:c0 + INPROJ_TN] = acc.astype(BF16)


def _inproj_call(x2, w_cat, mu_cat, seq_len):
    n = x2.shape[0]
    tm = INPROJ_TM
    assert seq_len % tm == 0 and (3 * RWKV_WIDTH) % INPROJ_TN == 0
    return pl.pallas_call(
        functools.partial(_inproj_kernel, tiles_per_seq=seq_len // tm),
        grid=(n // tm,),
        in_specs=[
            pl.BlockSpec((tm, D_MODEL), lambda i: (i, 0)),
            pl.BlockSpec((D_MODEL, IN_COLS_PAD), lambda i: (0, 0)),
            pl.BlockSpec((1, RWKV_COLS_PAD), lambda i: (0, 0)),
        ],
        out_specs=[
            pl.BlockSpec((tm, 3 * RWKV_WIDTH), lambda i: (i, 0)),
            pl.BlockSpec((tm, LORA_PAD), lambda i: (i, 0)),
            pl.BlockSpec((tm, 3 * MOBA_WIDTH), lambda i: (i, 0)),
        ],
        out_shape=[
            jax.ShapeDtypeStruct((n, 3 * RWKV_WIDTH), F32),
            jax.ShapeDtypeStruct((n, LORA_PAD), F32),
            jax.ShapeDtypeStruct((n, 3 * MOBA_WIDTH), BF16),
        ],
        scratch_shapes=[pltpu.VMEM((8, RWKV_COLS_PAD), F32)],
        compiler_params=pltpu.CompilerParams(
            dimension_semantics=("arbitrary",), vmem_limit_bytes=VMEM_LIMIT),
        name="inproj_shift",
    )(x2, w_cat, mu_cat)


def _softplus(z):
    return jnp.maximum(z, 0.0) + jnp.log(1.0 + jnp.exp(-jnp.abs(z)))


def _sigmoid(z):
    return 1.0 / (1.0 + jnp.exp(-z))


def _rwkv_chunks(rt, kt, at, bt, v, d_incl, s_prev, passes, state_passes):
    c = RWKV_CHUNK
    n_chunks = rt.shape[0] // c
    n_pairs = len(s_prev)
    row = lax.broadcasted_iota(jnp.int32, (c, PAIR), 0)
    col = lax.broadcasted_iota(jnp.int32, (c, PAIR), 1) % HEAD_DIM
    strict = row > col
    incl = row >= col
    eye_c = (row == col).astype(F32)
    lane = lax.broadcasted_iota(jnp.int32, (1, PAIR), 1)
    head0 = lane < HEAD_DIM
    head1 = jnp.logical_not(head0)
    prow = lax.broadcasted_iota(jnp.int32, (PAIR, PAIR), 0)
    pcol = lax.broadcasted_iota(jnp.int32, (PAIR, PAIR), 1)
    same_head = (prow < HEAD_DIM) == (pcol < HEAD_DIM)
    eye_p = (prow == pcol).astype(F32)
    rows = [slice(ci * c, (ci + 1) * c) for ci in range(n_chunks)]
    sl = [slice(p * PAIR, (p + 1) * PAIR) for p in range(n_pairs)]
    pairs = [(ci, p) for ci in range(n_chunks) for p in range(n_pairs)]
    cut = lambda t, ci, p: t[rows[ci], sl[p]]

    def by_head(m):
        return jnp.concatenate([jnp.where(head0, m, 0.0), jnp.where(head1, m, 0.0)], axis=0)

    def by_head2(m, n):
        return jnp.concatenate([by_head(m), by_head(n)], axis=1)

    at_p = {k_: cut(at, *k_) for k_ in pairs}
    rt_p = {k_: cut(rt, *k_) for k_ in pairs}
    bt_p = {k_: cut(bt, *k_) for k_ in pairs}
    kt_p = {k_: cut(kt, *k_) for k_ in pairs}
    v_p = {k_: cut(v, *k_) for k_ in pairs}

    z = {k_: _mm(jnp.concatenate([at_p[k_], rt_p[k_]], axis=0),
                 jnp.concatenate([by_head(bt_p[k_]).T, by_head(kt_p[k_]).T], axis=1), NN, passes)
         for k_ in pairs}
    l_ab = {k_: jnp.where(strict, z[k_][:c, :PAIR], 0.0) for k_ in pairs}
    l_ak = {k_: jnp.where(strict, z[k_][:c, PAIR:], 0.0) for k_ in pairs}
    m_rb = {k_: jnp.where(incl, z[k_][c:, :PAIR], 0.0) for k_ in pairs}
    m_rk = {k_: jnp.where(incl, z[k_][c:, PAIR:], 0.0) for k_ in pairs}
    pw = {k_: _mm(l_ab[k_], by_head(l_ab[k_]), NN, passes) for k_ in pairs}
    t_inv = {k_: eye_c + l_ab[k_] for k_ in pairs}
    for _ in range(int(math.log2(c)) - 1):
        tp = {k_: _mm(jnp.concatenate([t_inv[k_], pw[k_]], axis=0), by_head(pw[k_]), NN, passes)
              for k_ in pairs}
        t_inv = {k_: t_inv[k_] + tp[k_][:c] for k_ in pairs}
        pw = {k_: tp[k_][c:] for k_ in pairs}
    lm = {k_: _mm(jnp.concatenate([l_ak[k_], m_rk[k_]], axis=0), by_head(v_p[k_]), NN, passes)
          for k_ in pairs}
    lv = {k_: lm[k_][:c] for k_ in pairs}
    mv = {k_: lm[k_][c:] for k_ in pairs}
    wu = {k_: _mm(t_inv[k_], by_head2(at_p[k_], lv[k_]), NN, passes) for k_ in pairs}
    qy = {k_: _mm(m_rb[k_], by_head2(wu[k_][:, :PAIR], wu[k_][:, PAIR:]), NN, passes) for k_ in pairs}

    qeff, y1, phi, psi = {}, {}, {}, {}
    for ci, p in pairs:
        k_ = (ci, p)
        w, u0 = wu[k_][:, :PAIR], wu[k_][:, PAIR:]
        qeff[k_] = rt_p[k_] + qy[k_][:, :PAIR]
        y1[k_] = qy[k_][:, PAIR:] + mv[k_]
        d_p = d_incl[(ci + 1) * c - 1:(ci + 1) * c, sl[p]]
        phi[k_] = jnp.where(same_head, (eye_p + _mm(w.T, bt_p[k_], NN, passes)) * d_p, 0.0)
        uv_t = jnp.concatenate([u0, v_p[k_]], axis=0).T
        bk = jnp.concatenate([bt_p[k_], kt_p[k_]], axis=0)
        psi[k_] = jnp.where(same_head, _mm(uv_t, bk, NN, passes) * d_p, 0.0)

    state = list(s_prev)
    ys = [[None] * n_pairs for _ in range(n_chunks)]
    for ci in range(n_chunks):
        for p in range(n_pairs):
            ys[ci][p] = _mm(qeff[ci, p], state[p].T, NN, state_passes) + y1[ci, p]
            state[p] = _mm(state[p], phi[ci, p], NN, state_passes) + psi[ci, p]
    y = jnp.concatenate([jnp.concatenate(ys[ci], axis=1) for ci in range(n_chunks)], axis=0)
    return y, state


def _rwkv_kernel(prkv_ref, plora_ref, vec_ref, wl_ref, al_ref, gl_ref, bd_ref, y_ref, s_ref):
    rows = prkv_ref.shape[0]
    c = RWKV_CHUNK
    width = RWKV_WIDTH

    @pl.when(pl.program_id(1) == 0)
    def _():
        s_ref[...] = jnp.zeros_like(s_ref)

    r = prkv_ref[:, 0:width]
    k_raw = prkv_ref[:, width:2 * width]
    v = prkv_ref[:, 2 * width:3 * width]
    p_wd = p_ad = p_gd = plora_ref[...]
    w0 = vec_ref[0:1, :]
    a0 = vec_ref[1:2, :]
    k_k = vec_ref[2:3, :]
    k_a = vec_ref[3:4, :]
    r_k = vec_ref[4:5, :]
    gn_w = vec_ref[5:6, :]
    gn_b = vec_ref[6:7, :]
    bd = bd_ref[...]

    def seg_sum(z):
        halves = []
        for c0 in range(0, width, bd.shape[0]):
            z_hi, z_lo = _split_bf16(z[:, c0:c0 + bd.shape[0]], 2)
            halves.append(_dot(z_hi, bd) + _dot(z_lo, bd))
        return jnp.concatenate(halves, axis=1)

    w_log = -_softplus(-(w0 + _mm(jnp.tanh(p_wd), wl_ref[...], NN, 3))) - 0.5
    log_w = -jnp.exp(w_log)
    a = _sigmoid(a0 + _mm(p_ad, al_ref[...], NN, RWKV_PASSES))
    g = _mm(_sigmoid(p_gd), gl_ref[...], NN, RWKV_PASSES)
    kk = k_raw * k_k
    kk = kk / jnp.maximum(jnp.sqrt(seg_sum(kk * kk)), L2_EPS)
    k = k_raw * (1.0 + (a - 1.0) * k_a)

    row = lax.broadcasted_iota(jnp.int32, (rows, rows), 0)
    col = lax.broadcasted_iota(jnp.int32, (rows, rows), 1)
    tri = ((row >= col) & (row // c == col // c)).astype(BF16)
    cum = _mm_exact_lhs(tri, log_w)
    d_incl = jnp.exp(cum)
    d_inv = jnp.exp(-cum)
    d_excl = jnp.exp(cum - log_w)
    rt = r * d_incl
    kt = k * d_inv
    at = -kk * d_excl
    bt = kk * a * d_inv

    y, s_next = _rwkv_chunks(rt, kt, at, bt, v, d_incl, [s_ref[p] for p in range(N_PAIRS)],
                             RWKV_PASSES, RWKV_STATE_PASSES)
    for p in range(N_PAIRS):
        s_ref[p] = s_next[p]

    inv_n = 1.0 / HEAD_DIM
    mu = seg_sum(y) * inv_n
    yc = y - mu
    var = seg_sum(yc * yc) * inv_n
    yn = yc * lax.rsqrt(var + GN_EPS) * gn_w + gn_b
    bonus = seg_sum(r * k * r_k) * v
    y_ref[...] = ((yn + bonus) * g).astype(y_ref.dtype)


def _rwkv_call(p_rkv, p_lora, vecs, wl, al, gl, bd, batch, seq_len):
    n = p_rkv.shape[0]
    rows = RWKV_CHUNK * RWKV_CHUNKS_PER_STEP
    assert seq_len % rows == 0
    steps = seq_len // rows
    row_map = lambda b, i: (b * steps + i, 0)
    const = lambda b, i: (0, 0)
    return pl.pallas_call(
        _rwkv_kernel,
        grid=(batch, steps),
        in_specs=[
            pl.BlockSpec((rows, 3 * RWKV_WIDTH), row_map),
            pl.BlockSpec((rows, LORA_PAD), row_map),
            pl.BlockSpec((8, RWKV_WIDTH), const),
            pl.BlockSpec((LORA_PAD, RWKV_WIDTH), const),
            pl.BlockSpec((LORA_PAD, RWKV_WIDTH), const),
            pl.BlockSpec((LORA_PAD, RWKV_WIDTH), const),
            pl.BlockSpec((2 * PAIR, 2 * PAIR), const),
        ],
        out_specs=pl.BlockSpec((rows, RWKV_WIDTH), row_map),
        out_shape=jax.ShapeDtypeStruct((n, RWKV_WIDTH), BF16),
        scratch_shapes=[pltpu.VMEM((N_PAIRS, PAIR, PAIR), F32)],
        compiler_params=pltpu.CompilerParams(
            dimension_semantics=("arbitrary", "arbitrary"), vmem_limit_bytes=VMEM_LIMIT),
        name="rwkv7_chunked",
    )(p_rkv, p_lora, vecs, wl, al, gl, bd)


def _moba_kernel(q_ref, k_ref, v_ref, kaug_ref, o_ref, kmean_ref, vt_ref,
                 m0_ref, m1_ref, acc0_ref, acc1_ref, s_even_ref, s_odd_ref,
                 smax_even_ref, smax_odd_ref, *, n_blocks):
    blk = MOBA_BLOCK
    tk = MOBA_KV_TILE
    tq = q_ref.shape[0]
    i = pl.program_id(2)
    nb_pad = kmean_ref.shape[0]
    m_refs, acc_refs = (m0_ref, m1_ref), (acc0_ref, acc1_ref)

    @pl.when(i == 0)
    def _():
        kmean_ref[...] = jnp.zeros_like(kmean_ref)

        def mean_body(n, carry):
            off = pl.multiple_of(n * blk, blk)
            kb = k_ref[pl.ds(off, blk), :].astype(F32)
            kmean_ref[pl.ds(n, 1), :] = jnp.sum(kb, axis=0, keepdims=True) * (1.0 / blk)
            return carry
        lax.fori_loop(0, n_blocks, mean_body, 0)

        ones = jnp.ones((MOBA_V_ROWS - HEAD_DIM, tk), BF16)

        def vt_body(j, carry):
            off = pl.multiple_of(j * tk, tk)
            v_t = v_ref[pl.ds(off, tk), :].astype(F32).T.astype(BF16)
            for h in (0, 1):
                vt_ref[j, h] = jnp.concatenate([v_t[h * HEAD_DIM:(h + 1) * HEAD_DIM], ones], axis=0)
            return carry
        lax.fori_loop(0, vt_ref.shape[0], vt_body, 0)

    q_t = q_ref[...].astype(F32).T
    chan = lax.broadcasted_iota(jnp.int32, (PAIR, tq), 0)
    blk_row = lax.broadcasted_iota(jnp.int32, (nb_pad, tq), 0)
    own_blk = (i * tq + lax.broadcasted_iota(jnp.int32, (nb_pad, tq), 1)) // blk
    past = blk_row < own_blk
    aug_row = lax.broadcasted_iota(jnp.int32, (LANES, tq), 0)
    ones_rows = (aug_row >= n_blocks) & (aug_row < n_blocks + MOBA_ALIBI_PARTS)
    kmean = kmean_ref[...]

    qa_t = []
    for h in (0, 1):
        qh_t = jnp.where((chan < HEAD_DIM) == (h == 0), q_t, 0.0)
        gate = _dot(kmean, qh_t, NN, precision=lax.Precision.HIGHEST)
        gate = jnp.where(past, gate, F32_LOWEST)
        sel = jnp.zeros(gate.shape, jnp.bool_)
        for _ in range(MOBA_TOPK):
            mx = jnp.max(gate, axis=0, keepdims=True)
            first = jnp.min(jnp.where(gate == mx, blk_row, nb_pad), axis=0, keepdims=True)
            pick = (blk_row == first) & (mx > F32_LOWEST)
            sel = sel | pick
            gate = jnp.where(pick, F32_LOWEST, gate)
        sel_bias = jnp.where(past & jnp.logical_not(sel), NEG_INF, 0.0)
        aug_t = jnp.concatenate([sel_bias, jnp.zeros((LANES - nb_pad, tq), F32)], axis=0)
        aug_t = jnp.where(ones_rows, 1.0, aug_t)
        qa_t.append(jnp.concatenate([qh_t * (LOG2_E / math.sqrt(HEAD_DIM)), aug_t], axis=0).astype(BF16))

    def tile_scores(j):
        off = pl.multiple_of(j * tk, tk)
        k_t = k_ref[pl.ds(off, tk), :]
        return [_dot(jnp.concatenate([k_t, kaug_ref[h, pl.ds(off, tk), :]], axis=1), qa_t[h])
                for h in (0, 1)]

    def tile_update(j, buf):
        s_buf, smax_buf = buf
        for h in (0, 1):
            m_old = m_refs[h][...]
            m_new = jnp.maximum(m_old, smax_buf[h])
            p = jnp.exp2(s_buf[h] - m_new).astype(BF16)
            pv = _dot(vt_ref[j, h], p)
            acc_refs[h][...] = jnp.exp2(m_old - m_new) * acc_refs[h][...] + pv
            m_refs[h][...] = m_new

    for h in (0, 1):
        m_refs[h][...] = jnp.full(m_refs[h].shape, F32_LOWEST, F32)
        acc_refs[h][...] = jnp.zeros(acc_refs[h].shape, F32)

    even = (s_even_ref, smax_even_ref)
    odd = (s_odd_ref, smax_odd_ref)

    def put_scores(buf, s):
        for h in (0, 1):
            buf[0][h] = s[h]
            buf[1][h] = jnp.max(s[h], axis=0, keepdims=True)

    j_own = (i * tq) // tk
    key_pos = j_own * tk + lax.broadcasted_iota(jnp.int32, (tk, tq), 0)
    query_pos = i * tq + lax.broadcasted_iota(jnp.int32, (tk, tq), 1)
    causal = key_pos <= query_pos
    put_scores(even, [jnp.where(causal, s_h, NEG_INF) for s_h in tile_scores(j_own)])

    def previous_tile(j):
        return jnp.where(j == 0, j_own, j - 1)

    def pipelined_step(j, src, dst):
        put_scores(dst, tile_scores(j))
        tile_update(previous_tile(j), src)

    def kv_pair_step(u, carry):
        pipelined_step(2 * u, even, odd)
        pipelined_step(2 * u + 1, odd, even)
        return carry
    lax.fori_loop(0, j_own // 2, kv_pair_step, 0)

    @pl.when(j_own % 2 == 1)
    def _():
        pipelined_step(j_own - 1, even, odd)
        tile_update(j_own - 1, odd)

    @pl.when(j_own % 2 == 0)
    def _():
        tile_update(previous_tile(j_own), even)

    out_t = jnp.concatenate([acc_refs[h][0:HEAD_DIM, :] / acc_refs[h][HEAD_DIM:HEAD_DIM + 1, :]
                             for h in (0, 1)], axis=0)
    o_ref[...] = out_t.T.astype(o_ref.dtype)


def _moba_call(qkv, kaug, batch, seq_len):
    n = qkv.shape[0]
    blk = MOBA_BLOCK
    tq = MOBA_Q_TILE
    nb = seq_len // blk
    nq = seq_len // tq
    assert nb + MOBA_ALIBI_PARTS <= LANES and seq_len % MOBA_KV_TILE == 0
    assert MOBA_KV_TILE % tq == 0 and tq % blk == 0
    lane_groups = MOBA_WIDTH // LANES
    return pl.pallas_call(
        functools.partial(_moba_kernel, n_blocks=nb),
        grid=(batch, N_PAIRS, nq),
        in_specs=[
            pl.BlockSpec((tq, PAIR), lambda b, p, i: (b * nq + i, p)),
            pl.BlockSpec((seq_len, PAIR), lambda b, p, i: (b, lane_groups + p)),
            pl.BlockSpec((seq_len, PAIR), lambda b, p, i: (b, 2 * lane_groups + p)),
            pl.BlockSpec((2, seq_len, LANES), lambda b, p, i: (p, 0, 0)),
        ],
        out_specs=pl.BlockSpec((tq, PAIR), lambda b, p, i: (b * nq + i, p)),
        out_shape=jax.ShapeDtypeStruct((n, MOBA_WIDTH), BF16),
        scratch_shapes=[
            pltpu.VMEM((-(-nb // 8) * 8, PAIR), F32),
            pltpu.VMEM((seq_len // MOBA_KV_TILE, 2, MOBA_V_ROWS, MOBA_KV_TILE), BF16),
            pltpu.VMEM((1, tq), F32), pltpu.VMEM((1, tq), F32),
            pltpu.VMEM((MOBA_V_ROWS, tq), F32), pltpu.VMEM((MOBA_V_ROWS, tq), F32),
            pltpu.VMEM((2, MOBA_KV_TILE, tq), F32), pltpu.VMEM((2, MOBA_KV_TILE, tq), F32),
            pltpu.VMEM((2, 1, tq), F32), pltpu.VMEM((2, 1, tq), F32),
        ],
        compiler_params=pltpu.CompilerParams(
            dimension_semantics=("arbitrary", "arbitrary", "arbitrary"),
            vmem_limit_bytes=VMEM_LIMIT),
        name="moba_attention",
    )(qkv, qkv, qkv, kaug)


def _moba_key_aug(seq_len):
    nb = seq_len // MOBA_BLOCK
    heads = MOBA_WIDTH // HEAD_DIM
    pos = np.arange(seq_len, dtype=np.int32)
    slopes = (2.0 ** (-8.0 * (np.arange(heads, dtype=np.float32) + 1.0) / heads)).astype(np.float32)
    aug = np.zeros((heads, seq_len, LANES), np.float32)
    aug[:, pos, pos // MOBA_BLOCK] = 1.0
    rem = (np.float32(LOG2_E) * slopes)[:, None] * pos.astype(np.float32)[None, :]
    for part in range(MOBA_ALIBI_PARTS):
        piece = (rem.view(np.uint32) & np.uint32(0xFFFF0000)).view(np.float32)
        aug[:, :, nb + part] = piece
        rem = rem - piece
    return jnp.asarray(aug.astype(BF16))


def _layer_norm(z, g, b):
    mu = jnp.mean(z, axis=-1, keepdims=True)
    zc = z - mu
    var = jnp.mean(zc * zc, axis=-1, keepdims=True)
    return zc * lax.rsqrt(var + LN_EPS) * g + b


def _outproj_kernel(ya_ref, yb_ref, x_ref, wa_ref, wb_ref, g_ref, b_ref, wr_ref, br_ref,
                    h_ref, hb_ref, lg_ref):
    mix = _dot(ya_ref[...], wa_ref[...]) + _dot(yb_ref[...], wb_ref[...])
    h = _layer_norm(DEEPNORM_ALPHA * x_ref[...] + mix, g_ref[...], b_ref[...])
    h_ref[...] = h
    hb_ref[...] = h.astype(BF16)
    lg_ref[...] = _mm(h, wr_ref[...], NN, 3) + br_ref[...]


def _outproj_call(y_a, y_b, x2, wa, wb, ln_g, ln_b, w_router, b_router):
    n = x2.shape[0]
    tm = OUTPROJ_TM
    row = lambda i: (i, 0)
    const = lambda i: (0, 0)
    return pl.pallas_call(
        _outproj_kernel,
        grid=(n // tm,),
        in_specs=[
            pl.BlockSpec((tm, RWKV_WIDTH), row),
            pl.BlockSpec((tm, MOBA_WIDTH), row),
            pl.BlockSpec((tm, D_MODEL), row),
            pl.BlockSpec((RWKV_WIDTH, D_MODEL), const),
            pl.BlockSpec((MOBA_WIDTH, D_MODEL), const),
            pl.BlockSpec((1, D_MODEL), const),
            pl.BlockSpec((1, D_MODEL), const),
            pl.BlockSpec((D_MODEL, ROUTER_PAD), const),
            pl.BlockSpec((1, ROUTER_PAD), const),
        ],
        out_specs=[
            pl.BlockSpec((tm, D_MODEL), row),
            pl.BlockSpec((tm, D_MODEL), row),
            pl.BlockSpec((tm, ROUTER_PAD), row),
        ],
        out_shape=[
            jax.ShapeDtypeStruct((n, D_MODEL), F32),
            jax.ShapeDtypeStruct((n, D_MODEL), BF16),
            jax.ShapeDtypeStruct((n, ROUTER_PAD), F32),
        ],
        compiler_params=pltpu.CompilerParams(
            dimension_semantics=("arbitrary",), vmem_limit_bytes=VMEM_LIMIT),
        name="outproj_ln_router",
    )(y_a, y_b, x2, wa, wb, ln_g, ln_b, w_router, b_router)


def _route(logits):
    lane = lax.broadcasted_iota(jnp.int32, logits.shape, 1)
    is_group = (lane >= GROUP_LANE0) & (lane < GROUP_LANE0 + N_GROUPS)
    gl = jnp.where(is_group, logits, F32_LOWEST)
    g_max = jnp.max(gl, axis=-1, keepdims=True)
    g_first = jnp.min(jnp.where(gl == g_max, lane, LANES), axis=-1, keepdims=True)
    g_exp = jnp.where(is_group, jnp.exp(gl - g_max), 0.0)
    p_g = 1.0 / jnp.sum(g_exp, axis=-1, keepdims=True)
    g_idx = g_first - GROUP_LANE0
    in_group = (lane >= g_idx * EXPERTS_PER_GROUP) & (lane < (g_idx + 1) * EXPERTS_PER_GROUP)
    el = jnp.where(in_group, logits, F32_LOWEST)
    e_max = jnp.max(el, axis=-1, keepdims=True)
    e_exp = jnp.where(in_group, jnp.exp(el - e_max), 0.0)
    e_prob = e_exp / jnp.sum(e_exp, axis=-1, keepdims=True)
    cand = jnp.where(in_group, e_prob, -1.0)
    v1 = jnp.max(cand, axis=-1, keepdims=True)
    i1 = jnp.min(jnp.where(cand == v1, lane, LANES), axis=-1, keepdims=True)
    pick1 = lane == i1
    cand2 = jnp.where(pick1, -1.0, cand)
    v2 = jnp.max(cand2, axis=-1, keepdims=True)
    i2 = jnp.min(jnp.where(cand2 == v2, lane, LANES), axis=-1, keepdims=True)
    pick2 = lane == i2
    denom = v1 + v2
    gates = jnp.where(pick1, v1 / denom * p_g, jnp.where(pick2, v2 / denom * p_g, 0.0))
    return gates, g_idx


def _route_sort_kernel(lg_ref, gates_ref, pos_ref, flags_ref):
    tm = lg_ref.shape[0]
    gates, g_idx = _route(lg_ref[...])
    lane = lax.broadcasted_iota(jnp.int32, (tm, LANES), 1)
    in_own = lane == g_idx
    onehot = jnp.where(in_own, 1.0, 0.0)
    row = lax.broadcasted_iota(jnp.int32, (tm, tm), 0)
    col = lax.broadcasted_iota(jnp.int32, (tm, tm), 1)
    earlier = jnp.where(row > col, 1.0, 0.0).astype(BF16)
    rank = _dot(earlier, onehot.astype(BF16))
    count = jnp.sum(onehot, axis=0, keepdims=True)
    r128 = lax.broadcasted_iota(jnp.int32, (LANES, LANES), 0)
    c128 = lax.broadcasted_iota(jnp.int32, (LANES, LANES), 1)
    below = jnp.where(r128 < c128, 1.0, 0.0)
    start = _dot(jnp.broadcast_to(count, (8, LANES)), below, NN,
                 precision=lax.Precision.HIGHEST)[0:1]
    pos = jnp.sum(jnp.where(in_own, rank + start, 0.0), axis=-1, keepdims=True)
    gates_ref[...] = gates
    pos_ref[...] = jnp.broadcast_to(pos, (tm, LANES))
    start_c = jnp.broadcast_to(start, (LANES, LANES)).T
    count_c = jnp.broadcast_to(count, (LANES, LANES)).T
    sub_lo = c128 * MOE_SUB
    hit = (count_c > 0.0) & (start_c < (sub_lo + MOE_SUB).astype(F32)) & (start_c + count_c > sub_lo.astype(F32))
    flags_ref[0] = jnp.where(hit, 1, 0)[0:8].astype(jnp.int32)


def _route_sort_call(logits):
    n = logits.shape[0]
    tm = MOE_TM
    row = lambda t: (t, 0)
    return pl.pallas_call(
        _route_sort_kernel,
        grid=(n // tm,),
        in_specs=[pl.BlockSpec((tm, ROUTER_PAD), row)],
        out_specs=[
            pl.BlockSpec((tm, ROUTER_PAD), row),
            pl.BlockSpec((tm, LANES), row),
            pl.BlockSpec((1, 8, LANES), lambda t: (t, 0, 0)),
        ],
        out_shape=[
            jax.ShapeDtypeStruct((n, ROUTER_PAD), F32),
            jax.ShapeDtypeStruct((n, LANES), F32),
            jax.ShapeDtypeStruct((n // tm, 8, LANES), jnp.int32),
        ],
        compiler_params=pltpu.CompilerParams(
            dimension_semantics=("arbitrary",), vmem_limit_bytes=VMEM_LIMIT),
        name="route_sort",
    )(logits)


def _moe_kernel(flags_ref, hb_ref, h_ref, gates_ref, pos_ref, w1_ref, w3_ref, w2_ref, g_ref, b_ref,
                o_ref, xs_ref, gs_ref, acc_ref, pt_ref):
    tile = pl.program_id(0)
    step = pl.program_id(1)
    tm = hb_ref.shape[0]
    sub = MOE_SUB
    n_sub = tm // sub
    eps = MOE_EXPERTS_PER_STEP

    @pl.when(step == 0)
    def _():
        pos_b = pos_ref[...]
        pos_row = pos_b.T[0:1, :]
        g_hi, g_lo = _split_bf16(gates_ref[...], 2)
        src = jnp.concatenate([hb_ref[...], g_hi, g_lo], axis=1)
        for c0 in range(0, tm, sub):
            slot = (lax.broadcasted_iota(jnp.int32, (sub, tm), 0) + c0).astype(F32)
            p_c = jnp.where(slot == pos_row, 1.0, 0.0).astype(BF16)
            moved = _dot(p_c, src)
            xs_ref[c0:c0 + sub, :] = moved[:, :D_MODEL].astype(BF16)
            gs_ref[c0:c0 + sub, :] = moved[:, D_MODEL:D_MODEL + LANES] + moved[:, D_MODEL + LANES:]
        for c0 in range(0, tm, LANES):
            slot = (lax.broadcasted_iota(jnp.int32, (tm, LANES), 1) + c0).astype(F32)
            pt_ref[:, c0:c0 + LANES] = jnp.where(pos_b == slot, 1.0, 0.0).astype(BF16)
        acc_ref[...] = jnp.zeros_like(acc_ref)

    group = step // (EXPERTS_PER_GROUP // eps)

    def visit(r):
        rows = slice(r * sub, (r + 1) * sub)
        x_r = xs_ref[rows, :]
        g_r = gs_ref[rows, :]
        lane = lax.broadcasted_iota(jnp.int32, g_r.shape, 1)
        acc = acc_ref[rows, :]
        for e in range(eps):
            gate_e = jnp.sum(jnp.where(lane == step * eps + e, g_r, 0.0), axis=-1, keepdims=True)
            a1 = _dot(x_r, w1_ref[e])
            a3 = _dot(x_r, w3_ref[e])
            hid = (a1 * _sigmoid(a1)) * a3 * gate_e
            acc = acc + _dot(hid.astype(BF16), w2_ref[e])
        acc_ref[rows, :] = acc

    for r in range(n_sub):
        pl.when(flags_ref[(tile * N_GROUPS + group) * n_sub + r] != 0)(functools.partial(visit, r))

    @pl.when(step == N_EXPERTS // eps - 1)
    def _():
        ffn = _dot(pt_ref[...], acc_ref[...].astype(BF16))
        o_ref[...] = _layer_norm(DEEPNORM_ALPHA * h_ref[...] + ffn, g_ref[...], b_ref[...])


def _moe_call(flags, h_bf16, h_f32, gates, pos, w1, w3, w2, ln_g, ln_b):
    n = h_f32.shape[0]
    tm = MOE_TM
    row = lambda t, s, f: (t, 0)
    const = lambda t, s, f: (0, 0)
    wmap = lambda t, s, f: (s, 0, 0)
    eps = MOE_EXPERTS_PER_STEP
    assert EXPERTS_PER_GROUP % eps == 0 and tm % MOE_SUB == 0
    grid_spec = pltpu.PrefetchScalarGridSpec(
        num_scalar_prefetch=1,
        grid=(n // tm, N_EXPERTS // eps),
        in_specs=[
            pl.BlockSpec((tm, D_MODEL), row),
            pl.BlockSpec((tm, D_MODEL), row),
            pl.BlockSpec((tm, ROUTER_PAD), row),
            pl.BlockSpec((tm, LANES), row),
            pl.BlockSpec((eps, D_MODEL, D_EXPERT), wmap),
            pl.BlockSpec((eps, D_MODEL, D_EXPERT), wmap),
            pl.BlockSpec((eps, D_EXPERT, D_MODEL), wmap),
            pl.BlockSpec((1, D_MODEL), const),
            pl.BlockSpec((1, D_MODEL), const),
        ],
        out_specs=pl.BlockSpec((tm, D_MODEL), row),
        scratch_shapes=[
            pltpu.VMEM((tm, D_MODEL), BF16),
            pltpu.VMEM((tm, ROUTER_PAD), F32),
            pltpu.VMEM((tm, D_MODEL), F32),
            pltpu.VMEM((tm, tm), BF16),
        ],
    )
    return pl.pallas_call(
        _moe_kernel,
        grid_spec=grid_spec,
        out_shape=jax.ShapeDtypeStruct((n, D_MODEL), F32),
        compiler_params=pltpu.CompilerParams(
            dimension_semantics=("arbitrary", "arbitrary"), vmem_limit_bytes=VMEM_LIMIT),
        name="hier_moe_ln",
    )(flags, h_bf16, h_f32, gates, pos, w1, w3, w2, ln_g, ln_b)


def _pad_cols(w, width):
    return jnp.pad(w, ((0, 0), (0, width - w.shape[1])))


def _pad_rows(w, height):
    return jnp.pad(w, ((0, height - w.shape[0]), (0, 0)))


def kernel(x, w_in, mu_shift, w0, w_lora_up, a0, a_lora_up, g_lora_up, k_k, k_a, r_k, gn_w, gn_b, w_out, ln1_g, ln1_b, w_group, b_group, w_expert, b_expert, w1_exp, w3_exp, w2_exp, ln2_g, ln2_b):
    batch, seq_len, d = x.shape
    assert d == D_MODEL
    n = batch * seq_len
    x2 = x.reshape(n, d)

    c_rkv = 3 * RWKV_WIDTH
    c_wd = c_rkv + DECAY_RANK
    c_ad = c_wd + AAA_RANK
    c_gd = c_ad + GATE_RANK
    w_cat = jnp.concatenate([
        w_in[:, :c_rkv], _pad_cols(w_in[:, c_rkv:c_gd], LORA_PAD), w_in[:, c_gd:],
    ], axis=1).astype(BF16)
    mu2 = mu_shift[None, :]
    mu_cat = jnp.concatenate([mu2[:, :c_rkv], _pad_cols(mu2[:, c_rkv:c_gd], LORA_PAD)], axis=1)
    p_rkv, p_lora, p_moba = _inproj_call(x2, w_cat, mu_cat, seq_len)

    place = lambda w, first: jnp.pad(w, ((first, LORA_PAD - first - w.shape[0]), (0, 0)))
    lora_up = (place(w_lora_up, 0), place(a_lora_up, DECAY_RANK), place(g_lora_up, DECAY_RANK + AAA_RANK))

    vecs = jnp.stack([w0, a0, k_k, k_a, r_k.reshape(-1), gn_w, gn_b, jnp.zeros_like(w0)], axis=0)
    head_id = jnp.arange(2 * PAIR) // HEAD_DIM
    bd = (head_id[:, None] == head_id[None, :]).astype(BF16)
    y_a = _rwkv_call(p_rkv, p_lora, vecs, *lora_up, bd, batch, seq_len)

    y_b = _moba_call(p_moba, _moba_key_aug(seq_len), batch, seq_len)

    w_out_b = w_out.astype(BF16)
    w_router = _pad_cols(jnp.concatenate([w_expert, w_group], axis=1), ROUTER_PAD)
    b_router = _pad_cols(jnp.concatenate([b_expert, b_group])[None, :], ROUTER_PAD)
    h1, h1_b, logits = _outproj_call(y_a, y_b, x2, w_out_b[:RWKV_WIDTH], w_out_b[RWKV_WIDTH:],
                                     ln1_g[None, :], ln1_b[None, :], w_router, b_router)

    flat = lambda w: w.astype(BF16).reshape((N_EXPERTS,) + w.shape[2:])
    gates, pos, flags = _route_sort_call(logits)
    flags = flags[:, :N_GROUPS, :MOE_TM // MOE_SUB].reshape(-1)
    out = _moe_call(flags, h1_b, h1, gates, pos, flat(w1_exp), flat(w3_exp), flat(w2_exp),
                    ln2_g[None, :], ln2_b[None, :])
    return out.reshape(batch, seq_len, d)
```

```python
import functools
import math

import jax
import jax.numpy as jnp
import numpy as np
from jax import lax
from jax.experimental import pallas as pl
from jax.experimental.pallas import tpu as pltpu

F32 = jnp.float32
BF16 = jnp.bfloat16

D_MODEL = 1024
HEAD_DIM = 64
RWKV_WIDTH = 512
MOBA_WIDTH = 512
DECAY_RANK = 32
AAA_RANK = 32
GATE_RANK = 96
GN_EPS = 64e-5
L2_EPS = 1e-12
MOBA_BLOCK = 256
MOBA_TOPK = 3
N_GROUPS = 4
EXPERTS_PER_GROUP = 8
N_EXPERTS = N_GROUPS * EXPERTS_PER_GROUP
D_EXPERT = 256
LN_EPS = 1e-5
DEEPNORM_ALPHA = float(2.0 ** 0.25)
NEG_INF = -1e30
F32_LOWEST = -3.0e38

LANES = 128
PAIR = 2 * HEAD_DIM
N_PAIRS = RWKV_WIDTH // PAIR
LORA_PAD = 2 * LANES
RWKV_COLS_PAD = 3 * RWKV_WIDTH + LORA_PAD
IN_COLS_PAD = RWKV_COLS_PAD + 3 * MOBA_WIDTH
VMEM_LIMIT = 56 * 1024 * 1024

INPROJ_TM = 512
INPROJ_TN = 256
RWKV_CHUNK = 64
RWKV_CHUNKS_PER_STEP = 4
RWKV_PASSES = 1
RWKV_STATE_PASSES = 1
OUTPROJ_TM = 512
MOE_TM = 1024
MOE_EXPERTS_PER_STEP = 4
MOE_SUB = 256
MOE_WINDOW = 384
MOE_WINDOW_ALIGN = 128
MOE_META_FITS = 16
MOE_META_WINDOW = 17
MOBA_KV_TILE = 512
MOBA_Q_TILE = 512
MOBA_V_ROWS = HEAD_DIM + 16
MOBA_ALIBI_PARTS = 3
LOG2_E = 1.4426950408889634
ROUTER_PAD = LANES
GROUP_LANE0 = N_EXPERTS

NN = (((1,), (0,)), ((), ()))
NT = (((1,), (1,)), ((), ()))


def _dot(a, b, dims=NN, precision=None):
    return lax.dot_general(a, b, dims, precision=precision, preferred_element_type=F32)


def _split_bf16(x, parts):
    out = []
    rem = x
    for i in range(parts):
        p = rem.astype(BF16)
        out.append(p)
        if i + 1 < parts:
            rem = rem - p.astype(F32)
    return out


def _mm(a, b, dims=NN, passes=3):
    if passes == 1:
        return _dot(a.astype(BF16), b.astype(BF16), dims)
    if passes == 6:
        return _dot(a, b, dims, precision=lax.Precision.HIGHEST)
    a_hi, a_lo = _split_bf16(a, 2)
    b_hi, b_lo = _split_bf16(b, 2)
    return _dot(a_hi, b_hi, dims) + (_dot(a_hi, b_lo, dims) + _dot(a_lo, b_hi, dims))


def _mm_exact_lhs(a_bf16, b, dims=NN):
    b1, b2, b3 = _split_bf16(b, 3)
    return _dot(a_bf16, b1, dims) + (_dot(a_bf16, b2, dims) + _dot(a_bf16, b3, dims))


def _mm_exact_rhs(a, b_bf16, dims=NN):
    a1, a2, a3 = _split_bf16(a, 3)
    return _dot(a1, b_bf16, dims) + (_dot(a2, b_bf16, dims) + _dot(a3, b_bf16, dims))


def _inproj_kernel(x_ref, w_ref, mu_ref, prkv_ref, plora_ref, pm_ref, carry_ref, *, tiles_per_seq):
    tm = x_ref.shape[0]
    xb = x_ref[...].astype(BF16)
    seq_start = (pl.program_id(0) % tiles_per_seq) == 0
    row0 = lax.broadcasted_iota(jnp.int32, (tm, INPROJ_TN), 0) == 0
    n_shift_tiles = RWKV_COLS_PAD // INPROJ_TN
    for j in range(n_shift_tiles):
        c0 = j * INPROJ_TN
        acc = _dot(xb, w_ref[:, c0:c0 + INPROJ_TN])
        prev_last = jnp.where(seq_start, 0.0, carry_ref[0:1, c0:c0 + INPROJ_TN])
        shifted = jnp.where(row0, prev_last, pltpu.roll(acc, 1, 0))
        carry_ref[0:1, c0:c0 + INPROJ_TN] = acc[tm - 1:tm, :]
        out = acc + (shifted - acc) * mu_ref[:, c0:c0 + INPROJ_TN]
        if c0 < 3 * RWKV_WIDTH:
            prkv_ref[:, c0:c0 + INPROJ_TN] = out
        else:
            plora_ref[:, c0 - 3 * RWKV_WIDTH:c0 - 3 * RWKV_WIDTH + INPROJ_TN] = out
    for j in range(3 * MOBA_WIDTH // INPROJ_TN):
        c0 = j * INPROJ_TN
        acc = _dot(xb, w_ref[:, RWKV_COLS_PAD + c0:RWKV_COLS_PAD + c0 + INPROJ_TN])
        pm_ref[:, c0:c0 + INPROJ_TN] = acc.astype(BF16)


def _inproj_call(x2, w_cat, mu_cat, seq_len):
    n = x2.shape[0]
    tm = INPROJ_TM
    assert seq_len % tm == 0 and (3 * RWKV_WIDTH) % INPROJ_TN == 0
    return pl.pallas_call(
        functools.partial(_inproj_kernel, tiles_per_seq=seq_len // tm),
        grid=(n // tm,),
        in_specs=[
            pl.BlockSpec((tm, D_MODEL), lambda i: (i, 0)),
            pl.BlockSpec((D_MODEL, IN_COLS_PAD), lambda i: (0, 0)),
            pl.BlockSpec((1, RWKV_COLS_PAD), lambda i: (0, 0)),
        ],
        out_specs=[
            pl.BlockSpec((tm, 3 * RWKV_WIDTH), lambda i: (i, 0)),
            pl.BlockSpec((tm, LORA_PAD), lambda i: (i, 0)),
            pl.BlockSpec((tm, 3 * MOBA_WIDTH), lambda i: (i, 0)),
        ],
        out_shape=[
            jax.ShapeDtypeStruct((n, 3 * RWKV_WIDTH), F32),
            jax.ShapeDtypeStruct((n, LORA_PAD), F32),
            jax.ShapeDtypeStruct((n, 3 * MOBA_WIDTH), BF16),
        ],
        scratch_shapes=[pltpu.VMEM((8, RWKV_COLS_PAD), F32)],
        compiler_params=pltpu.CompilerParams(
            dimension_semantics=("arbitrary",), vmem_limit_bytes=VMEM_LIMIT),
        name="inproj_shift",
    )(x2, w_cat, mu_cat)


def _softplus(z):
    return jnp.maximum(z, 0.0) + jnp.log(1.0 + jnp.exp(-jnp.abs(z)))


def _sigmoid(z):
    return 1.0 / (1.0 + jnp.exp(-z))


def _rwkv_chunks(rt, kt, at, bt, v, d_incl, s_prev, passes, state_passes):
    c = RWKV_CHUNK
    n_chunks = rt.shape[0] // c
    n_pairs = len(s_prev)
    row = lax.broadcasted_iota(jnp.int32, (c, PAIR), 0)
    col = lax.broadcasted_iota(jnp.int32, (c, PAIR), 1) % HEAD_DIM
    strict = row > col
    incl = row >= col
    eye_c = (row == col).astype(F32)
    lane = lax.broadcasted_iota(jnp.int32, (1, PAIR), 1)
    head0 = lane < HEAD_DIM
    head1 = jnp.logical_not(head0)
    prow = lax.broadcasted_iota(jnp.int32, (PAIR, PAIR), 0)
    pcol = lax.broadcasted_iota(jnp.int32, (PAIR, PAIR), 1)
    same_head = (prow < HEAD_DIM) == (pcol < HEAD_DIM)
    eye_p = (prow == pcol).astype(F32)
    rows = [slice(ci * c, (ci + 1) * c) for ci in range(n_chunks)]
    sl = [slice(p * PAIR, (p + 1) * PAIR) for p in range(n_pairs)]
    pairs = [(ci, p) for ci in range(n_chunks) for p in range(n_pairs)]
    cut = lambda t, ci, p: t[rows[ci], sl[p]]

    def by_head(m):
        return jnp.concatenate([jnp.where(head0, m, 0.0), jnp.where(head1, m, 0.0)], axis=0)

    def by_head2(m, n):
        return jnp.concatenate([by_head(m), by_head(n)], axis=1)

    at_p = {k_: cut(at, *k_) for k_ in pairs}
    rt_p = {k_: cut(rt, *k_) for k_ in pairs}
    bt_p = {k_: cut(bt, *k_) for k_ in pairs}
    kt_p = {k_: cut(kt, *k_) for k_ in pairs}
    v_p = {k_: cut(v, *k_) for k_ in pairs}

    z = {k_: _mm(jnp.concatenate([at_p[k_], rt_p[k_]], axis=0),
                 jnp.concatenate([by_head(bt_p[k_]).T, by_head(kt_p[k_]).T], axis=1), NN, passes)
         for k_ in pairs}
    l_ab = {k_: jnp.where(strict, z[k_][:c, :PAIR], 0.0) for k_ in pairs}
    l_ak = {k_: jnp.where(strict, z[k_][:c, PAIR:], 0.0) for k_ in pairs}
    m_rb = {k_: jnp.where(incl, z[k_][c:, :PAIR], 0.0) for k_ in pairs}
    m_rk = {k_: jnp.where(incl, z[k_][c:, PAIR:], 0.0) for k_ in pairs}
    pw = {k_: _mm(l_ab[k_], by_head(l_ab[k_]), NN, passes) for k_ in pairs}
    t_inv = {k_: eye_c + l_ab[k_] for k_ in pairs}
    for _ in range(int(math.log2(c)) - 1):
        tp = {k_: _mm(jnp.concatenate([t_inv[k_], pw[k_]], axis=0), by_head(pw[k_]), NN, passes)
              for k_ in pairs}
        t_inv = {k_: t_inv[k_] + tp[k_][:c] for k_ in pairs}
        pw = {k_: tp[k_][c:] for k_ in pairs}
    lm = {k_: _mm(jnp.concatenate([l_ak[k_], m_rk[k_]], axis=0), by_head(v_p[k_]), NN, passes)
          for k_ in pairs}
    lv = {k_: lm[k_][:c] for k_ in pairs}
    mv = {k_: lm[k_][c:] for k_ in pairs}
    wu = {k_: _mm(t_inv[k_], by_head2(at_p[k_], lv[k_]), NN, passes) for k_ in pairs}
    qy = {k_: _mm(m_rb[k_], by_head2(wu[k_][:, :PAIR], wu[k_][:, PAIR:]), NN, passes) for k_ in pairs}

    qeff, y1, phi, psi = {}, {}, {}, {}
    for ci, p in pairs:
        k_ = (ci, p)
        w, u0 = wu[k_][:, :PAIR], wu[k_][:, PAIR:]
        qeff[k_] = rt_p[k_] + qy[k_][:, :PAIR]
        y1[k_] = qy[k_][:, PAIR:] + mv[k_]
        d_p = d_incl[(ci + 1) * c - 1:(ci + 1) * c, sl[p]]
        phi[k_] = jnp.where(same_head, (eye_p + _mm(w.T, bt_p[k_], NN, passes)) * d_p, 0.0)
        uv_t = jnp.concatenate([u0, v_p[k_]], axis=0).T
        bk = jnp.concatenate([bt_p[k_], kt_p[k_]], axis=0)
        psi[k_] = jnp.where(same_head, _mm(uv_t, bk, NN, passes) * d_p, 0.0)

    state = list(s_prev)
    ys = [[None] * n_pairs for _ in range(n_chunks)]
    for ci in range(n_chunks):
        for p in range(n_pairs):
            ys[ci][p] = _mm(qeff[ci, p], state[p].T, NN, state_passes) + y1[ci, p]
            state[p] = _mm(state[p], phi[ci, p], NN, state_passes) + psi[ci, p]
    y = jnp.concatenate([jnp.concatenate(ys[ci], axis=1) for ci in range(n_chunks)], axis=0)
    return y, state


def _rwkv_kernel(prkv_ref, plora_ref, vec_ref, wl_ref, al_ref, gl_ref, bd_ref, y_ref, s_ref):
    rows = prkv_ref.shape[0]
    c = RWKV_CHUNK
    width = RWKV_WIDTH

    @pl.when(pl.program_id(1) == 0)
    def _():
        s_ref[...] = jnp.zeros_like(s_ref)

    r = prkv_ref[:, 0:width]
    k_raw = prkv_ref[:, width:2 * width]
    v = prkv_ref[:, 2 * width:3 * width]
    p_wd = p_ad = p_gd = plora_ref[...]
    w0 = vec_ref[0:1, :]
    a0 = vec_ref[1:2, :]
    k_k = vec_ref[2:3, :]
    k_a = vec_ref[3:4, :]
    r_k = vec_ref[4:5, :]
    gn_w = vec_ref[5:6, :]
    gn_b = vec_ref[6:7, :]
    bd = bd_ref[...]

    def seg_sum(z):
        halves = []
        for c0 in range(0, width, bd.shape[0]):
            z_hi, z_lo = _split_bf16(z[:, c0:c0 + bd.shape[0]], 2)
            halves.append(_dot(z_hi, bd) + _dot(z_lo, bd))
        return jnp.concatenate(halves, axis=1)

    w_log = -_softplus(-(w0 + _mm(jnp.tanh(p_wd), wl_ref[...], NN, 3))) - 0.5
    log_w = -jnp.exp(w_log)
    a = _sigmoid(a0 + _mm(p_ad, al_ref[...], NN, RWKV_PASSES))
    g = _mm(_sigmoid(p_gd), gl_ref[...], NN, RWKV_PASSES)
    kk = k_raw * k_k
    kk = kk / jnp.maximum(jnp.sqrt(seg_sum(kk * kk)), L2_EPS)
    k = k_raw * (1.0 + (a - 1.0) * k_a)

    row = lax.broadcasted_iota(jnp.int32, (rows, rows), 0)
    col = lax.broadcasted_iota(jnp.int32, (rows, rows), 1)
    tri = ((row >= col) & (row // c == col // c)).astype(BF16)
    cum = _mm_exact_lhs(tri, log_w)
    d_incl = jnp.exp(cum)
    d_inv = jnp.exp(-cum)
    d_excl = jnp.exp(cum - log_w)
    rt = r * d_incl
    kt = k * d_inv
    at = -kk * d_excl
    bt = kk * a * d_inv

    y, s_next = _rwkv_chunks(rt, kt, at, bt, v, d_incl, [s_ref[p] for p in range(N_PAIRS)],
                             RWKV_PASSES, RWKV_STATE_PASSES)
    for p in range(N_PAIRS):
        s_ref[p] = s_next[p]

    inv_n = 1.0 / HEAD_DIM
    mu = seg_sum(y) * inv_n
    yc = y - mu
    var = seg_sum(yc * yc) * inv_n
    yn = yc * lax.rsqrt(var + GN_EPS) * gn_w + gn_b
    bonus = seg_sum(r * k * r_k) * v
    y_ref[...] = ((yn + bonus) * g).astype(y_ref.dtype)


def _rwkv_call(p_rkv, p_lora, vecs, wl, al, gl, bd, batch, seq_len):
    n = p_rkv.shape[0]
    rows = RWKV_CHUNK * RWKV_CHUNKS_PER_STEP
    assert seq_len % rows == 0
    steps = seq_len // rows
    row_map = lambda b, i: (b * steps + i, 0)
    const = lambda b, i: (0, 0)
    return pl.pallas_call(
        _rwkv_kernel,
        grid=(batch, steps),
        in_specs=[
            pl.BlockSpec((rows, 3 * RWKV_WIDTH), row_map),
            pl.BlockSpec((rows, LORA_PAD), row_map),
            pl.BlockSpec((8, RWKV_WIDTH), const),
            pl.BlockSpec((LORA_PAD, RWKV_WIDTH), const),
            pl.BlockSpec((LORA_PAD, RWKV_WIDTH), const),
            pl.BlockSpec((LORA_PAD, RWKV_WIDTH), const),
            pl.BlockSpec((2 * PAIR, 2 * PAIR), const),
        ],
        out_specs=pl.BlockSpec((rows, RWKV_WIDTH), row_map),
        out_shape=jax.ShapeDtypeStruct((n, RWKV_WIDTH), BF16),
        scratch_shapes=[pltpu.VMEM((N_PAIRS, PAIR, PAIR), F32)],
        compiler_params=pltpu.CompilerParams(
            dimension_semantics=("arbitrary", "arbitrary"), vmem_limit_bytes=VMEM_LIMIT),
        name="rwkv7_chunked",
    )(p_rkv, p_lora, vecs, wl, al, gl, bd)


def _moba_kernel(q_ref, k_ref, v_ref, kaug_ref, o_ref, kmean_ref, vt_ref,
                 m0_ref, m1_ref, acc0_ref, acc1_ref, s_even_ref, s_odd_ref,
                 smax_even_ref, smax_odd_ref, *, n_blocks):
    blk = MOBA_BLOCK
    tk = MOBA_KV_TILE
    tq = q_ref.shape[0]
    i = pl.program_id(2)
    nb_pad = kmean_ref.shape[0]
    m_refs, acc_refs = (m0_ref, m1_ref), (acc0_ref, acc1_ref)

    @pl.when(i == 0)
    def _():
        kmean_ref[...] = jnp.zeros_like(kmean_ref)

        def mean_body(n, carry):
            off = pl.multiple_of(n * blk, blk)
            kb = k_ref[pl.ds(off, blk), :].astype(F32)
            kmean_ref[pl.ds(n, 1), :] = jnp.sum(kb, axis=0, keepdims=True) * (1.0 / blk)
            return carry
        lax.fori_loop(0, n_blocks, mean_body, 0)

        ones = jnp.ones((MOBA_V_ROWS - HEAD_DIM, tk), BF16)

        def vt_body(j, carry):
            off = pl.multiple_of(j * tk, tk)
            v_t = v_ref[pl.ds(off, tk), :].astype(F32).T.astype(BF16)
            for h in (0, 1):
                vt_ref[j, h] = jnp.concatenate([v_t[h * HEAD_DIM:(h + 1) * HEAD_DIM], ones], axis=0)
            return carry
        lax.fori_loop(0, vt_ref.shape[0], vt_body, 0)

    q_t = q_ref[...].astype(F32).T
    chan = lax.broadcasted_iota(jnp.int32, (PAIR, tq), 0)
    blk_row = lax.broadcasted_iota(jnp.int32, (nb_pad, tq), 0)
    own_blk = (i * tq + lax.broadcasted_iota(jnp.int32, (nb_pad, tq), 1)) // blk
    past = blk_row < own_blk
    aug_row = lax.broadcasted_iota(jnp.int32, (LANES, tq), 0)
    ones_rows = (aug_row >= n_blocks) & (aug_row < n_blocks + MOBA_ALIBI_PARTS)
    kmean = kmean_ref[...]

    qa_t = []
    for h in (0, 1):
        qh_t = jnp.where((chan < HEAD_DIM) == (h == 0), q_t, 0.0)
        gate = _dot(kmean, qh_t, NN, precision=lax.Precision.HIGHEST)
        gate = jnp.where(past, gate, F32_LOWEST)
        sel = jnp.zeros(gate.shape, jnp.bool_)
        for _ in range(MOBA_TOPK):
            mx = jnp.max(gate, axis=0, keepdims=True)
            first = jnp.min(jnp.where(gate == mx, blk_row, nb_pad), axis=0, keepdims=True)
            pick = (blk_row == first) & (mx > F32_LOWEST)
            sel = sel | pick
            gate = jnp.where(pick, F32_LOWEST, gate)
        sel_bias = jnp.where(past & jnp.logical_not(sel), NEG_INF, 0.0)
        aug_t = jnp.concatenate([sel_bias, jnp.zeros((LANES - nb_pad, tq), F32)], axis=0)
        aug_t = jnp.where(ones_rows, 1.0, aug_t)
        qa_t.append(jnp.concatenate([qh_t * (LOG2_E / math.sqrt(HEAD_DIM)), aug_t], axis=0).astype(BF16))

    def tile_scores(j):
        off = pl.multiple_of(j * tk, tk)
        k_t = k_ref[pl.ds(off, tk), :]
        return [_dot(jnp.concatenate([k_t, kaug_ref[h, pl.ds(off, tk), :]], axis=1), qa_t[h])
                for h in (0, 1)]

    def tile_update(j, buf):
        s_buf, smax_buf = buf
        for h in (0, 1):
            m_old = m_refs[h][...]
            m_new = jnp.maximum(m_old, smax_buf[h])
            p = jnp.exp2(s_buf[h] - m_new).astype(BF16)
            pv = _dot(vt_ref[j, h], p)
            acc_refs[h][...] = jnp.exp2(m_old - m_new) * acc_refs[h][...] + pv
            m_refs[h][...] = m_new

    for h in (0, 1):
        m_refs[h][...] = jnp.full(m_refs[h].shape, F32_LOWEST, F32)
        acc_refs[h][...] = jnp.zeros(acc_refs[h].shape, F32)

    even = (s_even_ref, smax_even_ref)
    odd = (s_odd_ref, smax_odd_ref)

    def put_scores(buf, s):
        for h in (0, 1):
            buf[0][h] = s[h]
            buf[1][h] = jnp.max(s[h], axis=0, keepdims=True)

    j_own = (i * tq) // tk
    key_pos = j_own * tk + lax.broadcasted_iota(jnp.int32, (tk, tq), 0)
    query_pos = i * tq + lax.broadcasted_iota(jnp.int32, (tk, tq), 1)
    causal = key_pos <= query_pos
    put_scores(even, [jnp.where(causal, s_h, NEG_INF) for s_h in tile_scores(j_own)])

    def previous_tile(j):
        return jnp.where(j == 0, j_own, j - 1)

    def pipelined_step(j, src, dst):
        put_scores(dst, tile_scores(j))
        tile_update(previous_tile(j), src)

    def kv_pair_step(u, carry):
        pipelined_step(2 * u, even, odd)
        pipelined_step(2 * u + 1, odd, even)
        return carry
    lax.fori_loop(0, j_own // 2, kv_pair_step, 0)

    @pl.when(j_own % 2 == 1)
    def _():
        pipelined_step(j_own - 1, even, odd)
        tile_update(j_own - 1, odd)

    @pl.when(j_own % 2 == 0)
    def _():
        tile_update(previous_tile(j_own), even)

    out_t = jnp.concatenate([acc_refs[h][0:HEAD_DIM, :] / acc_refs[h][HEAD_DIM:HEAD_DIM + 1, :]
                             for h in (0, 1)], axis=0)
    o_ref[...] = out_t.T.astype(o_ref.dtype)


def _moba_call(qkv, kaug, batch, seq_len):
    n = qkv.shape[0]
    blk = MOBA_BLOCK
    tq = MOBA_Q_TILE
    nb = seq_len // blk
    nq = seq_len // tq
    assert nb + MOBA_ALIBI_PARTS <= LANES and seq_len % MOBA_KV_TILE == 0
    assert MOBA_KV_TILE % tq == 0 and tq % blk == 0
    lane_groups = MOBA_WIDTH // LANES
    return pl.pallas_call(
        functools.partial(_moba_kernel, n_blocks=nb),
        grid=(batch, N_PAIRS, nq),
        in_specs=[
            pl.BlockSpec((tq, PAIR), lambda b, p, i: (b * nq + i, p)),
            pl.BlockSpec((seq_len, PAIR), lambda b, p, i: (b, lane_groups + p)),
            pl.BlockSpec((seq_len, PAIR), lambda b, p, i: (b, 2 * lane_groups + p)),
            pl.BlockSpec((2, seq_len, LANES), lambda b, p, i: (p, 0, 0)),
        ],
        out_specs=pl.BlockSpec((tq, PAIR), lambda b, p, i: (b * nq + i, p)),
        out_shape=jax.ShapeDtypeStruct((n, MOBA_WIDTH), BF16),
        scratch_shapes=[
            pltpu.VMEM((-(-nb // 8) * 8, PAIR), F32),
            pltpu.VMEM((seq_len // MOBA_KV_TILE, 2, MOBA_V_ROWS, MOBA_KV_TILE), BF16),
            pltpu.VMEM((1, tq), F32), pltpu.VMEM((1, tq), F32),
            pltpu.VMEM((MOBA_V_ROWS, tq), F32), pltpu.VMEM((MOBA_V_ROWS, tq), F32),
            pltpu.VMEM((2, MOBA_KV_TILE, tq), F32), pltpu.VMEM((2, MOBA_KV_TILE, tq), F32),
            pltpu.VMEM((2, 1, tq), F32), pltpu.VMEM((2, 1, tq), F32),
        ],
        compiler_params=pltpu.CompilerParams(
            dimension_semantics=("arbitrary", "arbitrary", "arbitrary"),
            vmem_limit_bytes=VMEM_LIMIT),
        name="moba_attention",
    )(qkv, qkv, qkv, kaug)


def _moba_key_aug(seq_len):
    nb = seq_len // MOBA_BLOCK
    heads = MOBA_WIDTH // HEAD_DIM
    pos = np.arange(seq_len, dtype=np.int32)
    slopes = (2.0 ** (-8.0 * (np.arange(heads, dtype=np.float32) + 1.0) / heads)).astype(np.float32)
    aug = np.zeros((heads, seq_len, LANES), np.float32)
    aug[:, pos, pos // MOBA_BLOCK] = 1.0
    rem = (np.float32(LOG2_E) * slopes)[:, None] * pos.astype(np.float32)[None, :]
    for part in range(MOBA_ALIBI_PARTS):
        piece = (rem.view(np.uint32) & np.uint32(0xFFFF0000)).view(np.float32)
        aug[:, :, nb + part] = piece
        rem = rem - piece
    return jnp.asarray(aug.astype(BF16))


def _layer_norm(z, g, b):
    mu = jnp.mean(z, axis=-1, keepdims=True)
    zc = z - mu
    var = jnp.mean(zc * zc, axis=-1, keepdims=True)
    return zc * lax.rsqrt(var + LN_EPS) * g + b


def _outproj_kernel(ya_ref, yb_ref, x_ref, wa_ref, wb_ref, g_ref, b_ref, wr_ref, br_ref,
                    h_ref, hb_ref, lg_ref):
    mix = _dot(ya_ref[...], wa_ref[...]) + _dot(yb_ref[...], wb_ref[...])
    h = _layer_norm(DEEPNORM_ALPHA * x_ref[...] + mix, g_ref[...], b_ref[...])
    h_ref[...] = h
    hb_ref[...] = h.astype(BF16)
    lg_ref[...] = _mm(h, wr_ref[...], NN, 3) + br_ref[...]


def _outproj_call(y_a, y_b, x2, wa, wb, ln_g, ln_b, w_router, b_router):
    n = x2.shape[0]
    tm = OUTPROJ_TM
    row = lambda i: (i, 0)
    const = lambda i: (0, 0)
    return pl.pallas_call(
        _outproj_kernel,
        grid=(n // tm,),
        in_specs=[
            pl.BlockSpec((tm, RWKV_WIDTH), row),
            pl.BlockSpec((tm, MOBA_WIDTH), row),
            pl.BlockSpec((tm, D_MODEL), row),
            pl.BlockSpec((RWKV_WIDTH, D_MODEL), const),
            pl.BlockSpec((MOBA_WIDTH, D_MODEL), const),
            pl.BlockSpec((1, D_MODEL), const),
            pl.BlockSpec((1, D_MODEL), const),
            pl.BlockSpec((D_MODEL, ROUTER_PAD), const),
            pl.BlockSpec((1, ROUTER_PAD), const),
        ],
        out_specs=[
            pl.BlockSpec((tm, D_MODEL), row),
            pl.BlockSpec((tm, D_MODEL), row),
            pl.BlockSpec((tm, ROUTER_PAD), row),
        ],
        out_shape=[
            jax.ShapeDtypeStruct((n, D_MODEL), F32),
            jax.ShapeDtypeStruct((n, D_MODEL), BF16),
            jax.ShapeDtypeStruct((n, ROUTER_PAD), F32),
        ],
        compiler_params=pltpu.CompilerParams(
            dimension_semantics=("arbitrary",), vmem_limit_bytes=VMEM_LIMIT),
        name="outproj_ln_router",
    )(y_a, y_b, x2, wa, wb, ln_g, ln_b, w_router, b_router)


def _route(logits):
    lane = lax.broadcasted_iota(jnp.int32, logits.shape, 1)
    is_group = (lane >= GROUP_LANE0) & (lane < GROUP_LANE0 + N_GROUPS)
    gl = jnp.where(is_group, logits, F32_LOWEST)
    g_max = jnp.max(gl, axis=-1, keepdims=True)
    g_first = jnp.min(jnp.where(gl == g_max, lane, LANES), axis=-1, keepdims=True)
    g_exp = jnp.where(is_group, jnp.exp(gl - g_max), 0.0)
    p_g = 1.0 / jnp.sum(g_exp, axis=-1, keepdims=True)
    g_idx = g_first - GROUP_LANE0
    in_group = (lane >= g_idx * EXPERTS_PER_GROUP) & (lane < (g_idx + 1) * EXPERTS_PER_GROUP)
    el = jnp.where(in_group, logits, F32_LOWEST)
    e_max = jnp.max(el, axis=-1, keepdims=True)
    e_exp = jnp.where(in_group, jnp.exp(el - e_max), 0.0)
    e_prob = e_exp / jnp.sum(e_exp, axis=-1, keepdims=True)
    cand = jnp.where(in_group, e_prob, -1.0)
    v1 = jnp.max(cand, axis=-1, keepdims=True)
    i1 = jnp.min(jnp.where(cand == v1, lane, LANES), axis=-1, keepdims=True)
    pick1 = lane == i1
    cand2 = jnp.where(pick1, -1.0, cand)
    v2 = jnp.max(cand2, axis=-1, keepdims=True)
    i2 = jnp.min(jnp.where(cand2 == v2, lane, LANES), axis=-1, keepdims=True)
    pick2 = lane == i2
    denom = v1 + v2
    gates = jnp.where(pick1, v1 / denom * p_g, jnp.where(pick2, v2 / denom * p_g, 0.0))
    return gates, g_idx


def _route_sort_kernel(lg_ref, gates_ref, pos_ref, flags_ref, earlier_ref):
    tm = lg_ref.shape[0]

    @pl.when(pl.program_id(0) == 0)
    def _():
        row = lax.broadcasted_iota(jnp.int32, (tm, tm), 0)
        col = lax.broadcasted_iota(jnp.int32, (tm, tm), 1)
        earlier_ref[...] = jnp.where(row > col, 1.0, 0.0).astype(BF16)

    gates, g_idx = _route(lg_ref[...])
    lane = lax.broadcasted_iota(jnp.int32, (tm, LANES), 1)
    in_own = lane == g_idx
    onehot = jnp.where(in_own, 1.0, 0.0)
    rank = _dot(earlier_ref[...], onehot.astype(BF16))
    count = jnp.sum(onehot, axis=0, keepdims=True)
    r128 = lax.broadcasted_iota(jnp.int32, (LANES, LANES), 0)
    c128 = lax.broadcasted_iota(jnp.int32, (LANES, LANES), 1)
    below = jnp.where(r128 < c128, 1.0, 0.0)
    start = _dot(jnp.broadcast_to(count, (8, LANES)), below, NN,
                 precision=lax.Precision.HIGHEST)[0:1]
    pos = jnp.sum(jnp.where(in_own, rank + start, 0.0), axis=-1, keepdims=True)
    gates_ref[...] = gates
    pos_ref[...] = jnp.broadcast_to(pos, (tm, LANES))
    start_c = jnp.broadcast_to(start, (LANES, LANES)).T
    count_c = jnp.broadcast_to(count, (LANES, LANES)).T
    sub_lo = c128 * MOE_SUB
    hit = (count_c > 0.0) & (start_c < (sub_lo + MOE_SUB).astype(F32)) & (start_c + count_c > sub_lo.astype(F32))
    start_i = start_c.astype(jnp.int32)
    count_i = count_c.astype(jnp.int32)
    win = jnp.minimum((start_i // MOE_WINDOW_ALIGN) * MOE_WINDOW_ALIGN, tm - MOE_WINDOW)
    fits = (count_i > 0) & (start_i + count_i <= win + MOE_WINDOW)
    meta = jnp.where(c128 == MOE_META_FITS, jnp.where(fits, 1, 0),
                     jnp.where(c128 == MOE_META_WINDOW, win, jnp.where(hit, 1, 0)))
    flags_ref[0] = meta[0:8].astype(jnp.int32)


def _route_sort_call(logits):
    n = logits.shape[0]
    tm = MOE_TM
    row = lambda t: (t, 0)
    return pl.pallas_call(
        _route_sort_kernel,
        grid=(n // tm,),
        in_specs=[pl.BlockSpec((tm, ROUTER_PAD), row)],
        out_specs=[
            pl.BlockSpec((tm, ROUTER_PAD), row),
            pl.BlockSpec((tm, LANES), row),
            pl.BlockSpec((1, 8, LANES), lambda t: (t, 0, 0)),
        ],
        out_shape=[
            jax.ShapeDtypeStruct((n, ROUTER_PAD), F32),
            jax.ShapeDtypeStruct((n, LANES), F32),
            jax.ShapeDtypeStruct((n // tm, 8, LANES), jnp.int32),
        ],
        scratch_shapes=[pltpu.VMEM((tm, tm), BF16)],
        compiler_params=pltpu.CompilerParams(
            dimension_semantics=("arbitrary",), vmem_limit_bytes=VMEM_LIMIT),
        name="route_sort",
    )(logits)


def _moe_kernel(flags_ref, hb_ref, h_ref, gates_ref, pos_ref, w1_ref, w3_ref, w2_ref, g_ref, b_ref,
                o_ref, xs_ref, gs_ref, acc_ref, pt_ref):
    tile = pl.program_id(0)
    step = pl.program_id(1)
    tm = hb_ref.shape[0]
    sub = MOE_SUB
    n_sub = tm // sub
    eps = MOE_EXPERTS_PER_STEP

    @pl.when(step == 0)
    def _():
        pos_b = pos_ref[...]
        pos_row = pos_b.T[0:1, :]
        g_hi, g_lo = _split_bf16(gates_ref[...], 2)
        src = jnp.concatenate([hb_ref[...], g_hi, g_lo], axis=1)
        for c0 in range(0, tm, sub):
            slot = (lax.broadcasted_iota(jnp.int32, (sub, tm), 0) + c0).astype(F32)
            p_c = jnp.where(slot == pos_row, 1.0, 0.0).astype(BF16)
            moved = _dot(p_c, src)
            xs_ref[c0:c0 + sub, :] = moved[:, :D_MODEL].astype(BF16)
            gs_ref[c0:c0 + sub, :] = moved[:, D_MODEL:D_MODEL + LANES] + moved[:, D_MODEL + LANES:]
        for c0 in range(0, tm, LANES):
            slot = (lax.broadcasted_iota(jnp.int32, (tm, LANES), 1) + c0).astype(F32)
            pt_ref[:, c0:c0 + LANES] = jnp.where(pos_b == slot, 1.0, 0.0).astype(BF16)
        acc_ref[...] = jnp.zeros_like(acc_ref)

    group = step // (EXPERTS_PER_GROUP // eps)

    def visit(rows):
        x_r = xs_ref[rows, :]
        g_r = gs_ref[rows, :]
        lane = lax.broadcasted_iota(jnp.int32, g_r.shape, 1)
        acc = acc_ref[rows, :]
        for e in range(eps):
            gate_e = jnp.sum(jnp.where(lane == step * eps + e, g_r, 0.0), axis=-1, keepdims=True)
            a1 = _dot(x_r, w1_ref[e])
            a3 = _dot(x_r, w3_ref[e])
            hid = (a1 * _sigmoid(a1)) * a3 * gate_e
            acc = acc + _dot(hid.astype(BF16), w2_ref[e])
        acc_ref[rows, :] = acc

    base = (tile * N_GROUPS + group) * (n_sub + 2)
    fits = flags_ref[base + n_sub] != 0

    @pl.when(fits)
    def _():
        first = pl.multiple_of(flags_ref[base + n_sub + 1], MOE_WINDOW_ALIGN)
        visit(pl.ds(first, MOE_WINDOW))

    for r in range(n_sub):
        pl.when(jnp.logical_not(fits) & (flags_ref[base + r] != 0))(
            functools.partial(visit, slice(r * sub, (r + 1) * sub)))

    @pl.when(step == N_EXPERTS // eps - 1)
    def _():
        ffn = _dot(pt_ref[...], acc_ref[...].astype(BF16))
        o_ref[...] = _layer_norm(DEEPNORM_ALPHA * h_ref[...] + ffn, g_ref[...], b_ref[...])


def _moe_call(flags, h_bf16, h_f32, gates, pos, w1, w3, w2, ln_g, ln_b):
    n = h_f32.shape[0]
    tm = MOE_TM
    row = lambda t, s, f: (t, 0)
    const = lambda t, s, f: (0, 0)
    wmap = lambda t, s, f: (s, 0, 0)
    eps = MOE_EXPERTS_PER_STEP
    assert EXPERTS_PER_GROUP % eps == 0 and tm % MOE_SUB == 0
    grid_spec = pltpu.PrefetchScalarGridSpec(
        num_scalar_prefetch=1,
        grid=(n // tm, N_EXPERTS // eps),
        in_specs=[
            pl.BlockSpec((tm, D_MODEL), row),
            pl.BlockSpec((tm, D_MODEL), row),
            pl.BlockSpec((tm, ROUTER_PAD), row),
            pl.BlockSpec((tm, LANES), row),
            pl.BlockSpec((eps, D_MODEL, D_EXPERT), wmap),
            pl.BlockSpec((eps, D_MODEL, D_EXPERT), wmap),
            pl.BlockSpec((eps, D_EXPERT, D_MODEL), wmap),
            pl.BlockSpec((1, D_MODEL), const),
            pl.BlockSpec((1, D_MODEL), const),
        ],
        out_specs=pl.BlockSpec((tm, D_MODEL), row),
        scratch_shapes=[
            pltpu.VMEM((tm, D_MODEL), BF16),
            pltpu.VMEM((tm, ROUTER_PAD), F32),
            pltpu.VMEM((tm, D_MODEL), F32),
            pltpu.VMEM((tm, tm), BF16),
        ],
    )
    return pl.pallas_call(
        _moe_kernel,
        grid_spec=grid_spec,
        out_shape=jax.ShapeDtypeStruct((n, D_MODEL), F32),
        compiler_params=pltpu.CompilerParams(
            dimension_semantics=("arbitrary", "arbitrary"), vmem_limit_bytes=VMEM_LIMIT),
        name="hier_moe_ln",
    )(flags, h_bf16, h_f32, gates, pos, w1, w3, w2, ln_g, ln_b)


def _pad_cols(w, width):
    return jnp.pad(w, ((0, 0), (0, width - w.shape[1])))


def _pad_rows(w, height):
    return jnp.pad(w, ((0, height - w.shape[0]), (0, 0)))


def kernel(x, w_in, mu_shift, w0, w_lora_up, a0, a_lora_up, g_lora_up, k_k, k_a, r_k, gn_w, gn_b, w_out, ln1_g, ln1_b, w_group, b_group, w_expert, b_expert, w1_exp, w3_exp, w2_exp, ln2_g, ln2_b):
    batch, seq_len, d = x.shape
    assert d == D_MODEL
    n = batch * seq_len
    x2 = x.reshape(n, d)

    c_rkv = 3 * RWKV_WIDTH
    c_wd = c_rkv + DECAY_RANK
    c_ad = c_wd + AAA_RANK
    c_gd = c_ad + GATE_RANK
    w_cat = jnp.concatenate([
        w_in[:, :c_rkv], _pad_cols(w_in[:, c_rkv:c_gd], LORA_PAD), w_in[:, c_gd:],
    ], axis=1).astype(BF16)
    mu2 = mu_shift[None, :]
    mu_cat = jnp.concatenate([mu2[:, :c_rkv], _pad_cols(mu2[:, c_rkv:c_gd], LORA_PAD)], axis=1)
    p_rkv, p_lora, p_moba = _inproj_call(x2, w_cat, mu_cat, seq_len)

    place = lambda w, first: jnp.pad(w, ((first, LORA_PAD - first - w.shape[0]), (0, 0)))
    lora_up = (place(w_lora_up, 0), place(a_lora_up, DECAY_RANK), place(g_lora_up, DECAY_RANK + AAA_RANK))

    vecs = jnp.stack([w0, a0, k_k, k_a, r_k.reshape(-1), gn_w, gn_b, jnp.zeros_like(w0)], axis=0)
    head_id = jnp.arange(2 * PAIR) // HEAD_DIM
    bd = (head_id[:, None] == head_id[None, :]).astype(BF16)
    y_a = _rwkv_call(p_rkv, p_lora, vecs, *lora_up, bd, batch, seq_len)

    y_b = _moba_call(p_moba, _moba_key_aug(seq_len), batch, seq_len)

    w_out_b = w_out.astype(BF16)
    w_router = _pad_cols(jnp.concatenate([w_expert, w_group], axis=1), ROUTER_PAD)
    b_router = _pad_cols(jnp.concatenate([b_expert, b_group])[None, :], ROUTER_PAD)
    h1, h1_b, logits = _outproj_call(y_a, y_b, x2, w_out_b[:RWKV_WIDTH], w_out_b[RWKV_WIDTH:],
                                     ln1_g[None, :], ln1_b[None, :], w_router, b_router)

    flat = lambda w: w.astype(BF16).reshape((N_EXPERTS,) + w.shape[2:])
    gates, pos, flags = _route_sort_call(logits)
    flags = jnp.concatenate([flags[:, :N_GROUPS, :MOE_TM // MOE_SUB],
                             flags[:, :N_GROUPS, MOE_META_FITS:MOE_META_WINDOW + 1]], axis=-1).reshape(-1)
    out = _moe_call(flags, h1_b, h1, gates, pos, flat(w1_exp), flat(w3_exp), flat(w2_exp),
                    ln2_g[None, :], ln2_b[None, :])
    return out.reshape(batch, seq_len, d)
```

```python
import functools
import math

import jax
import jax.numpy as jnp
import numpy as np
from jax import lax
from jax.experimental import pallas as pl
from jax.experimental.pallas import tpu as pltpu

F32 = jnp.float32
BF16 = jnp.bfloat16

D_MODEL = 1024
HEAD_DIM = 64
RWKV_WIDTH = 512
MOBA_WIDTH = 512
DECAY_RANK = 32
AAA_RANK = 32
GATE_RANK = 96
GN_EPS = 64e-5
L2_EPS = 1e-12
MOBA_BLOCK = 256
MOBA_TOPK = 3
N_GROUPS = 4
EXPERTS_PER_GROUP = 8
N_EXPERTS = N_GROUPS * EXPERTS_PER_GROUP
D_EXPERT = 256
LN_EPS = 1e-5
DEEPNORM_ALPHA = float(2.0 ** 0.25)
NEG_INF = -1e30
F32_LOWEST = -3.0e38

LANES = 128
PAIR = 2 * HEAD_DIM
N_PAIRS = RWKV_WIDTH // PAIR
LORA_PAD = 2 * LANES
RWKV_COLS_PAD = 3 * RWKV_WIDTH + LORA_PAD
IN_COLS_PAD = RWKV_COLS_PAD + 3 * MOBA_WIDTH
VMEM_LIMIT = 56 * 1024 * 1024

INPROJ_TM = 512
INPROJ_TN = 256
RWKV_CHUNK = 64
RWKV_CHUNKS_PER_STEP = 4
RWKV_PASSES = 1
RWKV_STATE_PASSES = 1
OUTPROJ_TM = 512
MOE_TM = 1024
MOE_EXPERTS_PER_STEP = 4
MOE_SUB = 256
MOE_WINDOW = 384
MOE_WINDOW_ALIGN = 128
MOE_META_FITS = 16
MOE_META_WINDOW = 17
MOBA_KV_TILE = 512
MOBA_Q_TILE = 1024
MOBA_V_ROWS = HEAD_DIM + 16
MOBA_ALIBI_PARTS = 3
LOG2_E = 1.4426950408889634
ROUTER_PAD = LANES
GROUP_LANE0 = N_EXPERTS

NN = (((1,), (0,)), ((), ()))
NT = (((1,), (1,)), ((), ()))


def _dot(a, b, dims=NN, precision=None):
    return lax.dot_general(a, b, dims, precision=precision, preferred_element_type=F32)


def _split_bf16(x, parts):
    out = []
    rem = x
    for i in range(parts):
        p = rem.astype(BF16)
        out.append(p)
        if i + 1 < parts:
            rem = rem - p.astype(F32)
    return out


def _mm(a, b, dims=NN, passes=3):
    if passes == 1:
        return _dot(a.astype(BF16), b.astype(BF16), dims)
    if passes == 6:
        return _dot(a, b, dims, precision=lax.Precision.HIGHEST)
    a_hi, a_lo = _split_bf16(a, 2)
    b_hi, b_lo = _split_bf16(b, 2)
    return _dot(a_hi, b_hi, dims) + (_dot(a_hi, b_lo, dims) + _dot(a_lo, b_hi, dims))


def _mm_exact_lhs(a_bf16, b, dims=NN):
    b1, b2, b3 = _split_bf16(b, 3)
    return _dot(a_bf16, b1, dims) + (_dot(a_bf16, b2, dims) + _dot(a_bf16, b3, dims))


def _mm_exact_rhs(a, b_bf16, dims=NN):
    a1, a2, a3 = _split_bf16(a, 3)
    return _dot(a1, b_bf16, dims) + (_dot(a2, b_bf16, dims) + _dot(a3, b_bf16, dims))


def _inproj_kernel(x_ref, w_ref, mu_ref, prkv_ref, plora_ref, pm_ref, carry_ref, *, tiles_per_seq):
    tm = x_ref.shape[0]
    xb = x_ref[...].astype(BF16)
    seq_start = (pl.program_id(0) % tiles_per_seq) == 0
    row0 = lax.broadcasted_iota(jnp.int32, (tm, INPROJ_TN), 0) == 0
    n_shift_tiles = RWKV_COLS_PAD // INPROJ_TN
    for j in range(n_shift_tiles):
        c0 = j * INPROJ_TN
        acc = _dot(xb, w_ref[:, c0:c0 + INPROJ_TN])
        prev_last = jnp.where(seq_start, 0.0, carry_ref[0:1, c0:c0 + INPROJ_TN])
        shifted = jnp.where(row0, prev_last, pltpu.roll(acc, 1, 0))
        carry_ref[0:1, c0:c0 + INPROJ_TN] = acc[tm - 1:tm, :]
        out = acc + (shifted - acc) * mu_ref[:, c0:c0 + INPROJ_TN]
        if c0 < 3 * RWKV_WIDTH:
            prkv_ref[:, c0:c0 + INPROJ_TN] = out
        else:
            plora_ref[:, c0 - 3 * RWKV_WIDTH:c0 - 3 * RWKV_WIDTH + INPROJ_TN] = out
    for j in range(3 * MOBA_WIDTH // INPROJ_TN):
        c0 = j * INPROJ_TN
        acc = _dot(xb, w_ref[:, RWKV_COLS_PAD + c0:RWKV_COLS_PAD + c0 + INPROJ_TN])
        pm_ref[:, c0:c0 + INPROJ_TN] = acc.astype(BF16)


def _inproj_call(x2, w_cat, mu_cat, seq_len):
    n = x2.shape[0]
    tm = INPROJ_TM
    assert seq_len % tm == 0 and (3 * RWKV_WIDTH) % INPROJ_TN == 0
    return pl.pallas_call(
        functools.partial(_inproj_kernel, tiles_per_seq=seq_len // tm),
        grid=(n // tm,),
        in_specs=[
            pl.BlockSpec((tm, D_MODEL), lambda i: (i, 0)),
            pl.BlockSpec((D_MODEL, IN_COLS_PAD), lambda i: (0, 0)),
            pl.BlockSpec((1, RWKV_COLS_PAD), lambda i: (0, 0)),
        ],
        out_specs=[
            pl.BlockSpec((tm, 3 * RWKV_WIDTH), lambda i: (i, 0)),
            pl.BlockSpec((tm, LORA_PAD), lambda i: (i, 0)),
            pl.BlockSpec((tm, 3 * MOBA_WIDTH), lambda i: (i, 0)),
        ],
        out_shape=[
            jax.ShapeDtypeStruct((n, 3 * RWKV_WIDTH), F32),
            jax.ShapeDtypeStruct((n, LORA_PAD), F32),
            jax.ShapeDtypeStruct((n, 3 * MOBA_WIDTH), BF16),
        ],
        scratch_shapes=[pltpu.VMEM((8, RWKV_COLS_PAD), F32)],
        compiler_params=pltpu.CompilerParams(
            dimension_semantics=("arbitrary",), vmem_limit_bytes=VMEM_LIMIT),
        name="inproj_shift",
    )(x2, w_cat, mu_cat)


def _softplus(z):
    return jnp.maximum(z, 0.0) + jnp.log(1.0 + jnp.exp(-jnp.abs(z)))


def _sigmoid(z):
    return 1.0 / (1.0 + jnp.exp(-z))


def _rwkv_chunks(rt, kt, at, bt, v, d_incl, s_prev, passes, state_passes):
    c = RWKV_CHUNK
    n_chunks = rt.shape[0] // c
    n_pairs = len(s_prev)
    row = lax.broadcasted_iota(jnp.int32, (c, PAIR), 0)
    col = lax.broadcasted_iota(jnp.int32, (c, PAIR), 1) % HEAD_DIM
    strict = row > col
    incl = row >= col
    eye_c = (row == col).astype(F32)
    lane = lax.broadcasted_iota(jnp.int32, (1, PAIR), 1)
    head0 = lane < HEAD_DIM
    head1 = jnp.logical_not(head0)
    prow = lax.broadcasted_iota(jnp.int32, (PAIR, PAIR), 0)
    pcol = lax.broadcasted_iota(jnp.int32, (PAIR, PAIR), 1)
    same_head = (prow < HEAD_DIM) == (pcol < HEAD_DIM)
    eye_p = (prow == pcol).astype(F32)
    rows = [slice(ci * c, (ci + 1) * c) for ci in range(n_chunks)]
    sl = [slice(p * PAIR, (p + 1) * PAIR) for p in range(n_pairs)]
    pairs = [(ci, p) for ci in range(n_chunks) for p in range(n_pairs)]
    cut = lambda t, ci, p: t[rows[ci], sl[p]]

    def by_head(m):
        return jnp.concatenate([jnp.where(head0, m, 0.0), jnp.where(head1, m, 0.0)], axis=0)

    def by_head2(m, n):
        return jnp.concatenate([by_head(m), by_head(n)], axis=1)

    at_p = {k_: cut(at, *k_) for k_ in pairs}
    rt_p = {k_: cut(rt, *k_) for k_ in pairs}
    bt_p = {k_: cut(bt, *k_) for k_ in pairs}
    kt_p = {k_: cut(kt, *k_) for k_ in pairs}
    v_p = {k_: cut(v, *k_) for k_ in pairs}

    z = {k_: _mm(jnp.concatenate([at_p[k_], rt_p[k_]], axis=0),
                 jnp.concatenate([by_head(bt_p[k_]).T, by_head(kt_p[k_]).T], axis=1), NN, passes)
         for k_ in pairs}
    l_ab = {k_: jnp.where(strict, z[k_][:c, :PAIR], 0.0) for k_ in pairs}
    l_ak = {k_: jnp.where(strict, z[k_][:c, PAIR:], 0.0) for k_ in pairs}
    m_rb = {k_: jnp.where(incl, z[k_][c:, :PAIR], 0.0) for k_ in pairs}
    m_rk = {k_: jnp.where(incl, z[k_][c:, PAIR:], 0.0) for k_ in pairs}
    pw = {k_: _mm(l_ab[k_], by_head(l_ab[k_]), NN, passes) for k_ in pairs}
    t_inv = {k_: eye_c + l_ab[k_] for k_ in pairs}
    for _ in range(int(math.log2(c)) - 1):
        tp = {k_: _mm(jnp.concatenate([t_inv[k_], pw[k_]], axis=0), by_head(pw[k_]), NN, passes)
              for k_ in pairs}
        t_inv = {k_: t_inv[k_] + tp[k_][:c] for k_ in pairs}
        pw = {k_: tp[k_][c:] for k_ in pairs}
    lm = {k_: _mm(jnp.concatenate([l_ak[k_], m_rk[k_]], axis=0), by_head(v_p[k_]), NN, passes)
          for k_ in pairs}
    lv = {k_: lm[k_][:c] for k_ in pairs}
    mv = {k_: lm[k_][c:] for k_ in pairs}
    wu = {k_: _mm(t_inv[k_], by_head2(at_p[k_], lv[k_]), NN, passes) for k_ in pairs}
    qy = {k_: _mm(m_rb[k_], by_head2(wu[k_][:, :PAIR], wu[k_][:, PAIR:]), NN, passes) for k_ in pairs}

    qeff, y1, phi, psi = {}, {}, {}, {}
    for ci, p in pairs:
        k_ = (ci, p)
        w, u0 = wu[k_][:, :PAIR], wu[k_][:, PAIR:]
        qeff[k_] = rt_p[k_] + qy[k_][:, :PAIR]
        y1[k_] = qy[k_][:, PAIR:] + mv[k_]
        d_p = d_incl[(ci + 1) * c - 1:(ci + 1) * c, sl[p]]
        phi[k_] = jnp.where(same_head, (eye_p + _mm(w.T, bt_p[k_], NN, passes)) * d_p, 0.0)
        uv_t = jnp.concatenate([u0, v_p[k_]], axis=0).T
        bk = jnp.concatenate([bt_p[k_], kt_p[k_]], axis=0)
        psi[k_] = jnp.where(same_head, _mm(uv_t, bk, NN, passes) * d_p, 0.0)

    state = list(s_prev)
    ys = [[None] * n_pairs for _ in range(n_chunks)]
    for ci in range(n_chunks):
        for p in range(n_pairs):
            ys[ci][p] = _mm(qeff[ci, p], state[p].T, NN, state_passes) + y1[ci, p]
            state[p] = _mm(state[p], phi[ci, p], NN, state_passes) + psi[ci, p]
    y = jnp.concatenate([jnp.concatenate(ys[ci], axis=1) for ci in range(n_chunks)], axis=0)
    return y, state


def _rwkv_kernel(prkv_ref, plora_ref, vec_ref, wl_ref, al_ref, gl_ref, bd_ref, y_ref, s_ref):
    rows = prkv_ref.shape[0]
    c = RWKV_CHUNK
    width = RWKV_WIDTH

    @pl.when(pl.program_id(1) == 0)
    def _():
        s_ref[...] = jnp.zeros_like(s_ref)

    r = prkv_ref[:, 0:width]
    k_raw = prkv_ref[:, width:2 * width]
    v = prkv_ref[:, 2 * width:3 * width]
    p_wd = p_ad = p_gd = plora_ref[...]
    w0 = vec_ref[0:1, :]
    a0 = vec_ref[1:2, :]
    k_k = vec_ref[2:3, :]
    k_a = vec_ref[3:4, :]
    r_k = vec_ref[4:5, :]
    gn_w = vec_ref[5:6, :]
    gn_b = vec_ref[6:7, :]
    bd = bd_ref[...]

    def seg_sum(z):
        halves = []
        for c0 in range(0, width, bd.shape[0]):
            z_hi, z_lo = _split_bf16(z[:, c0:c0 + bd.shape[0]], 2)
            halves.append(_dot(z_hi, bd) + _dot(z_lo, bd))
        return jnp.concatenate(halves, axis=1)

    w_log = -_softplus(-(w0 + _mm(jnp.tanh(p_wd), wl_ref[...], NN, 3))) - 0.5
    log_w = -jnp.exp(w_log)
    a = _sigmoid(a0 + _mm(p_ad, al_ref[...], NN, RWKV_PASSES))
    g = _mm(_sigmoid(p_gd), gl_ref[...], NN, RWKV_PASSES)
    kk = k_raw * k_k
    kk = kk / jnp.maximum(jnp.sqrt(seg_sum(kk * kk)), L2_EPS)
    k = k_raw * (1.0 + (a - 1.0) * k_a)

    row = lax.broadcasted_iota(jnp.int32, (rows, rows), 0)
    col = lax.broadcasted_iota(jnp.int32, (rows, rows), 1)
    tri = ((row >= col) & (row // c == col // c)).astype(BF16)
    cum = _mm_exact_lhs(tri, log_w)
    d_incl = jnp.exp(cum)
    d_inv = jnp.exp(-cum)
    d_excl = jnp.exp(cum - log_w)
    rt = r * d_incl
    kt = k * d_inv
    at = -kk * d_excl
    bt = kk * a * d_inv

    y, s_next = _rwkv_chunks(rt, kt, at, bt, v, d_incl, [s_ref[p] for p in range(N_PAIRS)],
                             RWKV_PASSES, RWKV_STATE_PASSES)
    for p in range(N_PAIRS):
        s_ref[p] = s_next[p]

    inv_n = 1.0 / HEAD_DIM
    mu = seg_sum(y) * inv_n
    yc = y - mu
    var = seg_sum(yc * yc) * inv_n
    yn = yc * lax.rsqrt(var + GN_EPS) * gn_w + gn_b
    bonus = seg_sum(r * k * r_k) * v
    y_ref[...] = ((yn + bonus) * g).astype(y_ref.dtype)


def _rwkv_call(p_rkv, p_lora, vecs, wl, al, gl, bd, batch, seq_len):
    n = p_rkv.shape[0]
    rows = RWKV_CHUNK * RWKV_CHUNKS_PER_STEP
    assert seq_len % rows == 0
    steps = seq_len // rows
    row_map = lambda b, i: (b * steps + i, 0)
    const = lambda b, i: (0, 0)
    return pl.pallas_call(
        _rwkv_kernel,
        grid=(batch, steps),
        in_specs=[
            pl.BlockSpec((rows, 3 * RWKV_WIDTH), row_map),
            pl.BlockSpec((rows, LORA_PAD), row_map),
            pl.BlockSpec((8, RWKV_WIDTH), const),
            pl.BlockSpec((LORA_PAD, RWKV_WIDTH), const),
            pl.BlockSpec((LORA_PAD, RWKV_WIDTH), const),
            pl.BlockSpec((LORA_PAD, RWKV_WIDTH), const),
            pl.BlockSpec((2 * PAIR, 2 * PAIR), const),
        ],
        out_specs=pl.BlockSpec((rows, RWKV_WIDTH), row_map),
        out_shape=jax.ShapeDtypeStruct((n, RWKV_WIDTH), BF16),
        scratch_shapes=[pltpu.VMEM((N_PAIRS, PAIR, PAIR), F32)],
        compiler_params=pltpu.CompilerParams(
            dimension_semantics=("arbitrary", "arbitrary"), vmem_limit_bytes=VMEM_LIMIT),
        name="rwkv7_chunked",
    )(p_rkv, p_lora, vecs, wl, al, gl, bd)


def _moba_kernel(q_ref, k_ref, v_ref, kaug_ref, o_ref, kmean_ref, vt_ref,
                 m0_ref, m1_ref, acc0_ref, acc1_ref, s_even_ref, s_odd_ref,
                 smax_even_ref, smax_odd_ref, *, n_blocks):
    blk = MOBA_BLOCK
    tk = MOBA_KV_TILE
    tq = q_ref.shape[0]
    i = pl.program_id(2)
    nb_pad = kmean_ref.shape[0]
    m_refs, acc_refs = (m0_ref, m1_ref), (acc0_ref, acc1_ref)

    @pl.when(i == 0)
    def _():
        kmean_ref[...] = jnp.zeros_like(kmean_ref)

        def mean_body(n, carry):
            off = pl.multiple_of(n * blk, blk)
            kb = k_ref[pl.ds(off, blk), :].astype(F32)
            kmean_ref[pl.ds(n, 1), :] = jnp.sum(kb, axis=0, keepdims=True) * (1.0 / blk)
            return carry
        lax.fori_loop(0, n_blocks, mean_body, 0)

        ones = jnp.ones((MOBA_V_ROWS - HEAD_DIM, tk), BF16)

        def vt_body(j, carry):
            off = pl.multiple_of(j * tk, tk)
            v_t = v_ref[pl.ds(off, tk), :].astype(F32).T.astype(BF16)
            for h in (0, 1):
                vt_ref[j, h] = jnp.concatenate([v_t[h * HEAD_DIM:(h + 1) * HEAD_DIM], ones], axis=0)
            return carry
        lax.fori_loop(0, vt_ref.shape[0], vt_body, 0)

    q_t = q_ref[...].astype(F32).T
    chan = lax.broadcasted_iota(jnp.int32, (PAIR, tq), 0)
    blk_row = lax.broadcasted_iota(jnp.int32, (nb_pad, tq), 0)
    own_blk = (i * tq + lax.broadcasted_iota(jnp.int32, (nb_pad, tq), 1)) // blk
    past = blk_row < own_blk
    aug_row = lax.broadcasted_iota(jnp.int32, (LANES, tq), 0)
    ones_rows = (aug_row >= n_blocks) & (aug_row < n_blocks + MOBA_ALIBI_PARTS)
    kmean = kmean_ref[...]

    qa_t = []
    for h in (0, 1):
        qh_t = jnp.where((chan < HEAD_DIM) == (h == 0), q_t, 0.0)
        gate = _dot(kmean, qh_t, NN, precision=lax.Precision.HIGHEST)
        gate = jnp.where(past, gate, F32_LOWEST)
        sel = jnp.zeros(gate.shape, jnp.bool_)
        for _ in range(MOBA_TOPK):
            mx = jnp.max(gate, axis=0, keepdims=True)
            first = jnp.min(jnp.where(gate == mx, blk_row, nb_pad), axis=0, keepdims=True)
            pick = (blk_row == first) & (mx > F32_LOWEST)
            sel = sel | pick
            gate = jnp.where(pick, F32_LOWEST, gate)
        sel_bias = jnp.where(past & jnp.logical_not(sel), NEG_INF, 0.0)
        aug_t = jnp.concatenate([sel_bias, jnp.zeros((LANES - nb_pad, tq), F32)], axis=0)
        aug_t = jnp.where(ones_rows, 1.0, aug_t)
        qa_t.append(jnp.concatenate([qh_t * (LOG2_E / math.sqrt(HEAD_DIM)), aug_t], axis=0).astype(BF16))

    def tile_scores(j, lanes=slice(None)):
        off = pl.multiple_of(j * tk, tk)
        k_t = k_ref[pl.ds(off, tk), :]
        return [_dot(jnp.concatenate([k_t, kaug_ref[h, pl.ds(off, tk), :]], axis=1), qa_t[h][:, lanes])
                for h in (0, 1)]

    def put_scores(buf, s, lanes=slice(None)):
        for h in (0, 1):
            buf[0][h, :, lanes] = s[h]
            buf[1][h, :, lanes] = jnp.max(s[h], axis=0, keepdims=True)

    def tile_update(j, buf, lanes=slice(None)):
        s_buf, smax_buf = buf
        for h in (0, 1):
            m_old = m_refs[h][:, lanes]
            m_new = jnp.maximum(m_old, smax_buf[h, :, lanes])
            p = jnp.exp2(s_buf[h, :, lanes] - m_new).astype(BF16)
            pv = _dot(vt_ref[j, h], p)
            acc_refs[h][:, lanes] = jnp.exp2(m_old - m_new) * acc_refs[h][:, lanes] + pv
            m_refs[h][:, lanes] = m_new

    for h in (0, 1):
        m_refs[h][...] = jnp.full(m_refs[h].shape, F32_LOWEST, F32)
        acc_refs[h][...] = jnp.zeros(acc_refs[h].shape, F32)

    buffers = ((s_even_ref, smax_even_ref), (s_odd_ref, smax_odd_ref))
    n_own = tq // tk
    j_first = i * n_own
    diagonal = (lax.broadcasted_iota(jnp.int32, (tk, tk), 0) <= lax.broadcasted_iota(jnp.int32, (tk, tk), 1))

    def own_scores(g):
        s = tile_scores(j_first + g, slice(g * tk, tq))
        own = [jnp.where(diagonal, s_h[:, :tk], NEG_INF) for s_h in s]
        if g == n_own - 1:
            return own
        return [jnp.concatenate([own_h, s_h[:, tk:]], axis=1) for own_h, s_h in zip(own, s)]

    put_scores(buffers[0], own_scores(n_own - 1), slice((n_own - 1) * tk, tq))
    for m in range(1, n_own):
        g = n_own - 1 - m
        put_scores(buffers[m % 2], own_scores(g), slice(g * tk, tq))
        tile_update(j_first + g + 1, buffers[(m - 1) % 2], slice((g + 1) * tk, tq))
    cur, nxt = buffers[(n_own - 1) % 2], buffers[n_own % 2]

    def previous_tile(j):
        return jnp.where(j == 0, j_first, j - 1)

    def pipelined_step(j, src, dst):
        put_scores(dst, tile_scores(j))
        tile_update(previous_tile(j), src)

    def kv_pair_step(u, carry):
        pipelined_step(2 * u, cur, nxt)
        pipelined_step(2 * u + 1, nxt, cur)
        return carry
    lax.fori_loop(0, j_first // 2, kv_pair_step, 0)

    @pl.when(j_first % 2 == 1)
    def _():
        pipelined_step(j_first - 1, cur, nxt)
        tile_update(j_first - 1, nxt)

    @pl.when(j_first % 2 == 0)
    def _():
        tile_update(previous_tile(j_first), cur)

    out_t = jnp.concatenate([acc_refs[h][0:HEAD_DIM, :] / acc_refs[h][HEAD_DIM:HEAD_DIM + 1, :]
                             for h in (0, 1)], axis=0)
    o_ref[...] = out_t.T.astype(o_ref.dtype)


def _moba_call(qkv, kaug, batch, seq_len):
    n = qkv.shape[0]
    blk = MOBA_BLOCK
    tq = MOBA_Q_TILE
    nb = seq_len // blk
    nq = seq_len // tq
    assert nb + MOBA_ALIBI_PARTS <= LANES and seq_len % MOBA_KV_TILE == 0
    assert tq % MOBA_KV_TILE == 0 and seq_len % tq == 0
    lane_groups = MOBA_WIDTH // LANES
    return pl.pallas_call(
        functools.partial(_moba_kernel, n_blocks=nb),
        grid=(batch, N_PAIRS, nq),
        in_specs=[
            pl.BlockSpec((tq, PAIR), lambda b, p, i: (b * nq + i, p)),
            pl.BlockSpec((seq_len, PAIR), lambda b, p, i: (b, lane_groups + p)),
            pl.BlockSpec((seq_len, PAIR), lambda b, p, i: (b, 2 * lane_groups + p)),
            pl.BlockSpec((2, seq_len, LANES), lambda b, p, i: (p, 0, 0)),
        ],
        out_specs=pl.BlockSpec((tq, PAIR), lambda b, p, i: (b * nq + i, p)),
        out_shape=jax.ShapeDtypeStruct((n, MOBA_WIDTH), BF16),
        scratch_shapes=[
            pltpu.VMEM((-(-nb // 8) * 8, PAIR), F32),
            pltpu.VMEM((seq_len // MOBA_KV_TILE, 2, MOBA_V_ROWS, MOBA_KV_TILE), BF16),
            pltpu.VMEM((1, tq), F32), pltpu.VMEM((1, tq), F32),
            pltpu.VMEM((MOBA_V_ROWS, tq), F32), pltpu.VMEM((MOBA_V_ROWS, tq), F32),
            pltpu.VMEM((2, MOBA_KV_TILE, tq), F32), pltpu.VMEM((2, MOBA_KV_TILE, tq), F32),
            pltpu.VMEM((2, 1, tq), F32), pltpu.VMEM((2, 1, tq), F32),
        ],
        compiler_params=pltpu.CompilerParams(
            dimension_semantics=("arbitrary", "arbitrary", "arbitrary"),
            vmem_limit_bytes=VMEM_LIMIT),
        name="moba_attention",
    )(qkv, qkv, qkv, kaug)


def _moba_key_aug(seq_len):
    nb = seq_len // MOBA_BLOCK
    heads = MOBA_WIDTH // HEAD_DIM
    pos = np.arange(seq_len, dtype=np.int32)
    slopes = (2.0 ** (-8.0 * (np.arange(heads, dtype=np.float32) + 1.0) / heads)).astype(np.float32)
    aug = np.zeros((heads, seq_len, LANES), np.float32)
    aug[:, pos, pos // MOBA_BLOCK] = 1.0
    rem = (np.float32(LOG2_E) * slopes)[:, None] * pos.astype(np.float32)[None, :]
    for part in range(MOBA_ALIBI_PARTS):
        piece = (rem.view(np.uint32) & np.uint32(0xFFFF0000)).view(np.float32)
        aug[:, :, nb + part] = piece
        rem = rem - piece
    return jnp.asarray(aug.astype(BF16))


def _layer_norm(z, g, b):
    mu = jnp.mean(z, axis=-1, keepdims=True)
    zc = z - mu
    var = jnp.mean(zc * zc, axis=-1, keepdims=True)
    return zc * lax.rsqrt(var + LN_EPS) * g + b


def _outproj_kernel(ya_ref, yb_ref, x_ref, wa_ref, wb_ref, g_ref, b_ref, wr_ref, br_ref,
                    h_ref, hb_ref, lg_ref):
    mix = _dot(ya_ref[...], wa_ref[...]) + _dot(yb_ref[...], wb_ref[...])
    h = _layer_norm(DEEPNORM_ALPHA * x_ref[...] + mix, g_ref[...], b_ref[...])
    h_ref[...] = h
    hb_ref[...] = h.astype(BF16)
    lg_ref[...] = _mm(h, wr_ref[...], NN, 3) + br_ref[...]


def _outproj_call(y_a, y_b, x2, wa, wb, ln_g, ln_b, w_router, b_router):
    n = x2.shape[0]
    tm = OUTPROJ_TM
    row = lambda i: (i, 0)
    const = lambda i: (0, 0)
    return pl.pallas_call(
        _outproj_kernel,
        grid=(n // tm,),
        in_specs=[
            pl.BlockSpec((tm, RWKV_WIDTH), row),
            pl.BlockSpec((tm, MOBA_WIDTH), row),
            pl.BlockSpec((tm, D_MODEL), row),
            pl.BlockSpec((RWKV_WIDTH, D_MODEL), const),
            pl.BlockSpec((MOBA_WIDTH, D_MODEL), const),
            pl.BlockSpec((1, D_MODEL), const),
            pl.BlockSpec((1, D_MODEL), const),
            pl.BlockSpec((D_MODEL, ROUTER_PAD), const),
            pl.BlockSpec((1, ROUTER_PAD), const),
        ],
        out_specs=[
            pl.BlockSpec((tm, D_MODEL), row),
            pl.BlockSpec((tm, D_MODEL), row),
            pl.BlockSpec((tm, ROUTER_PAD), row),
        ],
        out_shape=[
            jax.ShapeDtypeStruct((n, D_MODEL), F32),
            jax.ShapeDtypeStruct((n, D_MODEL), BF16),
            jax.ShapeDtypeStruct((n, ROUTER_PAD), F32),
        ],
        compiler_params=pltpu.CompilerParams(
            dimension_semantics=("arbitrary",), vmem_limit_bytes=VMEM_LIMIT),
        name="outproj_ln_router",
    )(y_a, y_b, x2, wa, wb, ln_g, ln_b, w_router, b_router)


def _route(logits):
    lane = lax.broadcasted_iota(jnp.int32, logits.shape, 1)
    is_group = (lane >= GROUP_LANE0) & (lane < GROUP_LANE0 + N_GROUPS)
    gl = jnp.where(is_group, logits, F32_LOWEST)
    g_max = jnp.max(gl, axis=-1, keepdims=True)
    g_first = jnp.min(jnp.where(gl == g_max, lane, LANES), axis=-1, keepdims=True)
    g_exp = jnp.where(is_group, jnp.exp(gl - g_max), 0.0)
    p_g = 1.0 / jnp.sum(g_exp, axis=-1, keepdims=True)
    g_idx = g_first - GROUP_LANE0
    in_group = (lane >= g_idx * EXPERTS_PER_GROUP) & (lane < (g_idx + 1) * EXPERTS_PER_GROUP)
    el = jnp.where(in_group, logits, F32_LOWEST)
    e_max = jnp.max(el, axis=-1, keepdims=True)
    e_exp = jnp.where(in_group, jnp.exp(el - e_max), 0.0)
    e_prob = e_exp / jnp.sum(e_exp, axis=-1, keepdims=True)
    cand = jnp.where(in_group, e_prob, -1.0)
    v1 = jnp.max(cand, axis=-1, keepdims=True)
    i1 = jnp.min(jnp.where(cand == v1, lane, LANES), axis=-1, keepdims=True)
    pick1 = lane == i1
    cand2 = jnp.where(pick1, -1.0, cand)
    v2 = jnp.max(cand2, axis=-1, keepdims=True)
    i2 = jnp.min(jnp.where(cand2 == v2, lane, LANES), axis=-1, keepdims=True)
    pick2 = lane == i2
    denom = v1 + v2
    gates = jnp.where(pick1, v1 / denom * p_g, jnp.where(pick2, v2 / denom * p_g, 0.0))
    return gates, g_idx


def _route_sort_kernel(lg_ref, gates_ref, pos_ref, flags_ref, earlier_ref):
    tm = lg_ref.shape[0]

    @pl.when(pl.program_id(0) == 0)
    def _():
        row = lax.broadcasted_iota(jnp.int32, (tm, tm), 0)
        col = lax.broadcasted_iota(jnp.int32, (tm, tm), 1)
        earlier_ref[...] = jnp.where(row > col, 1.0, 0.0).astype(BF16)

    gates, g_idx = _route(lg_ref[...])
    lane = lax.broadcasted_iota(jnp.int32, (tm, LANES), 1)
    in_own = lane == g_idx
    onehot = jnp.where(in_own, 1.0, 0.0)
    rank = _dot(earlier_ref[...], onehot.astype(BF16))
    count = jnp.sum(onehot, axis=0, keepdims=True)
    r128 = lax.broadcasted_iota(jnp.int32, (LANES, LANES), 0)
    c128 = lax.broadcasted_iota(jnp.int32, (LANES, LANES), 1)
    below = jnp.where(r128 < c128, 1.0, 0.0)
    start = _dot(jnp.broadcast_to(count, (8, LANES)), below, NN,
                 precision=lax.Precision.HIGHEST)[0:1]
    pos = jnp.sum(jnp.where(in_own, rank + start, 0.0), axis=-1, keepdims=True)
    gates_ref[...] = gates
    pos_ref[...] = jnp.broadcast_to(pos, (tm, LANES))
    start_c = jnp.broadcast_to(start, (LANES, LANES)).T
    count_c = jnp.broadcast_to(count, (LANES, LANES)).T
    sub_lo = c128 * MOE_SUB
    hit = (count_c > 0.0) & (start_c < (sub_lo + MOE_SUB).astype(F32)) & (start_c + count_c > sub_lo.astype(F32))
    start_i = start_c.astype(jnp.int32)
    count_i = count_c.astype(jnp.int32)
    win = jnp.minimum((start_i // MOE_WINDOW_ALIGN) * MOE_WINDOW_ALIGN, tm - MOE_WINDOW)
    fits = (count_i > 0) & (start_i + count_i <= win + MOE_WINDOW)
    meta = jnp.where(c128 == MOE_META_FITS, jnp.where(fits, 1, 0),
                     jnp.where(c128 == MOE_META_WINDOW, win, jnp.where(hit, 1, 0)))
    flags_ref[0] = meta[0:8].astype(jnp.int32)


def _route_sort_call(logits):
    n = logits.shape[0]
    tm = MOE_TM
    row = lambda t: (t, 0)
    return pl.pallas_call(
        _route_sort_kernel,
        grid=(n // tm,),
        in_specs=[pl.BlockSpec((tm, ROUTER_PAD), row)],
        out_specs=[
            pl.BlockSpec((tm, ROUTER_PAD), row),
            pl.BlockSpec((tm, LANES), row),
            pl.BlockSpec((1, 8, LANES), lambda t: (t, 0, 0)),
        ],
        out_shape=[
            jax.ShapeDtypeStruct((n, ROUTER_PAD), F32),
            jax.ShapeDtypeStruct((n, LANES), F32),
            jax.ShapeDtypeStruct((n // tm, 8, LANES), jnp.int32),
        ],
        scratch_shapes=[pltpu.VMEM((tm, tm), BF16)],
        compiler_params=pltpu.CompilerParams(
            dimension_semantics=("arbitrary",), vmem_limit_bytes=VMEM_LIMIT),
        name="route_sort",
    )(logits)


def _moe_kernel(flags_ref, hb_ref, h_ref, gates_ref, pos_ref, w1_ref, w3_ref, w2_ref, g_ref, b_ref,
                o_ref, xs_ref, gs_ref, acc_ref, pt_ref):
    tile = pl.program_id(0)
    step = pl.program_id(1)
    tm = hb_ref.shape[0]
    sub = MOE_SUB
    n_sub = tm // sub
    eps = MOE_EXPERTS_PER_STEP

    @pl.when(step == 0)
    def _():
        pos_b = pos_ref[...]
        pos_row = pos_b.T[0:1, :]
        g_hi, g_lo = _split_bf16(gates_ref[...], 2)
        src = jnp.concatenate([hb_ref[...], g_hi, g_lo], axis=1)
        for c0 in range(0, tm, sub):
            slot = (lax.broadcasted_iota(jnp.int32, (sub, tm), 0) + c0).astype(F32)
            p_c = jnp.where(slot == pos_row, 1.0, 0.0).astype(BF16)
            moved = _dot(p_c, src)
            xs_ref[c0:c0 + sub, :] = moved[:, :D_MODEL].astype(BF16)
            gs_ref[c0:c0 + sub, :] = moved[:, D_MODEL:D_MODEL + LANES] + moved[:, D_MODEL + LANES:]
        for c0 in range(0, tm, LANES):
            slot = (lax.broadcasted_iota(jnp.int32, (tm, LANES), 1) + c0).astype(F32)
            pt_ref[:, c0:c0 + LANES] = jnp.where(pos_b == slot, 1.0, 0.0).astype(BF16)
        acc_ref[...] = jnp.zeros_like(acc_ref)

    group = step // (EXPERTS_PER_GROUP // eps)

    def visit(rows):
        x_r = xs_ref[rows, :]
        g_r = gs_ref[rows, :]
        lane = lax.broadcasted_iota(jnp.int32, g_r.shape, 1)
        acc = acc_ref[rows, :]
        for e in range(eps):
            gate_e = jnp.sum(jnp.where(lane == step * eps + e, g_r, 0.0), axis=-1, keepdims=True)
            a1 = _dot(x_r, w1_ref[e])
            a3 = _dot(x_r, w3_ref[e])
            hid = (a1 * _sigmoid(a1)) * a3 * gate_e
            acc = acc + _dot(hid.astype(BF16), w2_ref[e])
        acc_ref[rows, :] = acc

    base = (tile * N_GROUPS + group) * (n_sub + 2)
    fits = flags_ref[base + n_sub] != 0

    @pl.when(fits)
    def _():
        first = pl.multiple_of(flags_ref[base + n_sub + 1], MOE_WINDOW_ALIGN)
        visit(pl.ds(first, MOE_WINDOW))

    for r in range(n_sub):
        pl.when(jnp.logical_not(fits) & (flags_ref[base + r] != 0))(
            functools.partial(visit, slice(r * sub, (r + 1) * sub)))

    @pl.when(step == N_EXPERTS // eps - 1)
    def _():
        ffn = _dot(pt_ref[...], acc_ref[...].astype(BF16))
        o_ref[...] = _layer_norm(DEEPNORM_ALPHA * h_ref[...] + ffn, g_ref[...], b_ref[...])


def _moe_call(flags, h_bf16, h_f32, gates, pos, w1, w3, w2, ln_g, ln_b):
    n = h_f32.shape[0]
    tm = MOE_TM
    row = lambda t, s, f: (t, 0)
    const = lambda t, s, f: (0, 0)
    wmap = lambda t, s, f: (s, 0, 0)
    eps = MOE_EXPERTS_PER_STEP
    assert EXPERTS_PER_GROUP % eps == 0 and tm % MOE_SUB == 0
    grid_spec = pltpu.PrefetchScalarGridSpec(
        num_scalar_prefetch=1,
        grid=(n // tm, N_EXPERTS // eps),
        in_specs=[
            pl.BlockSpec((tm, D_MODEL), row),
            pl.BlockSpec((tm, D_MODEL), row),
            pl.BlockSpec((tm, ROUTER_PAD), row),
            pl.BlockSpec((tm, LANES), row),
            pl.BlockSpec((eps, D_MODEL, D_EXPERT), wmap),
            pl.BlockSpec((eps, D_MODEL, D_EXPERT), wmap),
            pl.BlockSpec((eps, D_EXPERT, D_MODEL), wmap),
            pl.BlockSpec((1, D_MODEL), const),
            pl.BlockSpec((1, D_MODEL), const),
        ],
        out_specs=pl.BlockSpec((tm, D_MODEL), row),
        scratch_shapes=[
            pltpu.VMEM((tm, D_MODEL), BF16),
            pltpu.VMEM((tm, ROUTER_PAD), F32),
            pltpu.VMEM((tm, D_MODEL), F32),
            pltpu.VMEM((tm, tm), BF16),
        ],
    )
    return pl.pallas_call(
        _moe_kernel,
        grid_spec=grid_spec,
        out_shape=jax.ShapeDtypeStruct((n, D_MODEL), F32),
        compiler_params=pltpu.CompilerParams(
            dimension_semantics=("arbitrary", "arbitrary"), vmem_limit_bytes=VMEM_LIMIT),
        name="hier_moe_ln",
    )(flags, h_bf16, h_f32, gates, pos, w1, w3, w2, ln_g, ln_b)


def _pad_cols(w, width):
    return jnp.pad(w, ((0, 0), (0, width - w.shape[1])))


def _pad_rows(w, height):
    return jnp.pad(w, ((0, height - w.shape[0]), (0, 0)))


def kernel(x, w_in, mu_shift, w0, w_lora_up, a0, a_lora_up, g_lora_up, k_k, k_a, r_k, gn_w, gn_b, w_out, ln1_g, ln1_b, w_group, b_group, w_expert, b_expert, w1_exp, w3_exp, w2_exp, ln2_g, ln2_b):
    batch, seq_len, d = x.shape
    assert d == D_MODEL
    n = batch * seq_len
    x2 = x.reshape(n, d)

    c_rkv = 3 * RWKV_WIDTH
    c_wd = c_rkv + DECAY_RANK
    c_ad = c_wd + AAA_RANK
    c_gd = c_ad + GATE_RANK
    w_cat = jnp.concatenate([
        w_in[:, :c_rkv], _pad_cols(w_in[:, c_rkv:c_gd], LORA_PAD), w_in[:, c_gd:],
    ], axis=1).astype(BF16)
    mu2 = mu_shift[None, :]
    mu_cat = jnp.concatenate([mu2[:, :c_rkv], _pad_cols(mu2[:, c_rkv:c_gd], LORA_PAD)], axis=1)
    p_rkv, p_lora, p_moba = _inproj_call(x2, w_cat, mu_cat, seq_len)

    place = lambda w, first: jnp.pad(w, ((first, LORA_PAD - first - w.shape[0]), (0, 0)))
    lora_up = (place(w_lora_up, 0), place(a_lora_up, DECAY_RANK), place(g_lora_up, DECAY_RANK + AAA_RANK))

    vecs = jnp.stack([w0, a0, k_k, k_a, r_k.reshape(-1), gn_w, gn_b, jnp.zeros_like(w0)], axis=0)
    head_id = jnp.arange(2 * PAIR) // HEAD_DIM
    bd = (head_id[:, None] == head_id[None, :]).astype(BF16)
    y_a = _rwkv_call(p_rkv, p_lora, vecs, *lora_up, bd, batch, seq_len)

    y_b = _moba_call(p_moba, _moba_key_aug(seq_len), batch, seq_len)

    w_out_b = w_out.astype(BF16)
    w_router = _pad_cols(jnp.concatenate([w_expert, w_group], axis=1), ROUTER_PAD)
    b_router = _pad_cols(jnp.concatenate([b_expert, b_group])[None, :], ROUTER_PAD)
    h1, h1_b, logits = _outproj_call(y_a, y_b, x2, w_out_b[:RWKV_WIDTH], w_out_b[RWKV_WIDTH:],
                                     ln1_g[None, :], ln1_b[None, :], w_router, b_router)

    flat = lambda w: w.astype(BF16).reshape((N_EXPERTS,) + w.shape[2:])
    gates, pos, flags = _route_sort_call(logits)
    flags = jnp.concatenate([flags[:, :N_GROUPS, :MOE_TM // MOE_SUB],
                             flags[:, :N_GROUPS, MOE_META_FITS:MOE_META_WINDOW + 1]], axis=-1).reshape(-1)
    out = _moe_call(flags, h1_b, h1, gates, pos, flat(w1_exp), flat(w3_exp), flat(w2_exp),
                    ln2_g[None, :], ln2_b[None, :])
    return out.reshape(batch, seq_len, d)
```

```python
import functools
import math

import jax
import jax.numpy as jnp
import numpy as np
from jax import lax
from jax.experimental import pallas as pl
from jax.experimental.pallas import tpu as pltpu

F32 = jnp.float32
BF16 = jnp.bfloat16

D_MODEL = 1024
HEAD_DIM = 64
RWKV_WIDTH = 512
MOBA_WIDTH = 512
DECAY_RANK = 32
AAA_RANK = 32
GATE_RANK = 96
GN_EPS = 64e-5
L2_EPS = 1e-12
MOBA_BLOCK = 256
MOBA_TOPK = 3
N_GROUPS = 4
EXPERTS_PER_GROUP = 8
N_EXPERTS = N_GROUPS * EXPERTS_PER_GROUP
D_EXPERT = 256
LN_EPS = 1e-5
DEEPNORM_ALPHA = float(2.0 ** 0.25)
NEG_INF = -1e30
F32_LOWEST = -3.0e38

LANES = 128
PAIR = 2 * HEAD_DIM
N_PAIRS = RWKV_WIDTH // PAIR
LORA_PAD = 2 * LANES
RWKV_COLS_PAD = 3 * RWKV_WIDTH + LORA_PAD
IN_COLS_PAD = RWKV_COLS_PAD + 3 * MOBA_WIDTH
VMEM_LIMIT = 56 * 1024 * 1024

INPROJ_TM = 512
INPROJ_TN = 256
RWKV_CHUNK = 64
RWKV_CHUNKS_PER_STEP = 4
RWKV_PASSES = 1
RWKV_STATE_PASSES = 1
OUTPROJ_TM = 512
MOE_TM = 1024
MOE_EXPERTS_PER_STEP = 8
MOE_SUB = 256
MOE_WINDOW = 384
MOE_WINDOW_ALIGN = 128
MOE_META_FITS = 16
MOE_META_WINDOW = 17
MOBA_KV_TILE = 512
MOBA_Q_TILE = 2048
MOBA_V_ROWS = HEAD_DIM + 16
MOBA_ALIBI_PARTS = 3
LOG2_E = 1.4426950408889634
ROUTER_PAD = LANES
GROUP_LANE0 = N_EXPERTS

NN = (((1,), (0,)), ((), ()))
NT = (((1,), (1,)), ((), ()))


def _dot(a, b, dims=NN, precision=None):
    return lax.dot_general(a, b, dims, precision=precision, preferred_element_type=F32)


def _split_bf16(x, parts):
    out = []
    rem = x
    for i in range(parts):
        p = rem.astype(BF16)
        out.append(p)
        if i + 1 < parts:
            rem = rem - p.astype(F32)
    return out


def _mm(a, b, dims=NN, passes=3):
    if passes == 1:
        return _dot(a.astype(BF16), b.astype(BF16), dims)
    if passes == 6:
        return _dot(a, b, dims, precision=lax.Precision.HIGHEST)
    a_hi, a_lo = _split_bf16(a, 2)
    b_hi, b_lo = _split_bf16(b, 2)
    return _dot(a_hi, b_hi, dims) + (_dot(a_hi, b_lo, dims) + _dot(a_lo, b_hi, dims))


def _mm_exact_lhs(a_bf16, b, dims=NN):
    b1, b2, b3 = _split_bf16(b, 3)
    return _dot(a_bf16, b1, dims) + (_dot(a_bf16, b2, dims) + _dot(a_bf16, b3, dims))


def _mm_exact_rhs(a, b_bf16, dims=NN):
    a1, a2, a3 = _split_bf16(a, 3)
    return _dot(a1, b_bf16, dims) + (_dot(a2, b_bf16, dims) + _dot(a3, b_bf16, dims))


def _inproj_kernel(x_ref, w_ref, mu_ref, prkv_ref, plora_ref, pm_ref, carry_ref, *, tiles_per_seq):
    tm = x_ref.shape[0]
    xb = x_ref[...].astype(BF16)
    seq_start = (pl.program_id(0) % tiles_per_seq) == 0
    row0 = lax.broadcasted_iota(jnp.int32, (tm, INPROJ_TN), 0) == 0
    n_shift_tiles = RWKV_COLS_PAD // INPROJ_TN
    for j in range(n_shift_tiles):
        c0 = j * INPROJ_TN
        acc = _dot(xb, w_ref[:, c0:c0 + INPROJ_TN])
        prev_last = jnp.where(seq_start, 0.0, carry_ref[0:1, c0:c0 + INPROJ_TN])
        shifted = jnp.where(row0, prev_last, pltpu.roll(acc, 1, 0))
        carry_ref[0:1, c0:c0 + INPROJ_TN] = acc[tm - 1:tm, :]
        out = acc + (shifted - acc) * mu_ref[:, c0:c0 + INPROJ_TN]
        if c0 < 3 * RWKV_WIDTH:
            prkv_ref[:, c0:c0 + INPROJ_TN] = out
        else:
            plora_ref[:, c0 - 3 * RWKV_WIDTH:c0 - 3 * RWKV_WIDTH + INPROJ_TN] = out
    for j in range(3 * MOBA_WIDTH // INPROJ_TN):
        c0 = j * INPROJ_TN
        acc = _dot(xb, w_ref[:, RWKV_COLS_PAD + c0:RWKV_COLS_PAD + c0 + INPROJ_TN])
        pm_ref[:, c0:c0 + INPROJ_TN] = acc.astype(BF16)


def _inproj_call(x2, w_cat, mu_cat, seq_len):
    n = x2.shape[0]
    tm = INPROJ_TM
    assert seq_len % tm == 0 and (3 * RWKV_WIDTH) % INPROJ_TN == 0
    return pl.pallas_call(
        functools.partial(_inproj_kernel, tiles_per_seq=seq_len // tm),
        grid=(n // tm,),
        in_specs=[
            pl.BlockSpec((tm, D_MODEL), lambda i: (i, 0)),
            pl.BlockSpec((D_MODEL, IN_COLS_PAD), lambda i: (0, 0)),
            pl.BlockSpec((1, RWKV_COLS_PAD), lambda i: (0, 0)),
        ],
        out_specs=[
            pl.BlockSpec((tm, 3 * RWKV_WIDTH), lambda i: (i, 0)),
            pl.BlockSpec((tm, LORA_PAD), lambda i: (i, 0)),
            pl.BlockSpec((tm, 3 * MOBA_WIDTH), lambda i: (i, 0)),
        ],
        out_shape=[
            jax.ShapeDtypeStruct((n, 3 * RWKV_WIDTH), F32),
            jax.ShapeDtypeStruct((n, LORA_PAD), F32),
            jax.ShapeDtypeStruct((n, 3 * MOBA_WIDTH), BF16),
        ],
        scratch_shapes=[pltpu.VMEM((8, RWKV_COLS_PAD), F32)],
        compiler_params=pltpu.CompilerParams(
            dimension_semantics=("arbitrary",), vmem_limit_bytes=VMEM_LIMIT),
        name="inproj_shift",
    )(x2, w_cat, mu_cat)


def _softplus(z):
    return jnp.maximum(z, 0.0) + jnp.log(1.0 + jnp.exp(-jnp.abs(z)))


def _sigmoid(z):
    return 1.0 / (1.0 + jnp.exp(-z))


def _rwkv_chunks(rt, kt, at, bt, v, d_incl, s_prev, passes, state_passes):
    c = RWKV_CHUNK
    n_chunks = rt.shape[0] // c
    n_pairs = len(s_prev)
    row = lax.broadcasted_iota(jnp.int32, (c, PAIR), 0)
    col = lax.broadcasted_iota(jnp.int32, (c, PAIR), 1) % HEAD_DIM
    strict = row > col
    incl = row >= col
    eye_c = (row == col).astype(F32)
    lane = lax.broadcasted_iota(jnp.int32, (1, PAIR), 1)
    head0 = lane < HEAD_DIM
    head1 = jnp.logical_not(head0)
    prow = lax.broadcasted_iota(jnp.int32, (PAIR, PAIR), 0)
    pcol = lax.broadcasted_iota(jnp.int32, (PAIR, PAIR), 1)
    same_head = (prow < HEAD_DIM) == (pcol < HEAD_DIM)
    eye_p = (prow == pcol).astype(F32)
    rows = [slice(ci * c, (ci + 1) * c) for ci in range(n_chunks)]
    sl = [slice(p * PAIR, (p + 1) * PAIR) for p in range(n_pairs)]
    pairs = [(ci, p) for ci in range(n_chunks) for p in range(n_pairs)]
    cut = lambda t, ci, p: t[rows[ci], sl[p]]

    def by_head(m):
        return jnp.concatenate([jnp.where(head0, m, 0.0), jnp.where(head1, m, 0.0)], axis=0)

    def by_head2(m, n):
        return jnp.concatenate([by_head(m), by_head(n)], axis=1)

    at_p = {k_: cut(at, *k_) for k_ in pairs}
    rt_p = {k_: cut(rt, *k_) for k_ in pairs}
    bt_p = {k_: cut(bt, *k_) for k_ in pairs}
    kt_p = {k_: cut(kt, *k_) for k_ in pairs}
    v_p = {k_: cut(v, *k_) for k_ in pairs}

    z = {k_: _mm(jnp.concatenate([at_p[k_], rt_p[k_]], axis=0),
                 jnp.concatenate([by_head(bt_p[k_]).T, by_head(kt_p[k_]).T], axis=1), NN, passes)
         for k_ in pairs}
    l_ab = {k_: jnp.where(strict, z[k_][:c, :PAIR], 0.0) for k_ in pairs}
    l_ak = {k_: jnp.where(strict, z[k_][:c, PAIR:], 0.0) for k_ in pairs}
    m_rb = {k_: jnp.where(incl, z[k_][c:, :PAIR], 0.0) for k_ in pairs}
    m_rk = {k_: jnp.where(incl, z[k_][c:, PAIR:], 0.0) for k_ in pairs}
    pw = {k_: _mm(l_ab[k_], by_head(l_ab[k_]), NN, passes) for k_ in pairs}
    t_inv = {k_: eye_c + l_ab[k_] for k_ in pairs}
    for _ in range(int(math.log2(c)) - 1):
        tp = {k_: _mm(jnp.concatenate([t_inv[k_], pw[k_]], axis=0), by_head(pw[k_]), NN, passes)
              for k_ in pairs}
        t_inv = {k_: t_inv[k_] + tp[k_][:c] for k_ in pairs}
        pw = {k_: tp[k_][c:] for k_ in pairs}
    lm = {k_: _mm(jnp.concatenate([l_ak[k_], m_rk[k_]], axis=0), by_head(v_p[k_]), NN, passes)
          for k_ in pairs}
    lv = {k_: lm[k_][:c] for k_ in pairs}
    mv = {k_: lm[k_][c:] for k_ in pairs}
    wu = {k_: _mm(t_inv[k_], by_head2(at_p[k_], lv[k_]), NN, passes) for k_ in pairs}
    qy = {k_: _mm(m_rb[k_], by_head2(wu[k_][:, :PAIR], wu[k_][:, PAIR:]), NN, passes) for k_ in pairs}

    qeff, y1, phi, psi = {}, {}, {}, {}
    for ci, p in pairs:
        k_ = (ci, p)
        w, u0 = wu[k_][:, :PAIR], wu[k_][:, PAIR:]
        qeff[k_] = rt_p[k_] + qy[k_][:, :PAIR]
        y1[k_] = qy[k_][:, PAIR:] + mv[k_]
        d_p = d_incl[(ci + 1) * c - 1:(ci + 1) * c, sl[p]]
        phi[k_] = jnp.where(same_head, (eye_p + _mm(w.T, bt_p[k_], NN, passes)) * d_p, 0.0)
        uv_t = jnp.concatenate([u0, v_p[k_]], axis=0).T
        bk = jnp.concatenate([bt_p[k_], kt_p[k_]], axis=0)
        psi[k_] = jnp.where(same_head, _mm(uv_t, bk, NN, passes) * d_p, 0.0)

    state = list(s_prev)
    ys = [[None] * n_pairs for _ in range(n_chunks)]
    for ci in range(n_chunks):
        for p in range(n_pairs):
            ys[ci][p] = _mm(qeff[ci, p], state[p].T, NN, state_passes) + y1[ci, p]
            state[p] = _mm(state[p], phi[ci, p], NN, state_passes) + psi[ci, p]
    y = jnp.concatenate([jnp.concatenate(ys[ci], axis=1) for ci in range(n_chunks)], axis=0)
    return y, state


def _rwkv_kernel(prkv_ref, plora_ref, vec_ref, wl_ref, al_ref, gl_ref, bd_ref, y_ref, s_ref):
    rows = prkv_ref.shape[0]
    c = RWKV_CHUNK
    width = RWKV_WIDTH

    @pl.when(pl.program_id(1) == 0)
    def _():
        s_ref[...] = jnp.zeros_like(s_ref)

    r = prkv_ref[:, 0:width]
    k_raw = prkv_ref[:, width:2 * width]
    v = prkv_ref[:, 2 * width:3 * width]
    p_wd = p_ad = p_gd = plora_ref[...]
    w0 = vec_ref[0:1, :]
    a0 = vec_ref[1:2, :]
    k_k = vec_ref[2:3, :]
    k_a = vec_ref[3:4, :]
    r_k = vec_ref[4:5, :]
    gn_w = vec_ref[5:6, :]
    gn_b = vec_ref[6:7, :]
    bd = bd_ref[...]

    def seg_sum(z):
        halves = []
        for c0 in range(0, width, bd.shape[0]):
            z_hi, z_lo = _split_bf16(z[:, c0:c0 + bd.shape[0]], 2)
            halves.append(_dot(z_hi, bd) + _dot(z_lo, bd))
        return jnp.concatenate(halves, axis=1)

    w_log = -_softplus(-(w0 + _mm(jnp.tanh(p_wd), wl_ref[...], NN, 3))) - 0.5
    log_w = -jnp.exp(w_log)
    a = _sigmoid(a0 + _mm(p_ad, al_ref[...], NN, RWKV_PASSES))
    g = _mm(_sigmoid(p_gd), gl_ref[...], NN, RWKV_PASSES)
    kk = k_raw * k_k
    kk = kk / jnp.maximum(jnp.sqrt(seg_sum(kk * kk)), L2_EPS)
    k = k_raw * (1.0 + (a - 1.0) * k_a)

    row = lax.broadcasted_iota(jnp.int32, (rows, rows), 0)
    col = lax.broadcasted_iota(jnp.int32, (rows, rows), 1)
    tri = ((row >= col) & (row // c == col // c)).astype(BF16)
    cum = _mm_exact_lhs(tri, log_w)
    d_incl = jnp.exp(cum)
    d_inv = jnp.exp(-cum)
    d_excl = jnp.exp(cum - log_w)
    rt = r * d_incl
    kt = k * d_inv
    at = -kk * d_excl
    bt = kk * a * d_inv

    y, s_next = _rwkv_chunks(rt, kt, at, bt, v, d_incl, [s_ref[p] for p in range(N_PAIRS)],
                             RWKV_PASSES, RWKV_STATE_PASSES)
    for p in range(N_PAIRS):
        s_ref[p] = s_next[p]

    inv_n = 1.0 / HEAD_DIM
    mu = seg_sum(y) * inv_n
    yc = y - mu
    var = seg_sum(yc * yc) * inv_n
    yn = yc * lax.rsqrt(var + GN_EPS) * gn_w + gn_b
    bonus = seg_sum(r * k * r_k) * v
    y_ref[...] = ((yn + bonus) * g).astype(y_ref.dtype)


def _rwkv_call(p_rkv, p_lora, vecs, wl, al, gl, bd, batch, seq_len):
    n = p_rkv.shape[0]
    rows = RWKV_CHUNK * RWKV_CHUNKS_PER_STEP
    assert seq_len % rows == 0
    steps = seq_len // rows
    row_map = lambda b, i: (b * steps + i, 0)
    const = lambda b, i: (0, 0)
    return pl.pallas_call(
        _rwkv_kernel,
        grid=(batch, steps),
        in_specs=[
            pl.BlockSpec((rows, 3 * RWKV_WIDTH), row_map),
            pl.BlockSpec((rows, LORA_PAD), row_map),
            pl.BlockSpec((8, RWKV_WIDTH), const),
            pl.BlockSpec((LORA_PAD, RWKV_WIDTH), const),
            pl.BlockSpec((LORA_PAD, RWKV_WIDTH), const),
            pl.BlockSpec((LORA_PAD, RWKV_WIDTH), const),
            pl.BlockSpec((2 * PAIR, 2 * PAIR), const),
        ],
        out_specs=pl.BlockSpec((rows, RWKV_WIDTH), row_map),
        out_shape=jax.ShapeDtypeStruct((n, RWKV_WIDTH), BF16),
        scratch_shapes=[pltpu.VMEM((N_PAIRS, PAIR, PAIR), F32)],
        compiler_params=pltpu.CompilerParams(
            dimension_semantics=("arbitrary", "arbitrary"), vmem_limit_bytes=VMEM_LIMIT),
        name="rwkv7_chunked",
    )(p_rkv, p_lora, vecs, wl, al, gl, bd)


def _moba_kernel(q_ref, k_ref, v_ref, kaug_ref, o_ref, kmean_ref, vt_ref,
                 m0_ref, m1_ref, acc0_ref, acc1_ref, s_even_ref, s_odd_ref,
                 smax_even_ref, smax_odd_ref, *, n_blocks):
    blk = MOBA_BLOCK
    tk = MOBA_KV_TILE
    tq = q_ref.shape[0]
    i = pl.program_id(2)
    nb_pad = kmean_ref.shape[0]
    m_refs, acc_refs = (m0_ref, m1_ref), (acc0_ref, acc1_ref)

    @pl.when(i == 0)
    def _():
        kmean_ref[...] = jnp.zeros_like(kmean_ref)

        def mean_body(n, carry):
            off = pl.multiple_of(n * blk, blk)
            kb = k_ref[pl.ds(off, blk), :].astype(F32)
            kmean_ref[pl.ds(n, 1), :] = jnp.sum(kb, axis=0, keepdims=True) * (1.0 / blk)
            return carry
        lax.fori_loop(0, n_blocks, mean_body, 0)

        ones = jnp.ones((MOBA_V_ROWS - HEAD_DIM, tk), BF16)

        def vt_body(j, carry):
            off = pl.multiple_of(j * tk, tk)
            v_t = v_ref[pl.ds(off, tk), :].astype(F32).T.astype(BF16)
            for h in (0, 1):
                vt_ref[j, h] = jnp.concatenate([v_t[h * HEAD_DIM:(h + 1) * HEAD_DIM], ones], axis=0)
            return carry
        lax.fori_loop(0, vt_ref.shape[0], vt_body, 0)

    q_t = q_ref[...].astype(F32).T
    chan = lax.broadcasted_iota(jnp.int32, (PAIR, tq), 0)
    blk_row = lax.broadcasted_iota(jnp.int32, (nb_pad, tq), 0)
    own_blk = (i * tq + lax.broadcasted_iota(jnp.int32, (nb_pad, tq), 1)) // blk
    past = blk_row < own_blk
    aug_row = lax.broadcasted_iota(jnp.int32, (LANES, tq), 0)
    ones_rows = (aug_row >= n_blocks) & (aug_row < n_blocks + MOBA_ALIBI_PARTS)
    kmean = kmean_ref[...]

    qa_t = []
    for h in (0, 1):
        qh_t = jnp.where((chan < HEAD_DIM) == (h == 0), q_t, 0.0)
        gate = _dot(kmean, qh_t, NN, precision=lax.Precision.HIGHEST)
        gate = jnp.where(past, gate, F32_LOWEST)
        sel = jnp.zeros(gate.shape, jnp.bool_)
        for _ in range(MOBA_TOPK):
            mx = jnp.max(gate, axis=0, keepdims=True)
            first = jnp.min(jnp.where(gate == mx, blk_row, nb_pad), axis=0, keepdims=True)
            pick = (blk_row == first) & (mx > F32_LOWEST)
            sel = sel | pick
            gate = jnp.where(pick, F32_LOWEST, gate)
        sel_bias = jnp.where(past & jnp.logical_not(sel), NEG_INF, 0.0)
        aug_t = jnp.concatenate([sel_bias, jnp.zeros((LANES - nb_pad, tq), F32)], axis=0)
        aug_t = jnp.where(ones_rows, 1.0, aug_t)
        qa_t.append(jnp.concatenate([qh_t * (LOG2_E / math.sqrt(HEAD_DIM)), aug_t], axis=0).astype(BF16))

    def tile_scores(j, lanes=slice(None)):
        off = pl.multiple_of(j * tk, tk)
        k_t = k_ref[pl.ds(off, tk), :]
        return [_dot(jnp.concatenate([k_t, kaug_ref[h, pl.ds(off, tk), :]], axis=1), qa_t[h][:, lanes])
                for h in (0, 1)]

    def put_scores(buf, s, lanes=slice(None)):
        for h in (0, 1):
            buf[0][h, :, lanes] = s[h]
            buf[1][h, :, lanes] = jnp.max(s[h], axis=0, keepdims=True)

    def tile_update(j, buf, lanes=slice(None)):
        s_buf, smax_buf = buf
        for h in (0, 1):
            m_old = m_refs[h][:, lanes]
            m_new = jnp.maximum(m_old, smax_buf[h, :, lanes])
            p = jnp.exp2(s_buf[h, :, lanes] - m_new).astype(BF16)
            pv = _dot(vt_ref[j, h], p)
            acc_refs[h][:, lanes] = jnp.exp2(m_old - m_new) * acc_refs[h][:, lanes] + pv
            m_refs[h][:, lanes] = m_new

    for h in (0, 1):
        m_refs[h][...] = jnp.full(m_refs[h].shape, F32_LOWEST, F32)
        acc_refs[h][...] = jnp.zeros(acc_refs[h].shape, F32)

    buffers = ((s_even_ref, smax_even_ref), (s_odd_ref, smax_odd_ref))
    n_own = tq // tk
    j_first = i * n_own
    diagonal = (lax.broadcasted_iota(jnp.int32, (tk, tk), 0) <= lax.broadcasted_iota(jnp.int32, (tk, tk), 1))

    def own_scores(g):
        s = tile_scores(j_first + g, slice(g * tk, tq))
        own = [jnp.where(diagonal, s_h[:, :tk], NEG_INF) for s_h in s]
        if g == n_own - 1:
            return own
        return [jnp.concatenate([own_h, s_h[:, tk:]], axis=1) for own_h, s_h in zip(own, s)]

    put_scores(buffers[0], own_scores(n_own - 1), slice((n_own - 1) * tk, tq))
    for m in range(1, n_own):
        g = n_own - 1 - m
        put_scores(buffers[m % 2], own_scores(g), slice(g * tk, tq))
        tile_update(j_first + g + 1, buffers[(m - 1) % 2], slice((g + 1) * tk, tq))
    cur, nxt = buffers[(n_own - 1) % 2], buffers[n_own % 2]

    def previous_tile(j):
        return jnp.where(j == 0, j_first, j - 1)

    def pipelined_step(j, src, dst):
        put_scores(dst, tile_scores(j))
        tile_update(previous_tile(j), src)

    def kv_pair_step(u, carry):
        pipelined_step(2 * u, cur, nxt)
        pipelined_step(2 * u + 1, nxt, cur)
        return carry
    lax.fori_loop(0, j_first // 2, kv_pair_step, 0)

    @pl.when(j_first % 2 == 1)
    def _():
        pipelined_step(j_first - 1, cur, nxt)
        tile_update(j_first - 1, nxt)

    @pl.when(j_first % 2 == 0)
    def _():
        tile_update(previous_tile(j_first), cur)

    out_t = jnp.concatenate([acc_refs[h][0:HEAD_DIM, :] / acc_refs[h][HEAD_DIM:HEAD_DIM + 1, :]
                             for h in (0, 1)], axis=0)
    o_ref[...] = out_t.T.astype(o_ref.dtype)


def _moba_call(qkv, kaug, batch, seq_len):
    n = qkv.shape[0]
    blk = MOBA_BLOCK
    tq = MOBA_Q_TILE
    nb = seq_len // blk
    nq = seq_len // tq
    assert nb + MOBA_ALIBI_PARTS <= LANES and seq_len % MOBA_KV_TILE == 0
    assert tq % MOBA_KV_TILE == 0 and seq_len % tq == 0
    lane_groups = MOBA_WIDTH // LANES
    return pl.pallas_call(
        functools.partial(_moba_kernel, n_blocks=nb),
        grid=(batch, N_PAIRS, nq),
        in_specs=[
            pl.BlockSpec((tq, PAIR), lambda b, p, i: (b * nq + i, p)),
            pl.BlockSpec((seq_len, PAIR), lambda b, p, i: (b, lane_groups + p)),
            pl.BlockSpec((seq_len, PAIR), lambda b, p, i: (b, 2 * lane_groups + p)),
            pl.BlockSpec((2, seq_len, LANES), lambda b, p, i: (p, 0, 0)),
        ],
        out_specs=pl.BlockSpec((tq, PAIR), lambda b, p, i: (b * nq + i, p)),
        out_shape=jax.ShapeDtypeStruct((n, MOBA_WIDTH), BF16),
        scratch_shapes=[
            pltpu.VMEM((-(-nb // 8) * 8, PAIR), F32),
            pltpu.VMEM((seq_len // MOBA_KV_TILE, 2, MOBA_V_ROWS, MOBA_KV_TILE), BF16),
            pltpu.VMEM((1, tq), F32), pltpu.VMEM((1, tq), F32),
            pltpu.VMEM((MOBA_V_ROWS, tq), F32), pltpu.VMEM((MOBA_V_ROWS, tq), F32),
            pltpu.VMEM((2, MOBA_KV_TILE, tq), F32), pltpu.VMEM((2, MOBA_KV_TILE, tq), F32),
            pltpu.VMEM((2, 1, tq), F32), pltpu.VMEM((2, 1, tq), F32),
        ],
        compiler_params=pltpu.CompilerParams(
            dimension_semantics=("arbitrary", "arbitrary", "arbitrary"),
            vmem_limit_bytes=VMEM_LIMIT),
        name="moba_attention",
    )(qkv, qkv, qkv, kaug)


def _moba_key_aug(seq_len):
    nb = seq_len // MOBA_BLOCK
    heads = MOBA_WIDTH // HEAD_DIM
    pos = np.arange(seq_len, dtype=np.int32)
    slopes = (2.0 ** (-8.0 * (np.arange(heads, dtype=np.float32) + 1.0) / heads)).astype(np.float32)
    aug = np.zeros((heads, seq_len, LANES), np.float32)
    aug[:, pos, pos // MOBA_BLOCK] = 1.0
    rem = (np.float32(LOG2_E) * slopes)[:, None] * pos.astype(np.float32)[None, :]
    for part in range(MOBA_ALIBI_PARTS):
        piece = (rem.view(np.uint32) & np.uint32(0xFFFF0000)).view(np.float32)
        aug[:, :, nb + part] = piece
        rem = rem - piece
    return jnp.asarray(aug.astype(BF16))


def _layer_norm(z, g, b):
    mu = jnp.mean(z, axis=-1, keepdims=True)
    zc = z - mu
    var = jnp.mean(zc * zc, axis=-1, keepdims=True)
    return zc * lax.rsqrt(var + LN_EPS) * g + b


def _outproj_kernel(ya_ref, yb_ref, x_ref, wa_ref, wb_ref, g_ref, b_ref, wr_ref, br_ref,
                    h_ref, lg_ref):
    mix = _dot(ya_ref[...], wa_ref[...]) + _dot(yb_ref[...], wb_ref[...])
    h = _layer_norm(DEEPNORM_ALPHA * x_ref[...] + mix, g_ref[...], b_ref[...])
    h_ref[...] = h
    lg_ref[...] = _mm(h, wr_ref[...], NN, 3) + br_ref[...]


def _outproj_call(y_a, y_b, x2, wa, wb, ln_g, ln_b, w_router, b_router):
    n = x2.shape[0]
    tm = OUTPROJ_TM
    row = lambda i: (i, 0)
    const = lambda i: (0, 0)
    return pl.pallas_call(
        _outproj_kernel,
        grid=(n // tm,),
        in_specs=[
            pl.BlockSpec((tm, RWKV_WIDTH), row),
            pl.BlockSpec((tm, MOBA_WIDTH), row),
            pl.BlockSpec((tm, D_MODEL), row),
            pl.BlockSpec((RWKV_WIDTH, D_MODEL), const),
            pl.BlockSpec((MOBA_WIDTH, D_MODEL), const),
            pl.BlockSpec((1, D_MODEL), const),
            pl.BlockSpec((1, D_MODEL), const),
            pl.BlockSpec((D_MODEL, ROUTER_PAD), const),
            pl.BlockSpec((1, ROUTER_PAD), const),
        ],
        out_specs=[
            pl.BlockSpec((tm, D_MODEL), row),
            pl.BlockSpec((tm, ROUTER_PAD), row),
        ],
        out_shape=[
            jax.ShapeDtypeStruct((n, D_MODEL), F32),
            jax.ShapeDtypeStruct((n, ROUTER_PAD), F32),
        ],
        compiler_params=pltpu.CompilerParams(
            dimension_semantics=("arbitrary",), vmem_limit_bytes=VMEM_LIMIT),
        name="outproj_ln_router",
    )(y_a, y_b, x2, wa, wb, ln_g, ln_b, w_router, b_router)


def _route(logits):
    lane = lax.broadcasted_iota(jnp.int32, logits.shape, 1)
    is_group = (lane >= GROUP_LANE0) & (lane < GROUP_LANE0 + N_GROUPS)
    gl = jnp.where(is_group, logits, F32_LOWEST)
    g_max = jnp.max(gl, axis=-1, keepdims=True)
    g_first = jnp.min(jnp.where(gl == g_max, lane, LANES), axis=-1, keepdims=True)
    g_exp = jnp.where(is_group, jnp.exp(gl - g_max), 0.0)
    p_g = 1.0 / jnp.sum(g_exp, axis=-1, keepdims=True)
    g_idx = g_first - GROUP_LANE0
    in_group = (lane >= g_idx * EXPERTS_PER_GROUP) & (lane < (g_idx + 1) * EXPERTS_PER_GROUP)
    el = jnp.where(in_group, logits, F32_LOWEST)
    e_max = jnp.max(el, axis=-1, keepdims=True)
    e_exp = jnp.where(in_group, jnp.exp(el - e_max), 0.0)
    e_prob = e_exp / jnp.sum(e_exp, axis=-1, keepdims=True)
    cand = jnp.where(in_group, e_prob, -1.0)
    v1 = jnp.max(cand, axis=-1, keepdims=True)
    i1 = jnp.min(jnp.where(cand == v1, lane, LANES), axis=-1, keepdims=True)
    pick1 = lane == i1
    cand2 = jnp.where(pick1, -1.0, cand)
    v2 = jnp.max(cand2, axis=-1, keepdims=True)
    i2 = jnp.min(jnp.where(cand2 == v2, lane, LANES), axis=-1, keepdims=True)
    pick2 = lane == i2
    denom = v1 + v2
    gates = jnp.where(pick1, v1 / denom * p_g, jnp.where(pick2, v2 / denom * p_g, 0.0))
    return gates, g_idx


def _route_sort_kernel(lg_ref, gates_ref, pos_ref, flags_ref, earlier_ref):
    tm = lg_ref.shape[0]

    @pl.when(pl.program_id(0) == 0)
    def _():
        row = lax.broadcasted_iota(jnp.int32, (tm, tm), 0)
        col = lax.broadcasted_iota(jnp.int32, (tm, tm), 1)
        earlier_ref[...] = jnp.where(row > col, 1.0, 0.0).astype(BF16)

    gates, g_idx = _route(lg_ref[...])
    lane = lax.broadcasted_iota(jnp.int32, (tm, LANES), 1)
    in_own = lane == g_idx
    onehot = jnp.where(in_own, 1.0, 0.0)
    rank = _dot(earlier_ref[...], onehot.astype(BF16))
    count = jnp.sum(onehot, axis=0, keepdims=True)
    r128 = lax.broadcasted_iota(jnp.int32, (LANES, LANES), 0)
    c128 = lax.broadcasted_iota(jnp.int32, (LANES, LANES), 1)
    below = jnp.where(r128 < c128, 1.0, 0.0)
    start = _dot(jnp.broadcast_to(count, (8, LANES)), below, NN,
                 precision=lax.Precision.HIGHEST)[0:1]
    pos = jnp.sum(jnp.where(in_own, rank + start, 0.0), axis=-1, keepdims=True)
    gates_ref[...] = gates
    pos_ref[...] = jnp.broadcast_to(pos, (tm, LANES))
    start_c = jnp.broadcast_to(start, (LANES, LANES)).T
    count_c = jnp.broadcast_to(count, (LANES, LANES)).T
    sub_lo = c128 * MOE_SUB
    hit = (count_c > 0.0) & (start_c < (sub_lo + MOE_SUB).astype(F32)) & (start_c + count_c > sub_lo.astype(F32))
    start_i = start_c.astype(jnp.int32)
    count_i = count_c.astype(jnp.int32)
    win = jnp.minimum((start_i // MOE_WINDOW_ALIGN) * MOE_WINDOW_ALIGN, tm - MOE_WINDOW)
    fits = (count_i > 0) & (start_i + count_i <= win + MOE_WINDOW)
    meta = jnp.where(c128 == MOE_META_FITS, jnp.where(fits, 1, 0),
                     jnp.where(c128 == MOE_META_WINDOW, win, jnp.where(hit, 1, 0)))
    flags_ref[0] = meta[0:8].astype(jnp.int32)


def _route_sort_call(logits):
    n = logits.shape[0]
    tm = MOE_TM
    row = lambda t: (t, 0)
    return pl.pallas_call(
        _route_sort_kernel,
        grid=(n // tm,),
        in_specs=[pl.BlockSpec((tm, ROUTER_PAD), row)],
        out_specs=[
            pl.BlockSpec((tm, ROUTER_PAD), row),
            pl.BlockSpec((tm, LANES), row),
            pl.BlockSpec((1, 8, LANES), lambda t: (t, 0, 0)),
        ],
        out_shape=[
            jax.ShapeDtypeStruct((n, ROUTER_PAD), F32),
            jax.ShapeDtypeStruct((n, LANES), F32),
            jax.ShapeDtypeStruct((n // tm, 8, LANES), jnp.int32),
        ],
        scratch_shapes=[pltpu.VMEM((tm, tm), BF16)],
        compiler_params=pltpu.CompilerParams(
            dimension_semantics=("arbitrary",), vmem_limit_bytes=VMEM_LIMIT),
        name="route_sort",
    )(logits)


def _moe_kernel(flags_ref, h_ref, gates_ref, pos_ref, w1_ref, w3_ref, w2_ref, g_ref, b_ref,
                o_ref, xs_ref, gs_ref, acc_ref, pt_ref):
    tile = pl.program_id(0)
    step = pl.program_id(1)
    tm = h_ref.shape[0]
    sub = MOE_SUB
    n_sub = tm // sub
    eps = MOE_EXPERTS_PER_STEP

    @pl.when(step == 0)
    def _():
        pos_b = pos_ref[...]
        pos_row = pos_b.T[0:1, :]
        g_hi, g_lo = _split_bf16(gates_ref[...], 2)
        src = jnp.concatenate([h_ref[...].astype(BF16), g_hi, g_lo], axis=1)
        for c0 in range(0, tm, sub):
            slot = (lax.broadcasted_iota(jnp.int32, (sub, tm), 0) + c0).astype(F32)
            p_c = jnp.where(slot == pos_row, 1.0, 0.0).astype(BF16)
            moved = _dot(p_c, src)
            xs_ref[c0:c0 + sub, :] = moved[:, :D_MODEL].astype(BF16)
            gs_ref[c0:c0 + sub, :] = moved[:, D_MODEL:D_MODEL + LANES] + moved[:, D_MODEL + LANES:]
        for c0 in range(0, tm, LANES):
            slot = (lax.broadcasted_iota(jnp.int32, (tm, LANES), 1) + c0).astype(F32)
            pt_ref[:, c0:c0 + LANES] = jnp.where(pos_b == slot, 1.0, 0.0).astype(BF16)
        acc_ref[...] = jnp.zeros_like(acc_ref)

    group = step // (EXPERTS_PER_GROUP // eps)

    def visit(rows):
        x_r = xs_ref[rows, :]
        g_r = gs_ref[rows, :]
        lane = lax.broadcasted_iota(jnp.int32, g_r.shape, 1)
        acc = acc_ref[rows, :]
        for e in range(eps):
            gate_e = jnp.sum(jnp.where(lane == step * eps + e, g_r, 0.0), axis=-1, keepdims=True)
            a1 = _dot(x_r, w1_ref[e])
            a3 = _dot(x_r, w3_ref[e])
            hid = (a1 * _sigmoid(a1)) * a3 * gate_e
            acc = acc + _dot(hid.astype(BF16), w2_ref[e])
        acc_ref[rows, :] = acc

    base = (tile * N_GROUPS + group) * (n_sub + 2)
    fits = flags_ref[base + n_sub] != 0

    @pl.when(fits)
    def _():
        first = pl.multiple_of(flags_ref[base + n_sub + 1], MOE_WINDOW_ALIGN)
        visit(pl.ds(first, MOE_WINDOW))

    for r in range(n_sub):
        pl.when(jnp.logical_not(fits) & (flags_ref[base + r] != 0))(
            functools.partial(visit, slice(r * sub, (r + 1) * sub)))

    @pl.when(step == N_EXPERTS // eps - 1)
    def _():
        ffn = _dot(pt_ref[...], acc_ref[...].astype(BF16))
        o_ref[...] = _layer_norm(DEEPNORM_ALPHA * h_ref[...] + ffn, g_ref[...], b_ref[...])


def _moe_call(flags, h_f32, gates, pos, w1, w3, w2, ln_g, ln_b):
    n = h_f32.shape[0]
    tm = MOE_TM
    row = lambda t, s, f: (t, 0)
    const = lambda t, s, f: (0, 0)
    wmap = lambda t, s, f: (s, 0, 0)
    eps = MOE_EXPERTS_PER_STEP
    assert EXPERTS_PER_GROUP % eps == 0 and tm % MOE_SUB == 0
    grid_spec = pltpu.PrefetchScalarGridSpec(
        num_scalar_prefetch=1,
        grid=(n // tm, N_EXPERTS // eps),
        in_specs=[
            pl.BlockSpec((tm, D_MODEL), row),
            pl.BlockSpec((tm, ROUTER_PAD), row),
            pl.BlockSpec((tm, LANES), row),
            pl.BlockSpec((eps, D_MODEL, D_EXPERT), wmap),
            pl.BlockSpec((eps, D_MODEL, D_EXPERT), wmap),
            pl.BlockSpec((eps, D_EXPERT, D_MODEL), wmap),
            pl.BlockSpec((1, D_MODEL), const),
            pl.BlockSpec((1, D_MODEL), const),
        ],
        out_specs=pl.BlockSpec((tm, D_MODEL), row),
        scratch_shapes=[
            pltpu.VMEM((tm, D_MODEL), BF16),
            pltpu.VMEM((tm, ROUTER_PAD), F32),
            pltpu.VMEM((tm, D_MODEL), F32),
            pltpu.VMEM((tm, tm), BF16),
        ],
    )
    return pl.pallas_call(
        _moe_kernel,
        grid_spec=grid_spec,
        out_shape=jax.ShapeDtypeStruct((n, D_MODEL), F32),
        compiler_params=pltpu.CompilerParams(
            dimension_semantics=("arbitrary", "arbitrary"), vmem_limit_bytes=VMEM_LIMIT),
        name="hier_moe_ln",
    )(flags, h_f32, gates, pos, w1, w3, w2, ln_g, ln_b)


def _pad_cols(w, width):
    return jnp.pad(w, ((0, 0), (0, width - w.shape[1])))


def _pad_rows(w, height):
    return jnp.pad(w, ((0, height - w.shape[0]), (0, 0)))


def kernel(x, w_in, mu_shift, w0, w_lora_up, a0, a_lora_up, g_lora_up, k_k, k_a, r_k, gn_w, gn_b, w_out, ln1_g, ln1_b, w_group, b_group, w_expert, b_expert, w1_exp, w3_exp, w2_exp, ln2_g, ln2_b):
    batch, seq_len, d = x.shape
    assert d == D_MODEL
    n = batch * seq_len
    x2 = x.reshape(n, d)

    c_rkv = 3 * RWKV_WIDTH
    c_wd = c_rkv + DECAY_RANK
    c_ad = c_wd + AAA_RANK
    c_gd = c_ad + GATE_RANK
    w_cat = jnp.concatenate([
        w_in[:, :c_rkv], _pad_cols(w_in[:, c_rkv:c_gd], LORA_PAD), w_in[:, c_gd:],
    ], axis=1).astype(BF16)
    mu2 = mu_shift[None, :]
    mu_cat = jnp.concatenate([mu2[:, :c_rkv], _pad_cols(mu2[:, c_rkv:c_gd], LORA_PAD)], axis=1)
    p_rkv, p_lora, p_moba = _inproj_call(x2, w_cat, mu_cat, seq_len)

    place = lambda w, first: jnp.pad(w, ((first, LORA_PAD - first - w.shape[0]), (0, 0)))
    lora_up = (place(w_lora_up, 0), place(a_lora_up, DECAY_RANK), place(g_lora_up, DECAY_RANK + AAA_RANK))

    vecs = jnp.stack([w0, a0, k_k, k_a, r_k.reshape(-1), gn_w, gn_b, jnp.zeros_like(w0)], axis=0)
    head_id = jnp.arange(2 * PAIR) // HEAD_DIM
    bd = (head_id[:, None] == head_id[None, :]).astype(BF16)
    y_a = _rwkv_call(p_rkv, p_lora, vecs, *lora_up, bd, batch, seq_len)

    y_b = _moba_call(p_moba, _moba_key_aug(seq_len), batch, seq_len)

    w_out_b = w_out.astype(BF16)
    w_router = _pad_cols(jnp.concatenate([w_expert, w_group], axis=1), ROUTER_PAD)
    b_router = _pad_cols(jnp.concatenate([b_expert, b_group])[None, :], ROUTER_PAD)
    h1, logits = _outproj_call(y_a, y_b, x2, w_out_b[:RWKV_WIDTH], w_out_b[RWKV_WIDTH:],
                               ln1_g[None, :], ln1_b[None, :], w_router, b_router)

    flat = lambda w: w.astype(BF16).reshape((N_EXPERTS,) + w.shape[2:])
    gates, pos, flags = _route_sort_call(logits)
    flags = jnp.concatenate([flags[:, :N_GROUPS, :MOE_TM // MOE_SUB],
                             flags[:, :N_GROUPS, MOE_META_FITS:MOE_META_WINDOW + 1]], axis=-1).reshape(-1)
    out = _moe_call(flags, h1, gates, pos, flat(w1_exp), flat(w3_exp), flat(w2_exp),
                    ln2_g[None, :], ln2_b[None, :])
    return out.reshape(batch, seq_len, d)
```

```python
import functools
import math

import jax
import jax.numpy as jnp
import numpy as np
from jax import lax
from jax.experimental import pallas as pl
from jax.experimental.pallas import tpu as pltpu

F32 = jnp.float32
BF16 = jnp.bfloat16

D_MODEL = 1024
HEAD_DIM = 64
RWKV_WIDTH = 512
MOBA_WIDTH = 512
DECAY_RANK = 32
AAA_RANK = 32
GATE_RANK = 96
GN_EPS = 64e-5
L2_EPS = 1e-12
MOBA_BLOCK = 256
MOBA_TOPK = 3
N_GROUPS = 4
EXPERTS_PER_GROUP = 8
N_EXPERTS = N_GROUPS * EXPERTS_PER_GROUP
D_EXPERT = 256
LN_EPS = 1e-5
DEEPNORM_ALPHA = float(2.0 ** 0.25)
NEG_INF = -1e30
F32_LOWEST = -3.0e38

LANES = 128
PAIR = 2 * HEAD_DIM
N_PAIRS = RWKV_WIDTH // PAIR
LORA_PAD = 2 * LANES
RWKV_COLS_PAD = 3 * RWKV_WIDTH + LORA_PAD
IN_COLS_PAD = RWKV_COLS_PAD + 3 * MOBA_WIDTH
VMEM_LIMIT = 56 * 1024 * 1024

INPROJ_TM = 512
INPROJ_TN = 256
RWKV_CHUNK = 64
RWKV_CHUNKS_PER_STEP = 4
RWKV_PASSES = 1
RWKV_STATE_PASSES = 1
OUTPROJ_TM = 512
MOE_TM = 1024
MOE_EXPERTS_PER_STEP = 8
MOE_SUB = 256
MOE_WINDOW = 320
MOE_WINDOW_ALIGN = 64
MOE_META_FITS = 16
MOE_META_WINDOW = 17
MOBA_KV_TILE = 512
MOBA_Q_TILE = 2048
MOBA_V_ROWS = HEAD_DIM + 16
MOBA_ALIBI_PARTS = 3
LOG2_E = 1.4426950408889634
ROUTER_PAD = LANES
GROUP_LANE0 = N_EXPERTS

NN = (((1,), (0,)), ((), ()))
NT = (((1,), (1,)), ((), ()))


def _dot(a, b, dims=NN, precision=None):
    return lax.dot_general(a, b, dims, precision=precision, preferred_element_type=F32)


def _split_bf16(x, parts):
    out = []
    rem = x
    for i in range(parts):
        p = rem.astype(BF16)
        out.append(p)
        if i + 1 < parts:
            rem = rem - p.astype(F32)
    return out


def _mm(a, b, dims=NN, passes=3):
    if passes == 1:
        return _dot(a.astype(BF16), b.astype(BF16), dims)
    if passes == 6:
        return _dot(a, b, dims, precision=lax.Precision.HIGHEST)
    a_hi, a_lo = _split_bf16(a, 2)
    b_hi, b_lo = _split_bf16(b, 2)
    return _dot(a_hi, b_hi, dims) + (_dot(a_hi, b_lo, dims) + _dot(a_lo, b_hi, dims))


def _mm_exact_lhs(a_bf16, b, dims=NN):
    b1, b2, b3 = _split_bf16(b, 3)
    return _dot(a_bf16, b1, dims) + (_dot(a_bf16, b2, dims) + _dot(a_bf16, b3, dims))


def _mm_exact_rhs(a, b_bf16, dims=NN):
    a1, a2, a3 = _split_bf16(a, 3)
    return _dot(a1, b_bf16, dims) + (_dot(a2, b_bf16, dims) + _dot(a3, b_bf16, dims))


def _inproj_kernel(x_ref, w_ref, mu_ref, prkv_ref, plora_ref, pm_ref, carry_ref, *, tiles_per_seq):
    tm = x_ref.shape[0]
    xb = x_ref[...].astype(BF16)
    seq_start = (pl.program_id(0) % tiles_per_seq) == 0
    row0 = lax.broadcasted_iota(jnp.int32, (tm, INPROJ_TN), 0) == 0
    n_shift_tiles = RWKV_COLS_PAD // INPROJ_TN
    for j in range(n_shift_tiles):
        c0 = j * INPROJ_TN
        acc = _dot(xb, w_ref[:, c0:c0 + INPROJ_TN])
        prev_last = jnp.where(seq_start, 0.0, carry_ref[0:1, c0:c0 + INPROJ_TN])
        shifted = jnp.where(row0, prev_last, pltpu.roll(acc, 1, 0))
        carry_ref[0:1, c0:c0 + INPROJ_TN] = acc[tm - 1:tm, :]
        out = acc + (shifted - acc) * mu_ref[:, c0:c0 + INPROJ_TN]
        if c0 < 3 * RWKV_WIDTH:
            prkv_ref[:, c0:c0 + INPROJ_TN] = out
        else:
            plora_ref[:, c0 - 3 * RWKV_WIDTH:c0 - 3 * RWKV_WIDTH + INPROJ_TN] = out
    for j in range(3 * MOBA_WIDTH // INPROJ_TN):
        c0 = j * INPROJ_TN
        acc = _dot(xb, w_ref[:, RWKV_COLS_PAD + c0:RWKV_COLS_PAD + c0 + INPROJ_TN])
        pm_ref[:, c0:c0 + INPROJ_TN] = acc.astype(BF16)


def _inproj_call(x2, w_cat, mu_cat, seq_len):
    n = x2.shape[0]
    tm = INPROJ_TM
    assert seq_len % tm == 0 and (3 * RWKV_WIDTH) % INPROJ_TN == 0
    return pl.pallas_call(
        functools.partial(_inproj_kernel, tiles_per_seq=seq_len // tm),
        grid=(n // tm,),
        in_specs=[
            pl.BlockSpec((tm, D_MODEL), lambda i: (i, 0)),
            pl.BlockSpec((D_MODEL, IN_COLS_PAD), lambda i: (0, 0)),
            pl.BlockSpec((1, RWKV_COLS_PAD), lambda i: (0, 0)),
        ],
        out_specs=[
            pl.BlockSpec((tm, 3 * RWKV_WIDTH), lambda i: (i, 0)),
            pl.BlockSpec((tm, LORA_PAD), lambda i: (i, 0)),
            pl.BlockSpec((tm, 3 * MOBA_WIDTH), lambda i: (i, 0)),
        ],
        out_shape=[
            jax.ShapeDtypeStruct((n, 3 * RWKV_WIDTH), F32),
            jax.ShapeDtypeStruct((n, LORA_PAD), F32),
            jax.ShapeDtypeStruct((n, 3 * MOBA_WIDTH), BF16),
        ],
        scratch_shapes=[pltpu.VMEM((8, RWKV_COLS_PAD), F32)],
        compiler_params=pltpu.CompilerParams(
            dimension_semantics=("arbitrary",), vmem_limit_bytes=VMEM_LIMIT),
        name="inproj_shift",
    )(x2, w_cat, mu_cat)


def _softplus(z):
    return jnp.maximum(z, 0.0) + jnp.log(1.0 + jnp.exp(-jnp.abs(z)))


def _sigmoid(z):
    return 1.0 / (1.0 + jnp.exp(-z))


def _rwkv_chunks(rt, kt, at, bt, v, d_incl, s_prev, passes, state_passes):
    c = RWKV_CHUNK
    n_chunks = rt.shape[0] // c
    n_pairs = len(s_prev)
    row = lax.broadcasted_iota(jnp.int32, (c, PAIR), 0)
    col = lax.broadcasted_iota(jnp.int32, (c, PAIR), 1) % HEAD_DIM
    strict = row > col
    incl = row >= col
    eye_c = (row == col).astype(F32)
    lane = lax.broadcasted_iota(jnp.int32, (1, PAIR), 1)
    head0 = lane < HEAD_DIM
    head1 = jnp.logical_not(head0)
    prow = lax.broadcasted_iota(jnp.int32, (PAIR, PAIR), 0)
    pcol = lax.broadcasted_iota(jnp.int32, (PAIR, PAIR), 1)
    same_head = (prow < HEAD_DIM) == (pcol < HEAD_DIM)
    eye_p = (prow == pcol).astype(F32)
    rows = [slice(ci * c, (ci + 1) * c) for ci in range(n_chunks)]
    sl = [slice(p * PAIR, (p + 1) * PAIR) for p in range(n_pairs)]
    pairs = [(ci, p) for ci in range(n_chunks) for p in range(n_pairs)]
    cut = lambda t, ci, p: t[rows[ci], sl[p]]

    def by_head(m):
        return jnp.concatenate([jnp.where(head0, m, 0.0), jnp.where(head1, m, 0.0)], axis=0)

    def by_head2(m, n):
        return jnp.concatenate([by_head(m), by_head(n)], axis=1)

    at_p = {k_: cut(at, *k_) for k_ in pairs}
    rt_p = {k_: cut(rt, *k_) for k_ in pairs}
    bt_p = {k_: cut(bt, *k_) for k_ in pairs}
    kt_p = {k_: cut(kt, *k_) for k_ in pairs}
    v_p = {k_: cut(v, *k_) for k_ in pairs}

    z = {k_: _mm(jnp.concatenate([at_p[k_], rt_p[k_]], axis=0),
                 jnp.concatenate([by_head(bt_p[k_]).T, by_head(kt_p[k_]).T], axis=1), NN, passes)
         for k_ in pairs}
    l_ab = {k_: jnp.where(strict, z[k_][:c, :PAIR], 0.0) for k_ in pairs}
    l_ak = {k_: jnp.where(strict, z[k_][:c, PAIR:], 0.0) for k_ in pairs}
    m_rb = {k_: jnp.where(incl, z[k_][c:, :PAIR], 0.0) for k_ in pairs}
    m_rk = {k_: jnp.where(incl, z[k_][c:, PAIR:], 0.0) for k_ in pairs}
    pw = {k_: _mm(l_ab[k_], by_head(l_ab[k_]), NN, passes) for k_ in pairs}
    t_inv = {k_: eye_c + l_ab[k_] for k_ in pairs}
    for _ in range(int(math.log2(c)) - 1):
        tp = {k_: _mm(jnp.concatenate([t_inv[k_], pw[k_]], axis=0), by_head(pw[k_]), NN, passes)
              for k_ in pairs}
        t_inv = {k_: t_inv[k_] + tp[k_][:c] for k_ in pairs}
        pw = {k_: tp[k_][c:] for k_ in pairs}
    lm = {k_: _mm(jnp.concatenate([l_ak[k_], m_rk[k_]], axis=0), by_head(v_p[k_]), NN, passes)
          for k_ in pairs}
    lv = {k_: lm[k_][:c] for k_ in pairs}
    mv = {k_: lm[k_][c:] for k_ in pairs}
    wu = {k_: _mm(t_inv[k_], by_head2(at_p[k_], lv[k_]), NN, passes) for k_ in pairs}
    qy = {k_: _mm(m_rb[k_], by_head2(wu[k_][:, :PAIR], wu[k_][:, PAIR:]), NN, passes) for k_ in pairs}

    qeff, y1, phi, psi = {}, {}, {}, {}
    for ci, p in pairs:
        k_ = (ci, p)
        w, u0 = wu[k_][:, :PAIR], wu[k_][:, PAIR:]
        qeff[k_] = rt_p[k_] + qy[k_][:, :PAIR]
        y1[k_] = qy[k_][:, PAIR:] + mv[k_]
        d_p = d_incl[(ci + 1) * c - 1:(ci + 1) * c, sl[p]]
        phi[k_] = jnp.where(same_head, (eye_p + _mm(w.T, bt_p[k_], NN, passes)) * d_p, 0.0)
        uv_t = jnp.concatenate([u0, v_p[k_]], axis=0).T
        bk = jnp.concatenate([bt_p[k_], kt_p[k_]], axis=0)
        psi[k_] = jnp.where(same_head, _mm(uv_t, bk, NN, passes) * d_p, 0.0)

    state = list(s_prev)
    ys = [[None] * n_pairs for _ in range(n_chunks)]
    for ci in range(n_chunks):
        for p in range(n_pairs):
            ys[ci][p] = _mm(qeff[ci, p], state[p].T, NN, state_passes) + y1[ci, p]
            state[p] = _mm(state[p], phi[ci, p], NN, state_passes) + psi[ci, p]
    y = jnp.concatenate([jnp.concatenate(ys[ci], axis=1) for ci in range(n_chunks)], axis=0)
    return y, state


def _rwkv_kernel(prkv_ref, plora_ref, vec_ref, wl_ref, al_ref, gl_ref, bd_ref, y_ref, s_ref):
    rows = prkv_ref.shape[0]
    c = RWKV_CHUNK
    width = RWKV_WIDTH

    @pl.when(pl.program_id(1) == 0)
    def _():
        s_ref[...] = jnp.zeros_like(s_ref)

    r = prkv_ref[:, 0:width]
    k_raw = prkv_ref[:, width:2 * width]
    v = prkv_ref[:, 2 * width:3 * width]
    p_wd = p_ad = p_gd = plora_ref[...]
    w0 = vec_ref[0:1, :]
    a0 = vec_ref[1:2, :]
    k_k = vec_ref[2:3, :]
    k_a = vec_ref[3:4, :]
    r_k = vec_ref[4:5, :]
    gn_w = vec_ref[5:6, :]
    gn_b = vec_ref[6:7, :]
    bd = bd_ref[...]

    def seg_sum(z):
        halves = []
        for c0 in range(0, width, bd.shape[0]):
            z_hi, z_lo = _split_bf16(z[:, c0:c0 + bd.shape[0]], 2)
            halves.append(_dot(z_hi, bd) + _dot(z_lo, bd))
        return jnp.concatenate(halves, axis=1)

    w_log = -_softplus(-(w0 + _mm(jnp.tanh(p_wd), wl_ref[...], NN, 3))) - 0.5
    log_w = -jnp.exp(w_log)
    a = _sigmoid(a0 + _mm(p_ad, al_ref[...], NN, RWKV_PASSES))
    g = _mm(_sigmoid(p_gd), gl_ref[...], NN, RWKV_PASSES)
    kk = k_raw * k_k
    kk = kk / jnp.maximum(jnp.sqrt(seg_sum(kk * kk)), L2_EPS)
    k = k_raw * (1.0 + (a - 1.0) * k_a)

    row = lax.broadcasted_iota(jnp.int32, (rows, rows), 0)
    col = lax.broadcasted_iota(jnp.int32, (rows, rows), 1)
    tri = ((row >= col) & (row // c == col // c)).astype(BF16)
    cum = _mm_exact_lhs(tri, log_w)
    d_incl = jnp.exp(cum)
    d_inv = jnp.exp(-cum)
    d_excl = jnp.exp(cum - log_w)
    rt = r * d_incl
    kt = k * d_inv
    at = -kk * d_excl
    bt = kk * a * d_inv

    y, s_next = _rwkv_chunks(rt, kt, at, bt, v, d_incl, [s_ref[p] for p in range(N_PAIRS)],
                             RWKV_PASSES, RWKV_STATE_PASSES)
    for p in range(N_PAIRS):
        s_ref[p] = s_next[p]

    inv_n = 1.0 / HEAD_DIM
    mu = seg_sum(y) * inv_n
    yc = y - mu
    var = seg_sum(yc * yc) * inv_n
    yn = yc * lax.rsqrt(var + GN_EPS) * gn_w + gn_b
    bonus = seg_sum(r * k * r_k) * v
    y_ref[...] = ((yn + bonus) * g).astype(y_ref.dtype)


def _rwkv_call(p_rkv, p_lora, vecs, wl, al, gl, bd, batch, seq_len):
    n = p_rkv.shape[0]
    rows = RWKV_CHUNK * RWKV_CHUNKS_PER_STEP
    assert seq_len % rows == 0
    steps = seq_len // rows
    row_map = lambda b, i: (b * steps + i, 0)
    const = lambda b, i: (0, 0)
    return pl.pallas_call(
        _rwkv_kernel,
        grid=(batch, steps),
        in_specs=[
            pl.BlockSpec((rows, 3 * RWKV_WIDTH), row_map),
            pl.BlockSpec((rows, LORA_PAD), row_map),
            pl.BlockSpec((8, RWKV_WIDTH), const),
            pl.BlockSpec((LORA_PAD, RWKV_WIDTH), const),
            pl.BlockSpec((LORA_PAD, RWKV_WIDTH), const),
            pl.BlockSpec((LORA_PAD, RWKV_WIDTH), const),
            pl.BlockSpec((2 * PAIR, 2 * PAIR), const),
        ],
        out_specs=pl.BlockSpec((rows, RWKV_WIDTH), row_map),
        out_shape=jax.ShapeDtypeStruct((n, RWKV_WIDTH), BF16),
        scratch_shapes=[pltpu.VMEM((N_PAIRS, PAIR, PAIR), F32)],
        compiler_params=pltpu.CompilerParams(
            dimension_semantics=("arbitrary", "arbitrary"), vmem_limit_bytes=VMEM_LIMIT),
        name="rwkv7_chunked",
    )(p_rkv, p_lora, vecs, wl, al, gl, bd)


def _moba_kernel(q_ref, k_ref, v_ref, kaug_ref, o_ref, kmean_ref, vt_ref,
                 m0_ref, m1_ref, acc0_ref, acc1_ref, s_even_ref, s_odd_ref,
                 smax_even_ref, smax_odd_ref, *, n_blocks):
    blk = MOBA_BLOCK
    tk = MOBA_KV_TILE
    tq = q_ref.shape[0]
    i = pl.program_id(2)
    nb_pad = kmean_ref.shape[0]
    m_refs, acc_refs = (m0_ref, m1_ref), (acc0_ref, acc1_ref)

    @pl.when(i == 0)
    def _():
        kmean_ref[...] = jnp.zeros_like(kmean_ref)

        def mean_body(n, carry):
            off = pl.multiple_of(n * blk, blk)
            kb = k_ref[pl.ds(off, blk), :].astype(F32)
            kmean_ref[pl.ds(n, 1), :] = jnp.sum(kb, axis=0, keepdims=True) * (1.0 / blk)
            return carry
        lax.fori_loop(0, n_blocks, mean_body, 0)

        ones = jnp.ones((MOBA_V_ROWS - HEAD_DIM, tk), BF16)

        def vt_body(j, carry):
            off = pl.multiple_of(j * tk, tk)
            v_t = v_ref[pl.ds(off, tk), :].astype(F32).T.astype(BF16)
            for h in (0, 1):
                vt_ref[j, h] = jnp.concatenate([v_t[h * HEAD_DIM:(h + 1) * HEAD_DIM], ones], axis=0)
            return carry
        lax.fori_loop(0, vt_ref.shape[0], vt_body, 0)

    q_t = q_ref[...].astype(F32).T
    chan = lax.broadcasted_iota(jnp.int32, (PAIR, tq), 0)
    blk_row = lax.broadcasted_iota(jnp.int32, (nb_pad, tq), 0)
    own_blk = (i * tq + lax.broadcasted_iota(jnp.int32, (nb_pad, tq), 1)) // blk
    past = blk_row < own_blk
    aug_row = lax.broadcasted_iota(jnp.int32, (LANES, tq), 0)
    ones_rows = (aug_row >= n_blocks) & (aug_row < n_blocks + MOBA_ALIBI_PARTS)
    kmean = kmean_ref[...]

    qa_t = []
    for h in (0, 1):
        qh_t = jnp.where((chan < HEAD_DIM) == (h == 0), q_t, 0.0)
        gate = _mm_exact_rhs(kmean, qh_t.astype(BF16))
        gate = jnp.where(past, gate, F32_LOWEST)
        sel = jnp.zeros(gate.shape, jnp.bool_)
        for _ in range(MOBA_TOPK):
            mx = jnp.max(gate, axis=0, keepdims=True)
            first = jnp.min(jnp.where(gate == mx, blk_row, nb_pad), axis=0, keepdims=True)
            pick = (blk_row == first) & (mx > F32_LOWEST)
            sel = sel | pick
            gate = jnp.where(pick, F32_LOWEST, gate)
        sel_bias = jnp.where(past & jnp.logical_not(sel), NEG_INF, 0.0)
        aug_t = jnp.concatenate([sel_bias, jnp.zeros((LANES - nb_pad, tq), F32)], axis=0)
        aug_t = jnp.where(ones_rows, 1.0, aug_t)
        qa_t.append(jnp.concatenate([qh_t * (LOG2_E / math.sqrt(HEAD_DIM)), aug_t], axis=0).astype(BF16))

    def tile_scores(j, lanes=slice(None)):
        off = pl.multiple_of(j * tk, tk)
        k_t = k_ref[pl.ds(off, tk), :]
        return [_dot(jnp.concatenate([k_t, kaug_ref[h, pl.ds(off, tk), :]], axis=1), qa_t[h][:, lanes])
                for h in (0, 1)]

    def put_scores(buf, s, lanes=slice(None)):
        for h in (0, 1):
            buf[0][h, :, lanes] = s[h]
            buf[1][h, :, lanes] = jnp.max(s[h], axis=0, keepdims=True)

    def tile_update(j, buf, lanes=slice(None)):
        s_buf, smax_buf = buf
        for h in (0, 1):
            m_old = m_refs[h][:, lanes]
            m_new = jnp.maximum(m_old, smax_buf[h, :, lanes])
            p = jnp.exp2(s_buf[h, :, lanes] - m_new).astype(BF16)
            pv = _dot(vt_ref[j, h], p)
            acc_refs[h][:, lanes] = jnp.exp2(m_old - m_new) * acc_refs[h][:, lanes] + pv
            m_refs[h][:, lanes] = m_new

    for h in (0, 1):
        m_refs[h][...] = jnp.full(m_refs[h].shape, F32_LOWEST, F32)
        acc_refs[h][...] = jnp.zeros(acc_refs[h].shape, F32)

    buffers = ((s_even_ref, smax_even_ref), (s_odd_ref, smax_odd_ref))
    n_own = tq // tk
    j_first = i * n_own
    diagonal = (lax.broadcasted_iota(jnp.int32, (tk, tk), 0) <= lax.broadcasted_iota(jnp.int32, (tk, tk), 1))

    def own_scores(g):
        s = tile_scores(j_first + g, slice(g * tk, tq))
        own = [jnp.where(diagonal, s_h[:, :tk], NEG_INF) for s_h in s]
        if g == n_own - 1:
            return own
        return [jnp.concatenate([own_h, s_h[:, tk:]], axis=1) for own_h, s_h in zip(own, s)]

    put_scores(buffers[0], own_scores(n_own - 1), slice((n_own - 1) * tk, tq))
    for m in range(1, n_own):
        g = n_own - 1 - m
        put_scores(buffers[m % 2], own_scores(g), slice(g * tk, tq))
        tile_update(j_first + g + 1, buffers[(m - 1) % 2], slice((g + 1) * tk, tq))
    cur, nxt = buffers[(n_own - 1) % 2], buffers[n_own % 2]

    def previous_tile(j):
        return jnp.where(j == 0, j_first, j - 1)

    def pipelined_step(j, src, dst):
        put_scores(dst, tile_scores(j))
        tile_update(previous_tile(j), src)

    def kv_pair_step(u, carry):
        pipelined_step(2 * u, cur, nxt)
        pipelined_step(2 * u + 1, nxt, cur)
        return carry
    lax.fori_loop(0, j_first // 2, kv_pair_step, 0)

    @pl.when(j_first % 2 == 1)
    def _():
        pipelined_step(j_first - 1, cur, nxt)
        tile_update(j_first - 1, nxt)

    @pl.when(j_first % 2 == 0)
    def _():
        tile_update(previous_tile(j_first), cur)

    out_t = jnp.concatenate([acc_refs[h][0:HEAD_DIM, :] / acc_refs[h][HEAD_DIM:HEAD_DIM + 1, :]
                             for h in (0, 1)], axis=0)
    o_ref[...] = out_t.T.astype(o_ref.dtype)


def _moba_call(qkv, kaug, batch, seq_len):
    n = qkv.shape[0]
    blk = MOBA_BLOCK
    tq = MOBA_Q_TILE
    nb = seq_len // blk
    nq = seq_len // tq
    assert nb + MOBA_ALIBI_PARTS <= LANES and seq_len % MOBA_KV_TILE == 0
    assert tq % MOBA_KV_TILE == 0 and seq_len % tq == 0
    lane_groups = MOBA_WIDTH // LANES
    return pl.pallas_call(
        functools.partial(_moba_kernel, n_blocks=nb),
        grid=(batch, N_PAIRS, nq),
        in_specs=[
            pl.BlockSpec((tq, PAIR), lambda b, p, i: (b * nq + i, p)),
            pl.BlockSpec((seq_len, PAIR), lambda b, p, i: (b, lane_groups + p)),
            pl.BlockSpec((seq_len, PAIR), lambda b, p, i: (b, 2 * lane_groups + p)),
            pl.BlockSpec((2, seq_len, LANES), lambda b, p, i: (p, 0, 0)),
        ],
        out_specs=pl.BlockSpec((tq, PAIR), lambda b, p, i: (b * nq + i, p)),
        out_shape=jax.ShapeDtypeStruct((n, MOBA_WIDTH), BF16),
        scratch_shapes=[
            pltpu.VMEM((-(-nb // 8) * 8, PAIR), F32),
            pltpu.VMEM((seq_len // MOBA_KV_TILE, 2, MOBA_V_ROWS, MOBA_KV_TILE), BF16),
            pltpu.VMEM((1, tq), F32), pltpu.VMEM((1, tq), F32),
            pltpu.VMEM((MOBA_V_ROWS, tq), F32), pltpu.VMEM((MOBA_V_ROWS, tq), F32),
            pltpu.VMEM((2, MOBA_KV_TILE, tq), F32), pltpu.VMEM((2, MOBA_KV_TILE, tq), F32),
            pltpu.VMEM((2, 1, tq), F32), pltpu.VMEM((2, 1, tq), F32),
        ],
        compiler_params=pltpu.CompilerParams(
            dimension_semantics=("arbitrary", "arbitrary", "arbitrary"),
            vmem_limit_bytes=VMEM_LIMIT),
        name="moba_attention",
    )(qkv, qkv, qkv, kaug)


def _moba_key_aug(seq_len):
    nb = seq_len // MOBA_BLOCK
    heads = MOBA_WIDTH // HEAD_DIM
    pos = np.arange(seq_len, dtype=np.int32)
    slopes = (2.0 ** (-8.0 * (np.arange(heads, dtype=np.float32) + 1.0) / heads)).astype(np.float32)
    aug = np.zeros((heads, seq_len, LANES), np.float32)
    aug[:, pos, pos // MOBA_BLOCK] = 1.0
    rem = (np.float32(LOG2_E) * slopes)[:, None] * pos.astype(np.float32)[None, :]
    for part in range(MOBA_ALIBI_PARTS):
        piece = (rem.view(np.uint32) & np.uint32(0xFFFF0000)).view(np.float32)
        aug[:, :, nb + part] = piece
        rem = rem - piece
    return jnp.asarray(aug.astype(BF16))


def _layer_norm(z, g, b):
    mu = jnp.mean(z, axis=-1, keepdims=True)
    zc = z - mu
    var = jnp.mean(zc * zc, axis=-1, keepdims=True)
    return zc * lax.rsqrt(var + LN_EPS) * g + b


def _outproj_kernel(ya_ref, yb_ref, x_ref, wa_ref, wb_ref, g_ref, b_ref, wr_ref, br_ref,
                    h_ref, lg_ref):
    mix = _dot(ya_ref[...], wa_ref[...]) + _dot(yb_ref[...], wb_ref[...])
    h = _layer_norm(DEEPNORM_ALPHA * x_ref[...] + mix, g_ref[...], b_ref[...])
    h_ref[...] = h
    h_hi, h_lo = _split_bf16(h, 2)
    w_hi, w_lo = _split_bf16(wr_ref[...], 2)
    hh_hl = _dot(h_hi, jnp.concatenate([w_hi, w_lo], axis=1))
    lg_ref[...] = hh_hl[:, :ROUTER_PAD] + (hh_hl[:, ROUTER_PAD:] + _dot(h_lo, w_hi)) + br_ref[...]


def _outproj_call(y_a, y_b, x2, wa, wb, ln_g, ln_b, w_router, b_router):
    n = x2.shape[0]
    tm = OUTPROJ_TM
    row = lambda i: (i, 0)
    const = lambda i: (0, 0)
    return pl.pallas_call(
        _outproj_kernel,
        grid=(n // tm,),
        in_specs=[
            pl.BlockSpec((tm, RWKV_WIDTH), row),
            pl.BlockSpec((tm, MOBA_WIDTH), row),
            pl.BlockSpec((tm, D_MODEL), row),
            pl.BlockSpec((RWKV_WIDTH, D_MODEL), const),
            pl.BlockSpec((MOBA_WIDTH, D_MODEL), const),
            pl.BlockSpec((1, D_MODEL), const),
            pl.BlockSpec((1, D_MODEL), const),
            pl.BlockSpec((D_MODEL, ROUTER_PAD), const),
            pl.BlockSpec((1, ROUTER_PAD), const),
        ],
        out_specs=[
            pl.BlockSpec((tm, D_MODEL), row),
            pl.BlockSpec((tm, ROUTER_PAD), row),
        ],
        out_shape=[
            jax.ShapeDtypeStruct((n, D_MODEL), F32),
            jax.ShapeDtypeStruct((n, ROUTER_PAD), F32),
        ],
        compiler_params=pltpu.CompilerParams(
            dimension_semantics=("arbitrary",), vmem_limit_bytes=VMEM_LIMIT),
        name="outproj_ln_router",
    )(y_a, y_b, x2, wa, wb, ln_g, ln_b, w_router, b_router)


def _route(logits):
    lane = lax.broadcasted_iota(jnp.int32, logits.shape, 1)
    is_group = (lane >= GROUP_LANE0) & (lane < GROUP_LANE0 + N_GROUPS)
    gl = jnp.where(is_group, logits, F32_LOWEST)
    g_max = jnp.max(gl, axis=-1, keepdims=True)
    g_first = jnp.min(jnp.where(gl == g_max, lane, LANES), axis=-1, keepdims=True)
    g_exp = jnp.where(is_group, jnp.exp(gl - g_max), 0.0)
    p_g = 1.0 / jnp.sum(g_exp, axis=-1, keepdims=True)
    g_idx = g_first - GROUP_LANE0
    in_group = (lane >= g_idx * EXPERTS_PER_GROUP) & (lane < (g_idx + 1) * EXPERTS_PER_GROUP)
    el = jnp.where(in_group, logits, F32_LOWEST)
    e_max = jnp.max(el, axis=-1, keepdims=True)
    e_exp = jnp.where(in_group, jnp.exp(el - e_max), 0.0)
    e_prob = e_exp / jnp.sum(e_exp, axis=-1, keepdims=True)
    cand = jnp.where(in_group, e_prob, -1.0)
    v1 = jnp.max(cand, axis=-1, keepdims=True)
    i1 = jnp.min(jnp.where(cand == v1, lane, LANES), axis=-1, keepdims=True)
    pick1 = lane == i1
    cand2 = jnp.where(pick1, -1.0, cand)
    v2 = jnp.max(cand2, axis=-1, keepdims=True)
    i2 = jnp.min(jnp.where(cand2 == v2, lane, LANES), axis=-1, keepdims=True)
    pick2 = lane == i2
    denom = v1 + v2
    gates = jnp.where(pick1, v1 / denom * p_g, jnp.where(pick2, v2 / denom * p_g, 0.0))
    return gates, g_idx


def _route_sort_kernel(lg_ref, gates_ref, pos_ref, flags_ref, earlier_ref):
    tm = lg_ref.shape[0]

    @pl.when(pl.program_id(0) == 0)
    def _():
        row = lax.broadcasted_iota(jnp.int32, (tm, tm), 0)
        col = lax.broadcasted_iota(jnp.int32, (tm, tm), 1)
        earlier_ref[...] = jnp.where(row > col, 1.0, 0.0).astype(BF16)

    gates, g_idx = _route(lg_ref[...])
    lane = lax.broadcasted_iota(jnp.int32, (tm, LANES), 1)
    in_own = lane == g_idx
    onehot = jnp.where(in_own, 1.0, 0.0)
    rank = _dot(earlier_ref[...], onehot.astype(BF16))
    count = jnp.sum(onehot, axis=0, keepdims=True)
    r128 = lax.broadcasted_iota(jnp.int32, (LANES, LANES), 0)
    c128 = lax.broadcasted_iota(jnp.int32, (LANES, LANES), 1)
    below = jnp.where(r128 < c128, 1.0, 0.0)
    start = _dot(jnp.broadcast_to(count, (8, LANES)), below, NN,
                 precision=lax.Precision.HIGHEST)[0:1]
    pos = jnp.sum(jnp.where(in_own, rank + start, 0.0), axis=-1, keepdims=True)
    gates_ref[...] = gates
    pos_ref[...] = jnp.broadcast_to(pos, (tm, LANES))
    start_c = jnp.broadcast_to(start, (LANES, LANES)).T
    count_c = jnp.broadcast_to(count, (LANES, LANES)).T
    sub_lo = c128 * MOE_SUB
    hit = (count_c > 0.0) & (start_c < (sub_lo + MOE_SUB).astype(F32)) & (start_c + count_c > sub_lo.astype(F32))
    start_i = start_c.astype(jnp.int32)
    count_i = count_c.astype(jnp.int32)
    win = jnp.minimum((start_i // MOE_WINDOW_ALIGN) * MOE_WINDOW_ALIGN, tm - MOE_WINDOW)
    fits = (count_i > 0) & (start_i + count_i <= win + MOE_WINDOW)
    meta = jnp.where(c128 == MOE_META_FITS, jnp.where(fits, 1, 0),
                     jnp.where(c128 == MOE_META_WINDOW, win, jnp.where(hit, 1, 0)))
    flags_ref[0] = meta[0:8].astype(jnp.int32)


def _route_sort_call(logits):
    n = logits.shape[0]
    tm = MOE_TM
    row = lambda t: (t, 0)
    return pl.pallas_call(
        _route_sort_kernel,
        grid=(n // tm,),
        in_specs=[pl.BlockSpec((tm, ROUTER_PAD), row)],
        out_specs=[
            pl.BlockSpec((tm, ROUTER_PAD), row),
            pl.BlockSpec((tm, LANES), row),
            pl.BlockSpec((1, 8, LANES), lambda t: (t, 0, 0)),
        ],
        out_shape=[
            jax.ShapeDtypeStruct((n, ROUTER_PAD), F32),
            jax.ShapeDtypeStruct((n, LANES), F32),
            jax.ShapeDtypeStruct((n // tm, 8, LANES), jnp.int32),
        ],
        scratch_shapes=[pltpu.VMEM((tm, tm), BF16)],
        compiler_params=pltpu.CompilerParams(
            dimension_semantics=("arbitrary",), vmem_limit_bytes=VMEM_LIMIT),
        name="route_sort",
    )(logits)


def _moe_kernel(flags_ref, h_ref, gates_ref, pos_ref, w1_ref, w3_ref, w2_ref, g_ref, b_ref,
                o_ref, xs_ref, gs_ref, acc_ref, pt_ref):
    tile = pl.program_id(0)
    step = pl.program_id(1)
    tm = h_ref.shape[0]
    sub = MOE_SUB
    n_sub = tm // sub
    eps = MOE_EXPERTS_PER_STEP

    @pl.when(step == 0)
    def _():
        pos_b = pos_ref[...]
        pos_row = pos_b.T[0:1, :]
        g_hi, g_lo = _split_bf16(gates_ref[...], 2)
        src = jnp.concatenate([h_ref[...].astype(BF16), g_hi, g_lo], axis=1)
        for c0 in range(0, tm, sub):
            slot = (lax.broadcasted_iota(jnp.int32, (sub, tm), 0) + c0).astype(F32)
            p_c = jnp.where(slot == pos_row, 1.0, 0.0).astype(BF16)
            moved = _dot(p_c, src)
            xs_ref[c0:c0 + sub, :] = moved[:, :D_MODEL].astype(BF16)
            gs_ref[c0:c0 + sub, :] = moved[:, D_MODEL:D_MODEL + LANES] + moved[:, D_MODEL + LANES:]
        for c0 in range(0, tm, LANES):
            slot = (lax.broadcasted_iota(jnp.int32, (tm, LANES), 1) + c0).astype(F32)
            pt_ref[:, c0:c0 + LANES] = jnp.where(pos_b == slot, 1.0, 0.0).astype(BF16)
        acc_ref[...] = jnp.zeros_like(acc_ref)

    group = step // (EXPERTS_PER_GROUP // eps)

    def visit(rows):
        x_r = xs_ref[rows, :]
        g_r = gs_ref[rows, :]
        lane = lax.broadcasted_iota(jnp.int32, g_r.shape, 1)
        acc = acc_ref[rows, :]
        for e in range(eps):
            gate_e = jnp.sum(jnp.where(lane == step * eps + e, g_r, 0.0), axis=-1, keepdims=True)
            a1 = _dot(x_r, w1_ref[e])
            a3 = _dot(x_r, w3_ref[e])
            hid = (a1 * _sigmoid(a1)) * a3 * gate_e
            acc = acc + _dot(hid.astype(BF16), w2_ref[e])
        acc_ref[rows, :] = acc

    base = (tile * N_GROUPS + group) * (n_sub + 2)
    fits = flags_ref[base + n_sub] != 0

    @pl.when(fits)
    def _():
        first = pl.multiple_of(flags_ref[base + n_sub + 1], MOE_WINDOW_ALIGN)
        visit(pl.ds(first, MOE_WINDOW))

    for r in range(n_sub):
        pl.when(jnp.logical_not(fits) & (flags_ref[base + r] != 0))(
            functools.partial(visit, slice(r * sub, (r + 1) * sub)))

    @pl.when(step == N_EXPERTS // eps - 1)
    def _():
        ffn = _dot(pt_ref[...], acc_ref[...].astype(BF16))
        o_ref[...] = _layer_norm(DEEPNORM_ALPHA * h_ref[...] + ffn, g_ref[...], b_ref[...])


def _moe_call(flags, h_f32, gates, pos, w1, w3, w2, ln_g, ln_b):
    n = h_f32.shape[0]
    tm = MOE_TM
    row = lambda t, s, f: (t, 0)
    const = lambda t, s, f: (0, 0)
    wmap = lambda t, s, f: (s, 0, 0)
    eps = MOE_EXPERTS_PER_STEP
    assert EXPERTS_PER_GROUP % eps == 0 and tm % MOE_SUB == 0
    grid_spec = pltpu.PrefetchScalarGridSpec(
        num_scalar_prefetch=1,
        grid=(n // tm, N_EXPERTS // eps),
        in_specs=[
            pl.BlockSpec((tm, D_MODEL), row),
            pl.BlockSpec((tm, ROUTER_PAD), row),
            pl.BlockSpec((tm, LANES), row),
            pl.BlockSpec((eps, D_MODEL, D_EXPERT), wmap),
            pl.BlockSpec((eps, D_MODEL, D_EXPERT), wmap),
            pl.BlockSpec((eps, D_EXPERT, D_MODEL), wmap),
            pl.BlockSpec((1, D_MODEL), const),
            pl.BlockSpec((1, D_MODEL), const),
        ],
        out_specs=pl.BlockSpec((tm, D_MODEL), row),
        scratch_shapes=[
            pltpu.VMEM((tm, D_MODEL), BF16),
            pltpu.VMEM((tm, ROUTER_PAD), F32),
            pltpu.VMEM((tm, D_MODEL), F32),
            pltpu.VMEM((tm, tm), BF16),
        ],
    )
    return pl.pallas_call(
        _moe_kernel,
        grid_spec=grid_spec,
        out_shape=jax.ShapeDtypeStruct((n, D_MODEL), F32),
        compiler_params=pltpu.CompilerParams(
            dimension_semantics=("arbitrary", "arbitrary"), vmem_limit_bytes=VMEM_LIMIT),
        name="hier_moe_ln",
    )(flags, h_f32, gates, pos, w1, w3, w2, ln_g, ln_b)


def _pad_cols(w, width):
    return jnp.pad(w, ((0, 0), (0, width - w.shape[1])))


def _pad_rows(w, height):
    return jnp.pad(w, ((0, height - w.shape[0]), (0, 0)))


def kernel(x, w_in, mu_shift, w0, w_lora_up, a0, a_lora_up, g_lora_up, k_k, k_a, r_k, gn_w, gn_b, w_out, ln1_g, ln1_b, w_group, b_group, w_expert, b_expert, w1_exp, w3_exp, w2_exp, ln2_g, ln2_b):
    batch, seq_len, d = x.shape
    assert d == D_MODEL
    n = batch * seq_len
    x2 = x.reshape(n, d)

    c_rkv = 3 * RWKV_WIDTH
    c_wd = c_rkv + DECAY_RANK
    c_ad = c_wd + AAA_RANK
    c_gd = c_ad + GATE_RANK
    w_cat = jnp.concatenate([
        w_in[:, :c_rkv], _pad_cols(w_in[:, c_rkv:c_gd], LORA_PAD), w_in[:, c_gd:],
    ], axis=1).astype(BF16)
    mu2 = mu_shift[None, :]
    mu_cat = jnp.concatenate([mu2[:, :c_rkv], _pad_cols(mu2[:, c_rkv:c_gd], LORA_PAD)], axis=1)
    p_rkv, p_lora, p_moba = _inproj_call(x2, w_cat, mu_cat, seq_len)

    place = lambda w, first: jnp.pad(w, ((first, LORA_PAD - first - w.shape[0]), (0, 0)))
    lora_up = (place(w_lora_up, 0), place(a_lora_up, DECAY_RANK), place(g_lora_up, DECAY_RANK + AAA_RANK))

    vecs = jnp.stack([w0, a0, k_k, k_a, r_k.reshape(-1), gn_w, gn_b, jnp.zeros_like(w0)], axis=0)
    head_id = jnp.arange(2 * PAIR) // HEAD_DIM
    bd = (head_id[:, None] == head_id[None, :]).astype(BF16)
    y_a = _rwkv_call(p_rkv, p_lora, vecs, *lora_up, bd, batch, seq_len)

    y_b = _moba_call(p_moba, _moba_key_aug(seq_len), batch, seq_len)

    w_out_b = w_out.astype(BF16)
    w_router = _pad_cols(jnp.concatenate([w_expert, w_group], axis=1), ROUTER_PAD)
    b_router = _pad_cols(jnp.concatenate([b_expert, b_group])[None, :], ROUTER_PAD)
    h1, logits = _outproj_call(y_a, y_b, x2, w_out_b[:RWKV_WIDTH], w_out_b[RWKV_WIDTH:],
                               ln1_g[None, :], ln1_b[None, :], w_router, b_router)

    flat = lambda w: w.astype(BF16).reshape((N_EXPERTS,) + w.shape[2:])
    gates, pos, flags = _route_sort_call(logits)
    flags = jnp.concatenate([flags[:, :N_GROUPS, :MOE_TM // MOE_SUB],
                             flags[:, :N_GROUPS, MOE_META_FITS:MOE_META_WINDOW + 1]], axis=-1).reshape(-1)
    out = _moe_call(flags, h1, gates, pos, flat(w1_exp), flat(w3_exp), flat(w2_exp),
                    ln2_g[None, :], ln2_b[None, :])
    return out.reshape(batch, seq_len, d)
```

```python
import functools
import math

import jax
import jax.numpy as jnp
import numpy as np
from jax import lax
from jax.experimental import pallas as pl
from jax.experimental.pallas import tpu as pltpu

F32 = jnp.float32
BF16 = jnp.bfloat16

D_MODEL = 1024
HEAD_DIM = 64
RWKV_WIDTH = 512
MOBA_WIDTH = 512
DECAY_RANK = 32
AAA_RANK = 32
GATE_RANK = 96
GN_EPS = 64e-5
L2_EPS = 1e-12
MOBA_BLOCK = 256
MOBA_TOPK = 3
N_GROUPS = 4
EXPERTS_PER_GROUP = 8
N_EXPERTS = N_GROUPS * EXPERTS_PER_GROUP
D_EXPERT = 256
LN_EPS = 1e-5
DEEPNORM_ALPHA = float(2.0 ** 0.25)
NEG_INF = -1e30
F32_LOWEST = -3.0e38

LANES = 128
PAIR = 2 * HEAD_DIM
N_PAIRS = RWKV_WIDTH // PAIR
LORA_PAD = 2 * LANES
RWKV_COLS_PAD = 3 * RWKV_WIDTH + LORA_PAD
IN_COLS_PAD = RWKV_COLS_PAD + 3 * MOBA_WIDTH
VMEM_LIMIT = 56 * 1024 * 1024

INPROJ_TM = 512
INPROJ_TN = 256
RWKV_CHUNK = 64
RWKV_CHUNKS_PER_STEP = 4
RWKV_PASSES = 1
RWKV_STATE_PASSES = 1
OUTPROJ_TM = 512
MOE_TM = 1024
MOE_EXPERTS_PER_STEP = 8
MOE_SUB = 256
MOE_WINDOW = 304
MOE_WINDOW_ALIGN = 16
MOE_META_FITS = 16
MOE_META_WINDOW = 17
MOBA_KV_TILE = 512
MOBA_Q_TILE = 2048
MOBA_V_ROWS = HEAD_DIM + 16
MOBA_ALIBI_PARTS = 3
LOG2_E = 1.4426950408889634
ROUTER_PAD = LANES
GROUP_LANE0 = N_EXPERTS

NN = (((1,), (0,)), ((), ()))
NT = (((1,), (1,)), ((), ()))


def _dot(a, b, dims=NN, precision=None):
    return lax.dot_general(a, b, dims, precision=precision, preferred_element_type=F32)


def _split_bf16(x, parts):
    out = []
    rem = x
    for i in range(parts):
        p = rem.astype(BF16)
        out.append(p)
        if i + 1 < parts:
            rem = rem - p.astype(F32)
    return out


def _mm(a, b, dims=NN, passes=3):
    if passes == 1:
        return _dot(a.astype(BF16), b.astype(BF16), dims)
    if passes == 6:
        return _dot(a, b, dims, precision=lax.Precision.HIGHEST)
    a_hi, a_lo = _split_bf16(a, 2)
    b_hi, b_lo = _split_bf16(b, 2)
    return _dot(a_hi, b_hi, dims) + (_dot(a_hi, b_lo, dims) + _dot(a_lo, b_hi, dims))


def _mm_exact_lhs(a_bf16, b, dims=NN):
    b1, b2, b3 = _split_bf16(b, 3)
    return _dot(a_bf16, b1, dims) + (_dot(a_bf16, b2, dims) + _dot(a_bf16, b3, dims))


def _mm_exact_rhs(a, b_bf16, dims=NN):
    a1, a2, a3 = _split_bf16(a, 3)
    return _dot(a1, b_bf16, dims) + (_dot(a2, b_bf16, dims) + _dot(a3, b_bf16, dims))


def _inproj_kernel(x_ref, w_ref, mu_ref, prkv_ref, plora_ref, pk_ref, qvt_ref, carry_ref, *, tiles_per_seq):
    tm = x_ref.shape[0]
    xb = x_ref[...].astype(BF16)
    seq_start = (pl.program_id(0) % tiles_per_seq) == 0
    row0 = lax.broadcasted_iota(jnp.int32, (tm, INPROJ_TN), 0) == 0
    n_shift_tiles = RWKV_COLS_PAD // INPROJ_TN
    for j in range(n_shift_tiles):
        c0 = j * INPROJ_TN
        acc = _dot(xb, w_ref[:, c0:c0 + INPROJ_TN])
        prev_last = jnp.where(seq_start, 0.0, carry_ref[0:1, c0:c0 + INPROJ_TN])
        shifted = jnp.where(row0, prev_last, pltpu.roll(acc, 1, 0))
        carry_ref[0:1, c0:c0 + INPROJ_TN] = acc[tm - 1:tm, :]
        out = acc + (shifted - acc) * mu_ref[:, c0:c0 + INPROJ_TN]
        if c0 < 3 * RWKV_WIDTH:
            prkv_ref[:, c0:c0 + INPROJ_TN] = out
        else:
            plora_ref[:, c0 - 3 * RWKV_WIDTH:c0 - 3 * RWKV_WIDTH + INPROJ_TN] = out
    tiles_per_part = MOBA_WIDTH // INPROJ_TN
    for j in range(3 * tiles_per_part):
        c0 = j * INPROJ_TN
        acc = _dot(xb, w_ref[:, RWKV_COLS_PAD + c0:RWKV_COLS_PAD + c0 + INPROJ_TN])
        part, r0 = divmod(c0, MOBA_WIDTH)
        if part == 1:
            pk_ref[:, r0:r0 + INPROJ_TN] = acc.astype(BF16)
        else:
            r0 += (part // 2) * MOBA_WIDTH
            qvt_ref[r0:r0 + INPROJ_TN, :] = acc.T.astype(BF16)


def _inproj_call(x2, w_cat, mu_cat, seq_len):
    n = x2.shape[0]
    tm = INPROJ_TM
    assert seq_len % tm == 0 and (3 * RWKV_WIDTH) % INPROJ_TN == 0
    return pl.pallas_call(
        functools.partial(_inproj_kernel, tiles_per_seq=seq_len // tm),
        grid=(n // tm,),
        in_specs=[
            pl.BlockSpec((tm, D_MODEL), lambda i: (i, 0)),
            pl.BlockSpec((D_MODEL, IN_COLS_PAD), lambda i: (0, 0)),
            pl.BlockSpec((1, RWKV_COLS_PAD), lambda i: (0, 0)),
        ],
        out_specs=[
            pl.BlockSpec((tm, 3 * RWKV_WIDTH), lambda i: (i, 0)),
            pl.BlockSpec((tm, LORA_PAD), lambda i: (i, 0)),
            pl.BlockSpec((tm, MOBA_WIDTH), lambda i: (i, 0)),
            pl.BlockSpec((2 * MOBA_WIDTH, tm), lambda i: (0, i)),
        ],
        out_shape=[
            jax.ShapeDtypeStruct((n, 3 * RWKV_WIDTH), F32),
            jax.ShapeDtypeStruct((n, LORA_PAD), F32),
            jax.ShapeDtypeStruct((n, MOBA_WIDTH), BF16),
            jax.ShapeDtypeStruct((2 * MOBA_WIDTH, n), BF16),
        ],
        scratch_shapes=[pltpu.VMEM((8, RWKV_COLS_PAD), F32)],
        compiler_params=pltpu.CompilerParams(
            dimension_semantics=("arbitrary",), vmem_limit_bytes=VMEM_LIMIT),
        name="inproj_shift",
    )(x2, w_cat, mu_cat)


def _softplus(z):
    return jnp.maximum(z, 0.0) + jnp.log(1.0 + jnp.exp(-jnp.abs(z)))


def _sigmoid(z):
    return 1.0 / (1.0 + jnp.exp(-z))


def _rwkv_chunks(rt, kt, at, bt, v, d_incl, s_prev, passes, state_passes):
    c = RWKV_CHUNK
    n_chunks = rt.shape[0] // c
    n_pairs = len(s_prev)
    row = lax.broadcasted_iota(jnp.int32, (c, PAIR), 0)
    col = lax.broadcasted_iota(jnp.int32, (c, PAIR), 1) % HEAD_DIM
    strict = row > col
    incl = row >= col
    eye_c = (row == col).astype(F32)
    lane = lax.broadcasted_iota(jnp.int32, (1, PAIR), 1)
    head0 = lane < HEAD_DIM
    head1 = jnp.logical_not(head0)
    prow = lax.broadcasted_iota(jnp.int32, (PAIR, PAIR), 0)
    pcol = lax.broadcasted_iota(jnp.int32, (PAIR, PAIR), 1)
    same_head = (prow < HEAD_DIM) == (pcol < HEAD_DIM)
    eye_p = (prow == pcol).astype(F32)
    rows = [slice(ci * c, (ci + 1) * c) for ci in range(n_chunks)]
    sl = [slice(p * PAIR, (p + 1) * PAIR) for p in range(n_pairs)]
    pairs = [(ci, p) for ci in range(n_chunks) for p in range(n_pairs)]
    cut = lambda t, ci, p: t[rows[ci], sl[p]]

    def by_head(m):
        return jnp.concatenate([jnp.where(head0, m, 0.0), jnp.where(head1, m, 0.0)], axis=0)

    def by_head2(m, n):
        return jnp.concatenate([by_head(m), by_head(n)], axis=1)

    at_p = {k_: cut(at, *k_) for k_ in pairs}
    rt_p = {k_: cut(rt, *k_) for k_ in pairs}
    bt_p = {k_: cut(bt, *k_) for k_ in pairs}
    kt_p = {k_: cut(kt, *k_) for k_ in pairs}
    v_p = {k_: cut(v, *k_) for k_ in pairs}

    z = {k_: _mm(jnp.concatenate([at_p[k_], rt_p[k_]], axis=0),
                 jnp.concatenate([by_head(bt_p[k_]).T, by_head(kt_p[k_]).T], axis=1), NN, passes)
         for k_ in pairs}
    l_ab = {k_: jnp.where(strict, z[k_][:c, :PAIR], 0.0) for k_ in pairs}
    l_ak = {k_: jnp.where(strict, z[k_][:c, PAIR:], 0.0) for k_ in pairs}
    m_rb = {k_: jnp.where(incl, z[k_][c:, :PAIR], 0.0) for k_ in pairs}
    m_rk = {k_: jnp.where(incl, z[k_][c:, PAIR:], 0.0) for k_ in pairs}
    pw = {k_: _mm(l_ab[k_], by_head(l_ab[k_]), NN, passes) for k_ in pairs}
    t_inv = {k_: eye_c + l_ab[k_] for k_ in pairs}
    for _ in range(int(math.log2(c)) - 1):
        tp = {k_: _mm(jnp.concatenate([t_inv[k_], pw[k_]], axis=0), by_head(pw[k_]), NN, passes)
              for k_ in pairs}
        t_inv = {k_: t_inv[k_] + tp[k_][:c] for k_ in pairs}
        pw = {k_: tp[k_][c:] for k_ in pairs}
    lm = {k_: _mm(jnp.concatenate([l_ak[k_], m_rk[k_]], axis=0), by_head(v_p[k_]), NN, passes)
          for k_ in pairs}
    lv = {k_: lm[k_][:c] for k_ in pairs}
    mv = {k_: lm[k_][c:] for k_ in pairs}
    wu = {k_: _mm(t_inv[k_], by_head2(at_p[k_], lv[k_]), NN, passes) for k_ in pairs}
    qy = {k_: _mm(m_rb[k_], by_head2(wu[k_][:, :PAIR], wu[k_][:, PAIR:]), NN, passes) for k_ in pairs}

    qeff, y1, phi, psi = {}, {}, {}, {}
    for ci, p in pairs:
        k_ = (ci, p)
        w, u0 = wu[k_][:, :PAIR], wu[k_][:, PAIR:]
        qeff[k_] = rt_p[k_] + qy[k_][:, :PAIR]
        y1[k_] = qy[k_][:, PAIR:] + mv[k_]
        d_p = d_incl[(ci + 1) * c - 1:(ci + 1) * c, sl[p]]
        phi[k_] = jnp.where(same_head, (eye_p + _mm(w.T, bt_p[k_], NN, passes)) * d_p, 0.0)
        uv_t = jnp.concatenate([u0, v_p[k_]], axis=0).T
        bk = jnp.concatenate([bt_p[k_], kt_p[k_]], axis=0)
        psi[k_] = jnp.where(same_head, _mm(uv_t, bk, NN, passes) * d_p, 0.0)

    state = list(s_prev)
    ys = [[None] * n_pairs for _ in range(n_chunks)]
    for ci in range(n_chunks):
        for p in range(n_pairs):
            ys[ci][p] = _mm(qeff[ci, p], state[p].T, NN, state_passes) + y1[ci, p]
            state[p] = _mm(state[p], phi[ci, p], NN, state_passes) + psi[ci, p]
    y = jnp.concatenate([jnp.concatenate(ys[ci], axis=1) for ci in range(n_chunks)], axis=0)
    return y, state


def _rwkv_kernel(prkv_ref, plora_ref, vec_ref, wl_ref, al_ref, gl_ref, bd_ref, y_ref, s_ref):
    rows = prkv_ref.shape[0]
    c = RWKV_CHUNK
    width = RWKV_WIDTH

    @pl.when(pl.program_id(1) == 0)
    def _():
        s_ref[...] = jnp.zeros_like(s_ref)

    r = prkv_ref[:, 0:width]
    k_raw = prkv_ref[:, width:2 * width]
    v = prkv_ref[:, 2 * width:3 * width]
    p_wd = p_ad = p_gd = plora_ref[...]
    w0 = vec_ref[0:1, :]
    a0 = vec_ref[1:2, :]
    k_k = vec_ref[2:3, :]
    k_a = vec_ref[3:4, :]
    r_k = vec_ref[4:5, :]
    gn_w = vec_ref[5:6, :]
    gn_b = vec_ref[6:7, :]
    bd = bd_ref[...]

    def seg_sum(z):
        halves = []
        for c0 in range(0, width, bd.shape[0]):
            z_hi, z_lo = _split_bf16(z[:, c0:c0 + bd.shape[0]], 2)
            halves.append(_dot(z_hi, bd) + _dot(z_lo, bd))
        return jnp.concatenate(halves, axis=1)

    w_log = -_softplus(-(w0 + _mm(jnp.tanh(p_wd), wl_ref[...], NN, 3))) - 0.5
    log_w = -jnp.exp(w_log)
    a = _sigmoid(a0 + _mm(p_ad, al_ref[...], NN, RWKV_PASSES))
    g = _mm(_sigmoid(p_gd), gl_ref[...], NN, RWKV_PASSES)
    kk = k_raw * k_k
    kk = kk / jnp.maximum(jnp.sqrt(seg_sum(kk * kk)), L2_EPS)
    k = k_raw * (1.0 + (a - 1.0) * k_a)

    row = lax.broadcasted_iota(jnp.int32, (rows, rows), 0)
    col = lax.broadcasted_iota(jnp.int32, (rows, rows), 1)
    tri = ((row >= col) & (row // c == col // c)).astype(BF16)
    cum = _mm_exact_lhs(tri, log_w)
    d_incl = jnp.exp(cum)
    d_inv = jnp.exp(-cum)
    d_excl = jnp.exp(cum - log_w)
    rt = r * d_incl
    kt = k * d_inv
    at = -kk * d_excl
    bt = kk * a * d_inv

    y, s_next = _rwkv_chunks(rt, kt, at, bt, v, d_incl, [s_ref[p] for p in range(N_PAIRS)],
                             RWKV_PASSES, RWKV_STATE_PASSES)
    for p in range(N_PAIRS):
        s_ref[p] = s_next[p]

    inv_n = 1.0 / HEAD_DIM
    mu = seg_sum(y) * inv_n
    yc = y - mu
    var = seg_sum(yc * yc) * inv_n
    yn = yc * lax.rsqrt(var + GN_EPS) * gn_w + gn_b
    bonus = seg_sum(r * k * r_k) * v
    y_ref[...] = ((yn + bonus) * g).astype(y_ref.dtype)


def _rwkv_call(p_rkv, p_lora, vecs, wl, al, gl, bd, batch, seq_len):
    n = p_rkv.shape[0]
    rows = RWKV_CHUNK * RWKV_CHUNKS_PER_STEP
    assert seq_len % rows == 0
    steps = seq_len // rows
    row_map = lambda b, i: (b * steps + i, 0)
    const = lambda b, i: (0, 0)
    return pl.pallas_call(
        _rwkv_kernel,
        grid=(batch, steps),
        in_specs=[
            pl.BlockSpec((rows, 3 * RWKV_WIDTH), row_map),
            pl.BlockSpec((rows, LORA_PAD), row_map),
            pl.BlockSpec((8, RWKV_WIDTH), const),
            pl.BlockSpec((LORA_PAD, RWKV_WIDTH), const),
            pl.BlockSpec((LORA_PAD, RWKV_WIDTH), const),
            pl.BlockSpec((LORA_PAD, RWKV_WIDTH), const),
            pl.BlockSpec((2 * PAIR, 2 * PAIR), const),
        ],
        out_specs=pl.BlockSpec((rows, RWKV_WIDTH), row_map),
        out_shape=jax.ShapeDtypeStruct((n, RWKV_WIDTH), BF16),
        scratch_shapes=[pltpu.VMEM((N_PAIRS, PAIR, PAIR), F32)],
        compiler_params=pltpu.CompilerParams(
            dimension_semantics=("arbitrary", "arbitrary"), vmem_limit_bytes=VMEM_LIMIT),
        name="rwkv7_chunked",
    )(p_rkv, p_lora, vecs, wl, al, gl, bd)


def _moba_kernel(qt_ref, k_ref, vt_in_ref, kaug_ref, o_ref, kmean_ref, vt_ref,
                 m0_ref, m1_ref, acc0_ref, acc1_ref, s_even_ref, s_odd_ref,
                 smax_even_ref, smax_odd_ref, *, n_blocks):
    blk = MOBA_BLOCK
    tk = MOBA_KV_TILE
    tq = qt_ref.shape[1]
    i = pl.program_id(2)
    nb_pad = kmean_ref.shape[0]
    m_refs, acc_refs = (m0_ref, m1_ref), (acc0_ref, acc1_ref)

    @pl.when(i == 0)
    def _():
        kmean_ref[...] = jnp.zeros_like(kmean_ref)

        def mean_body(n, carry):
            off = pl.multiple_of(n * blk, blk)
            kb = k_ref[pl.ds(off, blk), :].astype(F32)
            kmean_ref[pl.ds(n, 1), :] = jnp.sum(kb, axis=0, keepdims=True) * (1.0 / blk)
            return carry
        lax.fori_loop(0, n_blocks, mean_body, 0)

        ones = jnp.ones((MOBA_V_ROWS - HEAD_DIM, tk), BF16)

        for j in range(vt_ref.shape[0]):
            for h in (0, 1):
                vt_ref[j, h] = jnp.concatenate(
                    [vt_in_ref[h * HEAD_DIM:(h + 1) * HEAD_DIM, j * tk:(j + 1) * tk], ones], axis=0)

    q_t = qt_ref[...].astype(F32)
    chan = lax.broadcasted_iota(jnp.int32, (PAIR, tq), 0)
    blk_row = lax.broadcasted_iota(jnp.int32, (nb_pad, tq), 0)
    own_blk = (i * tq + lax.broadcasted_iota(jnp.int32, (nb_pad, tq), 1)) // blk
    past = blk_row < own_blk
    aug_row = lax.broadcasted_iota(jnp.int32, (LANES, tq), 0)
    ones_rows = (aug_row >= n_blocks) & (aug_row < n_blocks + MOBA_ALIBI_PARTS)
    kmean = kmean_ref[...]

    qa_t = []
    for h in (0, 1):
        qh_t = jnp.where((chan < HEAD_DIM) == (h == 0), q_t, 0.0)
        gate = _mm_exact_rhs(kmean, qh_t.astype(BF16))
        gate = jnp.where(past, gate, F32_LOWEST)
        sel = jnp.zeros(gate.shape, jnp.bool_)
        for _ in range(MOBA_TOPK):
            mx = jnp.max(gate, axis=0, keepdims=True)
            first = jnp.min(jnp.where(gate == mx, blk_row, nb_pad), axis=0, keepdims=True)
            pick = (blk_row == first) & (mx > F32_LOWEST)
            sel = sel | pick
            gate = jnp.where(pick, F32_LOWEST, gate)
        sel_bias = jnp.where(past & jnp.logical_not(sel), NEG_INF, 0.0)
        aug_t = jnp.concatenate([sel_bias, jnp.zeros((LANES - nb_pad, tq), F32)], axis=0)
        aug_t = jnp.where(ones_rows, 1.0, aug_t)
        qa_t.append(jnp.concatenate([qh_t * (LOG2_E / math.sqrt(HEAD_DIM)), aug_t], axis=0).astype(BF16))

    def tile_scores(j, lanes=slice(None)):
        off = pl.multiple_of(j * tk, tk)
        k_t = k_ref[pl.ds(off, tk), :]
        return [_dot(jnp.concatenate([k_t, kaug_ref[h, pl.ds(off, tk), :]], axis=1), qa_t[h][:, lanes])
                for h in (0, 1)]

    def put_scores(buf, s, lanes=slice(None)):
        for h in (0, 1):
            buf[0][h, :, lanes] = s[h]
            buf[1][h, :, lanes] = jnp.max(s[h], axis=0, keepdims=True)

    def tile_update(j, buf, lanes=slice(None)):
        s_buf, smax_buf = buf
        for h in (0, 1):
            m_old = m_refs[h][:, lanes]
            m_new = jnp.maximum(m_old, smax_buf[h, :, lanes])
            p = jnp.exp2(s_buf[h, :, lanes] - m_new).astype(BF16)
            pv = _dot(vt_ref[j, h], p)
            acc_refs[h][:, lanes] = jnp.exp2(m_old - m_new) * acc_refs[h][:, lanes] + pv
            m_refs[h][:, lanes] = m_new

    for h in (0, 1):
        m_refs[h][...] = jnp.full(m_refs[h].shape, F32_LOWEST, F32)
        acc_refs[h][...] = jnp.zeros(acc_refs[h].shape, F32)

    buffers = ((s_even_ref, smax_even_ref), (s_odd_ref, smax_odd_ref))
    n_own = tq // tk
    j_first = i * n_own
    diagonal = (lax.broadcasted_iota(jnp.int32, (tk, tk), 0) <= lax.broadcasted_iota(jnp.int32, (tk, tk), 1))

    def own_scores(g):
        s = tile_scores(j_first + g, slice(g * tk, tq))
        own = [jnp.where(diagonal, s_h[:, :tk], NEG_INF) for s_h in s]
        if g == n_own - 1:
            return own
        return [jnp.concatenate([own_h, s_h[:, tk:]], axis=1) for own_h, s_h in zip(own, s)]

    put_scores(buffers[0], own_scores(n_own - 1), slice((n_own - 1) * tk, tq))
    for m in range(1, n_own):
        g = n_own - 1 - m
        put_scores(buffers[m % 2], own_scores(g), slice(g * tk, tq))
        tile_update(j_first + g + 1, buffers[(m - 1) % 2], slice((g + 1) * tk, tq))
    cur, nxt = buffers[(n_own - 1) % 2], buffers[n_own % 2]

    def previous_tile(j):
        return jnp.where(j == 0, j_first, j - 1)

    def pipelined_step(j, src, dst):
        put_scores(dst, tile_scores(j))
        tile_update(previous_tile(j), src)

    def kv_pair_step(u, carry):
        pipelined_step(2 * u, cur, nxt)
        pipelined_step(2 * u + 1, nxt, cur)
        return carry
    lax.fori_loop(0, j_first // 2, kv_pair_step, 0)

    @pl.when(j_first % 2 == 1)
    def _():
        pipelined_step(j_first - 1, cur, nxt)
        tile_update(j_first - 1, nxt)

    @pl.when(j_first % 2 == 0)
    def _():
        tile_update(previous_tile(j_first), cur)

    out_t = jnp.concatenate([acc_refs[h][0:HEAD_DIM, :] / acc_refs[h][HEAD_DIM:HEAD_DIM + 1, :]
                             for h in (0, 1)], axis=0)
    o_ref[...] = out_t.T.astype(o_ref.dtype)


def _moba_call(pk, qvt, kaug, batch, seq_len):
    n = pk.shape[0]
    blk = MOBA_BLOCK
    tq = MOBA_Q_TILE
    nb = seq_len // blk
    nq = seq_len // tq
    assert nb + MOBA_ALIBI_PARTS <= LANES and seq_len % MOBA_KV_TILE == 0
    assert tq % MOBA_KV_TILE == 0 and seq_len % tq == 0
    lane_groups = MOBA_WIDTH // LANES
    return pl.pallas_call(
        functools.partial(_moba_kernel, n_blocks=nb),
        grid=(batch, N_PAIRS, nq),
        in_specs=[
            pl.BlockSpec((PAIR, tq), lambda b, p, i: (p, b * nq + i)),
            pl.BlockSpec((seq_len, PAIR), lambda b, p, i: (b, p)),
            pl.BlockSpec((PAIR, seq_len), lambda b, p, i: (lane_groups + p, b)),
            pl.BlockSpec((2, seq_len, LANES), lambda b, p, i: (p, 0, 0)),
        ],
        out_specs=pl.BlockSpec((tq, PAIR), lambda b, p, i: (b * nq + i, p)),
        out_shape=jax.ShapeDtypeStruct((n, MOBA_WIDTH), BF16),
        scratch_shapes=[
            pltpu.VMEM((-(-nb // 8) * 8, PAIR), F32),
            pltpu.VMEM((seq_len // MOBA_KV_TILE, 2, MOBA_V_ROWS, MOBA_KV_TILE), BF16),
            pltpu.VMEM((1, tq), F32), pltpu.VMEM((1, tq), F32),
            pltpu.VMEM((MOBA_V_ROWS, tq), F32), pltpu.VMEM((MOBA_V_ROWS, tq), F32),
            pltpu.VMEM((2, MOBA_KV_TILE, tq), F32), pltpu.VMEM((2, MOBA_KV_TILE, tq), F32),
            pltpu.VMEM((2, 1, tq), F32), pltpu.VMEM((2, 1, tq), F32),
        ],
        compiler_params=pltpu.CompilerParams(
            dimension_semantics=("arbitrary", "arbitrary", "arbitrary"),
            vmem_limit_bytes=VMEM_LIMIT),
        name="moba_attention",
    )(qvt, pk, qvt, kaug)


def _moba_key_aug(seq_len):
    nb = seq_len // MOBA_BLOCK
    heads = MOBA_WIDTH // HEAD_DIM
    pos = np.arange(seq_len, dtype=np.int32)
    slopes = (2.0 ** (-8.0 * (np.arange(heads, dtype=np.float32) + 1.0) / heads)).astype(np.float32)
    aug = np.zeros((heads, seq_len, LANES), np.float32)
    aug[:, pos, pos // MOBA_BLOCK] = 1.0
    rem = (np.float32(LOG2_E) * slopes)[:, None] * pos.astype(np.float32)[None, :]
    for part in range(MOBA_ALIBI_PARTS):
        piece = (rem.view(np.uint32) & np.uint32(0xFFFF0000)).view(np.float32)
        aug[:, :, nb + part] = piece
        rem = rem - piece
    return jnp.asarray(aug.astype(BF16))


def _layer_norm(z, g, b):
    mu = jnp.mean(z, axis=-1, keepdims=True)
    zc = z - mu
    var = jnp.mean(zc * zc, axis=-1, keepdims=True)
    return zc * lax.rsqrt(var + LN_EPS) * g + b


def _outproj_kernel(ya_ref, yb_ref, x_ref, wa_ref, wb_ref, g_ref, b_ref, wr_ref, br_ref,
                    h_ref, lg_ref):
    mix = _dot(ya_ref[...], wa_ref[...]) + _dot(yb_ref[...], wb_ref[...])
    h = _layer_norm(DEEPNORM_ALPHA * x_ref[...] + mix, g_ref[...], b_ref[...])
    h_ref[...] = h
    h_hi, h_lo = _split_bf16(h, 2)
    w_hi, w_lo = _split_bf16(wr_ref[...], 2)
    hh_hl = _dot(h_hi, jnp.concatenate([w_hi, w_lo], axis=1))
    lg_ref[...] = hh_hl[:, :ROUTER_PAD] + (hh_hl[:, ROUTER_PAD:] + _dot(h_lo, w_hi)) + br_ref[...]


def _outproj_call(y_a, y_b, x2, wa, wb, ln_g, ln_b, w_router, b_router):
    n = x2.shape[0]
    tm = OUTPROJ_TM
    row = lambda i: (i, 0)
    const = lambda i: (0, 0)
    return pl.pallas_call(
        _outproj_kernel,
        grid=(n // tm,),
        in_specs=[
            pl.BlockSpec((tm, RWKV_WIDTH), row),
            pl.BlockSpec((tm, MOBA_WIDTH), row),
            pl.BlockSpec((tm, D_MODEL), row),
            pl.BlockSpec((RWKV_WIDTH, D_MODEL), const),
            pl.BlockSpec((MOBA_WIDTH, D_MODEL), const),
            pl.BlockSpec((1, D_MODEL), const),
            pl.BlockSpec((1, D_MODEL), const),
            pl.BlockSpec((D_MODEL, ROUTER_PAD), const),
            pl.BlockSpec((1, ROUTER_PAD), const),
        ],
        out_specs=[
            pl.BlockSpec((tm, D_MODEL), row),
            pl.BlockSpec((tm, ROUTER_PAD), row),
        ],
        out_shape=[
            jax.ShapeDtypeStruct((n, D_MODEL), F32),
            jax.ShapeDtypeStruct((n, ROUTER_PAD), F32),
        ],
        compiler_params=pltpu.CompilerParams(
            dimension_semantics=("arbitrary",), vmem_limit_bytes=VMEM_LIMIT),
        name="outproj_ln_router",
    )(y_a, y_b, x2, wa, wb, ln_g, ln_b, w_router, b_router)


def _route(logits):
    lane = lax.broadcasted_iota(jnp.int32, logits.shape, 1)
    is_group = (lane >= GROUP_LANE0) & (lane < GROUP_LANE0 + N_GROUPS)
    gl = jnp.where(is_group, logits, F32_LOWEST)
    g_max = jnp.max(gl, axis=-1, keepdims=True)
    g_first = jnp.min(jnp.where(gl == g_max, lane, LANES), axis=-1, keepdims=True)
    g_exp = jnp.where(is_group, jnp.exp(gl - g_max), 0.0)
    p_g = 1.0 / jnp.sum(g_exp, axis=-1, keepdims=True)
    g_idx = g_first - GROUP_LANE0
    in_group = (lane >= g_idx * EXPERTS_PER_GROUP) & (lane < (g_idx + 1) * EXPERTS_PER_GROUP)
    el = jnp.where(in_group, logits, F32_LOWEST)
    e_max = jnp.max(el, axis=-1, keepdims=True)
    e_exp = jnp.where(in_group, jnp.exp(el - e_max), 0.0)
    e_prob = e_exp / jnp.sum(e_exp, axis=-1, keepdims=True)
    cand = jnp.where(in_group, e_prob, -1.0)
    v1 = jnp.max(cand, axis=-1, keepdims=True)
    i1 = jnp.min(jnp.where(cand == v1, lane, LANES), axis=-1, keepdims=True)
    pick1 = lane == i1
    cand2 = jnp.where(pick1, -1.0, cand)
    v2 = jnp.max(cand2, axis=-1, keepdims=True)
    i2 = jnp.min(jnp.where(cand2 == v2, lane, LANES), axis=-1, keepdims=True)
    pick2 = lane == i2
    denom = v1 + v2
    gates = jnp.where(pick1, v1 / denom * p_g, jnp.where(pick2, v2 / denom * p_g, 0.0))
    return gates, g_idx


def _route_sort_kernel(lg_ref, gates_ref, pos_ref, flags_ref, earlier_ref):
    tm = lg_ref.shape[0]

    @pl.when(pl.program_id(0) == 0)
    def _():
        row = lax.broadcasted_iota(jnp.int32, (tm, tm), 0)
        col = lax.broadcasted_iota(jnp.int32, (tm, tm), 1)
        earlier_ref[...] = jnp.where(row > col, 1.0, 0.0).astype(BF16)

    gates, g_idx = _route(lg_ref[...])
    lane = lax.broadcasted_iota(jnp.int32, (tm, LANES), 1)
    in_own = lane == g_idx
    onehot = jnp.where(in_own, 1.0, 0.0)
    rank = _dot(earlier_ref[...], onehot.astype(BF16))
    count = jnp.sum(onehot, axis=0, keepdims=True)
    r128 = lax.broadcasted_iota(jnp.int32, (LANES, LANES), 0)
    c128 = lax.broadcasted_iota(jnp.int32, (LANES, LANES), 1)
    below = jnp.where(r128 < c128, 1.0, 0.0)
    start = _dot(jnp.broadcast_to(count, (8, LANES)), below, NN,
                 precision=lax.Precision.HIGHEST)[0:1]
    pos = jnp.sum(jnp.where(in_own, rank + start, 0.0), axis=-1, keepdims=True)
    gates_ref[...] = gates
    pos_ref[...] = jnp.broadcast_to(pos, (tm, LANES))
    start_c = jnp.broadcast_to(start, (LANES, LANES)).T
    count_c = jnp.broadcast_to(count, (LANES, LANES)).T
    sub_lo = c128 * MOE_SUB
    hit = (count_c > 0.0) & (start_c < (sub_lo + MOE_SUB).astype(F32)) & (start_c + count_c > sub_lo.astype(F32))
    start_i = start_c.astype(jnp.int32)
    count_i = count_c.astype(jnp.int32)
    win = jnp.minimum((start_i // MOE_WINDOW_ALIGN) * MOE_WINDOW_ALIGN, tm - MOE_WINDOW)
    fits = (count_i > 0) & (start_i + count_i <= win + MOE_WINDOW)
    meta = jnp.where(c128 == MOE_META_FITS, jnp.where(fits, 1, 0),
                     jnp.where(c128 == MOE_META_WINDOW, win, jnp.where(hit, 1, 0)))
    flags_ref[0] = meta[0:8].astype(jnp.int32)


def _route_sort_call(logits):
    n = logits.shape[0]
    tm = MOE_TM
    row = lambda t: (t, 0)
    return pl.pallas_call(
        _route_sort_kernel,
        grid=(n // tm,),
        in_specs=[pl.BlockSpec((tm, ROUTER_PAD), row)],
        out_specs=[
            pl.BlockSpec((tm, ROUTER_PAD), row),
            pl.BlockSpec((tm, LANES), row),
            pl.BlockSpec((1, 8, LANES), lambda t: (t, 0, 0)),
        ],
        out_shape=[
            jax.ShapeDtypeStruct((n, ROUTER_PAD), F32),
            jax.ShapeDtypeStruct((n, LANES), F32),
            jax.ShapeDtypeStruct((n // tm, 8, LANES), jnp.int32),
        ],
        scratch_shapes=[pltpu.VMEM((tm, tm), BF16)],
        compiler_params=pltpu.CompilerParams(
            dimension_semantics=("arbitrary",), vmem_limit_bytes=VMEM_LIMIT),
        name="route_sort",
    )(logits)


def _moe_kernel(flags_ref, h_ref, gates_ref, pos_ref, w1_ref, w3_ref, w2_ref, g_ref, b_ref,
                o_ref, xs_ref, gs_ref, acc_ref, pt_ref):
    tile = pl.program_id(0)
    step = pl.program_id(1)
    tm = h_ref.shape[0]
    sub = MOE_SUB
    n_sub = tm // sub
    eps = MOE_EXPERTS_PER_STEP

    @pl.when(step == 0)
    def _():
        pos_b = pos_ref[...]
        pos_row = pos_b.T[0:1, :]
        g_hi, g_lo = _split_bf16(gates_ref[...], 2)
        src = jnp.concatenate([h_ref[...].astype(BF16), g_hi, g_lo], axis=1)
        for c0 in range(0, tm, sub):
            slot = (lax.broadcasted_iota(jnp.int32, (sub, tm), 0) + c0).astype(F32)
            p_c = jnp.where(slot == pos_row, 1.0, 0.0).astype(BF16)
            moved = _dot(p_c, src)
            xs_ref[c0:c0 + sub, :] = moved[:, :D_MODEL].astype(BF16)
            gs_ref[c0:c0 + sub, :] = moved[:, D_MODEL:D_MODEL + LANES] + moved[:, D_MODEL + LANES:]
        for c0 in range(0, tm, LANES):
            slot = (lax.broadcasted_iota(jnp.int32, (tm, LANES), 1) + c0).astype(F32)
            pt_ref[:, c0:c0 + LANES] = jnp.where(pos_b == slot, 1.0, 0.0).astype(BF16)
        acc_ref[...] = jnp.zeros_like(acc_ref)

    group = step // (EXPERTS_PER_GROUP // eps)

    def visit(rows):
        x_r = xs_ref[rows, :]
        g_r = gs_ref[rows, :]
        lane = lax.broadcasted_iota(jnp.int32, g_r.shape, 1)
        acc = acc_ref[rows, :]
        for e in range(eps):
            gate_e = jnp.sum(jnp.where(lane == step * eps + e, g_r, 0.0), axis=-1, keepdims=True)
            a1 = _dot(x_r, w1_ref[e])
            a3 = _dot(x_r, w3_ref[e])
            hid = (a1 * _sigmoid(a1)) * a3 * gate_e
            acc = acc + _dot(hid.astype(BF16), w2_ref[e])
        acc_ref[rows, :] = acc

    base = (tile * N_GROUPS + group) * (n_sub + 2)
    fits = flags_ref[base + n_sub] != 0

    @pl.when(fits)
    def _():
        first = pl.multiple_of(flags_ref[base + n_sub + 1], MOE_WINDOW_ALIGN)
        visit(pl.ds(first, MOE_WINDOW))

    for r in range(n_sub):
        pl.when(jnp.logical_not(fits) & (flags_ref[base + r] != 0))(
            functools.partial(visit, slice(r * sub, (r + 1) * sub)))

    @pl.when(step == N_EXPERTS // eps - 1)
    def _():
        ffn = _dot(pt_ref[...], acc_ref[...].astype(BF16))
        o_ref[...] = _layer_norm(DEEPNORM_ALPHA * h_ref[...] + ffn, g_ref[...], b_ref[...])


def _moe_call(flags, h_f32, gates, pos, w1, w3, w2, ln_g, ln_b):
    n = h_f32.shape[0]
    tm = MOE_TM
    row = lambda t, s, f: (t, 0)
    const = lambda t, s, f: (0, 0)
    wmap = lambda t, s, f: (s, 0, 0)
    eps = MOE_EXPERTS_PER_STEP
    assert EXPERTS_PER_GROUP % eps == 0 and tm % MOE_SUB == 0
    grid_spec = pltpu.PrefetchScalarGridSpec(
        num_scalar_prefetch=1,
        grid=(n // tm, N_EXPERTS // eps),
        in_specs=[
            pl.BlockSpec((tm, D_MODEL), row),
            pl.BlockSpec((tm, ROUTER_PAD), row),
            pl.BlockSpec((tm, LANES), row),
            pl.BlockSpec((eps, D_MODEL, D_EXPERT), wmap),
            pl.BlockSpec((eps, D_MODEL, D_EXPERT), wmap),
            pl.BlockSpec((eps, D_EXPERT, D_MODEL), wmap),
            pl.BlockSpec((1, D_MODEL), const),
            pl.BlockSpec((1, D_MODEL), const),
        ],
        out_specs=pl.BlockSpec((tm, D_MODEL), row),
        scratch_shapes=[
            pltpu.VMEM((tm, D_MODEL), BF16),
            pltpu.VMEM((tm, ROUTER_PAD), F32),
            pltpu.VMEM((tm, D_MODEL), F32),
            pltpu.VMEM((tm, tm), BF16),
        ],
    )
    return pl.pallas_call(
        _moe_kernel,
        grid_spec=grid_spec,
        out_shape=jax.ShapeDtypeStruct((n, D_MODEL), F32),
        compiler_params=pltpu.CompilerParams(
            dimension_semantics=("arbitrary", "arbitrary"), vmem_limit_bytes=VMEM_LIMIT),
        name="hier_moe_ln",
    )(flags, h_f32, gates, pos, w1, w3, w2, ln_g, ln_b)


def _pad_cols(w, width):
    return jnp.pad(w, ((0, 0), (0, width - w.shape[1])))


def _pad_rows(w, height):
    return jnp.pad(w, ((0, height - w.shape[0]), (0, 0)))


def kernel(x, w_in, mu_shift, w0, w_lora_up, a0, a_lora_up, g_lora_up, k_k, k_a, r_k, gn_w, gn_b, w_out, ln1_g, ln1_b, w_group, b_group, w_expert, b_expert, w1_exp, w3_exp, w2_exp, ln2_g, ln2_b):
    batch, seq_len, d = x.shape
    assert d == D_MODEL
    n = batch * seq_len
    x2 = x.reshape(n, d)

    c_rkv = 3 * RWKV_WIDTH
    c_wd = c_rkv + DECAY_RANK
    c_ad = c_wd + AAA_RANK
    c_gd = c_ad + GATE_RANK
    w_cat = jnp.concatenate([
        w_in[:, :c_rkv], _pad_cols(w_in[:, c_rkv:c_gd], LORA_PAD), w_in[:, c_gd:],
    ], axis=1).astype(BF16)
    mu2 = mu_shift[None, :]
    mu_cat = jnp.concatenate([mu2[:, :c_rkv], _pad_cols(mu2[:, c_rkv:c_gd], LORA_PAD)], axis=1)
    p_rkv, p_lora, p_k, p_qvt = _inproj_call(x2, w_cat, mu_cat, seq_len)

    place = lambda w, first: jnp.pad(w, ((first, LORA_PAD - first - w.shape[0]), (0, 0)))
    lora_up = (place(w_lora_up, 0), place(a_lora_up, DECAY_RANK), place(g_lora_up, DECAY_RANK + AAA_RANK))

    vecs = jnp.stack([w0, a0, k_k, k_a, r_k.reshape(-1), gn_w, gn_b, jnp.zeros_like(w0)], axis=0)
    head_id = jnp.arange(2 * PAIR) // HEAD_DIM
    bd = (head_id[:, None] == head_id[None, :]).astype(BF16)
    y_a = _rwkv_call(p_rkv, p_lora, vecs, *lora_up, bd, batch, seq_len)

    y_b = _moba_call(p_k, p_qvt, _moba_key_aug(seq_len), batch, seq_len)

    w_out_b = w_out.astype(BF16)
    w_router = _pad_cols(jnp.concatenate([w_expert, w_group], axis=1), ROUTER_PAD)
    b_router = _pad_cols(jnp.concatenate([b_expert, b_group])[None, :], ROUTER_PAD)
    h1, logits = _outproj_call(y_a, y_b, x2, w_out_b[:RWKV_WIDTH], w_out_b[RWKV_WIDTH:],
                               ln1_g[None, :], ln1_b[None, :], w_router, b_router)

    flat = lambda w: w.astype(BF16).reshape((N_EXPERTS,) + w.shape[2:])
    gates, pos, flags = _route_sort_call(logits)
    flags = jnp.concatenate([flags[:, :N_GROUPS, :MOE_TM // MOE_SUB],
                             flags[:, :N_GROUPS, MOE_META_FITS:MOE_META_WINDOW + 1]], axis=-1).reshape(-1)
    out = _moe_call(flags, h1, gates, pos, flat(w1_exp), flat(w3_exp), flat(w2_exp),
                    ln2_g[None, :], ln2_b[None, :])
    return out.reshape(batch, seq_len, d)
```

```python
import functools
import math

import jax
import jax.numpy as jnp
import numpy as np
from jax import lax
from jax.experimental import pallas as pl
from jax.experimental.pallas import tpu as pltpu

F32 = jnp.float32
BF16 = jnp.bfloat16

D_MODEL = 1024
HEAD_DIM = 64
RWKV_WIDTH = 512
MOBA_WIDTH = 512
DECAY_RANK = 32
AAA_RANK = 32
GATE_RANK = 96
GN_EPS = 64e-5
L2_EPS = 1e-12
MOBA_BLOCK = 256
MOBA_TOPK = 3
N_GROUPS = 4
EXPERTS_PER_GROUP = 8
N_EXPERTS = N_GROUPS * EXPERTS_PER_GROUP
D_EXPERT = 256
LN_EPS = 1e-5
DEEPNORM_ALPHA = float(2.0 ** 0.25)
NEG_INF = -1e30
F32_LOWEST = -3.0e38

LANES = 128
PAIR = 2 * HEAD_DIM
N_PAIRS = RWKV_WIDTH // PAIR
LORA_PAD = 2 * LANES
RWKV_COLS_PAD = 3 * RWKV_WIDTH + LORA_PAD
IN_COLS_PAD = RWKV_COLS_PAD + 3 * MOBA_WIDTH
VMEM_LIMIT = 56 * 1024 * 1024

INPROJ_TM = 512
INPROJ_TN = 256
RWKV_CHUNK = 64
RWKV_CHUNKS_PER_STEP = 4
RWKV_PASSES = 1
RWKV_STATE_PASSES = 1
OUTPROJ_TM = 512
MOE_TM = 1024
MOE_EXPERTS_PER_STEP = 8
MOE_SUB = 256
MOE_WINDOW = 304
MOE_WINDOW_ALIGN = 16
MOE_META_FITS = 16
MOE_META_WINDOW = 17
MOBA_KV_TILE = 512
MOBA_Q_TILE = 2048
MOBA_V_ROWS = HEAD_DIM + 16
MOBA_ALIBI_PARTS = 3
LOG2_E = 1.4426950408889634
ROUTER_PAD = LANES
GROUP_LANE0 = N_EXPERTS

NN = (((1,), (0,)), ((), ()))
NT = (((1,), (1,)), ((), ()))


def _dot(a, b, dims=NN, precision=None):
    return lax.dot_general(a, b, dims, precision=precision, preferred_element_type=F32)


def _split_bf16(x, parts):
    out = []
    rem = x
    for i in range(parts):
        p = rem.astype(BF16)
        out.append(p)
        if i + 1 < parts:
            rem = rem - p.astype(F32)
    return out


def _mm(a, b, dims=NN, passes=3):
    if passes == 1:
        return _dot(a.astype(BF16), b.astype(BF16), dims)
    if passes == 6:
        return _dot(a, b, dims, precision=lax.Precision.HIGHEST)
    a_hi, a_lo = _split_bf16(a, 2)
    b_hi, b_lo = _split_bf16(b, 2)
    return _dot(a_hi, b_hi, dims) + (_dot(a_hi, b_lo, dims) + _dot(a_lo, b_hi, dims))


def _mm_exact_lhs(a_bf16, b, dims=NN, parts=3):
    out = None
    for piece in reversed(_split_bf16(b, parts)):
        term = _dot(a_bf16, piece, dims)
        out = term if out is None else term + out
    return out


def _mm_exact_rhs(a, b_bf16, dims=NN):
    a1, a2, a3 = _split_bf16(a, 3)
    return _dot(a1, b_bf16, dims) + (_dot(a2, b_bf16, dims) + _dot(a3, b_bf16, dims))


def _inproj_kernel(x_ref, w_ref, mu_ref, prkv_ref, plora_ref, pk_ref, qvt_ref, carry_ref, *, tiles_per_seq):
    tm = x_ref.shape[0]
    xb = x_ref[...].astype(BF16)
    seq_start = (pl.program_id(0) % tiles_per_seq) == 0
    row0 = lax.broadcasted_iota(jnp.int32, (tm, INPROJ_TN), 0) == 0
    n_shift_tiles = RWKV_COLS_PAD // INPROJ_TN
    for j in range(n_shift_tiles):
        c0 = j * INPROJ_TN
        acc = _dot(xb, w_ref[:, c0:c0 + INPROJ_TN])
        prev_last = jnp.where(seq_start, 0.0, carry_ref[0:1, c0:c0 + INPROJ_TN])
        shifted = jnp.where(row0, prev_last, pltpu.roll(acc, 1, 0))
        carry_ref[0:1, c0:c0 + INPROJ_TN] = acc[tm - 1:tm, :]
        out = acc + (shifted - acc) * mu_ref[:, c0:c0 + INPROJ_TN]
        if c0 < 3 * RWKV_WIDTH:
            prkv_ref[:, c0:c0 + INPROJ_TN] = out
        else:
            plora_ref[:, c0 - 3 * RWKV_WIDTH:c0 - 3 * RWKV_WIDTH + INPROJ_TN] = out
    tiles_per_part = MOBA_WIDTH // INPROJ_TN
    for j in range(3 * tiles_per_part):
        c0 = j * INPROJ_TN
        acc = _dot(xb, w_ref[:, RWKV_COLS_PAD + c0:RWKV_COLS_PAD + c0 + INPROJ_TN])
        part, r0 = divmod(c0, MOBA_WIDTH)
        if part == 1:
            pk_ref[:, r0:r0 + INPROJ_TN] = acc.astype(BF16)
        else:
            r0 += (part // 2) * MOBA_WIDTH
            qvt_ref[r0:r0 + INPROJ_TN, :] = acc.T.astype(BF16)


def _inproj_call(x2, w_cat, mu_cat, seq_len):
    n = x2.shape[0]
    tm = INPROJ_TM
    assert seq_len % tm == 0 and (3 * RWKV_WIDTH) % INPROJ_TN == 0
    return pl.pallas_call(
        functools.partial(_inproj_kernel, tiles_per_seq=seq_len // tm),
        grid=(n // tm,),
        in_specs=[
            pl.BlockSpec((tm, D_MODEL), lambda i: (i, 0)),
            pl.BlockSpec((D_MODEL, IN_COLS_PAD), lambda i: (0, 0)),
            pl.BlockSpec((1, RWKV_COLS_PAD), lambda i: (0, 0)),
        ],
        out_specs=[
            pl.BlockSpec((tm, 3 * RWKV_WIDTH), lambda i: (i, 0)),
            pl.BlockSpec((tm, LORA_PAD), lambda i: (i, 0)),
            pl.BlockSpec((tm, MOBA_WIDTH), lambda i: (i, 0)),
            pl.BlockSpec((2 * MOBA_WIDTH, tm), lambda i: (0, i)),
        ],
        out_shape=[
            jax.ShapeDtypeStruct((n, 3 * RWKV_WIDTH), F32),
            jax.ShapeDtypeStruct((n, LORA_PAD), F32),
            jax.ShapeDtypeStruct((n, MOBA_WIDTH), BF16),
            jax.ShapeDtypeStruct((2 * MOBA_WIDTH, n), BF16),
        ],
        scratch_shapes=[pltpu.VMEM((8, RWKV_COLS_PAD), F32)],
        compiler_params=pltpu.CompilerParams(
            dimension_semantics=("arbitrary",), vmem_limit_bytes=VMEM_LIMIT),
        name="inproj_shift",
    )(x2, w_cat, mu_cat)


def _softplus(z):
    return jnp.maximum(z, 0.0) + jnp.log(1.0 + jnp.exp(-jnp.abs(z)))


def _sigmoid(z):
    return 1.0 / (1.0 + jnp.exp(-z))


def _rwkv_chunks(rt, kt, at, bt, v, d_incl, s_prev, passes, state_passes):
    c = RWKV_CHUNK
    n_chunks = rt.shape[0] // c
    n_pairs = len(s_prev)
    row = lax.broadcasted_iota(jnp.int32, (c, PAIR), 0)
    col = lax.broadcasted_iota(jnp.int32, (c, PAIR), 1) % HEAD_DIM
    strict = row > col
    incl = row >= col
    eye_c = (row == col).astype(F32)
    lane = lax.broadcasted_iota(jnp.int32, (1, PAIR), 1)
    head0 = lane < HEAD_DIM
    head1 = jnp.logical_not(head0)
    prow = lax.broadcasted_iota(jnp.int32, (PAIR, PAIR), 0)
    pcol = lax.broadcasted_iota(jnp.int32, (PAIR, PAIR), 1)
    same_head = (prow < HEAD_DIM) == (pcol < HEAD_DIM)
    eye_p = (prow == pcol).astype(F32)
    rows = [slice(ci * c, (ci + 1) * c) for ci in range(n_chunks)]
    sl = [slice(p * PAIR, (p + 1) * PAIR) for p in range(n_pairs)]
    pairs = [(ci, p) for ci in range(n_chunks) for p in range(n_pairs)]
    cut = lambda t, ci, p: t[rows[ci], sl[p]]

    rhs_dtype = BF16 if passes == 1 else F32

    def by_head(m, dtype=rhs_dtype):
        m = m.astype(dtype)
        zero = jnp.zeros_like(m)
        return jnp.concatenate([jnp.where(head0, m, zero), jnp.where(head1, m, zero)], axis=0)

    def by_head2(m, n):
        return jnp.concatenate([by_head(m), by_head(n)], axis=1)

    at_p = {k_: cut(at, *k_) for k_ in pairs}
    rt_p = {k_: cut(rt, *k_) for k_ in pairs}
    bt_p = {k_: cut(bt, *k_) for k_ in pairs}
    kt_p = {k_: cut(kt, *k_) for k_ in pairs}
    v_p = {k_: cut(v, *k_) for k_ in pairs}

    z = {k_: _mm(jnp.concatenate([at_p[k_], rt_p[k_]], axis=0),
                 jnp.concatenate([by_head(bt_p[k_], F32).T, by_head(kt_p[k_], F32).T], axis=1), NN, passes)
         for k_ in pairs}
    l_ab = {k_: jnp.where(strict, z[k_][:c, :PAIR], 0.0) for k_ in pairs}
    l_ak = {k_: jnp.where(strict, z[k_][:c, PAIR:], 0.0) for k_ in pairs}
    m_rb = {k_: jnp.where(incl, z[k_][c:, :PAIR], 0.0) for k_ in pairs}
    m_rk = {k_: jnp.where(incl, z[k_][c:, PAIR:], 0.0) for k_ in pairs}
    pw = {k_: _mm(l_ab[k_], by_head(l_ab[k_]), NN, passes) for k_ in pairs}
    t_inv = {k_: eye_c + l_ab[k_] for k_ in pairs}
    for _ in range(int(math.log2(c)) - 1):
        tp = {k_: _mm(jnp.concatenate([t_inv[k_], pw[k_]], axis=0), by_head(pw[k_]), NN, passes)
              for k_ in pairs}
        t_inv = {k_: t_inv[k_] + tp[k_][:c] for k_ in pairs}
        pw = {k_: tp[k_][c:] for k_ in pairs}
    lm = {k_: _mm(jnp.concatenate([l_ak[k_], m_rk[k_]], axis=0), by_head(v_p[k_]), NN, passes)
          for k_ in pairs}
    lv = {k_: lm[k_][:c] for k_ in pairs}
    mv = {k_: lm[k_][c:] for k_ in pairs}
    wu = {k_: _mm(t_inv[k_], by_head2(at_p[k_], lv[k_]), NN, passes) for k_ in pairs}
    qy = {k_: _mm(m_rb[k_], by_head2(wu[k_][:, :PAIR], wu[k_][:, PAIR:]), NN, passes) for k_ in pairs}

    qeff, y1, phi, psi = {}, {}, {}, {}
    for ci, p in pairs:
        k_ = (ci, p)
        w, u0 = wu[k_][:, :PAIR], wu[k_][:, PAIR:]
        qeff[k_] = rt_p[k_] + qy[k_][:, :PAIR]
        y1[k_] = qy[k_][:, PAIR:] + mv[k_]
        d_p = d_incl[(ci + 1) * c - 1:(ci + 1) * c, sl[p]]
        phi[k_] = jnp.where(same_head, (eye_p + _mm(w.T, bt_p[k_], NN, passes)) * d_p, 0.0)
        uv_t = jnp.concatenate([u0, v_p[k_]], axis=0).T
        bk = jnp.concatenate([bt_p[k_], kt_p[k_]], axis=0)
        psi[k_] = jnp.where(same_head, _mm(uv_t, bk, NN, passes) * d_p, 0.0)

    state = list(s_prev)
    ys = [[None] * n_pairs for _ in range(n_chunks)]
    for ci in range(n_chunks):
        for p in range(n_pairs):
            ys[ci][p] = _mm(qeff[ci, p], state[p].T, NN, state_passes) + y1[ci, p]
            state[p] = _mm(state[p], phi[ci, p], NN, state_passes) + psi[ci, p]
    y = jnp.concatenate([jnp.concatenate(ys[ci], axis=1) for ci in range(n_chunks)], axis=0)
    return y, state


def _rwkv_kernel(prkv_ref, plora_ref, vec_ref, wl_ref, al_ref, gl_ref, bd_ref, y_ref, s_ref):
    rows = prkv_ref.shape[0]
    c = RWKV_CHUNK
    width = RWKV_WIDTH

    @pl.when(pl.program_id(1) == 0)
    def _():
        s_ref[...] = jnp.zeros_like(s_ref)

    r = prkv_ref[:, 0:width]
    k_raw = prkv_ref[:, width:2 * width]
    v = prkv_ref[:, 2 * width:3 * width]
    p_wd = p_ad = p_gd = plora_ref[...]
    w0 = vec_ref[0:1, :]
    a0 = vec_ref[1:2, :]
    k_k = vec_ref[2:3, :]
    k_a = vec_ref[3:4, :]
    r_k = vec_ref[4:5, :]
    gn_w = vec_ref[5:6, :]
    gn_b = vec_ref[6:7, :]
    bd = bd_ref[...]

    def seg_sum(z):
        halves = []
        for c0 in range(0, width, bd.shape[0]):
            halves.append(_dot(z[:, c0:c0 + bd.shape[0]].astype(BF16), bd))
        return jnp.concatenate(halves, axis=1)

    w_log = -_softplus(-(w0 + _mm(jnp.tanh(p_wd), wl_ref[...], NN, 3))) - 0.5
    log_w = -jnp.exp(w_log)
    a = _sigmoid(a0 + _mm(p_ad, al_ref[...], NN, RWKV_PASSES))
    g = _mm(_sigmoid(p_gd), gl_ref[...], NN, RWKV_PASSES)
    kk = k_raw * k_k
    kk = kk * lax.rsqrt(jnp.maximum(seg_sum(kk * kk), L2_EPS * L2_EPS))
    k = k_raw * (1.0 + (a - 1.0) * k_a)

    row = lax.broadcasted_iota(jnp.int32, (rows, rows), 0)
    col = lax.broadcasted_iota(jnp.int32, (rows, rows), 1)
    tri = ((row >= col) & (row // c == col // c)).astype(BF16)
    cum = _mm_exact_lhs(tri, log_w, parts=2)
    d_incl = jnp.exp(cum)
    d_inv = jnp.exp(-cum)
    d_excl = jnp.exp(cum - log_w)
    rt = r * d_incl
    kt = k * d_inv
    at = -kk * d_excl
    bt = kk * a * d_inv

    y, s_next = _rwkv_chunks(rt, kt, at, bt, v, d_incl, [s_ref[p] for p in range(N_PAIRS)],
                             RWKV_PASSES, RWKV_STATE_PASSES)
    for p in range(N_PAIRS):
        s_ref[p] = s_next[p]

    inv_n = 1.0 / HEAD_DIM
    mu = seg_sum(y) * inv_n
    yc = y - mu
    var = seg_sum(yc * yc) * inv_n
    yn = yc * lax.rsqrt(var + GN_EPS) * gn_w + gn_b
    bonus = seg_sum(r * k * r_k) * v
    y_ref[...] = ((yn + bonus) * g).astype(y_ref.dtype)


def _rwkv_call(p_rkv, p_lora, vecs, wl, al, gl, bd, batch, seq_len):
    n = p_rkv.shape[0]
    rows = RWKV_CHUNK * RWKV_CHUNKS_PER_STEP
    assert seq_len % rows == 0
    steps = seq_len // rows
    row_map = lambda b, i: (b * steps + i, 0)
    const = lambda b, i: (0, 0)
    return pl.pallas_call(
        _rwkv_kernel,
        grid=(batch, steps),
        in_specs=[
            pl.BlockSpec((rows, 3 * RWKV_WIDTH), row_map),
            pl.BlockSpec((rows, LORA_PAD), row_map),
            pl.BlockSpec((8, RWKV_WIDTH), const),
            pl.BlockSpec((LORA_PAD, RWKV_WIDTH), const),
            pl.BlockSpec((LORA_PAD, RWKV_WIDTH), const),
            pl.BlockSpec((LORA_PAD, RWKV_WIDTH), const),
            pl.BlockSpec((2 * PAIR, 2 * PAIR), const),
        ],
        out_specs=pl.BlockSpec((rows, RWKV_WIDTH), row_map),
        out_shape=jax.ShapeDtypeStruct((n, RWKV_WIDTH), BF16),
        scratch_shapes=[pltpu.VMEM((N_PAIRS, PAIR, PAIR), F32)],
        compiler_params=pltpu.CompilerParams(
            dimension_semantics=("arbitrary", "arbitrary"), vmem_limit_bytes=VMEM_LIMIT),
        name="rwkv7_chunked",
    )(p_rkv, p_lora, vecs, wl, al, gl, bd)


def _moba_kernel(qt_ref, k_ref, vt_in_ref, kaug_ref, o_ref, kmean_ref, vt_ref,
                 m0_ref, m1_ref, acc0_ref, acc1_ref, s_even_ref, s_odd_ref,
                 smax_even_ref, smax_odd_ref, *, n_blocks):
    blk = MOBA_BLOCK
    tk = MOBA_KV_TILE
    tq = qt_ref.shape[1]
    i = pl.program_id(2)
    nb_pad = kmean_ref.shape[0]
    m_refs, acc_refs = (m0_ref, m1_ref), (acc0_ref, acc1_ref)

    @pl.when(i == 0)
    def _():
        kmean_ref[...] = jnp.zeros_like(kmean_ref)

        def mean_body(n, carry):
            off = pl.multiple_of(n * blk, blk)
            kb = k_ref[pl.ds(off, blk), :].astype(F32)
            kmean_ref[pl.ds(n, 1), :] = jnp.sum(kb, axis=0, keepdims=True) * (1.0 / blk)
            return carry
        lax.fori_loop(0, n_blocks, mean_body, 0)

        ones = jnp.ones((MOBA_V_ROWS - HEAD_DIM, tk), BF16)

        for j in range(vt_ref.shape[0]):
            for h in (0, 1):
                vt_ref[j, h] = jnp.concatenate(
                    [vt_in_ref[h * HEAD_DIM:(h + 1) * HEAD_DIM, j * tk:(j + 1) * tk], ones], axis=0)

    q_t = qt_ref[...].astype(F32)
    chan = lax.broadcasted_iota(jnp.int32, (PAIR, tq), 0)
    blk_row = lax.broadcasted_iota(jnp.int32, (nb_pad, tq), 0)
    own_blk = (i * tq + lax.broadcasted_iota(jnp.int32, (nb_pad, tq), 1)) // blk
    past = blk_row < own_blk
    aug_row = lax.broadcasted_iota(jnp.int32, (LANES, tq), 0)
    ones_rows = (aug_row >= n_blocks) & (aug_row < n_blocks + MOBA_ALIBI_PARTS)
    kmean = kmean_ref[...]

    qa_t = []
    for h in (0, 1):
        qh_t = jnp.where((chan < HEAD_DIM) == (h == 0), q_t, 0.0)
        gate = _mm_exact_rhs(kmean, qh_t.astype(BF16))
        gate = jnp.where(past, gate, F32_LOWEST)
        sel = jnp.zeros(gate.shape, jnp.bool_)
        for _ in range(MOBA_TOPK):
            mx = jnp.max(gate, axis=0, keepdims=True)
            first = jnp.min(jnp.where(gate == mx, blk_row, nb_pad), axis=0, keepdims=True)
            pick = (blk_row == first) & (mx > F32_LOWEST)
            sel = sel | pick
            gate = jnp.where(pick, F32_LOWEST, gate)
        sel_bias = jnp.where(past & jnp.logical_not(sel), NEG_INF, 0.0)
        aug_t = jnp.concatenate([sel_bias, jnp.zeros((LANES - nb_pad, tq), F32)], axis=0)
        aug_t = jnp.where(ones_rows, 1.0, aug_t)
        qa_t.append(jnp.concatenate([qh_t * (LOG2_E / math.sqrt(HEAD_DIM)), aug_t], axis=0).astype(BF16))

    def tile_scores(j, lanes=slice(None)):
        off = pl.multiple_of(j * tk, tk)
        k_t = k_ref[pl.ds(off, tk), :]
        return [_dot(jnp.concatenate([k_t, kaug_ref[h, pl.ds(off, tk), :]], axis=1), qa_t[h][:, lanes])
                for h in (0, 1)]

    def put_scores(buf, s, lanes=slice(None)):
        for h in (0, 1):
            buf[0][h, :, lanes] = s[h]
            buf[1][h, :, lanes] = jnp.max(s[h], axis=0, keepdims=True)

    def tile_update(j, buf, lanes=slice(None)):
        s_buf, smax_buf = buf
        for h in (0, 1):
            m_old = m_refs[h][:, lanes]
            m_new = jnp.maximum(m_old, smax_buf[h, :, lanes])
            p = jnp.exp2(s_buf[h, :, lanes] - m_new).astype(BF16)
            pv = _dot(vt_ref[j, h], p)
            acc_refs[h][:, lanes] = jnp.exp2(m_old - m_new) * acc_refs[h][:, lanes] + pv
            m_refs[h][:, lanes] = m_new

    for h in (0, 1):
        m_refs[h][...] = jnp.full(m_refs[h].shape, F32_LOWEST, F32)
        acc_refs[h][...] = jnp.zeros(acc_refs[h].shape, F32)

    buffers = ((s_even_ref, smax_even_ref), (s_odd_ref, smax_odd_ref))
    n_own = tq // tk
    j_first = i * n_own
    diagonal = (lax.broadcasted_iota(jnp.int32, (tk, tk), 0) <= lax.broadcasted_iota(jnp.int32, (tk, tk), 1))

    def own_scores(g):
        s = tile_scores(j_first + g, slice(g * tk, tq))
        own = [jnp.where(diagonal, s_h[:, :tk], NEG_INF) for s_h in s]
        if g == n_own - 1:
            return own
        return [jnp.concatenate([own_h, s_h[:, tk:]], axis=1) for own_h, s_h in zip(own, s)]

    put_scores(buffers[0], own_scores(n_own - 1), slice((n_own - 1) * tk, tq))
    for m in range(1, n_own):
        g = n_own - 1 - m
        put_scores(buffers[m % 2], own_scores(g), slice(g * tk, tq))
        tile_update(j_first + g + 1, buffers[(m - 1) % 2], slice((g + 1) * tk, tq))
    cur, nxt = buffers[(n_own - 1) % 2], buffers[n_own % 2]

    def previous_tile(j):
        return jnp.where(j == 0, j_first, j - 1)

    def pipelined_step(j, src, dst):
        put_scores(dst, tile_scores(j))
        tile_update(previous_tile(j), src)

    def kv_pair_step(u, carry):
        pipelined_step(2 * u, cur, nxt)
        pipelined_step(2 * u + 1, nxt, cur)
        return carry
    lax.fori_loop(0, j_first // 2, kv_pair_step, 0)

    @pl.when(j_first % 2 == 1)
    def _():
        pipelined_step(j_first - 1, cur, nxt)
        tile_update(j_first - 1, nxt)

    @pl.when(j_first % 2 == 0)
    def _():
        tile_update(previous_tile(j_first), cur)

    out_t = jnp.concatenate([acc_refs[h][0:HEAD_DIM, :] / acc_refs[h][HEAD_DIM:HEAD_DIM + 1, :]
                             for h in (0, 1)], axis=0)
    o_ref[...] = out_t.T.astype(o_ref.dtype)


def _moba_call(pk, qvt, kaug, batch, seq_len):
    n = pk.shape[0]
    blk = MOBA_BLOCK
    tq = MOBA_Q_TILE
    nb = seq_len // blk
    nq = seq_len // tq
    assert nb + MOBA_ALIBI_PARTS <= LANES and seq_len % MOBA_KV_TILE == 0
    assert tq % MOBA_KV_TILE == 0 and seq_len % tq == 0
    lane_groups = MOBA_WIDTH // LANES
    return pl.pallas_call(
        functools.partial(_moba_kernel, n_blocks=nb),
        grid=(batch, N_PAIRS, nq),
        in_specs=[
            pl.BlockSpec((PAIR, tq), lambda b, p, i: (p, b * nq + i)),
            pl.BlockSpec((seq_len, PAIR), lambda b, p, i: (b, p)),
            pl.BlockSpec((PAIR, seq_len), lambda b, p, i: (lane_groups + p, b)),
            pl.BlockSpec((2, seq_len, LANES), lambda b, p, i: (p, 0, 0)),
        ],
        out_specs=pl.BlockSpec((tq, PAIR), lambda b, p, i: (b * nq + i, p)),
        out_shape=jax.ShapeDtypeStruct((n, MOBA_WIDTH), BF16),
        scratch_shapes=[
            pltpu.VMEM((-(-nb // 8) * 8, PAIR), F32),
            pltpu.VMEM((seq_len // MOBA_KV_TILE, 2, MOBA_V_ROWS, MOBA_KV_TILE), BF16),
            pltpu.VMEM((1, tq), F32), pltpu.VMEM((1, tq), F32),
            pltpu.VMEM((MOBA_V_ROWS, tq), F32), pltpu.VMEM((MOBA_V_ROWS, tq), F32),
            pltpu.VMEM((2, MOBA_KV_TILE, tq), F32), pltpu.VMEM((2, MOBA_KV_TILE, tq), F32),
            pltpu.VMEM((2, 1, tq), F32), pltpu.VMEM((2, 1, tq), F32),
        ],
        compiler_params=pltpu.CompilerParams(
            dimension_semantics=("arbitrary", "arbitrary", "arbitrary"),
            vmem_limit_bytes=VMEM_LIMIT),
        name="moba_attention",
    )(qvt, pk, qvt, kaug)


def _moba_key_aug(seq_len):
    nb = seq_len // MOBA_BLOCK
    heads = MOBA_WIDTH // HEAD_DIM
    pos = np.arange(seq_len, dtype=np.int32)
    slopes = (2.0 ** (-8.0 * (np.arange(heads, dtype=np.float32) + 1.0) / heads)).astype(np.float32)
    aug = np.zeros((heads, seq_len, LANES), np.float32)
    aug[:, pos, pos // MOBA_BLOCK] = 1.0
    rem = (np.float32(LOG2_E) * slopes)[:, None] * pos.astype(np.float32)[None, :]
    for part in range(MOBA_ALIBI_PARTS):
        piece = (rem.view(np.uint32) & np.uint32(0xFFFF0000)).view(np.float32)
        aug[:, :, nb + part] = piece
        rem = rem - piece
    return jnp.asarray(aug.astype(BF16))


def _layer_norm(z, g, b):
    mu = jnp.mean(z, axis=-1, keepdims=True)
    zc = z - mu
    var = jnp.mean(zc * zc, axis=-1, keepdims=True)
    return zc * lax.rsqrt(var + LN_EPS) * g + b


def _outproj_kernel(ya_ref, yb_ref, x_ref, wa_ref, wb_ref, g_ref, b_ref, wr_ref, br_ref,
                    h_ref, lg_ref):
    mix = _dot(ya_ref[...], wa_ref[...]) + _dot(yb_ref[...], wb_ref[...])
    h = _layer_norm(DEEPNORM_ALPHA * x_ref[...] + mix, g_ref[...], b_ref[...])
    h_ref[...] = h
    h_hi, h_lo = _split_bf16(h, 2)
    w_hi, w_lo = _split_bf16(wr_ref[...], 2)
    hh_hl = _dot(h_hi, jnp.concatenate([w_hi, w_lo], axis=1))
    lg_ref[...] = hh_hl[:, :ROUTER_PAD] + (hh_hl[:, ROUTER_PAD:] + _dot(h_lo, w_hi)) + br_ref[...]


def _outproj_call(y_a, y_b, x2, wa, wb, ln_g, ln_b, w_router, b_router):
    n = x2.shape[0]
    tm = OUTPROJ_TM
    row = lambda i: (i, 0)
    const = lambda i: (0, 0)
    return pl.pallas_call(
        _outproj_kernel,
        grid=(n // tm,),
        in_specs=[
            pl.BlockSpec((tm, RWKV_WIDTH), row),
            pl.BlockSpec((tm, MOBA_WIDTH), row),
            pl.BlockSpec((tm, D_MODEL), row),
            pl.BlockSpec((RWKV_WIDTH, D_MODEL), const),
            pl.BlockSpec((MOBA_WIDTH, D_MODEL), const),
            pl.BlockSpec((1, D_MODEL), const),
            pl.BlockSpec((1, D_MODEL), const),
            pl.BlockSpec((D_MODEL, ROUTER_PAD), const),
            pl.BlockSpec((1, ROUTER_PAD), const),
        ],
        out_specs=[
            pl.BlockSpec((tm, D_MODEL), row),
            pl.BlockSpec((tm, ROUTER_PAD), row),
        ],
        out_shape=[
            jax.ShapeDtypeStruct((n, D_MODEL), F32),
            jax.ShapeDtypeStruct((n, ROUTER_PAD), F32),
        ],
        compiler_params=pltpu.CompilerParams(
            dimension_semantics=("arbitrary",), vmem_limit_bytes=VMEM_LIMIT),
        name="outproj_ln_router",
    )(y_a, y_b, x2, wa, wb, ln_g, ln_b, w_router, b_router)


def _route(logits):
    lane = lax.broadcasted_iota(jnp.int32, logits.shape, 1)
    is_group = (lane >= GROUP_LANE0) & (lane < GROUP_LANE0 + N_GROUPS)
    gl = jnp.where(is_group, logits, F32_LOWEST)
    g_max = jnp.max(gl, axis=-1, keepdims=True)
    g_first = jnp.min(jnp.where(gl == g_max, lane, LANES), axis=-1, keepdims=True)
    g_exp = jnp.where(is_group, jnp.exp(gl - g_max), 0.0)
    p_g = 1.0 / jnp.sum(g_exp, axis=-1, keepdims=True)
    g_idx = g_first - GROUP_LANE0
    in_group = (lane >= g_idx * EXPERTS_PER_GROUP) & (lane < (g_idx + 1) * EXPERTS_PER_GROUP)
    el = jnp.where(in_group, logits, F32_LOWEST)
    e_max = jnp.max(el, axis=-1, keepdims=True)
    e_exp = jnp.where(in_group, jnp.exp(el - e_max), 0.0)
    e_prob = e_exp / jnp.sum(e_exp, axis=-1, keepdims=True)
    cand = jnp.where(in_group, e_prob, -1.0)
    v1 = jnp.max(cand, axis=-1, keepdims=True)
    i1 = jnp.min(jnp.where(cand == v1, lane, LANES), axis=-1, keepdims=True)
    pick1 = lane == i1
    cand2 = jnp.where(pick1, -1.0, cand)
    v2 = jnp.max(cand2, axis=-1, keepdims=True)
    i2 = jnp.min(jnp.where(cand2 == v2, lane, LANES), axis=-1, keepdims=True)
    pick2 = lane == i2
    denom = v1 + v2
    gates = jnp.where(pick1, v1 / denom * p_g, jnp.where(pick2, v2 / denom * p_g, 0.0))
    return gates, g_idx


def _route_sort_kernel(lg_ref, gates_ref, pos_ref, flags_ref, earlier_ref):
    tm = lg_ref.shape[0]

    @pl.when(pl.program_id(0) == 0)
    def _():
        row = lax.broadcasted_iota(jnp.int32, (tm, tm), 0)
        col = lax.broadcasted_iota(jnp.int32, (tm, tm), 1)
        earlier_ref[...] = jnp.where(row > col, 1.0, 0.0).astype(BF16)

    gates, g_idx = _route(lg_ref[...])
    lane = lax.broadcasted_iota(jnp.int32, (tm, LANES), 1)
    in_own = lane == g_idx
    onehot = jnp.where(in_own, 1.0, 0.0)
    rank = _dot(earlier_ref[...], onehot.astype(BF16))
    count = jnp.sum(onehot, axis=0, keepdims=True)
    r128 = lax.broadcasted_iota(jnp.int32, (LANES, LANES), 0)
    c128 = lax.broadcasted_iota(jnp.int32, (LANES, LANES), 1)
    below = jnp.where(r128 < c128, 1.0, 0.0)
    start = _dot(jnp.broadcast_to(count, (8, LANES)), below, NN,
                 precision=lax.Precision.HIGHEST)[0:1]
    pos = jnp.sum(jnp.where(in_own, rank + start, 0.0), axis=-1, keepdims=True)
    gates_ref[...] = gates
    pos_ref[...] = jnp.broadcast_to(pos, (tm, LANES))
    start_c = jnp.broadcast_to(start, (LANES, LANES)).T
    count_c = jnp.broadcast_to(count, (LANES, LANES)).T
    sub_lo = c128 * MOE_SUB
    hit = (count_c > 0.0) & (start_c < (sub_lo + MOE_SUB).astype(F32)) & (start_c + count_c > sub_lo.astype(F32))
    start_i = start_c.astype(jnp.int32)
    count_i = count_c.astype(jnp.int32)
    win = jnp.minimum((start_i // MOE_WINDOW_ALIGN) * MOE_WINDOW_ALIGN, tm - MOE_WINDOW)
    fits = (count_i > 0) & (start_i + count_i <= win + MOE_WINDOW)
    meta = jnp.where(c128 == MOE_META_FITS, jnp.where(fits, 1, 0),
                     jnp.where(c128 == MOE_META_WINDOW, win, jnp.where(hit, 1, 0)))
    flags_ref[0] = meta[0:8].astype(jnp.int32)


def _route_sort_call(logits):
    n = logits.shape[0]
    tm = MOE_TM
    row = lambda t: (t, 0)
    return pl.pallas_call(
        _route_sort_kernel,
        grid=(n // tm,),
        in_specs=[pl.BlockSpec((tm, ROUTER_PAD), row)],
        out_specs=[
            pl.BlockSpec((tm, ROUTER_PAD), row),
            pl.BlockSpec((tm, LANES), row),
            pl.BlockSpec((1, 8, LANES), lambda t: (t, 0, 0)),
        ],
        out_shape=[
            jax.ShapeDtypeStruct((n, ROUTER_PAD), F32),
            jax.ShapeDtypeStruct((n, LANES), F32),
            jax.ShapeDtypeStruct((n // tm, 8, LANES), jnp.int32),
        ],
        scratch_shapes=[pltpu.VMEM((tm, tm), BF16)],
        compiler_params=pltpu.CompilerParams(
            dimension_semantics=("arbitrary",), vmem_limit_bytes=VMEM_LIMIT),
        name="route_sort",
    )(logits)


def _moe_kernel(flags_ref, h_ref, gates_ref, pos_ref, w1_ref, w3_ref, w2_ref, g_ref, b_ref,
                o_ref, xs_ref, gs_ref, acc_ref, pt_ref):
    tile = pl.program_id(0)
    step = pl.program_id(1)
    tm = h_ref.shape[0]
    sub = MOE_SUB
    n_sub = tm // sub
    eps = MOE_EXPERTS_PER_STEP

    @pl.when(step == 0)
    def _():
        pos_b = pos_ref[...]
        pos_row = pos_b.T[0:1, :]
        g_hi, g_lo = _split_bf16(gates_ref[...], 2)
        src = jnp.concatenate([h_ref[...].astype(BF16), g_hi, g_lo], axis=1)
        for c0 in range(0, tm, sub):
            slot = (lax.broadcasted_iota(jnp.int32, (sub, tm), 0) + c0).astype(F32)
            p_c = jnp.where(slot == pos_row, 1.0, 0.0).astype(BF16)
            moved = _dot(p_c, src)
            xs_ref[c0:c0 + sub, :] = moved[:, :D_MODEL].astype(BF16)
            gs_ref[c0:c0 + sub, :] = moved[:, D_MODEL:D_MODEL + LANES] + moved[:, D_MODEL + LANES:]
        for c0 in range(0, tm, LANES):
            slot = (lax.broadcasted_iota(jnp.int32, (tm, LANES), 1) + c0).astype(F32)
            pt_ref[:, c0:c0 + LANES] = jnp.where(pos_b == slot, 1.0, 0.0).astype(BF16)
        acc_ref[...] = jnp.zeros_like(acc_ref)

    group = step // (EXPERTS_PER_GROUP // eps)

    def visit(rows):
        x_r = xs_ref[rows, :]
        g_r = gs_ref[rows, :]
        lane = lax.broadcasted_iota(jnp.int32, g_r.shape, 1)
        acc = acc_ref[rows, :]
        for e in range(eps):
            gate_e = jnp.sum(jnp.where(lane == step * eps + e, g_r, 0.0), axis=-1, keepdims=True)
            a1 = _dot(x_r, w1_ref[e])
            a3 = _dot(x_r, w3_ref[e])
            hid = (a1 * _sigmoid(a1)) * a3 * gate_e
            acc = acc + _dot(hid.astype(BF16), w2_ref[e])
        acc_ref[rows, :] = acc

    base = (tile * N_GROUPS + group) * (n_sub + 2)
    fits = flags_ref[base + n_sub] != 0

    @pl.when(fits)
    def _():
        first = pl.multiple_of(flags_ref[base + n_sub + 1], MOE_WINDOW_ALIGN)
        visit(pl.ds(first, MOE_WINDOW))

    for r in range(n_sub):
        pl.when(jnp.logical_not(fits) & (flags_ref[base + r] != 0))(
            functools.partial(visit, slice(r * sub, (r + 1) * sub)))

    @pl.when(step == N_EXPERTS // eps - 1)
    def _():
        ffn = _dot(pt_ref[...], acc_ref[...].astype(BF16))
        o_ref[...] = _layer_norm(DEEPNORM_ALPHA * h_ref[...] + ffn, g_ref[...], b_ref[...])


def _moe_call(flags, h_f32, gates, pos, w1, w3, w2, ln_g, ln_b):
    n = h_f32.shape[0]
    tm = MOE_TM
    row = lambda t, s, f: (t, 0)
    const = lambda t, s, f: (0, 0)
    wmap = lambda t, s, f: (s, 0, 0)
    eps = MOE_EXPERTS_PER_STEP
    assert EXPERTS_PER_GROUP % eps == 0 and tm % MOE_SUB == 0
    grid_spec = pltpu.PrefetchScalarGridSpec(
        num_scalar_prefetch=1,
        grid=(n // tm, N_EXPERTS // eps),
        in_specs=[
            pl.BlockSpec((tm, D_MODEL), row),
            pl.BlockSpec((tm, ROUTER_PAD), row),
            pl.BlockSpec((tm, LANES), row),
            pl.BlockSpec((eps, D_MODEL, D_EXPERT), wmap),
            pl.BlockSpec((eps, D_MODEL, D_EXPERT), wmap),
            pl.BlockSpec((eps, D_EXPERT, D_MODEL), wmap),
            pl.BlockSpec((1, D_MODEL), const),
            pl.BlockSpec((1, D_MODEL), const),
        ],
        out_specs=pl.BlockSpec((tm, D_MODEL), row),
        scratch_shapes=[
            pltpu.VMEM((tm, D_MODEL), BF16),
            pltpu.VMEM((tm, ROUTER_PAD), F32),
            pltpu.VMEM((tm, D_MODEL), F32),
            pltpu.VMEM((tm, tm), BF16),
        ],
    )
    return pl.pallas_call(
        _moe_kernel,
        grid_spec=grid_spec,
        out_shape=jax.ShapeDtypeStruct((n, D_MODEL), F32),
        compiler_params=pltpu.CompilerParams(
            dimension_semantics=("arbitrary", "arbitrary"), vmem_limit_bytes=VMEM_LIMIT),
        name="hier_moe_ln",
    )(flags, h_f32, gates, pos, w1, w3, w2, ln_g, ln_b)


def _pad_cols(w, width):
    return jnp.pad(w, ((0, 0), (0, width - w.shape[1])))


def _pad_rows(w, height):
    return jnp.pad(w, ((0, height - w.shape[0]), (0, 0)))


def kernel(x, w_in, mu_shift, w0, w_lora_up, a0, a_lora_up, g_lora_up, k_k, k_a, r_k, gn_w, gn_b, w_out, ln1_g, ln1_b, w_group, b_group, w_expert, b_expert, w1_exp, w3_exp, w2_exp, ln2_g, ln2_b):
    batch, seq_len, d = x.shape
    assert d == D_MODEL
    n = batch * seq_len
    x2 = x.reshape(n, d)

    c_rkv = 3 * RWKV_WIDTH
    c_wd = c_rkv + DECAY_RANK
    c_ad = c_wd + AAA_RANK
    c_gd = c_ad + GATE_RANK
    w_cat = jnp.concatenate([
        w_in[:, :c_rkv], _pad_cols(w_in[:, c_rkv:c_gd], LORA_PAD), w_in[:, c_gd:],
    ], axis=1).astype(BF16)
    mu2 = mu_shift[None, :]
    mu_cat = jnp.concatenate([mu2[:, :c_rkv], _pad_cols(mu2[:, c_rkv:c_gd], LORA_PAD)], axis=1)
    p_rkv, p_lora, p_k, p_qvt = _inproj_call(x2, w_cat, mu_cat, seq_len)

    place = lambda w, first: jnp.pad(w, ((first, LORA_PAD - first - w.shape[0]), (0, 0)))
    lora_up = (place(w_lora_up, 0), place(a_lora_up, DECAY_RANK), place(g_lora_up, DECAY_RANK + AAA_RANK))

    vecs = jnp.stack([w0, a0, k_k, k_a, r_k.reshape(-1), gn_w, gn_b, jnp.zeros_like(w0)], axis=0)
    head_id = jnp.arange(2 * PAIR) // HEAD_DIM
    bd = (head_id[:, None] == head_id[None, :]).astype(BF16)
    y_a = _rwkv_call(p_rkv, p_lora, vecs, *lora_up, bd, batch, seq_len)

    y_b = _moba_call(p_k, p_qvt, _moba_key_aug(seq_len), batch, seq_len)

    w_out_b = w_out.astype(BF16)
    w_router = _pad_cols(jnp.concatenate([w_expert, w_group], axis=1), ROUTER_PAD)
    b_router = _pad_cols(jnp.concatenate([b_expert, b_group])[None, :], ROUTER_PAD)
    h1, logits = _outproj_call(y_a, y_b, x2, w_out_b[:RWKV_WIDTH], w_out_b[RWKV_WIDTH:],
                               ln1_g[None, :], ln1_b[None, :], w_router, b_router)

    flat = lambda w: w.astype(BF16).reshape((N_EXPERTS,) + w.shape[2:])
    gates, pos, flags = _route_sort_call(logits)
    flags = jnp.concatenate([flags[:, :N_GROUPS, :MOE_TM // MOE_SUB],
                             flags[:, :N_GROUPS, MOE_META_FITS:MOE_META_WINDOW + 1]], axis=-1).reshape(-1)
    out = _moe_call(flags, h1, gates, pos, flat(w1_exp), flat(w3_exp), flat(w2_exp),
                    ln2_g[None, :], ln2_b[None, :])
    return out.reshape(batch, seq_len, d)
```

```python
import functools
import math

import jax
import jax.numpy as jnp
import numpy as np
from jax import lax
from jax.experimental import pallas as pl
from jax.experimental.pallas import tpu as pltpu

F32 = jnp.float32
BF16 = jnp.bfloat16

D_MODEL = 1024
HEAD_DIM = 64
RWKV_WIDTH = 512
MOBA_WIDTH = 512
DECAY_RANK = 32
AAA_RANK = 32
GATE_RANK = 96
GN_EPS = 64e-5
L2_EPS = 1e-12
MOBA_BLOCK = 256
MOBA_TOPK = 3
N_GROUPS = 4
EXPERTS_PER_GROUP = 8
N_EXPERTS = N_GROUPS * EXPERTS_PER_GROUP
D_EXPERT = 256
LN_EPS = 1e-5
DEEPNORM_ALPHA = float(2.0 ** 0.25)
NEG_INF = -1e30
F32_LOWEST = -3.0e38

LANES = 128
PAIR = 2 * HEAD_DIM
N_PAIRS = RWKV_WIDTH // PAIR
LORA_PAD = 2 * LANES
RWKV_COLS_PAD = 3 * RWKV_WIDTH + LORA_PAD
IN_COLS_PAD = RWKV_COLS_PAD + 3 * MOBA_WIDTH
VMEM_LIMIT = 56 * 1024 * 1024

INPROJ_TM = 512
INPROJ_TN = 256
RWKV_CHUNK = 64
RWKV_CHUNKS_PER_STEP = 4
RWKV_PASSES = 1
RWKV_STATE_PASSES = 1
OUTPROJ_TM = 512
MOE_TM = 1024
MOE_EXPERTS_PER_STEP = 8
MOE_SUB = 256
MOE_WINDOW = 304
MOE_WINDOW_ALIGN = 16
MOE_META_FITS = 16
MOE_META_WINDOW = 17
MOBA_KV_TILE = 512
MOBA_Q_TILE = 2048
MOBA_V_ROWS = HEAD_DIM + 16
MOBA_ALIBI_PARTS = 3
LOG2_E = 1.4426950408889634
ROUTER_PAD = LANES
GROUP_LANE0 = N_EXPERTS
ROUTER_ROWS = 40

NN = (((1,), (0,)), ((), ()))
NT = (((1,), (1,)), ((), ()))


def _dot(a, b, dims=NN, precision=None):
    return lax.dot_general(a, b, dims, precision=precision, preferred_element_type=F32)


def _split_bf16(x, parts):
    out = []
    rem = x
    for i in range(parts):
        p = rem.astype(BF16)
        out.append(p)
        if i + 1 < parts:
            rem = rem - p.astype(F32)
    return out


def _mm(a, b, dims=NN, passes=3):
    if passes == 1:
        return _dot(a.astype(BF16), b.astype(BF16), dims)
    if passes == 6:
        return _dot(a, b, dims, precision=lax.Precision.HIGHEST)
    a_hi, a_lo = _split_bf16(a, 2)
    b_hi, b_lo = _split_bf16(b, 2)
    return _dot(a_hi, b_hi, dims) + (_dot(a_hi, b_lo, dims) + _dot(a_lo, b_hi, dims))


def _mm_exact_lhs(a_bf16, b, dims=NN, parts=3):
    out = None
    for piece in reversed(_split_bf16(b, parts)):
        term = _dot(a_bf16, piece, dims)
        out = term if out is None else term + out
    return out


def _mm_exact_rhs(a, b_bf16, dims=NN):
    a1, a2, a3 = _split_bf16(a, 3)
    return _dot(a1, b_bf16, dims) + (_dot(a2, b_bf16, dims) + _dot(a3, b_bf16, dims))


def _inproj_kernel(x_ref, w_ref, mu_ref, prkv_ref, plora_ref, pk_ref, qvt_ref, carry_ref, *, tiles_per_seq):
    tm = x_ref.shape[0]
    xb = x_ref[...].astype(BF16)
    seq_start = (pl.program_id(0) % tiles_per_seq) == 0
    row0 = lax.broadcasted_iota(jnp.int32, (tm, INPROJ_TN), 0) == 0
    n_shift_tiles = RWKV_COLS_PAD // INPROJ_TN
    for j in range(n_shift_tiles):
        c0 = j * INPROJ_TN
        acc = _dot(xb, w_ref[:, c0:c0 + INPROJ_TN])
        prev_last = jnp.where(seq_start, 0.0, carry_ref[0:1, c0:c0 + INPROJ_TN])
        shifted = jnp.where(row0, prev_last, pltpu.roll(acc, 1, 0))
        carry_ref[0:1, c0:c0 + INPROJ_TN] = acc[tm - 1:tm, :]
        out = acc + (shifted - acc) * mu_ref[:, c0:c0 + INPROJ_TN]
        if c0 < 3 * RWKV_WIDTH:
            prkv_ref[:, c0:c0 + INPROJ_TN] = out
        else:
            plora_ref[:, c0 - 3 * RWKV_WIDTH:c0 - 3 * RWKV_WIDTH + INPROJ_TN] = out
    tiles_per_part = MOBA_WIDTH // INPROJ_TN
    for j in range(3 * tiles_per_part):
        c0 = j * INPROJ_TN
        acc = _dot(xb, w_ref[:, RWKV_COLS_PAD + c0:RWKV_COLS_PAD + c0 + INPROJ_TN])
        part, r0 = divmod(c0, MOBA_WIDTH)
        if part == 1:
            pk_ref[:, r0:r0 + INPROJ_TN] = acc.astype(BF16)
        else:
            r0 += (part // 2) * MOBA_WIDTH
            qvt_ref[r0:r0 + INPROJ_TN, :] = acc.T.astype(BF16)


def _inproj_call(x2, w_cat, mu_cat, seq_len):
    n = x2.shape[0]
    tm = INPROJ_TM
    assert seq_len % tm == 0 and (3 * RWKV_WIDTH) % INPROJ_TN == 0
    return pl.pallas_call(
        functools.partial(_inproj_kernel, tiles_per_seq=seq_len // tm),
        grid=(n // tm,),
        in_specs=[
            pl.BlockSpec((tm, D_MODEL), lambda i: (i, 0)),
            pl.BlockSpec((D_MODEL, IN_COLS_PAD), lambda i: (0, 0)),
            pl.BlockSpec((1, RWKV_COLS_PAD), lambda i: (0, 0)),
        ],
        out_specs=[
            pl.BlockSpec((tm, 3 * RWKV_WIDTH), lambda i: (i, 0)),
            pl.BlockSpec((tm, LORA_PAD), lambda i: (i, 0)),
            pl.BlockSpec((tm, MOBA_WIDTH), lambda i: (i, 0)),
            pl.BlockSpec((2 * MOBA_WIDTH, tm), lambda i: (0, i)),
        ],
        out_shape=[
            jax.ShapeDtypeStruct((n, 3 * RWKV_WIDTH), F32),
            jax.ShapeDtypeStruct((n, LORA_PAD), F32),
            jax.ShapeDtypeStruct((n, MOBA_WIDTH), BF16),
            jax.ShapeDtypeStruct((2 * MOBA_WIDTH, n), BF16),
        ],
        scratch_shapes=[pltpu.VMEM((8, RWKV_COLS_PAD), F32)],
        compiler_params=pltpu.CompilerParams(
            dimension_semantics=("arbitrary",), vmem_limit_bytes=VMEM_LIMIT),
        name="inproj_shift",
    )(x2, w_cat, mu_cat)


def _softplus(z):
    return jnp.maximum(z, 0.0) + jnp.log(1.0 + jnp.exp(-jnp.abs(z)))


def _sigmoid(z):
    return 1.0 / (1.0 + jnp.exp(-z))


def _rwkv_chunks(rt, kt, at, bt, v, d_incl, s_prev, passes, state_passes):
    c = RWKV_CHUNK
    n_chunks = rt.shape[0] // c
    n_pairs = len(s_prev)
    row = lax.broadcasted_iota(jnp.int32, (c, PAIR), 0)
    col = lax.broadcasted_iota(jnp.int32, (c, PAIR), 1) % HEAD_DIM
    strict = row > col
    incl = row >= col
    eye_c = (row == col).astype(F32)
    lane = lax.broadcasted_iota(jnp.int32, (1, PAIR), 1)
    head0 = lane < HEAD_DIM
    head1 = jnp.logical_not(head0)
    prow = lax.broadcasted_iota(jnp.int32, (PAIR, PAIR), 0)
    pcol = lax.broadcasted_iota(jnp.int32, (PAIR, PAIR), 1)
    same_head = (prow < HEAD_DIM) == (pcol < HEAD_DIM)
    eye_p = (prow == pcol).astype(F32)
    rows = [slice(ci * c, (ci + 1) * c) for ci in range(n_chunks)]
    sl = [slice(p * PAIR, (p + 1) * PAIR) for p in range(n_pairs)]
    pairs = [(ci, p) for ci in range(n_chunks) for p in range(n_pairs)]
    cut = lambda t, ci, p: t[rows[ci], sl[p]]

    rhs_dtype = BF16 if passes == 1 else F32

    def by_head(m, dtype=rhs_dtype):
        m = m.astype(dtype)
        zero = jnp.zeros_like(m)
        return jnp.concatenate([jnp.where(head0, m, zero), jnp.where(head1, m, zero)], axis=0)

    def by_head2(m, n):
        return jnp.concatenate([by_head(m), by_head(n)], axis=1)

    at_p = {k_: cut(at, *k_) for k_ in pairs}
    rt_p = {k_: cut(rt, *k_) for k_ in pairs}
    bt_p = {k_: cut(bt, *k_) for k_ in pairs}
    kt_p = {k_: cut(kt, *k_) for k_ in pairs}
    v_p = {k_: cut(v, *k_) for k_ in pairs}

    z = {k_: _mm(jnp.concatenate([at_p[k_], rt_p[k_]], axis=0),
                 jnp.concatenate([by_head(bt_p[k_], F32).T, by_head(kt_p[k_], F32).T], axis=1), NN, passes)
         for k_ in pairs}
    l_ab = {k_: jnp.where(strict, z[k_][:c, :PAIR], 0.0) for k_ in pairs}
    l_ak = {k_: jnp.where(strict, z[k_][:c, PAIR:], 0.0) for k_ in pairs}
    m_rb = {k_: jnp.where(incl, z[k_][c:, :PAIR], 0.0) for k_ in pairs}
    m_rk = {k_: jnp.where(incl, z[k_][c:, PAIR:], 0.0) for k_ in pairs}
    pw = {k_: _mm(l_ab[k_], by_head(l_ab[k_]), NN, passes) for k_ in pairs}
    t_inv = {k_: eye_c + l_ab[k_] for k_ in pairs}
    for _ in range(int(math.log2(c)) - 1):
        tp = {k_: _mm(jnp.concatenate([t_inv[k_], pw[k_]], axis=0), by_head(pw[k_]), NN, passes)
              for k_ in pairs}
        t_inv = {k_: t_inv[k_] + tp[k_][:c] for k_ in pairs}
        pw = {k_: tp[k_][c:] for k_ in pairs}
    lm = {k_: _mm(jnp.concatenate([l_ak[k_], m_rk[k_]], axis=0), by_head(v_p[k_]), NN, passes)
          for k_ in pairs}
    lv = {k_: lm[k_][:c] for k_ in pairs}
    mv = {k_: lm[k_][c:] for k_ in pairs}
    wu = {k_: _mm(t_inv[k_], by_head2(at_p[k_], lv[k_]), NN, passes) for k_ in pairs}
    qy = {k_: _mm(m_rb[k_], by_head2(wu[k_][:, :PAIR], wu[k_][:, PAIR:]), NN, passes) for k_ in pairs}

    qeff, y1, phi, psi = {}, {}, {}, {}
    for ci, p in pairs:
        k_ = (ci, p)
        w, u0 = wu[k_][:, :PAIR], wu[k_][:, PAIR:]
        qeff[k_] = rt_p[k_] + qy[k_][:, :PAIR]
        y1[k_] = qy[k_][:, PAIR:] + mv[k_]
        d_p = d_incl[(ci + 1) * c - 1:(ci + 1) * c, sl[p]]
        phi[k_] = jnp.where(same_head, (eye_p + _mm(w.T, bt_p[k_], NN, passes)) * d_p, 0.0)
        uv_t = jnp.concatenate([u0, v_p[k_]], axis=0).T
        bk = jnp.concatenate([bt_p[k_], kt_p[k_]], axis=0)
        psi[k_] = jnp.where(same_head, _mm(uv_t, bk, NN, passes) * d_p, 0.0)

    state = list(s_prev)
    ys = [[None] * n_pairs for _ in range(n_chunks)]
    for ci in range(n_chunks):
        for p in range(n_pairs):
            ys[ci][p] = _mm(qeff[ci, p], state[p].T, NN, state_passes) + y1[ci, p]
            state[p] = _mm(state[p], phi[ci, p], NN, state_passes) + psi[ci, p]
    y = jnp.concatenate([jnp.concatenate(ys[ci], axis=1) for ci in range(n_chunks)], axis=0)
    return y, state


def _rwkv_kernel(prkv_ref, plora_ref, vec_ref, wl_ref, al_ref, gl_ref, bd_ref, y_ref, s_ref):
    rows = prkv_ref.shape[0]
    c = RWKV_CHUNK
    width = RWKV_WIDTH

    @pl.when(pl.program_id(1) == 0)
    def _():
        s_ref[...] = jnp.zeros_like(s_ref)

    r = prkv_ref[:, 0:width]
    k_raw = prkv_ref[:, width:2 * width]
    v = prkv_ref[:, 2 * width:3 * width]
    p_wd = p_ad = p_gd = plora_ref[...]
    w0 = vec_ref[0:1, :]
    a0 = vec_ref[1:2, :]
    k_k = vec_ref[2:3, :]
    k_a = vec_ref[3:4, :]
    r_k = vec_ref[4:5, :]
    gn_w = vec_ref[5:6, :]
    gn_b = vec_ref[6:7, :]
    bd = bd_ref[...]

    def seg_sum(z):
        halves = []
        for c0 in range(0, width, bd.shape[0]):
            halves.append(_dot(z[:, c0:c0 + bd.shape[0]].astype(BF16), bd))
        return jnp.concatenate(halves, axis=1)

    w_log = -_softplus(-(w0 + _mm(jnp.tanh(p_wd), wl_ref[...], NN, 3))) - 0.5
    log_w = -jnp.exp(w_log)
    a = _sigmoid(a0 + _mm(p_ad, al_ref[...], NN, RWKV_PASSES))
    g = _mm(_sigmoid(p_gd), gl_ref[...], NN, RWKV_PASSES)
    kk = k_raw * k_k
    kk = kk * lax.rsqrt(jnp.maximum(seg_sum(kk * kk), L2_EPS * L2_EPS))
    k = k_raw * (1.0 + (a - 1.0) * k_a)

    row = lax.broadcasted_iota(jnp.int32, (rows, rows), 0)
    col = lax.broadcasted_iota(jnp.int32, (rows, rows), 1)
    tri = ((row >= col) & (row // c == col // c)).astype(BF16)
    cum = _mm_exact_lhs(tri, log_w, parts=2)
    d_incl = jnp.exp(cum)
    d_inv = jnp.exp(-cum)
    d_excl = jnp.exp(cum - log_w)
    rt = r * d_incl
    kt = k * d_inv
    at = -kk * d_excl
    bt = kk * a * d_inv

    y, s_next = _rwkv_chunks(rt, kt, at, bt, v, d_incl, [s_ref[p] for p in range(N_PAIRS)],
                             RWKV_PASSES, RWKV_STATE_PASSES)
    for p in range(N_PAIRS):
        s_ref[p] = s_next[p]

    inv_n = 1.0 / HEAD_DIM
    mu = seg_sum(y) * inv_n
    yc = y - mu
    var = seg_sum(yc * yc) * inv_n
    yn = yc * lax.rsqrt(var + GN_EPS) * gn_w + gn_b
    bonus = seg_sum(r * k * r_k) * v
    y_ref[...] = ((yn + bonus) * g).astype(y_ref.dtype)


def _rwkv_call(p_rkv, p_lora, vecs, wl, al, gl, bd, batch, seq_len):
    n = p_rkv.shape[0]
    rows = RWKV_CHUNK * RWKV_CHUNKS_PER_STEP
    assert seq_len % rows == 0
    steps = seq_len // rows
    row_map = lambda b, i: (b * steps + i, 0)
    const = lambda b, i: (0, 0)
    return pl.pallas_call(
        _rwkv_kernel,
        grid=(batch, steps),
        in_specs=[
            pl.BlockSpec((rows, 3 * RWKV_WIDTH), row_map),
            pl.BlockSpec((rows, LORA_PAD), row_map),
            pl.BlockSpec((8, RWKV_WIDTH), const),
            pl.BlockSpec((LORA_PAD, RWKV_WIDTH), const),
            pl.BlockSpec((LORA_PAD, RWKV_WIDTH), const),
            pl.BlockSpec((LORA_PAD, RWKV_WIDTH), const),
            pl.BlockSpec((2 * PAIR, 2 * PAIR), const),
        ],
        out_specs=pl.BlockSpec((rows, RWKV_WIDTH), row_map),
        out_shape=jax.ShapeDtypeStruct((n, RWKV_WIDTH), BF16),
        scratch_shapes=[pltpu.VMEM((N_PAIRS, PAIR, PAIR), F32)],
        compiler_params=pltpu.CompilerParams(
            dimension_semantics=("arbitrary", "arbitrary"), vmem_limit_bytes=VMEM_LIMIT),
        name="rwkv7_chunked",
    )(p_rkv, p_lora, vecs, wl, al, gl, bd)


def _moba_kernel(qt_ref, k_ref, vt_in_ref, kaug_ref, o_ref, kmean_ref, vt_ref,
                 m0_ref, m1_ref, acc0_ref, acc1_ref, s_even_ref, s_odd_ref,
                 smax_even_ref, smax_odd_ref, *, n_blocks):
    blk = MOBA_BLOCK
    tk = MOBA_KV_TILE
    tq = qt_ref.shape[1]
    i = pl.program_id(2)
    nb_pad = kmean_ref.shape[0]
    m_refs, acc_refs = (m0_ref, m1_ref), (acc0_ref, acc1_ref)

    @pl.when(i == 0)
    def _():
        kmean_ref[...] = jnp.zeros_like(kmean_ref)

        def mean_body(n, carry):
            off = pl.multiple_of(n * blk, blk)
            kb = k_ref[pl.ds(off, blk), :].astype(F32)
            kmean_ref[pl.ds(n, 1), :] = jnp.sum(kb, axis=0, keepdims=True) * (1.0 / blk)
            return carry
        lax.fori_loop(0, n_blocks, mean_body, 0)

        ones = jnp.ones((MOBA_V_ROWS - HEAD_DIM, tk), BF16)

        for j in range(vt_ref.shape[0]):
            for h in (0, 1):
                vt_ref[j, h] = jnp.concatenate(
                    [vt_in_ref[h * HEAD_DIM:(h + 1) * HEAD_DIM, j * tk:(j + 1) * tk], ones], axis=0)

    q_t = qt_ref[...].astype(F32)
    chan = lax.broadcasted_iota(jnp.int32, (PAIR, tq), 0)
    blk_row = lax.broadcasted_iota(jnp.int32, (nb_pad, tq), 0)
    own_blk = (i * tq + lax.broadcasted_iota(jnp.int32, (nb_pad, tq), 1)) // blk
    past = blk_row < own_blk
    aug_row = lax.broadcasted_iota(jnp.int32, (LANES, tq), 0)
    ones_rows = (aug_row >= n_blocks) & (aug_row < n_blocks + MOBA_ALIBI_PARTS)
    kmean = kmean_ref[...]

    qa_t = []
    for h in (0, 1):
        qh_t = jnp.where((chan < HEAD_DIM) == (h == 0), q_t, 0.0)
        gate = _mm_exact_rhs(kmean, qh_t.astype(BF16))
        gate = jnp.where(past, gate, F32_LOWEST)
        sel = jnp.zeros(gate.shape, jnp.bool_)
        for _ in range(MOBA_TOPK):
            mx = jnp.max(gate, axis=0, keepdims=True)
            first = jnp.min(jnp.where(gate == mx, blk_row, nb_pad), axis=0, keepdims=True)
            pick = (blk_row == first) & (mx > F32_LOWEST)
            sel = sel | pick
            gate = jnp.where(pick, F32_LOWEST, gate)
        sel_bias = jnp.where(past & jnp.logical_not(sel), NEG_INF, 0.0)
        aug_t = jnp.concatenate([sel_bias, jnp.zeros((LANES - nb_pad, tq), F32)], axis=0)
        aug_t = jnp.where(ones_rows, 1.0, aug_t)
        qa_t.append(jnp.concatenate([qh_t * (LOG2_E / math.sqrt(HEAD_DIM)), aug_t], axis=0).astype(BF16))

    def tile_scores(j, lanes=slice(None)):
        off = pl.multiple_of(j * tk, tk)
        k_t = k_ref[pl.ds(off, tk), :]
        return [_dot(jnp.concatenate([k_t, kaug_ref[h, pl.ds(off, tk), :]], axis=1), qa_t[h][:, lanes])
                for h in (0, 1)]

    def put_scores(buf, s, lanes=slice(None)):
        for h in (0, 1):
            buf[0][h, :, lanes] = s[h]
            buf[1][h, :, lanes] = jnp.max(s[h], axis=0, keepdims=True)

    def tile_update(j, buf, lanes=slice(None)):
        s_buf, smax_buf = buf
        for h in (0, 1):
            m_old = m_refs[h][:, lanes]
            m_new = jnp.maximum(m_old, smax_buf[h, :, lanes])
            p = jnp.exp2(s_buf[h, :, lanes] - m_new).astype(BF16)
            pv = _dot(vt_ref[j, h], p)
            acc_refs[h][:, lanes] = jnp.exp2(m_old - m_new) * acc_refs[h][:, lanes] + pv
            m_refs[h][:, lanes] = m_new

    for h in (0, 1):
        m_refs[h][...] = jnp.full(m_refs[h].shape, F32_LOWEST, F32)
        acc_refs[h][...] = jnp.zeros(acc_refs[h].shape, F32)

    buffers = ((s_even_ref, smax_even_ref), (s_odd_ref, smax_odd_ref))
    n_own = tq // tk
    j_first = i * n_own
    diagonal = (lax.broadcasted_iota(jnp.int32, (tk, tk), 0) <= lax.broadcasted_iota(jnp.int32, (tk, tk), 1))

    def own_scores(g):
        s = tile_scores(j_first + g, slice(g * tk, tq))
        own = [jnp.where(diagonal, s_h[:, :tk], NEG_INF) for s_h in s]
        if g == n_own - 1:
            return own
        return [jnp.concatenate([own_h, s_h[:, tk:]], axis=1) for own_h, s_h in zip(own, s)]

    put_scores(buffers[0], own_scores(n_own - 1), slice((n_own - 1) * tk, tq))
    for m in range(1, n_own):
        g = n_own - 1 - m
        put_scores(buffers[m % 2], own_scores(g), slice(g * tk, tq))
        tile_update(j_first + g + 1, buffers[(m - 1) % 2], slice((g + 1) * tk, tq))
    cur, nxt = buffers[(n_own - 1) % 2], buffers[n_own % 2]

    def previous_tile(j):
        return jnp.where(j == 0, j_first, j - 1)

    def pipelined_step(j, src, dst):
        put_scores(dst, tile_scores(j))
        tile_update(previous_tile(j), src)

    def kv_pair_step(u, carry):
        pipelined_step(2 * u, cur, nxt)
        pipelined_step(2 * u + 1, nxt, cur)
        return carry
    lax.fori_loop(0, j_first // 2, kv_pair_step, 0)

    @pl.when(j_first % 2 == 1)
    def _():
        pipelined_step(j_first - 1, cur, nxt)
        tile_update(j_first - 1, nxt)

    @pl.when(j_first % 2 == 0)
    def _():
        tile_update(previous_tile(j_first), cur)

    out_t = jnp.concatenate([acc_refs[h][0:HEAD_DIM, :] / acc_refs[h][HEAD_DIM:HEAD_DIM + 1, :]
                             for h in (0, 1)], axis=0)
    o_ref[...] = out_t.T.astype(o_ref.dtype)


def _moba_call(pk, qvt, kaug, batch, seq_len):
    n = pk.shape[0]
    blk = MOBA_BLOCK
    tq = MOBA_Q_TILE
    nb = seq_len // blk
    nq = seq_len // tq
    assert nb + MOBA_ALIBI_PARTS <= LANES and seq_len % MOBA_KV_TILE == 0
    assert tq % MOBA_KV_TILE == 0 and seq_len % tq == 0
    lane_groups = MOBA_WIDTH // LANES
    return pl.pallas_call(
        functools.partial(_moba_kernel, n_blocks=nb),
        grid=(batch, N_PAIRS, nq),
        in_specs=[
            pl.BlockSpec((PAIR, tq), lambda b, p, i: (p, b * nq + i)),
            pl.BlockSpec((seq_len, PAIR), lambda b, p, i: (b, p)),
            pl.BlockSpec((PAIR, seq_len), lambda b, p, i: (lane_groups + p, b)),
            pl.BlockSpec((2, seq_len, LANES), lambda b, p, i: (p, 0, 0)),
        ],
        out_specs=pl.BlockSpec((tq, PAIR), lambda b, p, i: (b * nq + i, p)),
        out_shape=jax.ShapeDtypeStruct((n, MOBA_WIDTH), BF16),
        scratch_shapes=[
            pltpu.VMEM((-(-nb // 8) * 8, PAIR), F32),
            pltpu.VMEM((seq_len // MOBA_KV_TILE, 2, MOBA_V_ROWS, MOBA_KV_TILE), BF16),
            pltpu.VMEM((1, tq), F32), pltpu.VMEM((1, tq), F32),
            pltpu.VMEM((MOBA_V_ROWS, tq), F32), pltpu.VMEM((MOBA_V_ROWS, tq), F32),
            pltpu.VMEM((2, MOBA_KV_TILE, tq), F32), pltpu.VMEM((2, MOBA_KV_TILE, tq), F32),
            pltpu.VMEM((2, 1, tq), F32), pltpu.VMEM((2, 1, tq), F32),
        ],
        compiler_params=pltpu.CompilerParams(
            dimension_semantics=("arbitrary", "arbitrary", "arbitrary"),
            vmem_limit_bytes=VMEM_LIMIT),
        name="moba_attention",
    )(qvt, pk, qvt, kaug)


def _moba_key_aug(seq_len):
    nb = seq_len // MOBA_BLOCK
    heads = MOBA_WIDTH // HEAD_DIM
    pos = np.arange(seq_len, dtype=np.int32)
    slopes = (2.0 ** (-8.0 * (np.arange(heads, dtype=np.float32) + 1.0) / heads)).astype(np.float32)
    aug = np.zeros((heads, seq_len, LANES), np.float32)
    aug[:, pos, pos // MOBA_BLOCK] = 1.0
    rem = (np.float32(LOG2_E) * slopes)[:, None] * pos.astype(np.float32)[None, :]
    for part in range(MOBA_ALIBI_PARTS):
        piece = (rem.view(np.uint32) & np.uint32(0xFFFF0000)).view(np.float32)
        aug[:, :, nb + part] = piece
        rem = rem - piece
    return jnp.asarray(aug.astype(BF16))


def _layer_norm(z, g, b):
    mu = jnp.mean(z, axis=-1, keepdims=True)
    zc = z - mu
    var = jnp.mean(zc * zc, axis=-1, keepdims=True)
    return zc * lax.rsqrt(var + LN_EPS) * g + b


def _outproj_kernel(ya_ref, yb_ref, x_ref, wa_ref, wb_ref, g_ref, b_ref, wr_ref, br_ref,
                    h_ref, lg_ref):
    mix = _dot(ya_ref[...], wa_ref[...]) + _dot(yb_ref[...], wb_ref[...])
    h = _layer_norm(DEEPNORM_ALPHA * x_ref[...] + mix, g_ref[...], b_ref[...])
    h_ref[...] = h
    h_hi, h_lo = _split_bf16(h, 2)
    w_hi, w_lo = _split_bf16(wr_ref[...], 2)
    hh_hl = _dot(h_hi, jnp.concatenate([w_hi, w_lo], axis=1))
    lg_ref[...] = hh_hl[:, :ROUTER_PAD] + (hh_hl[:, ROUTER_PAD:] + _dot(h_lo, w_hi)) + br_ref[...]


def _outproj_call(y_a, y_b, x2, wa, wb, ln_g, ln_b, w_router, b_router):
    n = x2.shape[0]
    tm = OUTPROJ_TM
    row = lambda i: (i, 0)
    const = lambda i: (0, 0)
    return pl.pallas_call(
        _outproj_kernel,
        grid=(n // tm,),
        in_specs=[
            pl.BlockSpec((tm, RWKV_WIDTH), row),
            pl.BlockSpec((tm, MOBA_WIDTH), row),
            pl.BlockSpec((tm, D_MODEL), row),
            pl.BlockSpec((RWKV_WIDTH, D_MODEL), const),
            pl.BlockSpec((MOBA_WIDTH, D_MODEL), const),
            pl.BlockSpec((1, D_MODEL), const),
            pl.BlockSpec((1, D_MODEL), const),
            pl.BlockSpec((D_MODEL, ROUTER_PAD), const),
            pl.BlockSpec((1, ROUTER_PAD), const),
        ],
        out_specs=[
            pl.BlockSpec((tm, D_MODEL), row),
            pl.BlockSpec((tm, ROUTER_PAD), row),
        ],
        out_shape=[
            jax.ShapeDtypeStruct((n, D_MODEL), F32),
            jax.ShapeDtypeStruct((n, ROUTER_PAD), F32),
        ],
        compiler_params=pltpu.CompilerParams(
            dimension_semantics=("arbitrary",), vmem_limit_bytes=VMEM_LIMIT),
        name="outproj_ln_router",
    )(y_a, y_b, x2, wa, wb, ln_g, ln_b, w_router, b_router)


def _route(logits_t):
    row = lax.broadcasted_iota(jnp.int32, logits_t.shape, 0)
    n_rows = logits_t.shape[0]
    is_group = (row >= GROUP_LANE0) & (row < GROUP_LANE0 + N_GROUPS)
    gl = jnp.where(is_group, logits_t, F32_LOWEST)
    g_max = jnp.max(gl, axis=0, keepdims=True)
    g_first = jnp.min(jnp.where(gl == g_max, row, n_rows), axis=0, keepdims=True)
    g_exp = jnp.where(is_group, jnp.exp(gl - g_max), 0.0)
    p_g = 1.0 / jnp.sum(g_exp, axis=0, keepdims=True)
    g_idx = g_first - GROUP_LANE0
    in_group = (row >= g_idx * EXPERTS_PER_GROUP) & (row < (g_idx + 1) * EXPERTS_PER_GROUP)
    el = jnp.where(in_group, logits_t, F32_LOWEST)
    e_max = jnp.max(el, axis=0, keepdims=True)
    e_exp = jnp.where(in_group, jnp.exp(el - e_max), 0.0)
    e_prob = e_exp / jnp.sum(e_exp, axis=0, keepdims=True)
    cand = jnp.where(in_group, e_prob, -1.0)
    v1 = jnp.max(cand, axis=0, keepdims=True)
    i1 = jnp.min(jnp.where(cand == v1, row, n_rows), axis=0, keepdims=True)
    pick1 = row == i1
    cand2 = jnp.where(pick1, -1.0, cand)
    v2 = jnp.max(cand2, axis=0, keepdims=True)
    i2 = jnp.min(jnp.where(cand2 == v2, row, n_rows), axis=0, keepdims=True)
    pick2 = row == i2
    denom = v1 + v2
    gates = jnp.where(pick1, v1 / denom * p_g, jnp.where(pick2, v2 / denom * p_g, 0.0))
    return gates, g_idx


def _route_sort_kernel(lg_ref, gates_ref, pos_ref, flags_ref, later_ref):
    tm = lg_ref.shape[0]

    @pl.when(pl.program_id(0) == 0)
    def _():
        row = lax.broadcasted_iota(jnp.int32, (tm, tm), 0)
        col = lax.broadcasted_iota(jnp.int32, (tm, tm), 1)
        later_ref[...] = jnp.where(row < col, 1.0, 0.0).astype(BF16)

    gates_t, g_idx = _route(lg_ref[...].T[:ROUTER_ROWS])
    own = lax.broadcasted_iota(jnp.int32, (8, tm), 0) == g_idx
    onehot = jnp.where(own, 1.0, 0.0)
    rank = _dot(onehot.astype(BF16), later_ref[...])
    count = jnp.broadcast_to(jnp.sum(onehot, axis=1, keepdims=True), (8, LANES))
    starts, running = [], jnp.zeros((1, LANES), F32)
    for g in range(8):
        starts.append(running)
        running = running + count[g:g + 1]
    start = jnp.concatenate(starts, axis=0)
    pos = jnp.sum(jnp.where(own, rank + start[:, 0:1], 0.0), axis=0, keepdims=True)
    gates_ref[...] = jnp.concatenate([gates_t, jnp.zeros((LANES - ROUTER_ROWS, tm), F32)], axis=0).T
    pos_ref[...] = jnp.broadcast_to(pos, (LANES, tm)).T
    sub_lo = lax.broadcasted_iota(jnp.int32, (8, LANES), 1) * MOE_SUB
    lane = lax.broadcasted_iota(jnp.int32, (8, LANES), 1)
    hit = (count > 0.0) & (start < (sub_lo + MOE_SUB).astype(F32)) & (start + count > sub_lo.astype(F32))
    start_i = start.astype(jnp.int32)
    count_i = count.astype(jnp.int32)
    win = jnp.minimum((start_i // MOE_WINDOW_ALIGN) * MOE_WINDOW_ALIGN, tm - MOE_WINDOW)
    fits = (count_i > 0) & (start_i + count_i <= win + MOE_WINDOW)
    meta = jnp.where(lane == MOE_META_FITS, jnp.where(fits, 1, 0),
                     jnp.where(lane == MOE_META_WINDOW, win, jnp.where(hit, 1, 0)))
    flags_ref[0] = meta.astype(jnp.int32)


def _route_sort_call(logits):
    n = logits.shape[0]
    tm = MOE_TM
    row = lambda t: (t, 0)
    return pl.pallas_call(
        _route_sort_kernel,
        grid=(n // tm,),
        in_specs=[pl.BlockSpec((tm, ROUTER_PAD), row)],
        out_specs=[
            pl.BlockSpec((tm, ROUTER_PAD), row),
            pl.BlockSpec((tm, LANES), row),
            pl.BlockSpec((1, 8, LANES), lambda t: (t, 0, 0)),
        ],
        out_shape=[
            jax.ShapeDtypeStruct((n, ROUTER_PAD), F32),
            jax.ShapeDtypeStruct((n, LANES), F32),
            jax.ShapeDtypeStruct((n // tm, 8, LANES), jnp.int32),
        ],
        scratch_shapes=[pltpu.VMEM((tm, tm), BF16)],
        compiler_params=pltpu.CompilerParams(
            dimension_semantics=("arbitrary",), vmem_limit_bytes=VMEM_LIMIT),
        name="route_sort",
    )(logits)


def _moe_kernel(flags_ref, h_ref, gates_ref, pos_ref, w1_ref, w3_ref, w2_ref, g_ref, b_ref,
                o_ref, xs_ref, gs_ref, acc_ref, pt_ref):
    tile = pl.program_id(0)
    step = pl.program_id(1)
    tm = h_ref.shape[0]
    sub = MOE_SUB
    n_sub = tm // sub
    eps = MOE_EXPERTS_PER_STEP

    @pl.when(step == 0)
    def _():
        pos_b = pos_ref[...]
        pos_row = pos_b.T[0:1, :]
        g_hi, g_lo = _split_bf16(gates_ref[...], 2)
        src = jnp.concatenate([h_ref[...].astype(BF16), g_hi, g_lo], axis=1)
        for c0 in range(0, tm, sub):
            slot = (lax.broadcasted_iota(jnp.int32, (sub, tm), 0) + c0).astype(F32)
            p_c = jnp.where(slot == pos_row, 1.0, 0.0).astype(BF16)
            moved = _dot(p_c, src)
            xs_ref[c0:c0 + sub, :] = moved[:, :D_MODEL].astype(BF16)
            gs_ref[c0:c0 + sub, :] = moved[:, D_MODEL:D_MODEL + LANES] + moved[:, D_MODEL + LANES:]
        for c0 in range(0, tm, LANES):
            slot = (lax.broadcasted_iota(jnp.int32, (tm, LANES), 1) + c0).astype(F32)
            pt_ref[:, c0:c0 + LANES] = jnp.where(pos_b == slot, 1.0, 0.0).astype(BF16)
        acc_ref[...] = jnp.zeros_like(acc_ref)

    group = step // (EXPERTS_PER_GROUP // eps)

    def visit(rows):
        x_r = xs_ref[rows, :]
        g_r = gs_ref[rows, :]
        lane = lax.broadcasted_iota(jnp.int32, g_r.shape, 1)
        acc = acc_ref[rows, :]
        for e in range(eps):
            gate_e = jnp.sum(jnp.where(lane == step * eps + e, g_r, 0.0), axis=-1, keepdims=True)
            a1 = _dot(x_r, w1_ref[e])
            a3 = _dot(x_r, w3_ref[e])
            hid = (a1 * _sigmoid(a1)) * a3 * gate_e
            acc = acc + _dot(hid.astype(BF16), w2_ref[e])
        acc_ref[rows, :] = acc

    base = (tile * N_GROUPS + group) * (n_sub + 2)
    fits = flags_ref[base + n_sub] != 0

    @pl.when(fits)
    def _():
        first = pl.multiple_of(flags_ref[base + n_sub + 1], MOE_WINDOW_ALIGN)
        visit(pl.ds(first, MOE_WINDOW))

    for r in range(n_sub):
        pl.when(jnp.logical_not(fits) & (flags_ref[base + r] != 0))(
            functools.partial(visit, slice(r * sub, (r + 1) * sub)))

    @pl.when(step == N_EXPERTS // eps - 1)
    def _():
        ffn = _dot(pt_ref[...], acc_ref[...].astype(BF16))
        o_ref[...] = _layer_norm(DEEPNORM_ALPHA * h_ref[...] + ffn, g_ref[...], b_ref[...])


def _moe_call(flags, h_f32, gates, pos, w1, w3, w2, ln_g, ln_b):
    n = h_f32.shape[0]
    tm = MOE_TM
    row = lambda t, s, f: (t, 0)
    const = lambda t, s, f: (0, 0)
    wmap = lambda t, s, f: (s, 0, 0)
    eps = MOE_EXPERTS_PER_STEP
    assert EXPERTS_PER_GROUP % eps == 0 and tm % MOE_SUB == 0
    grid_spec = pltpu.PrefetchScalarGridSpec(
        num_scalar_prefetch=1,
        grid=(n // tm, N_EXPERTS // eps),
        in_specs=[
            pl.BlockSpec((tm, D_MODEL), row),
            pl.BlockSpec((tm, ROUTER_PAD), row),
            pl.BlockSpec((tm, LANES), row),
            pl.BlockSpec((eps, D_MODEL, D_EXPERT), wmap),
            pl.BlockSpec((eps, D_MODEL, D_EXPERT), wmap),
            pl.BlockSpec((eps, D_EXPERT, D_MODEL), wmap),
            pl.BlockSpec((1, D_MODEL), const),
            pl.BlockSpec((1, D_MODEL), const),
        ],
        out_specs=pl.BlockSpec((tm, D_MODEL), row),
        scratch_shapes=[
            pltpu.VMEM((tm, D_MODEL), BF16),
            pltpu.VMEM((tm, ROUTER_PAD), F32),
            pltpu.VMEM((tm, D_MODEL), F32),
            pltpu.VMEM((tm, tm), BF16),
        ],
    )
    return pl.pallas_call(
        _moe_kernel,
        grid_spec=grid_spec,
        out_shape=jax.ShapeDtypeStruct((n, D_MODEL), F32),
        compiler_params=pltpu.CompilerParams(
            dimension_semantics=("arbitrary", "arbitrary"), vmem_limit_bytes=VMEM_LIMIT),
        name="hier_moe_ln",
    )(flags, h_f32, gates, pos, w1, w3, w2, ln_g, ln_b)


def _pad_cols(w, width):
    return jnp.pad(w, ((0, 0), (0, width - w.shape[1])))


def _pad_rows(w, height):
    return jnp.pad(w, ((0, height - w.shape[0]), (0, 0)))


def kernel(x, w_in, mu_shift, w0, w_lora_up, a0, a_lora_up, g_lora_up, k_k, k_a, r_k, gn_w, gn_b, w_out, ln1_g, ln1_b, w_group, b_group, w_expert, b_expert, w1_exp, w3_exp, w2_exp, ln2_g, ln2_b):
    batch, seq_len, d = x.shape
    assert d == D_MODEL
    n = batch * seq_len
    x2 = x.reshape(n, d)

    c_rkv = 3 * RWKV_WIDTH
    c_wd = c_rkv + DECAY_RANK
    c_ad = c_wd + AAA_RANK
    c_gd = c_ad + GATE_RANK
    w_cat = jnp.concatenate([
        w_in[:, :c_rkv], _pad_cols(w_in[:, c_rkv:c_gd], LORA_PAD), w_in[:, c_gd:],
    ], axis=1).astype(BF16)
    mu2 = mu_shift[None, :]
    mu_cat = jnp.concatenate([mu2[:, :c_rkv], _pad_cols(mu2[:, c_rkv:c_gd], LORA_PAD)], axis=1)
    p_rkv, p_lora, p_k, p_qvt = _inproj_call(x2, w_cat, mu_cat, seq_len)

    place = lambda w, first: jnp.pad(w, ((first, LORA_PAD - first - w.shape[0]), (0, 0)))
    lora_up = (place(w_lora_up, 0), place(a_lora_up, DECAY_RANK), place(g_lora_up, DECAY_RANK + AAA_RANK))

    vecs = jnp.stack([w0, a0, k_k, k_a, r_k.reshape(-1), gn_w, gn_b, jnp.zeros_like(w0)], axis=0)
    head_id = jnp.arange(2 * PAIR) // HEAD_DIM
    bd = (head_id[:, None] == head_id[None, :]).astype(BF16)
    y_a = _rwkv_call(p_rkv, p_lora, vecs, *lora_up, bd, batch, seq_len)

    y_b = _moba_call(p_k, p_qvt, _moba_key_aug(seq_len), batch, seq_len)

    w_out_b = w_out.astype(BF16)
    w_router = _pad_cols(jnp.concatenate([w_expert, w_group], axis=1), ROUTER_PAD)
    b_router = _pad_cols(jnp.concatenate([b_expert, b_group])[None, :], ROUTER_PAD)
    h1, logits = _outproj_call(y_a, y_b, x2, w_out_b[:RWKV_WIDTH], w_out_b[RWKV_WIDTH:],
                               ln1_g[None, :], ln1_b[None, :], w_router, b_router)

    flat = lambda w: w.astype(BF16).reshape((N_EXPERTS,) + w.shape[2:])
    gates, pos, flags = _route_sort_call(logits)
    flags = jnp.concatenate([flags[:, :N_GROUPS, :MOE_TM // MOE_SUB],
                             flags[:, :N_GROUPS, MOE_META_FITS:MOE_META_WINDOW + 1]], axis=-1).reshape(-1)
    out = _moe_call(flags, h1, gates, pos, flat(w1_exp), flat(w3_exp), flat(w2_exp),
                    ln2_g[None, :], ln2_b[None, :])
    return out.reshape(batch, seq_len, d)
```

```python
import functools
import math

import jax
import jax.numpy as jnp
import numpy as np
from jax import lax
from jax.experimental import pallas as pl
from jax.experimental.pallas import tpu as pltpu

F32 = jnp.float32
BF16 = jnp.bfloat16

D_MODEL = 1024
HEAD_DIM = 64
RWKV_WIDTH = 512
MOBA_WIDTH = 512
DECAY_RANK = 32
AAA_RANK = 32
GATE_RANK = 96
GN_EPS = 64e-5
L2_EPS = 1e-12
MOBA_BLOCK = 256
MOBA_TOPK = 3
N_GROUPS = 4
EXPERTS_PER_GROUP = 8
N_EXPERTS = N_GROUPS * EXPERTS_PER_GROUP
D_EXPERT = 256
LN_EPS = 1e-5
DEEPNORM_ALPHA = float(2.0 ** 0.25)
NEG_INF = -1e30
F32_LOWEST = -3.0e38

LANES = 128
PAIR = 2 * HEAD_DIM
N_PAIRS = RWKV_WIDTH // PAIR
LORA_PAD = 2 * LANES
RWKV_COLS_PAD = 3 * RWKV_WIDTH + LORA_PAD
IN_COLS_PAD = RWKV_COLS_PAD + 3 * MOBA_WIDTH
VMEM_LIMIT = 56 * 1024 * 1024

INPROJ_TM = 512
INPROJ_TN = 256
RWKV_CHUNK = 64
RWKV_CHUNKS_PER_STEP = 4
RWKV_PASSES = 1
RWKV_STATE_PASSES = 1
OUTPROJ_TM = 512
MOE_TM = 1024
MOE_EXPERTS_PER_STEP = 8
MOE_SUB = 256
MOE_WINDOW = 304
MOE_WINDOW_ALIGN = 16
MOE_META_FITS = 16
MOE_META_WINDOW = 17
MOBA_KV_TILE = 512
MOBA_Q_TILE = 2048
MOBA_V_ROWS = HEAD_DIM + 16
MOBA_ALIBI_PARTS = 3
LOG2_E = 1.4426950408889634
ROUTER_PAD = LANES
GROUP_LANE0 = N_EXPERTS
ROUTER_ROWS = 40

NN = (((1,), (0,)), ((), ()))


def _dot(a, b, dims=NN, precision=None):
    return lax.dot_general(a, b, dims, precision=precision, preferred_element_type=F32)


def _split_bf16(x, parts):
    out = []
    rem = x
    for i in range(parts):
        p = rem.astype(BF16)
        out.append(p)
        if i + 1 < parts:
            rem = rem - p.astype(F32)
    return out


def _mm(a, b, dims=NN, passes=3):
    if passes == 1:
        return _dot(a.astype(BF16), b.astype(BF16), dims)
    assert passes == 3
    a_hi, a_lo = _split_bf16(a, 2)
    b_hi, b_lo = _split_bf16(b, 2)
    return _dot(a_hi, b_hi, dims) + (_dot(a_hi, b_lo, dims) + _dot(a_lo, b_hi, dims))


def _mm_exact_lhs(a_bf16, b, dims=NN, parts=3):
    out = None
    for piece in reversed(_split_bf16(b, parts)):
        term = _dot(a_bf16, piece, dims)
        out = term if out is None else term + out
    return out


def _mm_exact_rhs(a, b_bf16, dims=NN):
    a1, a2, a3 = _split_bf16(a, 3)
    return _dot(a1, b_bf16, dims) + (_dot(a2, b_bf16, dims) + _dot(a3, b_bf16, dims))


def _inproj_kernel(x_ref, w_ref, mu_ref, prkv_ref, plora_ref, pk_ref, qvt_ref, carry_ref, *, tiles_per_seq):
    tm = x_ref.shape[0]
    xb = x_ref[...].astype(BF16)
    seq_start = (pl.program_id(0) % tiles_per_seq) == 0
    row0 = lax.broadcasted_iota(jnp.int32, (tm, INPROJ_TN), 0) == 0
    n_shift_tiles = RWKV_COLS_PAD // INPROJ_TN
    for j in range(n_shift_tiles):
        c0 = j * INPROJ_TN
        acc = _dot(xb, w_ref[:, c0:c0 + INPROJ_TN])
        prev_last = jnp.where(seq_start, 0.0, carry_ref[0:1, c0:c0 + INPROJ_TN])
        shifted = jnp.where(row0, prev_last, pltpu.roll(acc, 1, 0))
        carry_ref[0:1, c0:c0 + INPROJ_TN] = acc[tm - 1:tm, :]
        out = acc + (shifted - acc) * mu_ref[:, c0:c0 + INPROJ_TN]
        if c0 < 3 * RWKV_WIDTH:
            prkv_ref[:, c0:c0 + INPROJ_TN] = out
        else:
            plora_ref[:, c0 - 3 * RWKV_WIDTH:c0 - 3 * RWKV_WIDTH + INPROJ_TN] = out
    tiles_per_part = MOBA_WIDTH // INPROJ_TN
    for j in range(3 * tiles_per_part):
        c0 = j * INPROJ_TN
        acc = _dot(xb, w_ref[:, RWKV_COLS_PAD + c0:RWKV_COLS_PAD + c0 + INPROJ_TN])
        part, r0 = divmod(c0, MOBA_WIDTH)
        if part == 1:
            pk_ref[:, r0:r0 + INPROJ_TN] = acc.astype(BF16)
        else:
            r0 += (part // 2) * MOBA_WIDTH
            qvt_ref[r0:r0 + INPROJ_TN, :] = acc.T.astype(BF16)


def _inproj_call(x2, w_cat, mu_cat, seq_len):
    n = x2.shape[0]
    tm = INPROJ_TM
    assert seq_len % tm == 0 and (3 * RWKV_WIDTH) % INPROJ_TN == 0
    return pl.pallas_call(
        functools.partial(_inproj_kernel, tiles_per_seq=seq_len // tm),
        grid=(n // tm,),
        in_specs=[
            pl.BlockSpec((tm, D_MODEL), lambda i: (i, 0)),
            pl.BlockSpec((D_MODEL, IN_COLS_PAD), lambda i: (0, 0)),
            pl.BlockSpec((1, RWKV_COLS_PAD), lambda i: (0, 0)),
        ],
        out_specs=[
            pl.BlockSpec((tm, 3 * RWKV_WIDTH), lambda i: (i, 0)),
            pl.BlockSpec((tm, LORA_PAD), lambda i: (i, 0)),
            pl.BlockSpec((tm, MOBA_WIDTH), lambda i: (i, 0)),
            pl.BlockSpec((2 * MOBA_WIDTH, tm), lambda i: (0, i)),
        ],
        out_shape=[
            jax.ShapeDtypeStruct((n, 3 * RWKV_WIDTH), F32),
            jax.ShapeDtypeStruct((n, LORA_PAD), F32),
            jax.ShapeDtypeStruct((n, MOBA_WIDTH), BF16),
            jax.ShapeDtypeStruct((2 * MOBA_WIDTH, n), BF16),
        ],
        scratch_shapes=[pltpu.VMEM((8, RWKV_COLS_PAD), F32)],
        compiler_params=pltpu.CompilerParams(
            dimension_semantics=("arbitrary",), vmem_limit_bytes=VMEM_LIMIT),
        name="inproj_shift",
    )(x2, w_cat, mu_cat)


def _softplus(z):
    return jnp.maximum(z, 0.0) + jnp.log(1.0 + jnp.exp(-jnp.abs(z)))


def _sigmoid(z):
    return 1.0 / (1.0 + jnp.exp(-z))


def _rwkv_chunks(rt, kt, at, bt, v, d_incl, s_prev, passes, state_passes):
    c = RWKV_CHUNK
    n_chunks = rt.shape[0] // c
    n_pairs = len(s_prev)
    row = lax.broadcasted_iota(jnp.int32, (c, PAIR), 0)
    col = lax.broadcasted_iota(jnp.int32, (c, PAIR), 1) % HEAD_DIM
    strict = row > col
    incl = row >= col
    eye_c = (row == col).astype(F32)
    lane = lax.broadcasted_iota(jnp.int32, (1, PAIR), 1)
    head0 = lane < HEAD_DIM
    head1 = jnp.logical_not(head0)
    prow = lax.broadcasted_iota(jnp.int32, (PAIR, PAIR), 0)
    pcol = lax.broadcasted_iota(jnp.int32, (PAIR, PAIR), 1)
    same_head = (prow < HEAD_DIM) == (pcol < HEAD_DIM)
    eye_p = (prow == pcol).astype(F32)
    rows = [slice(ci * c, (ci + 1) * c) for ci in range(n_chunks)]
    sl = [slice(p * PAIR, (p + 1) * PAIR) for p in range(n_pairs)]
    pairs = [(ci, p) for ci in range(n_chunks) for p in range(n_pairs)]
    cut = lambda t, ci, p: t[rows[ci], sl[p]]

    rhs_dtype = BF16 if passes == 1 else F32

    def by_head(m, dtype=rhs_dtype):
        m = m.astype(dtype)
        zero = jnp.zeros_like(m)
        return jnp.concatenate([jnp.where(head0, m, zero), jnp.where(head1, m, zero)], axis=0)

    def by_head2(m, n):
        return jnp.concatenate([by_head(m), by_head(n)], axis=1)

    at_p = {k_: cut(at, *k_) for k_ in pairs}
    rt_p = {k_: cut(rt, *k_) for k_ in pairs}
    bt_p = {k_: cut(bt, *k_) for k_ in pairs}
    kt_p = {k_: cut(kt, *k_) for k_ in pairs}
    v_p = {k_: cut(v, *k_) for k_ in pairs}

    z = {k_: _mm(jnp.concatenate([at_p[k_], rt_p[k_]], axis=0),
                 jnp.concatenate([by_head(bt_p[k_], F32).T, by_head(kt_p[k_], F32).T], axis=1), NN, passes)
         for k_ in pairs}
    l_ab = {k_: jnp.where(strict, z[k_][:c, :PAIR], 0.0) for k_ in pairs}
    l_ak = {k_: jnp.where(strict, z[k_][:c, PAIR:], 0.0) for k_ in pairs}
    m_rb = {k_: jnp.where(incl, z[k_][c:, :PAIR], 0.0) for k_ in pairs}
    m_rk = {k_: jnp.where(incl, z[k_][c:, PAIR:], 0.0) for k_ in pairs}
    pw = {k_: _mm(l_ab[k_], by_head(l_ab[k_]), NN, passes) for k_ in pairs}
    t_inv = {k_: eye_c + l_ab[k_] for k_ in pairs}
    for _ in range(int(math.log2(c)) - 2):
        tp = {k_: _mm(jnp.concatenate([t_inv[k_], pw[k_]], axis=0), by_head(pw[k_]), NN, passes)
              for k_ in pairs}
        t_inv = {k_: t_inv[k_] + tp[k_][:c] for k_ in pairs}
        pw = {k_: tp[k_][c:] for k_ in pairs}
    t_inv = {k_: t_inv[k_] + _mm(t_inv[k_], by_head(pw[k_]), NN, passes) for k_ in pairs}
    lm = {k_: _mm(jnp.concatenate([l_ak[k_], m_rk[k_]], axis=0), by_head(v_p[k_]), NN, passes)
          for k_ in pairs}
    lv = {k_: lm[k_][:c] for k_ in pairs}
    mv = {k_: lm[k_][c:] for k_ in pairs}
    wu = {k_: _mm(t_inv[k_], by_head2(at_p[k_], lv[k_]), NN, passes) for k_ in pairs}
    qy = {k_: _mm(m_rb[k_], by_head2(wu[k_][:, :PAIR], wu[k_][:, PAIR:]), NN, passes) for k_ in pairs}

    qeff, y1, phi, psi = {}, {}, {}, {}
    for ci, p in pairs:
        k_ = (ci, p)
        w, u0 = wu[k_][:, :PAIR], wu[k_][:, PAIR:]
        qeff[k_] = rt_p[k_] + qy[k_][:, :PAIR]
        y1[k_] = qy[k_][:, PAIR:] + mv[k_]
        d_p = d_incl[(ci + 1) * c - 1:(ci + 1) * c, sl[p]]
        phi[k_] = jnp.where(same_head, (eye_p + _mm(w.T, bt_p[k_], NN, passes)) * d_p, 0.0)
        uv_t = jnp.concatenate([u0, v_p[k_]], axis=0).T
        bk = jnp.concatenate([bt_p[k_], kt_p[k_]], axis=0)
        psi[k_] = jnp.where(same_head, _mm(uv_t, bk, NN, passes) * d_p, 0.0)

    state = list(s_prev)
    ys = [[None] * n_pairs for _ in range(n_chunks)]
    for ci in range(n_chunks):
        for p in range(n_pairs):
            ys[ci][p] = _mm(qeff[ci, p], state[p].T, NN, state_passes) + y1[ci, p]
            state[p] = _mm(state[p], phi[ci, p], NN, state_passes) + psi[ci, p]
    y = jnp.concatenate([jnp.concatenate(ys[ci], axis=1) for ci in range(n_chunks)], axis=0)
    return y, state


def _rwkv_kernel(prkv_ref, plora_ref, vec_ref, wl_ref, al_ref, gl_ref, bd_ref, y_ref, s_ref):
    rows = prkv_ref.shape[0]
    c = RWKV_CHUNK
    width = RWKV_WIDTH

    @pl.when(pl.program_id(1) == 0)
    def _():
        s_ref[...] = jnp.zeros_like(s_ref)

    r = prkv_ref[:, 0:width]
    k_raw = prkv_ref[:, width:2 * width]
    v = prkv_ref[:, 2 * width:3 * width]
    p_wd = p_ad = p_gd = plora_ref[...]
    w0 = vec_ref[0:1, :]
    a0 = vec_ref[1:2, :]
    k_k = vec_ref[2:3, :]
    k_a = vec_ref[3:4, :]
    r_k = vec_ref[4:5, :]
    gn_w = vec_ref[5:6, :]
    gn_b = vec_ref[6:7, :]
    bd = bd_ref[...]

    def seg_sum(z):
        halves = []
        for c0 in range(0, width, bd.shape[0]):
            halves.append(_dot(z[:, c0:c0 + bd.shape[0]].astype(BF16), bd))
        return jnp.concatenate(halves, axis=1)

    w_log = -_softplus(-(w0 + _mm(jnp.tanh(p_wd), wl_ref[...], NN, 3))) - 0.5
    log_w = -jnp.exp(w_log)
    a = _sigmoid(a0 + _mm(p_ad, al_ref[...], NN, RWKV_PASSES))
    g = _mm(_sigmoid(p_gd), gl_ref[...], NN, RWKV_PASSES)
    kk = k_raw * k_k
    kk = kk * lax.rsqrt(jnp.maximum(seg_sum(kk * kk), L2_EPS * L2_EPS))
    k = k_raw * (1.0 + (a - 1.0) * k_a)

    row = lax.broadcasted_iota(jnp.int32, (rows, rows), 0)
    col = lax.broadcasted_iota(jnp.int32, (rows, rows), 1)
    tri = ((row >= col) & (row // c == col // c)).astype(BF16)
    cum = _mm_exact_lhs(tri, log_w, parts=2)
    d_incl = jnp.exp(cum)
    d_inv = jnp.exp(-cum)
    d_excl = jnp.exp(cum - log_w)
    rt = r * d_incl
    kt = k * d_inv
    at = -kk * d_excl
    bt = kk * a * d_inv

    y, s_next = _rwkv_chunks(rt, kt, at, bt, v, d_incl, [s_ref[p] for p in range(N_PAIRS)],
                             RWKV_PASSES, RWKV_STATE_PASSES)
    for p in range(N_PAIRS):
        s_ref[p] = s_next[p]

    inv_n = 1.0 / HEAD_DIM
    mu = seg_sum(y) * inv_n
    yc = y - mu
    var = seg_sum(yc * yc) * inv_n
    yn = yc * lax.rsqrt(var + GN_EPS) * gn_w + gn_b
    bonus = seg_sum(r * k * r_k) * v
    y_ref[...] = ((yn + bonus) * g).astype(y_ref.dtype)


def _rwkv_call(p_rkv, p_lora, vecs, wl, al, gl, bd, batch, seq_len):
    n = p_rkv.shape[0]
    rows = RWKV_CHUNK * RWKV_CHUNKS_PER_STEP
    assert seq_len % rows == 0
    steps = seq_len // rows
    row_map = lambda b, i: (b * steps + i, 0)
    const = lambda b, i: (0, 0)
    return pl.pallas_call(
        _rwkv_kernel,
        grid=(batch, steps),
        in_specs=[
            pl.BlockSpec((rows, 3 * RWKV_WIDTH), row_map),
            pl.BlockSpec((rows, LORA_PAD), row_map),
            pl.BlockSpec((8, RWKV_WIDTH), const),
            pl.BlockSpec((LORA_PAD, RWKV_WIDTH), const),
            pl.BlockSpec((LORA_PAD, RWKV_WIDTH), const),
            pl.BlockSpec((LORA_PAD, RWKV_WIDTH), const),
            pl.BlockSpec((2 * PAIR, 2 * PAIR), const),
        ],
        out_specs=pl.BlockSpec((rows, RWKV_WIDTH), row_map),
        out_shape=jax.ShapeDtypeStruct((n, RWKV_WIDTH), BF16),
        scratch_shapes=[pltpu.VMEM((N_PAIRS, PAIR, PAIR), F32)],
        compiler_params=pltpu.CompilerParams(
            dimension_semantics=("arbitrary", "arbitrary"), vmem_limit_bytes=VMEM_LIMIT),
        name="rwkv7_chunked",
    )(p_rkv, p_lora, vecs, wl, al, gl, bd)


def _moba_kernel(qt_ref, k_ref, vt_in_ref, kaug_ref, o_ref, kmean_ref, vt_ref,
                 m0_ref, m1_ref, acc0_ref, acc1_ref, s_even_ref, s_odd_ref,
                 smax_even_ref, smax_odd_ref, *, n_blocks):
    blk = MOBA_BLOCK
    tk = MOBA_KV_TILE
    tq = qt_ref.shape[1]
    i = pl.program_id(2)
    nb_pad = kmean_ref.shape[0]
    m_refs, acc_refs = (m0_ref, m1_ref), (acc0_ref, acc1_ref)

    @pl.when(i == 0)
    def _():
        kmean_ref[...] = jnp.zeros_like(kmean_ref)

        def mean_body(n, carry):
            off = pl.multiple_of(n * blk, blk)
            kb = k_ref[pl.ds(off, blk), :].astype(F32)
            kmean_ref[pl.ds(n, 1), :] = jnp.sum(kb, axis=0, keepdims=True) * (1.0 / blk)
            return carry
        lax.fori_loop(0, n_blocks, mean_body, 0)

        ones = jnp.ones((MOBA_V_ROWS - HEAD_DIM, tk), BF16)

        for j in range(vt_ref.shape[0]):
            for h in (0, 1):
                vt_ref[j, h] = jnp.concatenate(
                    [vt_in_ref[h * HEAD_DIM:(h + 1) * HEAD_DIM, j * tk:(j + 1) * tk], ones], axis=0)

    q_t = qt_ref[...].astype(F32)
    chan = lax.broadcasted_iota(jnp.int32, (PAIR, tq), 0)
    blk_row = lax.broadcasted_iota(jnp.int32, (nb_pad, tq), 0)
    own_blk = (i * tq + lax.broadcasted_iota(jnp.int32, (nb_pad, tq), 1)) // blk
    past = blk_row < own_blk
    aug_row = lax.broadcasted_iota(jnp.int32, (LANES, tq), 0)
    ones_rows = (aug_row >= n_blocks) & (aug_row < n_blocks + MOBA_ALIBI_PARTS)
    kmean = kmean_ref[...]

    qa_t = []
    for h in (0, 1):
        qh_t = jnp.where((chan < HEAD_DIM) == (h == 0), q_t, 0.0)
        gate = _mm_exact_rhs(kmean, qh_t.astype(BF16))
        gate = jnp.where(past, gate, F32_LOWEST)
        sel = jnp.zeros(gate.shape, jnp.bool_)
        for _ in range(MOBA_TOPK):
            mx = jnp.max(gate, axis=0, keepdims=True)
            first = jnp.min(jnp.where(gate == mx, blk_row, nb_pad), axis=0, keepdims=True)
            pick = (blk_row == first) & (mx > F32_LOWEST)
            sel = sel | pick
            gate = jnp.where(pick, F32_LOWEST, gate)
        sel_bias = jnp.where(past & jnp.logical_not(sel), NEG_INF, 0.0)
        aug_t = jnp.concatenate([sel_bias, jnp.zeros((LANES - nb_pad, tq), F32)], axis=0)
        aug_t = jnp.where(ones_rows, 1.0, aug_t)
        qa_t.append(jnp.concatenate([qh_t * (LOG2_E / math.sqrt(HEAD_DIM)), aug_t], axis=0).astype(BF16))

    def tile_scores(j, lanes=slice(None)):
        off = pl.multiple_of(j * tk, tk)
        k_t = k_ref[pl.ds(off, tk), :]
        return [_dot(jnp.concatenate([k_t, kaug_ref[h, pl.ds(off, tk), :]], axis=1), qa_t[h][:, lanes])
                for h in (0, 1)]

    def put_scores(buf, s, lanes=slice(None)):
        for h in (0, 1):
            buf[0][h, :, lanes] = s[h]
            buf[1][h, :, lanes] = jnp.max(s[h], axis=0, keepdims=True)

    def tile_update(j, buf, lanes=slice(None)):
        s_buf, smax_buf = buf
        for h in (0, 1):
            m_old = m_refs[h][:, lanes]
            m_new = jnp.maximum(m_old, smax_buf[h, :, lanes])
            p = jnp.exp2(s_buf[h, :, lanes] - m_new).astype(BF16)
            pv = _dot(vt_ref[j, h], p)
            acc_refs[h][:, lanes] = jnp.exp2(m_old - m_new) * acc_refs[h][:, lanes] + pv
            m_refs[h][:, lanes] = m_new

    for h in (0, 1):
        m_refs[h][...] = jnp.full(m_refs[h].shape, F32_LOWEST, F32)
        acc_refs[h][...] = jnp.zeros(acc_refs[h].shape, F32)

    buffers = ((s_even_ref, smax_even_ref), (s_odd_ref, smax_odd_ref))
    n_own = tq // tk
    j_first = i * n_own
    diagonal = (lax.broadcasted_iota(jnp.int32, (tk, tk), 0) <= lax.broadcasted_iota(jnp.int32, (tk, tk), 1))

    def own_scores(g):
        s = tile_scores(j_first + g, slice(g * tk, tq))
        own = [jnp.where(diagonal, s_h[:, :tk], NEG_INF) for s_h in s]
        if g == n_own - 1:
            return own
        return [jnp.concatenate([own_h, s_h[:, tk:]], axis=1) for own_h, s_h in zip(own, s)]

    put_scores(buffers[0], own_scores(n_own - 1), slice((n_own - 1) * tk, tq))
    for m in range(1, n_own):
        g = n_own - 1 - m
        put_scores(buffers[m % 2], own_scores(g), slice(g * tk, tq))
        tile_update(j_first + g + 1, buffers[(m - 1) % 2], slice((g + 1) * tk, tq))
    cur, nxt = buffers[(n_own - 1) % 2], buffers[n_own % 2]

    def previous_tile(j):
        return jnp.where(j == 0, j_first, j - 1)

    def pipelined_step(j, src, dst):
        put_scores(dst, tile_scores(j))
        tile_update(previous_tile(j), src)

    def kv_pair_step(u, carry):
        pipelined_step(2 * u, cur, nxt)
        pipelined_step(2 * u + 1, nxt, cur)
        return carry
    lax.fori_loop(0, j_first // 2, kv_pair_step, 0)

    @pl.when(j_first % 2 == 1)
    def _():
        pipelined_step(j_first - 1, cur, nxt)
        tile_update(j_first - 1, nxt)

    @pl.when(j_first % 2 == 0)
    def _():
        tile_update(previous_tile(j_first), cur)

    out_t = jnp.concatenate([acc_refs[h][0:HEAD_DIM, :] / acc_refs[h][HEAD_DIM:HEAD_DIM + 1, :]
                             for h in (0, 1)], axis=0)
    o_ref[...] = out_t.T.astype(o_ref.dtype)


def _moba_call(pk, qvt, kaug, batch, seq_len):
    n = pk.shape[0]
    blk = MOBA_BLOCK
    tq = MOBA_Q_TILE
    nb = seq_len // blk
    nq = seq_len // tq
    assert nb + MOBA_ALIBI_PARTS <= LANES and seq_len % MOBA_KV_TILE == 0
    assert tq % MOBA_KV_TILE == 0 and seq_len % tq == 0
    lane_groups = MOBA_WIDTH // LANES
    return pl.pallas_call(
        functools.partial(_moba_kernel, n_blocks=nb),
        grid=(batch, N_PAIRS, nq),
        in_specs=[
            pl.BlockSpec((PAIR, tq), lambda b, p, i: (p, b * nq + i)),
            pl.BlockSpec((seq_len, PAIR), lambda b, p, i: (b, p)),
            pl.BlockSpec((PAIR, seq_len), lambda b, p, i: (lane_groups + p, b)),
            pl.BlockSpec((2, seq_len, LANES), lambda b, p, i: (p, 0, 0)),
        ],
        out_specs=pl.BlockSpec((tq, PAIR), lambda b, p, i: (b * nq + i, p)),
        out_shape=jax.ShapeDtypeStruct((n, MOBA_WIDTH), BF16),
        scratch_shapes=[
            pltpu.VMEM((-(-nb // 8) * 8, PAIR), F32),
            pltpu.VMEM((seq_len // MOBA_KV_TILE, 2, MOBA_V_ROWS, MOBA_KV_TILE), BF16),
            pltpu.VMEM((1, tq), F32), pltpu.VMEM((1, tq), F32),
            pltpu.VMEM((MOBA_V_ROWS, tq), F32), pltpu.VMEM((MOBA_V_ROWS, tq), F32),
            pltpu.VMEM((2, MOBA_KV_TILE, tq), F32), pltpu.VMEM((2, MOBA_KV_TILE, tq), F32),
            pltpu.VMEM((2, 1, tq), F32), pltpu.VMEM((2, 1, tq), F32),
        ],
        compiler_params=pltpu.CompilerParams(
            dimension_semantics=("arbitrary", "arbitrary", "arbitrary"),
            vmem_limit_bytes=VMEM_LIMIT),
        name="moba_attention",
    )(qvt, pk, qvt, kaug)


def _moba_key_aug(seq_len):
    nb = seq_len // MOBA_BLOCK
    heads = MOBA_WIDTH // HEAD_DIM
    pos = np.arange(seq_len, dtype=np.int32)
    slopes = (2.0 ** (-8.0 * (np.arange(heads, dtype=np.float32) + 1.0) / heads)).astype(np.float32)
    aug = np.zeros((heads, seq_len, LANES), np.float32)
    aug[:, pos, pos // MOBA_BLOCK] = 1.0
    rem = (np.float32(LOG2_E) * slopes)[:, None] * pos.astype(np.float32)[None, :]
    for part in range(MOBA_ALIBI_PARTS):
        piece = (rem.view(np.uint32) & np.uint32(0xFFFF0000)).view(np.float32)
        aug[:, :, nb + part] = piece
        rem = rem - piece
    return jnp.asarray(aug.astype(BF16))


def _layer_norm(z, g, b):
    mu = jnp.mean(z, axis=-1, keepdims=True)
    zc = z - mu
    var = jnp.mean(zc * zc, axis=-1, keepdims=True)
    return zc * lax.rsqrt(var + LN_EPS) * g + b


def _outproj_kernel(ya_ref, yb_ref, x_ref, wa_ref, wb_ref, g_ref, b_ref, wr_ref, br_ref,
                    h_ref, lg_ref):
    mix = _dot(ya_ref[...], wa_ref[...]) + _dot(yb_ref[...], wb_ref[...])
    h = _layer_norm(DEEPNORM_ALPHA * x_ref[...] + mix, g_ref[...], b_ref[...])
    h_ref[...] = h
    h_hi, h_lo = _split_bf16(h, 2)
    w_hi, w_lo = _split_bf16(wr_ref[...], 2)
    hh_hl = _dot(h_hi, jnp.concatenate([w_hi, w_lo], axis=1))
    lg_ref[...] = hh_hl[:, :ROUTER_PAD] + (hh_hl[:, ROUTER_PAD:] + _dot(h_lo, w_hi)) + br_ref[...]


def _outproj_call(y_a, y_b, x2, wa, wb, ln_g, ln_b, w_router, b_router):
    n = x2.shape[0]
    tm = OUTPROJ_TM
    row = lambda i: (i, 0)
    const = lambda i: (0, 0)
    return pl.pallas_call(
        _outproj_kernel,
        grid=(n // tm,),
        in_specs=[
            pl.BlockSpec((tm, RWKV_WIDTH), row),
            pl.BlockSpec((tm, MOBA_WIDTH), row),
            pl.BlockSpec((tm, D_MODEL), row),
            pl.BlockSpec((RWKV_WIDTH, D_MODEL), const),
            pl.BlockSpec((MOBA_WIDTH, D_MODEL), const),
            pl.BlockSpec((1, D_MODEL), const),
            pl.BlockSpec((1, D_MODEL), const),
            pl.BlockSpec((D_MODEL, ROUTER_PAD), const),
            pl.BlockSpec((1, ROUTER_PAD), const),
        ],
        out_specs=[
            pl.BlockSpec((tm, D_MODEL), row),
            pl.BlockSpec((tm, ROUTER_PAD), row),
        ],
        out_shape=[
            jax.ShapeDtypeStruct((n, D_MODEL), F32),
            jax.ShapeDtypeStruct((n, ROUTER_PAD), F32),
        ],
        compiler_params=pltpu.CompilerParams(
            dimension_semantics=("arbitrary",), vmem_limit_bytes=VMEM_LIMIT),
        name="outproj_ln_router",
    )(y_a, y_b, x2, wa, wb, ln_g, ln_b, w_router, b_router)


def _route(logits_t):
    row = lax.broadcasted_iota(jnp.int32, logits_t.shape, 0)
    n_rows = logits_t.shape[0]
    is_group = (row >= GROUP_LANE0) & (row < GROUP_LANE0 + N_GROUPS)
    gl = jnp.where(is_group, logits_t, F32_LOWEST)
    g_max = jnp.max(gl, axis=0, keepdims=True)
    g_first = jnp.min(jnp.where(gl == g_max, row, n_rows), axis=0, keepdims=True)
    g_exp = jnp.where(is_group, jnp.exp(gl - g_max), 0.0)
    p_g = 1.0 / jnp.sum(g_exp, axis=0, keepdims=True)
    g_idx = g_first - GROUP_LANE0
    in_group = (row >= g_idx * EXPERTS_PER_GROUP) & (row < (g_idx + 1) * EXPERTS_PER_GROUP)
    el = jnp.where(in_group, logits_t, F32_LOWEST)
    e_max = jnp.max(el, axis=0, keepdims=True)
    e_exp = jnp.where(in_group, jnp.exp(el - e_max), 0.0)
    e_prob = e_exp / jnp.sum(e_exp, axis=0, keepdims=True)
    cand = jnp.where(in_group, e_prob, -1.0)
    v1 = jnp.max(cand, axis=0, keepdims=True)
    i1 = jnp.min(jnp.where(cand == v1, row, n_rows), axis=0, keepdims=True)
    pick1 = row == i1
    cand2 = jnp.where(pick1, -1.0, cand)
    v2 = jnp.max(cand2, axis=0, keepdims=True)
    i2 = jnp.min(jnp.where(cand2 == v2, row, n_rows), axis=0, keepdims=True)
    pick2 = row == i2
    denom = v1 + v2
    gates = jnp.where(pick1, v1 / denom * p_g, jnp.where(pick2, v2 / denom * p_g, 0.0))
    return gates, g_idx


def _route_sort_kernel(lg_ref, gates_ref, pos_ref, flags_ref, later_ref):
    tm = lg_ref.shape[0]

    @pl.when(pl.program_id(0) == 0)
    def _():
        row = lax.broadcasted_iota(jnp.int32, (tm, tm), 0)
        col = lax.broadcasted_iota(jnp.int32, (tm, tm), 1)
        later_ref[...] = jnp.where(row < col, 1.0, 0.0).astype(BF16)

    gates_t, g_idx = _route(lg_ref[...].T[:ROUTER_ROWS])
    own = lax.broadcasted_iota(jnp.int32, (8, tm), 0) == g_idx
    onehot = jnp.where(own, 1.0, 0.0)
    rank = _dot(onehot.astype(BF16), later_ref[...])
    count = jnp.broadcast_to(jnp.sum(onehot, axis=1, keepdims=True), (8, LANES))
    starts, running = [], jnp.zeros((1, LANES), F32)
    for g in range(8):
        starts.append(running)
        running = running + count[g:g + 1]
    start = jnp.concatenate(starts, axis=0)
    pos = jnp.sum(jnp.where(own, rank + start[:, 0:1], 0.0), axis=0, keepdims=True)
    gates_ref[...] = jnp.concatenate([gates_t, jnp.zeros((LANES - ROUTER_ROWS, tm), F32)], axis=0).T
    pos_ref[...] = jnp.broadcast_to(pos, (LANES, tm)).T
    sub_lo = lax.broadcasted_iota(jnp.int32, (8, LANES), 1) * MOE_SUB
    lane = lax.broadcasted_iota(jnp.int32, (8, LANES), 1)
    hit = (count > 0.0) & (start < (sub_lo + MOE_SUB).astype(F32)) & (start + count > sub_lo.astype(F32))
    start_i = start.astype(jnp.int32)
    count_i = count.astype(jnp.int32)
    win = jnp.minimum((start_i // MOE_WINDOW_ALIGN) * MOE_WINDOW_ALIGN, tm - MOE_WINDOW)
    fits = (count_i > 0) & (start_i + count_i <= win + MOE_WINDOW)
    meta = jnp.where(lane == MOE_META_FITS, jnp.where(fits, 1, 0),
                     jnp.where(lane == MOE_META_WINDOW, win, jnp.where(hit, 1, 0)))
    flags_ref[0] = meta.astype(jnp.int32)


def _route_sort_call(logits):
    n = logits.shape[0]
    tm = MOE_TM
    row = lambda t: (t, 0)
    return pl.pallas_call(
        _route_sort_kernel,
        grid=(n // tm,),
        in_specs=[pl.BlockSpec((tm, ROUTER_PAD), row)],
        out_specs=[
            pl.BlockSpec((tm, ROUTER_PAD), row),
            pl.BlockSpec((tm, LANES), row),
            pl.BlockSpec((1, 8, LANES), lambda t: (t, 0, 0)),
        ],
        out_shape=[
            jax.ShapeDtypeStruct((n, ROUTER_PAD), F32),
            jax.ShapeDtypeStruct((n, LANES), F32),
            jax.ShapeDtypeStruct((n // tm, 8, LANES), jnp.int32),
        ],
        scratch_shapes=[pltpu.VMEM((tm, tm), BF16)],
        compiler_params=pltpu.CompilerParams(
            dimension_semantics=("arbitrary",), vmem_limit_bytes=VMEM_LIMIT),
        name="route_sort",
    )(logits)


def _moe_kernel(flags_ref, h_ref, gates_ref, pos_ref, w1_ref, w3_ref, w2_ref, g_ref, b_ref,
                o_ref, xs_ref, gs_ref, acc_ref, pt_ref):
    tile = pl.program_id(0)
    step = pl.program_id(1)
    tm = h_ref.shape[0]
    sub = MOE_SUB
    n_sub = tm // sub
    eps = MOE_EXPERTS_PER_STEP

    @pl.when(step == 0)
    def _():
        pos_b = pos_ref[...]
        pos_row = pos_b.T[0:1, :]
        g_hi, g_lo = _split_bf16(gates_ref[...], 2)
        src = jnp.concatenate([h_ref[...].astype(BF16), g_hi, g_lo], axis=1)
        for c0 in range(0, tm, sub):
            slot = (lax.broadcasted_iota(jnp.int32, (sub, tm), 0) + c0).astype(F32)
            p_c = jnp.where(slot == pos_row, 1.0, 0.0).astype(BF16)
            moved = _dot(p_c, src)
            xs_ref[c0:c0 + sub, :] = moved[:, :D_MODEL].astype(BF16)
            gs_ref[c0:c0 + sub, :] = moved[:, D_MODEL:D_MODEL + LANES] + moved[:, D_MODEL + LANES:]
        for c0 in range(0, tm, LANES):
            slot = (lax.broadcasted_iota(jnp.int32, (tm, LANES), 1) + c0).astype(F32)
            pt_ref[:, c0:c0 + LANES] = jnp.where(pos_b == slot, 1.0, 0.0).astype(BF16)
        acc_ref[...] = jnp.zeros_like(acc_ref)

    group = step // (EXPERTS_PER_GROUP // eps)

    def visit(rows):
        x_r = xs_ref[rows, :]
        g_r = gs_ref[rows, :]
        lane = lax.broadcasted_iota(jnp.int32, g_r.shape, 1)
        acc = acc_ref[rows, :]
        for e in range(eps):
            gate_e = jnp.sum(jnp.where(lane == step * eps + e, g_r, 0.0), axis=-1, keepdims=True)
            a1 = _dot(x_r, w1_ref[e])
            a3 = _dot(x_r, w3_ref[e])
            hid = (a1 * _sigmoid(a1)) * a3 * gate_e
            acc = acc + _dot(hid.astype(BF16), w2_ref[e])
        acc_ref[rows, :] = acc

    base = (tile * N_GROUPS + group) * (n_sub + 2)
    fits = flags_ref[base + n_sub] != 0

    @pl.when(fits)
    def _():
        first = pl.multiple_of(flags_ref[base + n_sub + 1], MOE_WINDOW_ALIGN)
        visit(pl.ds(first, MOE_WINDOW))

    for r in range(n_sub):
        pl.when(jnp.logical_not(fits) & (flags_ref[base + r] != 0))(
            functools.partial(visit, slice(r * sub, (r + 1) * sub)))

    @pl.when(step == N_EXPERTS // eps - 1)
    def _():
        ffn = _dot(pt_ref[...], acc_ref[...].astype(BF16))
        o_ref[...] = _layer_norm(DEEPNORM_ALPHA * h_ref[...] + ffn, g_ref[...], b_ref[...])


def _moe_call(flags, h_f32, gates, pos, w1, w3, w2, ln_g, ln_b):
    n = h_f32.shape[0]
    tm = MOE_TM
    row = lambda t, s, f: (t, 0)
    const = lambda t, s, f: (0, 0)
    wmap = lambda t, s, f: (s, 0, 0)
    eps = MOE_EXPERTS_PER_STEP
    assert EXPERTS_PER_GROUP % eps == 0 and tm % MOE_SUB == 0
    grid_spec = pltpu.PrefetchScalarGridSpec(
        num_scalar_prefetch=1,
        grid=(n // tm, N_EXPERTS // eps),
        in_specs=[
            pl.BlockSpec((tm, D_MODEL), row),
            pl.BlockSpec((tm, ROUTER_PAD), row),
            pl.BlockSpec((tm, LANES), row),
            pl.BlockSpec((eps, D_MODEL, D_EXPERT), wmap),
            pl.BlockSpec((eps, D_MODEL, D_EXPERT), wmap),
            pl.BlockSpec((eps, D_EXPERT, D_MODEL), wmap),
            pl.BlockSpec((1, D_MODEL), const),
            pl.BlockSpec((1, D_MODEL), const),
        ],
        out_specs=pl.BlockSpec((tm, D_MODEL), row),
        scratch_shapes=[
            pltpu.VMEM((tm, D_MODEL), BF16),
            pltpu.VMEM((tm, ROUTER_PAD), F32),
            pltpu.VMEM((tm, D_MODEL), F32),
            pltpu.VMEM((tm, tm), BF16),
        ],
    )
    return pl.pallas_call(
        _moe_kernel,
        grid_spec=grid_spec,
        out_shape=jax.ShapeDtypeStruct((n, D_MODEL), F32),
        compiler_params=pltpu.CompilerParams(
            dimension_semantics=("arbitrary", "arbitrary"), vmem_limit_bytes=VMEM_LIMIT),
        name="hier_moe_ln",
    )(flags, h_f32, gates, pos, w1, w3, w2, ln_g, ln_b)


def _pad_cols(w, width):
    return jnp.pad(w, ((0, 0), (0, width - w.shape[1])))


def kernel(x, w_in, mu_shift, w0, w_lora_up, a0, a_lora_up, g_lora_up, k_k, k_a, r_k, gn_w, gn_b, w_out, ln1_g, ln1_b, w_group, b_group, w_expert, b_expert, w1_exp, w3_exp, w2_exp, ln2_g, ln2_b):
    batch, seq_len, d = x.shape
    assert d == D_MODEL
    n = batch * seq_len
    x2 = x.reshape(n, d)

    c_rkv = 3 * RWKV_WIDTH
    c_wd = c_rkv + DECAY_RANK
    c_ad = c_wd + AAA_RANK
    c_gd = c_ad + GATE_RANK
    w_cat = jnp.concatenate([
        w_in[:, :c_rkv], _pad_cols(w_in[:, c_rkv:c_gd], LORA_PAD), w_in[:, c_gd:],
    ], axis=1).astype(BF16)
    mu2 = mu_shift[None, :]
    mu_cat = jnp.concatenate([mu2[:, :c_rkv], _pad_cols(mu2[:, c_rkv:c_gd], LORA_PAD)], axis=1)
    p_rkv, p_lora, p_k, p_qvt = _inproj_call(x2, w_cat, mu_cat, seq_len)

    place = lambda w, first: jnp.pad(w, ((first, LORA_PAD - first - w.shape[0]), (0, 0)))
    lora_up = (place(w_lora_up, 0), place(a_lora_up, DECAY_RANK), place(g_lora_up, DECAY_RANK + AAA_RANK))

    vecs = jnp.stack([w0, a0, k_k, k_a, r_k.reshape(-1), gn_w, gn_b, jnp.zeros_like(w0)], axis=0)
    head_id = jnp.arange(2 * PAIR) // HEAD_DIM
    bd = (head_id[:, None] == head_id[None, :]).astype(BF16)
    y_a = _rwkv_call(p_rkv, p_lora, vecs, *lora_up, bd, batch, seq_len)

    y_b = _moba_call(p_k, p_qvt, _moba_key_aug(seq_len), batch, seq_len)

    w_out_b = w_out.astype(BF16)
    w_router = _pad_cols(jnp.concatenate([w_expert, w_group], axis=1), ROUTER_PAD)
    b_router = _pad_cols(jnp.concatenate([b_expert, b_group])[None, :], ROUTER_PAD)
    h1, logits = _outproj_call(y_a, y_b, x2, w_out_b[:RWKV_WIDTH], w_out_b[RWKV_WIDTH:],
                               ln1_g[None, :], ln1_b[None, :], w_router, b_router)

    flat = lambda w: w.astype(BF16).reshape((N_EXPERTS,) + w.shape[2:])
    gates, pos, flags = _route_sort_call(logits)
    flags = jnp.concatenate([flags[:, :N_GROUPS, :MOE_TM // MOE_SUB],
                             flags[:, :N_GROUPS, MOE_META_FITS:MOE_META_WINDOW + 1]], axis=-1).reshape(-1)
    out = _moe_call(flags, h1, gates, pos, flat(w1_exp), flat(w3_exp), flat(w2_exp),
                    ln2_g[None, :], ln2_b[None, :])
    return out.reshape(batch, seq_len, d)
```

```python
import functools
import math

import jax
import jax.numpy as jnp
import numpy as np
from jax import lax
from jax.experimental import pallas as pl
from jax.experimental.pallas import tpu as pltpu

F32 = jnp.float32
BF16 = jnp.bfloat16

D_MODEL = 1024
HEAD_DIM = 64
RWKV_WIDTH = 512
MOBA_WIDTH = 512
DECAY_RANK = 32
AAA_RANK = 32
GATE_RANK = 96
GN_EPS = 64e-5
L2_EPS = 1e-12
MOBA_BLOCK = 256
MOBA_TOPK = 3
N_GROUPS = 4
EXPERTS_PER_GROUP = 8
N_EXPERTS = N_GROUPS * EXPERTS_PER_GROUP
D_EXPERT = 256
LN_EPS = 1e-5
DEEPNORM_ALPHA = float(2.0 ** 0.25)
NEG_INF = -1e30
F32_LOWEST = -3.0e38

LANES = 128
SUBLANES = 8
BF16_SUBLANES = 16
MXU_TILE = 256
V7X_VMEM_BYTES = 64 * 1024 * 1024
PAIR = 2 * HEAD_DIM
N_PAIRS = RWKV_WIDTH // PAIR
LORA_PAD = MXU_TILE
RWKV_COLS_PAD = 3 * RWKV_WIDTH + LORA_PAD
IN_COLS_PAD = RWKV_COLS_PAD + 3 * MOBA_WIDTH
VMEM_LIMIT = V7X_VMEM_BYTES * 7 // 8

INPROJ_TM = 1024
INPROJ_TN = MXU_TILE
RWKV_CHUNK = 64
RWKV_CHUNKS_PER_STEP = 8
RWKV_PASSES = 1
RWKV_STATE_PASSES = 1
OUTPROJ_TM = 1024
MOE_TM = 1024
MOE_EXPERTS_PER_STEP = 8
MOE_SUB = 256
MOE_WINDOW = 304
MOE_WINDOW_ALIGN = BF16_SUBLANES
MOE_META_FITS = 16
MOE_META_WINDOW = 17
MOBA_KV_TILE = 512
MOBA_Q_TILE = 2048
MOBA_V_ROWS = HEAD_DIM + BF16_SUBLANES
MOBA_ALIBI_PARTS = 3
LOG2_E = 1.4426950408889634
ROUTER_PAD = LANES
GROUP_LANE0 = N_EXPERTS
ROUTER_ROWS = 40

NN = (((1,), (0,)), ((), ()))


def _dot(a, b, dims=NN):
    return lax.dot_general(a, b, dims, preferred_element_type=F32)


def _split_bf16(x, parts):
    out = []
    rem = x
    for i in range(parts):
        p = rem.astype(BF16)
        out.append(p)
        if i + 1 < parts:
            rem = rem - p.astype(F32)
    return out


def _mm(a, b, dims=NN, passes=3):
    if passes == 1:
        return _dot(a.astype(BF16), b.astype(BF16), dims)
    assert passes == 3
    a_hi, a_lo = _split_bf16(a, 2)
    b_hi, b_lo = _split_bf16(b, 2)
    return _dot(a_hi, b_hi, dims) + (_dot(a_hi, b_lo, dims) + _dot(a_lo, b_hi, dims))


def _mm_exact_lhs(a_bf16, b, dims=NN, parts=3):
    out = None
    for piece in reversed(_split_bf16(b, parts)):
        term = _dot(a_bf16, piece, dims)
        out = term if out is None else term + out
    return out


def _mm_exact_rhs(a, b_bf16, dims=NN):
    a1, a2, a3 = _split_bf16(a, 3)
    return _dot(a1, b_bf16, dims) + (_dot(a2, b_bf16, dims) + _dot(a3, b_bf16, dims))


def _inproj_kernel(x_ref, w_ref, mu_ref, prkv_ref, plora_ref, pk_ref, qvt_ref, carry_ref, *, tiles_per_seq):
    tm = x_ref.shape[0]
    xb = x_ref[...].astype(BF16)
    seq_start = (pl.program_id(0) % tiles_per_seq) == 0
    row0 = lax.broadcasted_iota(jnp.int32, (tm, INPROJ_TN), 0) == 0
    n_shift_tiles = RWKV_COLS_PAD // INPROJ_TN
    for j in range(n_shift_tiles):
        c0 = j * INPROJ_TN
        acc = _dot(xb, w_ref[:, c0:c0 + INPROJ_TN])
        prev_last = jnp.where(seq_start, 0.0, carry_ref[0:1, c0:c0 + INPROJ_TN])
        shifted = jnp.where(row0, prev_last, pltpu.roll(acc, 1, 0))
        carry_ref[0:1, c0:c0 + INPROJ_TN] = acc[tm - 1:tm, :]
        out = acc + (shifted - acc) * mu_ref[:, c0:c0 + INPROJ_TN]
        if c0 < 3 * RWKV_WIDTH:
            prkv_ref[:, c0:c0 + INPROJ_TN] = out
        else:
            plora_ref[:, c0 - 3 * RWKV_WIDTH:c0 - 3 * RWKV_WIDTH + INPROJ_TN] = out
    tiles_per_part = MOBA_WIDTH // INPROJ_TN
    for j in range(3 * tiles_per_part):
        c0 = j * INPROJ_TN
        acc = _dot(xb, w_ref[:, RWKV_COLS_PAD + c0:RWKV_COLS_PAD + c0 + INPROJ_TN])
        part, r0 = divmod(c0, MOBA_WIDTH)
        if part == 1:
            pk_ref[:, r0:r0 + INPROJ_TN] = acc.astype(BF16)
        else:
            r0 += (part // 2) * MOBA_WIDTH
            qvt_ref[r0:r0 + INPROJ_TN, :] = acc.T.astype(BF16)


def _inproj_call(x2, w_cat, mu_cat, seq_len):
    n = x2.shape[0]
    tm = INPROJ_TM
    assert seq_len % tm == 0 and (3 * RWKV_WIDTH) % INPROJ_TN == 0
    return pl.pallas_call(
        functools.partial(_inproj_kernel, tiles_per_seq=seq_len // tm),
        grid=(n // tm,),
        in_specs=[
            pl.BlockSpec((tm, D_MODEL), lambda i: (i, 0)),
            pl.BlockSpec((D_MODEL, IN_COLS_PAD), lambda i: (0, 0)),
            pl.BlockSpec((1, RWKV_COLS_PAD), lambda i: (0, 0)),
        ],
        out_specs=[
            pl.BlockSpec((tm, 3 * RWKV_WIDTH), lambda i: (i, 0)),
            pl.BlockSpec((tm, LORA_PAD), lambda i: (i, 0)),
            pl.BlockSpec((tm, MOBA_WIDTH), lambda i: (i, 0)),
            pl.BlockSpec((2 * MOBA_WIDTH, tm), lambda i: (0, i)),
        ],
        out_shape=[
            jax.ShapeDtypeStruct((n, 3 * RWKV_WIDTH), F32),
            jax.ShapeDtypeStruct((n, LORA_PAD), F32),
            jax.ShapeDtypeStruct((n, MOBA_WIDTH), BF16),
            jax.ShapeDtypeStruct((2 * MOBA_WIDTH, n), BF16),
        ],
        scratch_shapes=[pltpu.VMEM((SUBLANES, RWKV_COLS_PAD), F32)],
        compiler_params=pltpu.CompilerParams(
            dimension_semantics=("arbitrary",), vmem_limit_bytes=VMEM_LIMIT),
        name="inproj_shift",
    )(x2, w_cat, mu_cat)


def _sigmoid(z):
    return 1.0 / (1.0 + jnp.exp(-z))


def _rwkv_chunks(rt, kt, at, bt, v, d_incl, s_prev, passes, state_passes):
    c = RWKV_CHUNK
    n_chunks = rt.shape[0] // c
    n_pairs = len(s_prev)
    row = lax.broadcasted_iota(jnp.int32, (c, PAIR), 0)
    col = lax.broadcasted_iota(jnp.int32, (c, PAIR), 1) % HEAD_DIM
    strict = row > col
    incl = row >= col
    eye_c = (row == col).astype(F32)
    lane = lax.broadcasted_iota(jnp.int32, (1, PAIR), 1)
    head0 = lane < HEAD_DIM
    head1 = jnp.logical_not(head0)
    prow = lax.broadcasted_iota(jnp.int32, (PAIR, PAIR), 0)
    pcol = lax.broadcasted_iota(jnp.int32, (PAIR, PAIR), 1)
    same_head = (prow < HEAD_DIM) == (pcol < HEAD_DIM)
    eye_p = (prow == pcol).astype(F32)
    rows = [slice(ci * c, (ci + 1) * c) for ci in range(n_chunks)]
    sl = [slice(p * PAIR, (p + 1) * PAIR) for p in range(n_pairs)]
    pairs = [(ci, p) for ci in range(n_chunks) for p in range(n_pairs)]
    cut = lambda t, ci, p: t[rows[ci], sl[p]]

    rhs_dtype = BF16 if passes == 1 else F32

    def by_head(m, dtype=rhs_dtype):
        m = m.astype(dtype)
        zero = jnp.zeros_like(m)
        return jnp.concatenate([jnp.where(head0, m, zero), jnp.where(head1, m, zero)], axis=0)

    def by_head2(m, n):
        return jnp.concatenate([by_head(m), by_head(n)], axis=1)

    at_p = {k_: cut(at, *k_) for k_ in pairs}
    rt_p = {k_: cut(rt, *k_) for k_ in pairs}
    bt_p = {k_: cut(bt, *k_) for k_ in pairs}
    kt_p = {k_: cut(kt, *k_) for k_ in pairs}
    v_p = {k_: cut(v, *k_) for k_ in pairs}

    z = {k_: _mm(jnp.concatenate([at_p[k_], rt_p[k_]], axis=0),
                 jnp.concatenate([by_head(bt_p[k_], F32).T, by_head(kt_p[k_], F32).T], axis=1), NN, passes)
         for k_ in pairs}
    l_ab = {k_: jnp.where(strict, z[k_][:c, :PAIR], 0.0) for k_ in pairs}
    l_ak = {k_: jnp.where(strict, z[k_][:c, PAIR:], 0.0) for k_ in pairs}
    m_rb = {k_: jnp.where(incl, z[k_][c:, :PAIR], 0.0) for k_ in pairs}
    m_rk = {k_: jnp.where(incl, z[k_][c:, PAIR:], 0.0) for k_ in pairs}
    pw = {k_: _mm(l_ab[k_], by_head(l_ab[k_]), NN, passes) for k_ in pairs}
    t_inv = {k_: eye_c + l_ab[k_] for k_ in pairs}
    for _ in range(int(math.log2(c)) - 2):
        tp = {k_: _mm(jnp.concatenate([t_inv[k_], pw[k_]], axis=0), by_head(pw[k_]), NN, passes)
              for k_ in pairs}
        t_inv = {k_: t_inv[k_] + tp[k_][:c] for k_ in pairs}
        pw = {k_: tp[k_][c:] for k_ in pairs}
    t_inv = {k_: t_inv[k_] + _mm(t_inv[k_], by_head(pw[k_]), NN, passes) for k_ in pairs}
    lm = {k_: _mm(jnp.concatenate([l_ak[k_], m_rk[k_]], axis=0), by_head(v_p[k_]), NN, passes)
          for k_ in pairs}
    lv = {k_: lm[k_][:c] for k_ in pairs}
    mv = {k_: lm[k_][c:] for k_ in pairs}
    wu = {k_: _mm(t_inv[k_], by_head2(at_p[k_], lv[k_]), NN, passes) for k_ in pairs}
    qy = {k_: _mm(m_rb[k_], by_head2(wu[k_][:, :PAIR], wu[k_][:, PAIR:]), NN, passes) for k_ in pairs}

    qeff, y1, phi, psi = {}, {}, {}, {}
    for ci, p in pairs:
        k_ = (ci, p)
        w, u0 = wu[k_][:, :PAIR], wu[k_][:, PAIR:]
        qeff[k_] = rt_p[k_] + qy[k_][:, :PAIR]
        y1[k_] = qy[k_][:, PAIR:] + mv[k_]
        d_p = d_incl[(ci + 1) * c - 1:(ci + 1) * c, sl[p]]
        phi[k_] = jnp.where(same_head, (eye_p + _mm(w.T, bt_p[k_], NN, passes)) * d_p, 0.0)
        uv_t = jnp.concatenate([u0, v_p[k_]], axis=0).T
        bk = jnp.concatenate([bt_p[k_], kt_p[k_]], axis=0)
        psi[k_] = jnp.where(same_head, _mm(uv_t, bk, NN, passes) * d_p, 0.0)

    state = list(s_prev)
    ys = [[None] * n_pairs for _ in range(n_chunks)]
    for ci in range(n_chunks):
        for p in range(n_pairs):
            ys[ci][p] = _mm(qeff[ci, p], state[p].T, NN, state_passes) + y1[ci, p]
            state[p] = _mm(state[p], phi[ci, p], NN, state_passes) + psi[ci, p]
    y = jnp.concatenate([jnp.concatenate(ys[ci], axis=1) for ci in range(n_chunks)], axis=0)
    return y, state


def _rwkv_kernel(prkv_ref, plora_ref, vec_ref, wl_ref, al_ref, gl_ref, bd_ref, y_ref, s_ref):
    rows = prkv_ref.shape[0]
    c = RWKV_CHUNK
    width = RWKV_WIDTH

    @pl.when(pl.program_id(1) == 0)
    def _():
        s_ref[...] = jnp.zeros_like(s_ref)

    r = prkv_ref[:, 0:width]
    k_raw = prkv_ref[:, width:2 * width]
    v = prkv_ref[:, 2 * width:3 * width]
    p_wd = p_ad = p_gd = plora_ref[...]
    w0 = vec_ref[0:1, :]
    a0 = vec_ref[1:2, :]
    k_k = vec_ref[2:3, :]
    k_a = vec_ref[3:4, :]
    r_k = vec_ref[4:5, :]
    gn_w = vec_ref[5:6, :]
    gn_b = vec_ref[6:7, :]
    bd = bd_ref[...]

    def seg_sum(z):
        halves = []
        for c0 in range(0, width, bd.shape[0]):
            halves.append(_dot(z[:, c0:c0 + bd.shape[0]].astype(BF16), bd))
        return jnp.concatenate(halves, axis=1)

    log_w = -math.exp(-0.5) * _sigmoid(w0 + _mm(jnp.tanh(p_wd), wl_ref[...], NN, 3))
    a = _sigmoid(a0 + _mm(p_ad, al_ref[...], NN, RWKV_PASSES))
    g = _mm(_sigmoid(p_gd), gl_ref[...], NN, RWKV_PASSES)
    kk = k_raw * k_k
    kk = kk * lax.rsqrt(jnp.maximum(seg_sum(kk * kk), L2_EPS * L2_EPS))
    k = k_raw * (1.0 + (a - 1.0) * k_a)

    row = lax.broadcasted_iota(jnp.int32, (rows, rows), 0)
    col = lax.broadcasted_iota(jnp.int32, (rows, rows), 1)
    tri = ((row >= col) & (row // c == col // c)).astype(BF16)
    cum = _mm_exact_lhs(tri, log_w, parts=2)
    d_incl = jnp.exp(cum)
    d_inv = jnp.exp(-cum)
    d_excl = jnp.exp(cum - log_w)
    rt = r * d_incl
    kt = k * d_inv
    at = -kk * d_excl
    bt = kk * a * d_inv

    y, s_next = _rwkv_chunks(rt, kt, at, bt, v, d_incl, [s_ref[p] for p in range(N_PAIRS)],
                             RWKV_PASSES, RWKV_STATE_PASSES)
    for p in range(N_PAIRS):
        s_ref[p] = s_next[p]

    inv_n = 1.0 / HEAD_DIM
    mu = seg_sum(y) * inv_n
    yc = y - mu
    var = seg_sum(yc * yc) * inv_n
    yn = yc * lax.rsqrt(var + GN_EPS) * gn_w + gn_b
    bonus = seg_sum(r * k * r_k) * v
    y_ref[...] = ((yn + bonus) * g).astype(y_ref.dtype)


def _rwkv_call(p_rkv, p_lora, vecs, wl, al, gl, bd, batch, seq_len):
    n = p_rkv.shape[0]
    rows = RWKV_CHUNK * RWKV_CHUNKS_PER_STEP
    assert seq_len % rows == 0
    steps = seq_len // rows
    row_map = lambda b, i: (b * steps + i, 0)
    const = lambda b, i: (0, 0)
    return pl.pallas_call(
        _rwkv_kernel,
        grid=(batch, steps),
        in_specs=[
            pl.BlockSpec((rows, 3 * RWKV_WIDTH), row_map),
            pl.BlockSpec((rows, LORA_PAD), row_map),
            pl.BlockSpec((SUBLANES, RWKV_WIDTH), const),
            pl.BlockSpec((LORA_PAD, RWKV_WIDTH), const),
            pl.BlockSpec((LORA_PAD, RWKV_WIDTH), const),
            pl.BlockSpec((LORA_PAD, RWKV_WIDTH), const),
            pl.BlockSpec((2 * PAIR, 2 * PAIR), const),
        ],
        out_specs=pl.BlockSpec((rows, RWKV_WIDTH), row_map),
        out_shape=jax.ShapeDtypeStruct((n, RWKV_WIDTH), BF16),
        scratch_shapes=[pltpu.VMEM((N_PAIRS, PAIR, PAIR), F32)],
        compiler_params=pltpu.CompilerParams(
            dimension_semantics=("arbitrary", "arbitrary"), vmem_limit_bytes=VMEM_LIMIT),
        name="rwkv7_chunked",
    )(p_rkv, p_lora, vecs, wl, al, gl, bd)


def _moba_kernel(qt_ref, k_ref, vt_in_ref, kaug_ref, o_ref, kmean_ref, vt_ref,
                 m0_ref, m1_ref, acc0_ref, acc1_ref, s_even_ref, s_odd_ref,
                 smax_even_ref, smax_odd_ref, *, n_blocks):
    blk = MOBA_BLOCK
    tk = MOBA_KV_TILE
    tq = qt_ref.shape[1]
    i = pl.program_id(2)
    nb_pad = kmean_ref.shape[0]
    m_refs, acc_refs = (m0_ref, m1_ref), (acc0_ref, acc1_ref)

    @pl.when(i == 0)
    def _():
        kmean_ref[...] = jnp.zeros_like(kmean_ref)

        def mean_body(n, carry):
            off = pl.multiple_of(n * blk, blk)
            kb = k_ref[pl.ds(off, blk), :].astype(F32)
            kmean_ref[pl.ds(n, 1), :] = jnp.sum(kb, axis=0, keepdims=True) * (1.0 / blk)
            return carry
        lax.fori_loop(0, n_blocks, mean_body, 0)

        ones = jnp.ones((MOBA_V_ROWS - HEAD_DIM, tk), BF16)

        for j in range(vt_ref.shape[0]):
            for h in (0, 1):
                vt_ref[j, h] = jnp.concatenate(
                    [vt_in_ref[h * HEAD_DIM:(h + 1) * HEAD_DIM, j * tk:(j + 1) * tk], ones], axis=0)

    q_t = qt_ref[...].astype(F32)
    chan = lax.broadcasted_iota(jnp.int32, (PAIR, tq), 0)
    blk_row = lax.broadcasted_iota(jnp.int32, (nb_pad, tq), 0)
    own_blk = (i * tq + lax.broadcasted_iota(jnp.int32, (nb_pad, tq), 1)) // blk
    past = blk_row < own_blk
    aug_row = lax.broadcasted_iota(jnp.int32, (LANES, tq), 0)
    ones_rows = (aug_row >= n_blocks) & (aug_row < n_blocks + MOBA_ALIBI_PARTS)
    kmean = kmean_ref[...]

    qa_t = []
    for h in (0, 1):
        qh_t = jnp.where((chan < HEAD_DIM) == (h == 0), q_t, 0.0)
        gate = _mm_exact_rhs(kmean, qh_t.astype(BF16))
        gate = jnp.where(past, gate, F32_LOWEST)
        sel = jnp.zeros(gate.shape, jnp.bool_)
        for _ in range(MOBA_TOPK):
            mx = jnp.max(gate, axis=0, keepdims=True)
            first = jnp.min(jnp.where(gate == mx, blk_row, nb_pad), axis=0, keepdims=True)
            pick = (blk_row == first) & (mx > F32_LOWEST)
            sel = sel | pick
            gate = jnp.where(pick, F32_LOWEST, gate)
        sel_bias = jnp.where(past & jnp.logical_not(sel), NEG_INF, 0.0)
        aug_t = jnp.concatenate([sel_bias, jnp.zeros((LANES - nb_pad, tq), F32)], axis=0)
        aug_t = jnp.where(ones_rows, 1.0, aug_t)
        qa_t.append(jnp.concatenate([qh_t * (LOG2_E / math.sqrt(HEAD_DIM)), aug_t], axis=0).astype(BF16))

    def tile_scores(j, lanes=slice(None)):
        off = pl.multiple_of(j * tk, tk)
        k_t = k_ref[pl.ds(off, tk), :]
        return [_dot(jnp.concatenate([k_t, kaug_ref[h, pl.ds(off, tk), :]], axis=1), qa_t[h][:, lanes])
                for h in (0, 1)]

    def put_scores(buf, s, lanes=slice(None)):
        for h in (0, 1):
            buf[0][h, :, lanes] = s[h]
            buf[1][h, :, lanes] = jnp.max(s[h], axis=0, keepdims=True)

    def tile_update(j, buf, lanes=slice(None)):
        s_buf, smax_buf = buf
        for h in (0, 1):
            m_old = m_refs[h][:, lanes]
            m_new = jnp.maximum(m_old, smax_buf[h, :, lanes])
            p = jnp.exp2(s_buf[h, :, lanes] - m_new).astype(BF16)
            pv = _dot(vt_ref[j, h], p)
            acc_refs[h][:, lanes] = jnp.exp2(m_old - m_new) * acc_refs[h][:, lanes] + pv
            m_refs[h][:, lanes] = m_new

    for h in (0, 1):
        m_refs[h][...] = jnp.full(m_refs[h].shape, F32_LOWEST, F32)
        acc_refs[h][...] = jnp.zeros(acc_refs[h].shape, F32)

    buffers = ((s_even_ref, smax_even_ref), (s_odd_ref, smax_odd_ref))
    n_own = tq // tk
    j_first = i * n_own
    diagonal = (lax.broadcasted_iota(jnp.int32, (tk, tk), 0) <= lax.broadcasted_iota(jnp.int32, (tk, tk), 1))

    def own_scores(g):
        s = tile_scores(j_first + g, slice(g * tk, tq))
        own = [jnp.where(diagonal, s_h[:, :tk], NEG_INF) for s_h in s]
        if g == n_own - 1:
            return own
        return [jnp.concatenate([own_h, s_h[:, tk:]], axis=1) for own_h, s_h in zip(own, s)]

    put_scores(buffers[0], own_scores(n_own - 1), slice((n_own - 1) * tk, tq))
    for m in range(1, n_own):
        g = n_own - 1 - m
        put_scores(buffers[m % 2], own_scores(g), slice(g * tk, tq))
        tile_update(j_first + g + 1, buffers[(m - 1) % 2], slice((g + 1) * tk, tq))
    cur, nxt = buffers[(n_own - 1) % 2], buffers[n_own % 2]

    def previous_tile(j):
        return jnp.where(j == 0, j_first, j - 1)

    def pipelined_step(j, src, dst):
        put_scores(dst, tile_scores(j))
        tile_update(previous_tile(j), src)

    def kv_pair_step(u, carry):
        pipelined_step(2 * u, cur, nxt)
        pipelined_step(2 * u + 1, nxt, cur)
        return carry
    lax.fori_loop(0, j_first // 2, kv_pair_step, 0)

    @pl.when(j_first % 2 == 1)
    def _():
        pipelined_step(j_first - 1, cur, nxt)
        tile_update(j_first - 1, nxt)

    @pl.when(j_first % 2 == 0)
    def _():
        tile_update(previous_tile(j_first), cur)

    out_t = jnp.concatenate([acc_refs[h][0:HEAD_DIM, :] / acc_refs[h][HEAD_DIM:HEAD_DIM + 1, :]
                             for h in (0, 1)], axis=0)
    o_ref[...] = out_t.T.astype(o_ref.dtype)


def _moba_call(pk, qvt, kaug, batch, seq_len):
    n = pk.shape[0]
    blk = MOBA_BLOCK
    tq = MOBA_Q_TILE
    nb = seq_len // blk
    nq = seq_len // tq
    assert nb + MOBA_ALIBI_PARTS <= LANES and seq_len % MOBA_KV_TILE == 0
    assert tq % MOBA_KV_TILE == 0 and seq_len % tq == 0
    lane_groups = MOBA_WIDTH // LANES
    return pl.pallas_call(
        functools.partial(_moba_kernel, n_blocks=nb),
        grid=(batch, N_PAIRS, nq),
        in_specs=[
            pl.BlockSpec((PAIR, tq), lambda b, p, i: (p, b * nq + i)),
            pl.BlockSpec((seq_len, PAIR), lambda b, p, i: (b, p)),
            pl.BlockSpec((PAIR, seq_len), lambda b, p, i: (lane_groups + p, b)),
            pl.BlockSpec((2, seq_len, LANES), lambda b, p, i: (p, 0, 0)),
        ],
        out_specs=pl.BlockSpec((tq, PAIR), lambda b, p, i: (b * nq + i, p)),
        out_shape=jax.ShapeDtypeStruct((n, MOBA_WIDTH), BF16),
        scratch_shapes=[
            pltpu.VMEM((-(-nb // 8) * 8, PAIR), F32),
            pltpu.VMEM((seq_len // MOBA_KV_TILE, 2, MOBA_V_ROWS, MOBA_KV_TILE), BF16),
            pltpu.VMEM((1, tq), F32), pltpu.VMEM((1, tq), F32),
            pltpu.VMEM((MOBA_V_ROWS, tq), F32), pltpu.VMEM((MOBA_V_ROWS, tq), F32),
            pltpu.VMEM((2, MOBA_KV_TILE, tq), F32), pltpu.VMEM((2, MOBA_KV_TILE, tq), F32),
            pltpu.VMEM((2, 1, tq), F32), pltpu.VMEM((2, 1, tq), F32),
        ],
        compiler_params=pltpu.CompilerParams(
            dimension_semantics=("arbitrary", "arbitrary", "arbitrary"),
            vmem_limit_bytes=VMEM_LIMIT),
        name="moba_attention",
    )(qvt, pk, qvt, kaug)


def _moba_key_aug(seq_len):
    nb = seq_len // MOBA_BLOCK
    heads = MOBA_WIDTH // HEAD_DIM
    pos = np.arange(seq_len, dtype=np.int32)
    slopes = (2.0 ** (-8.0 * (np.arange(heads, dtype=np.float32) + 1.0) / heads)).astype(np.float32)
    aug = np.zeros((heads, seq_len, LANES), np.float32)
    aug[:, pos, pos // MOBA_BLOCK] = 1.0
    rem = (np.float32(LOG2_E) * slopes)[:, None] * pos.astype(np.float32)[None, :]
    for part in range(MOBA_ALIBI_PARTS):
        piece = (rem.view(np.uint32) & np.uint32(0xFFFF0000)).view(np.float32)
        aug[:, :, nb + part] = piece
        rem = rem - piece
    return jnp.asarray(aug.astype(BF16))


def _layer_norm(z, g, b):
    mu = jnp.mean(z, axis=-1, keepdims=True)
    zc = z - mu
    var = jnp.mean(zc * zc, axis=-1, keepdims=True)
    return zc * lax.rsqrt(var + LN_EPS) * g + b


def _outproj_kernel(ya_ref, yb_ref, x_ref, wa_ref, wb_ref, g_ref, b_ref, wr_ref, br_ref,
                    h_ref, lg_ref):
    mix = _dot(ya_ref[...], wa_ref[...]) + _dot(yb_ref[...], wb_ref[...])
    h = _layer_norm(DEEPNORM_ALPHA * x_ref[...] + mix, g_ref[...], b_ref[...])
    h_ref[...] = h
    h_hi, h_lo = _split_bf16(h, 2)
    w_hi, w_lo = _split_bf16(wr_ref[...], 2)
    hh_hl = _dot(h_hi, jnp.concatenate([w_hi, w_lo], axis=1))
    lg_ref[...] = hh_hl[:, :ROUTER_PAD] + (hh_hl[:, ROUTER_PAD:] + _dot(h_lo, w_hi)) + br_ref[...]


def _outproj_call(y_a, y_b, x2, wa, wb, ln_g, ln_b, w_router, b_router):
    n = x2.shape[0]
    tm = OUTPROJ_TM
    row = lambda i: (i, 0)
    const = lambda i: (0, 0)
    return pl.pallas_call(
        _outproj_kernel,
        grid=(n // tm,),
        in_specs=[
            pl.BlockSpec((tm, RWKV_WIDTH), row),
            pl.BlockSpec((tm, MOBA_WIDTH), row),
            pl.BlockSpec((tm, D_MODEL), row),
            pl.BlockSpec((RWKV_WIDTH, D_MODEL), const),
            pl.BlockSpec((MOBA_WIDTH, D_MODEL), const),
            pl.BlockSpec((1, D_MODEL), const),
            pl.BlockSpec((1, D_MODEL), const),
            pl.BlockSpec((D_MODEL, ROUTER_PAD), const),
            pl.BlockSpec((1, ROUTER_PAD), const),
        ],
        out_specs=[
            pl.BlockSpec((tm, D_MODEL), row),
            pl.BlockSpec((tm, ROUTER_PAD), row),
        ],
        out_shape=[
            jax.ShapeDtypeStruct((n, D_MODEL), F32),
            jax.ShapeDtypeStruct((n, ROUTER_PAD), F32),
        ],
        compiler_params=pltpu.CompilerParams(
            dimension_semantics=("arbitrary",), vmem_limit_bytes=VMEM_LIMIT),
        name="outproj_ln_router",
    )(y_a, y_b, x2, wa, wb, ln_g, ln_b, w_router, b_router)


def _route(logits_t):
    row = lax.broadcasted_iota(jnp.int32, logits_t.shape, 0)
    n_rows = logits_t.shape[0]
    is_group = (row >= GROUP_LANE0) & (row < GROUP_LANE0 + N_GROUPS)
    gl = jnp.where(is_group, logits_t, F32_LOWEST)
    g_max = jnp.max(gl, axis=0, keepdims=True)
    g_first = jnp.min(jnp.where(gl == g_max, row, n_rows), axis=0, keepdims=True)
    g_exp = jnp.where(is_group, jnp.exp(gl - g_max), 0.0)
    p_g = 1.0 / jnp.sum(g_exp, axis=0, keepdims=True)
    g_idx = g_first - GROUP_LANE0
    in_group = (row >= g_idx * EXPERTS_PER_GROUP) & (row < (g_idx + 1) * EXPERTS_PER_GROUP)
    el = jnp.where(in_group, logits_t, F32_LOWEST)
    e_max = jnp.max(el, axis=0, keepdims=True)
    e_exp = jnp.where(in_group, jnp.exp(el - e_max), 0.0)
    e_prob = e_exp / jnp.sum(e_exp, axis=0, keepdims=True)
    cand = jnp.where(in_group, e_prob, -1.0)
    v1 = jnp.max(cand, axis=0, keepdims=True)
    i1 = jnp.min(jnp.where(cand == v1, row, n_rows), axis=0, keepdims=True)
    pick1 = row == i1
    cand2 = jnp.where(pick1, -1.0, cand)
    v2 = jnp.max(cand2, axis=0, keepdims=True)
    i2 = jnp.min(jnp.where(cand2 == v2, row, n_rows), axis=0, keepdims=True)
    pick2 = row == i2
    denom = v1 + v2
    gates = jnp.where(pick1, v1 / denom * p_g, jnp.where(pick2, v2 / denom * p_g, 0.0))
    return gates, g_idx


def _route_sort_kernel(lg_ref, gates_ref, pos_ref, flags_ref, later_ref):
    tm = lg_ref.shape[0]

    @pl.when(pl.program_id(0) == 0)
    def _():
        row = lax.broadcasted_iota(jnp.int32, (tm, tm), 0)
        col = lax.broadcasted_iota(jnp.int32, (tm, tm), 1)
        later_ref[...] = jnp.where(row < col, 1.0, 0.0).astype(BF16)

    gates_t, g_idx = _route(lg_ref[...].T[:ROUTER_ROWS])
    own = lax.broadcasted_iota(jnp.int32, (SUBLANES, tm), 0) == g_idx
    onehot = jnp.where(own, 1.0, 0.0)
    rank = _dot(onehot.astype(BF16), later_ref[...])
    count = jnp.broadcast_to(jnp.sum(onehot, axis=1, keepdims=True), (SUBLANES, LANES))
    starts, running = [], jnp.zeros((1, LANES), F32)
    for g in range(SUBLANES):
        starts.append(running)
        running = running + count[g:g + 1]
    start = jnp.concatenate(starts, axis=0)
    pos = jnp.sum(jnp.where(own, rank + start[:, 0:1], 0.0), axis=0, keepdims=True)
    gates_ref[...] = jnp.concatenate([gates_t, jnp.zeros((LANES - ROUTER_ROWS, tm), F32)], axis=0).T
    pos_ref[...] = jnp.broadcast_to(pos, (LANES, tm)).T
    lane = lax.broadcasted_iota(jnp.int32, (SUBLANES, LANES), 1)
    sub_lo = lane * MOE_SUB
    hit = (count > 0.0) & (start < (sub_lo + MOE_SUB).astype(F32)) & (start + count > sub_lo.astype(F32))
    start_i = start.astype(jnp.int32)
    count_i = count.astype(jnp.int32)
    win = jnp.minimum((start_i // MOE_WINDOW_ALIGN) * MOE_WINDOW_ALIGN, tm - MOE_WINDOW)
    fits = (count_i > 0) & (start_i + count_i <= win + MOE_WINDOW)
    meta = jnp.where(lane == MOE_META_FITS, jnp.where(fits, 1, 0),
                     jnp.where(lane == MOE_META_WINDOW, win, jnp.where(hit, 1, 0)))
    flags_ref[0] = meta.astype(jnp.int32)


def _route_sort_call(logits):
    n = logits.shape[0]
    tm = MOE_TM
    row = lambda t: (t, 0)
    return pl.pallas_call(
        _route_sort_kernel,
        grid=(n // tm,),
        in_specs=[pl.BlockSpec((tm, ROUTER_PAD), row)],
        out_specs=[
            pl.BlockSpec((tm, ROUTER_PAD), row),
            pl.BlockSpec((tm, LANES), row),
            pl.BlockSpec((1, SUBLANES, LANES), lambda t: (t, 0, 0)),
        ],
        out_shape=[
            jax.ShapeDtypeStruct((n, ROUTER_PAD), F32),
            jax.ShapeDtypeStruct((n, LANES), F32),
            jax.ShapeDtypeStruct((n // tm, SUBLANES, LANES), jnp.int32),
        ],
        scratch_shapes=[pltpu.VMEM((tm, tm), BF16)],
        compiler_params=pltpu.CompilerParams(
            dimension_semantics=("arbitrary",), vmem_limit_bytes=VMEM_LIMIT),
        name="route_sort",
    )(logits)


def _moe_kernel(flags_ref, h_ref, gates_ref, pos_ref, w1_ref, w3_ref, w2_ref, g_ref, b_ref,
                o_ref, xs_ref, gs_ref, acc_ref, pt_ref):
    tile = pl.program_id(0)
    step = pl.program_id(1)
    tm = h_ref.shape[0]
    sub = MOE_SUB
    n_sub = tm // sub
    eps = MOE_EXPERTS_PER_STEP

    @pl.when(step == 0)
    def _():
        pos_b = pos_ref[...]
        pos_row = pos_b.T[0:1, :]
        g_hi, g_lo = _split_bf16(gates_ref[...], 2)
        src = jnp.concatenate([h_ref[...].astype(BF16), g_hi, g_lo], axis=1)
        for c0 in range(0, tm, sub):
            slot = (lax.broadcasted_iota(jnp.int32, (sub, tm), 0) + c0).astype(F32)
            p_c = jnp.where(slot == pos_row, 1.0, 0.0).astype(BF16)
            moved = _dot(p_c, src)
            xs_ref[c0:c0 + sub, :] = moved[:, :D_MODEL].astype(BF16)
            gs_ref[c0:c0 + sub, :] = moved[:, D_MODEL:D_MODEL + LANES] + moved[:, D_MODEL + LANES:]
        for c0 in range(0, tm, LANES):
            slot = (lax.broadcasted_iota(jnp.int32, (tm, LANES), 1) + c0).astype(F32)
            pt_ref[:, c0:c0 + LANES] = jnp.where(pos_b == slot, 1.0, 0.0).astype(BF16)
        acc_ref[...] = jnp.zeros_like(acc_ref)

    group = step // (EXPERTS_PER_GROUP // eps)

    def visit(rows):
        x_r = xs_ref[rows, :]
        g_r = gs_ref[rows, :]
        lane = lax.broadcasted_iota(jnp.int32, g_r.shape, 1)
        acc = acc_ref[rows, :]
        for e in range(eps):
            gate_e = jnp.sum(jnp.where(lane == step * eps + e, g_r, 0.0), axis=-1, keepdims=True)
            a1 = _dot(x_r, w1_ref[e])
            a3 = _dot(x_r, w3_ref[e])
            hid = (a1 * _sigmoid(a1)) * a3 * gate_e
            acc = acc + _dot(hid.astype(BF16), w2_ref[e])
        acc_ref[rows, :] = acc

    base = (tile * N_GROUPS + group) * (n_sub + 2)
    fits = flags_ref[base + n_sub] != 0

    @pl.when(fits)
    def _():
        first = pl.multiple_of(flags_ref[base + n_sub + 1], MOE_WINDOW_ALIGN)
        visit(pl.ds(first, MOE_WINDOW))

    for r in range(n_sub):
        pl.when(jnp.logical_not(fits) & (flags_ref[base + r] != 0))(
            functools.partial(visit, slice(r * sub, (r + 1) * sub)))

    @pl.when(step == N_EXPERTS // eps - 1)
    def _():
        ffn = _dot(pt_ref[...], acc_ref[...].astype(BF16))
        o_ref[...] = _layer_norm(DEEPNORM_ALPHA * h_ref[...] + ffn, g_ref[...], b_ref[...])


def _moe_call(flags, h_f32, gates, pos, w1, w3, w2, ln_g, ln_b):
    n = h_f32.shape[0]
    tm = MOE_TM
    row = lambda t, s, f: (t, 0)
    const = lambda t, s, f: (0, 0)
    wmap = lambda t, s, f: (s, 0, 0)
    eps = MOE_EXPERTS_PER_STEP
    assert EXPERTS_PER_GROUP % eps == 0 and tm % MOE_SUB == 0
    grid_spec = pltpu.PrefetchScalarGridSpec(
        num_scalar_prefetch=1,
        grid=(n // tm, N_EXPERTS // eps),
        in_specs=[
            pl.BlockSpec((tm, D_MODEL), row),
            pl.BlockSpec((tm, ROUTER_PAD), row),
            pl.BlockSpec((tm, LANES), row),
            pl.BlockSpec((eps, D_MODEL, D_EXPERT), wmap),
            pl.BlockSpec((eps, D_MODEL, D_EXPERT), wmap),
            pl.BlockSpec((eps, D_EXPERT, D_MODEL), wmap),
            pl.BlockSpec((1, D_MODEL), const),
            pl.BlockSpec((1, D_MODEL), const),
        ],
        out_specs=pl.BlockSpec((tm, D_MODEL), row),
        scratch_shapes=[
            pltpu.VMEM((tm, D_MODEL), BF16),
            pltpu.VMEM((tm, ROUTER_PAD), F32),
            pltpu.VMEM((tm, D_MODEL), F32),
            pltpu.VMEM((tm, tm), BF16),
        ],
    )
    return pl.pallas_call(
        _moe_kernel,
        grid_spec=grid_spec,
        out_shape=jax.ShapeDtypeStruct((n, D_MODEL), F32),
        compiler_params=pltpu.CompilerParams(
            dimension_semantics=("arbitrary", "arbitrary"), vmem_limit_bytes=VMEM_LIMIT),
        name="hier_moe_ln",
    )(flags, h_f32, gates, pos, w1, w3, w2, ln_g, ln_b)


def _pad_cols(w, width):
    return jnp.pad(w, ((0, 0), (0, width - w.shape[1])))


def kernel(x, w_in, mu_shift, w0, w_lora_up, a0, a_lora_up, g_lora_up, k_k, k_a, r_k, gn_w, gn_b, w_out, ln1_g, ln1_b, w_group, b_group, w_expert, b_expert, w1_exp, w3_exp, w2_exp, ln2_g, ln2_b):
    batch, seq_len, d = x.shape
    assert d == D_MODEL
    n = batch * seq_len
    x2 = x.reshape(n, d)

    c_rkv = 3 * RWKV_WIDTH
    c_wd = c_rkv + DECAY_RANK
    c_ad = c_wd + AAA_RANK
    c_gd = c_ad + GATE_RANK
    w_cat = jnp.concatenate([
        w_in[:, :c_rkv], _pad_cols(w_in[:, c_rkv:c_gd], LORA_PAD), w_in[:, c_gd:],
    ], axis=1).astype(BF16)
    mu2 = mu_shift[None, :]
    mu_cat = jnp.concatenate([mu2[:, :c_rkv], _pad_cols(mu2[:, c_rkv:c_gd], LORA_PAD)], axis=1)
    p_rkv, p_lora, p_k, p_qvt = _inproj_call(x2, w_cat, mu_cat, seq_len)

    place = lambda w, first: jnp.pad(w, ((first, LORA_PAD - first - w.shape[0]), (0, 0)))
    lora_up = (place(w_lora_up, 0), place(a_lora_up, DECAY_RANK), place(g_lora_up, DECAY_RANK + AAA_RANK))

    vecs = jnp.stack([w0, a0, k_k, k_a, r_k.reshape(-1), gn_w, gn_b, jnp.zeros_like(w0)], axis=0)
    head_id = jnp.arange(2 * PAIR) // HEAD_DIM
    bd = (head_id[:, None] == head_id[None, :]).astype(BF16)
    y_a = _rwkv_call(p_rkv, p_lora, vecs, *lora_up, bd, batch, seq_len)

    y_b = _moba_call(p_k, p_qvt, _moba_key_aug(seq_len), batch, seq_len)

    w_out_b = w_out.astype(BF16)
    w_router = _pad_cols(jnp.concatenate([w_expert, w_group], axis=1), ROUTER_PAD)
    b_router = _pad_cols(jnp.concatenate([b_expert, b_group])[None, :], ROUTER_PAD)
    h1, logits = _outproj_call(y_a, y_b, x2, w_out_b[:RWKV_WIDTH], w_out_b[RWKV_WIDTH:],
                               ln1_g[None, :], ln1_b[None, :], w_router, b_router)

    flat = lambda w: w.astype(BF16).reshape((N_EXPERTS,) + w.shape[2:])
    gates, pos, flags = _route_sort_call(logits)
    flags = jnp.concatenate([flags[:, :N_GROUPS, :MOE_TM // MOE_SUB],
                             flags[:, :N_GROUPS, MOE_META_FITS:MOE_META_WINDOW + 1]], axis=-1).reshape(-1)
    out = _moe_call(flags, h1, gates, pos, flat(w1_exp), flat(w3_exp), flat(w2_exp),
                    ln2_g[None, :], ln2_b[None, :])
    return out.reshape(batch, seq_len, d)
```

```python
import functools
import math

import jax
import jax.numpy as jnp
import numpy as np
from jax import lax
from jax.experimental import pallas as pl
from jax.experimental.pallas import tpu as pltpu

F32 = jnp.float32
BF16 = jnp.bfloat16

D_MODEL = 1024
HEAD_DIM = 64
RWKV_WIDTH = 512
MOBA_WIDTH = 512
DECAY_RANK = 32
AAA_RANK = 32
GATE_RANK = 96
GN_EPS = 64e-5
L2_EPS = 1e-12
MOBA_BLOCK = 256
MOBA_TOPK = 3
N_GROUPS = 4
EXPERTS_PER_GROUP = 8
N_EXPERTS = N_GROUPS * EXPERTS_PER_GROUP
D_EXPERT = 256
LN_EPS = 1e-5
DEEPNORM_ALPHA = float(2.0 ** 0.25)
NEG_INF = -1e30
F32_LOWEST = -3.0e38

LANES = 128
SUBLANES = 8
BF16_SUBLANES = 16
MXU_TILE = 256
V7X_VMEM_BYTES = 64 * 1024 * 1024
PAIR = 2 * HEAD_DIM
N_PAIRS = RWKV_WIDTH // PAIR
LORA_PAD = MXU_TILE
RWKV_COLS_PAD = 3 * RWKV_WIDTH + LORA_PAD
IN_COLS_PAD = RWKV_COLS_PAD + 3 * MOBA_WIDTH
VMEM_LIMIT = V7X_VMEM_BYTES * 7 // 8

INPROJ_TM = 1024
INPROJ_TN = MXU_TILE
RWKV_CHUNK = 64
RWKV_CHUNKS_PER_STEP = 8
RWKV_PASSES = 1
RWKV_STATE_PASSES = 1
OUTPROJ_TM = 1024
MOE_TM = 1024
MOE_EXPERTS_PER_STEP = 8
MOE_SUB = 256
MOE_WINDOW = 304
MOE_WINDOW_ALIGN = BF16_SUBLANES
MOE_META_FITS = 16
MOE_META_WINDOW = 17
MOBA_KV_TILE = 512
MOBA_Q_TILE = 2048
MOBA_V_ROWS = HEAD_DIM + BF16_SUBLANES
MOBA_ALIBI_PARTS = 3
LOG2_E = 1.4426950408889634
ROUTER_PAD = LANES
GROUP_LANE0 = N_EXPERTS
ROUTER_ROWS = 40

NN = (((1,), (0,)), ((), ()))


def _dot(a, b, dims=NN):
    return lax.dot_general(a, b, dims, preferred_element_type=F32)


def _split_bf16(x, parts):
    out = []
    rem = x
    for i in range(parts):
        p = rem.astype(BF16)
        out.append(p)
        if i + 1 < parts:
            rem = rem - p.astype(F32)
    return out


def _mm(a, b, dims=NN, passes=3):
    if passes == 1:
        return _dot(a.astype(BF16), b.astype(BF16), dims)
    assert passes == 3
    a_hi, a_lo = _split_bf16(a, 2)
    b_hi, b_lo = _split_bf16(b, 2)
    return _dot(a_hi, b_hi, dims) + (_dot(a_hi, b_lo, dims) + _dot(a_lo, b_hi, dims))


def _mm_exact_lhs(a_bf16, b, dims=NN, parts=3):
    out = None
    for piece in reversed(_split_bf16(b, parts)):
        term = _dot(a_bf16, piece, dims)
        out = term if out is None else term + out
    return out


def _mm_exact_rhs(a, b_bf16, dims=NN):
    a1, a2, a3 = _split_bf16(a, 3)
    return _dot(a1, b_bf16, dims) + (_dot(a2, b_bf16, dims) + _dot(a3, b_bf16, dims))


def _inproj_kernel(x_ref, w_ref, mu_ref, prkv_ref, plora_ref, pk_ref, qvt_ref, carry_ref, *, tiles_per_seq):
    tm = x_ref.shape[0]
    xb = x_ref[...].astype(BF16)
    seq_start = (pl.program_id(0) % tiles_per_seq) == 0
    row0 = lax.broadcasted_iota(jnp.int32, (tm, INPROJ_TN), 0) == 0
    n_shift_tiles = RWKV_COLS_PAD // INPROJ_TN
    for j in range(n_shift_tiles):
        c0 = j * INPROJ_TN
        acc = _dot(xb, w_ref[:, c0:c0 + INPROJ_TN])
        prev_last = jnp.where(seq_start, 0.0, carry_ref[0:1, c0:c0 + INPROJ_TN])
        shifted = jnp.where(row0, prev_last, pltpu.roll(acc, 1, 0))
        carry_ref[0:1, c0:c0 + INPROJ_TN] = acc[tm - 1:tm, :]
        out = acc + (shifted - acc) * mu_ref[:, c0:c0 + INPROJ_TN]
        if c0 < 3 * RWKV_WIDTH:
            prkv_ref[:, c0:c0 + INPROJ_TN] = out
        else:
            plora_ref[:, c0 - 3 * RWKV_WIDTH:c0 - 3 * RWKV_WIDTH + INPROJ_TN] = out
    tiles_per_part = MOBA_WIDTH // INPROJ_TN
    for j in range(3 * tiles_per_part):
        c0 = j * INPROJ_TN
        acc = _dot(xb, w_ref[:, RWKV_COLS_PAD + c0:RWKV_COLS_PAD + c0 + INPROJ_TN])
        part, r0 = divmod(c0, MOBA_WIDTH)
        if part == 1:
            pk_ref[:, r0:r0 + INPROJ_TN] = acc.astype(BF16)
        else:
            r0 += (part // 2) * MOBA_WIDTH
            qvt_ref[r0:r0 + INPROJ_TN, :] = acc.T.astype(BF16)


def _inproj_call(x2, w_cat, mu_cat, seq_len):
    n = x2.shape[0]
    tm = INPROJ_TM
    assert seq_len % tm == 0 and (3 * RWKV_WIDTH) % INPROJ_TN == 0
    return pl.pallas_call(
        functools.partial(_inproj_kernel, tiles_per_seq=seq_len // tm),
        grid=(n // tm,),
        in_specs=[
            pl.BlockSpec((tm, D_MODEL), lambda i: (i, 0)),
            pl.BlockSpec((D_MODEL, IN_COLS_PAD), lambda i: (0, 0)),
            pl.BlockSpec((1, RWKV_COLS_PAD), lambda i: (0, 0)),
        ],
        out_specs=[
            pl.BlockSpec((tm, 3 * RWKV_WIDTH), lambda i: (i, 0)),
            pl.BlockSpec((tm, LORA_PAD), lambda i: (i, 0)),
            pl.BlockSpec((tm, MOBA_WIDTH), lambda i: (i, 0)),
            pl.BlockSpec((2 * MOBA_WIDTH, tm), lambda i: (0, i)),
        ],
        out_shape=[
            jax.ShapeDtypeStruct((n, 3 * RWKV_WIDTH), F32),
            jax.ShapeDtypeStruct((n, LORA_PAD), F32),
            jax.ShapeDtypeStruct((n, MOBA_WIDTH), BF16),
            jax.ShapeDtypeStruct((2 * MOBA_WIDTH, n), BF16),
        ],
        scratch_shapes=[pltpu.VMEM((SUBLANES, RWKV_COLS_PAD), F32)],
        compiler_params=pltpu.CompilerParams(
            dimension_semantics=("arbitrary",), vmem_limit_bytes=VMEM_LIMIT),
        name="inproj_shift",
    )(x2, w_cat, mu_cat)


def _sigmoid(z):
    return 1.0 / (1.0 + jnp.exp(-z))


def _rwkv_chunks(rt, kt, at, bt, v, d_incl, s_prev, passes, state_passes):
    c = RWKV_CHUNK
    n_chunks = rt.shape[0] // c
    n_pairs = len(s_prev)
    row = lax.broadcasted_iota(jnp.int32, (c, PAIR), 0)
    col = lax.broadcasted_iota(jnp.int32, (c, PAIR), 1) % HEAD_DIM
    strict = row > col
    incl = row >= col
    eye_c = (row == col).astype(F32)
    lane = lax.broadcasted_iota(jnp.int32, (1, PAIR), 1)
    head0 = lane < HEAD_DIM
    head1 = jnp.logical_not(head0)
    prow = lax.broadcasted_iota(jnp.int32, (PAIR, PAIR), 0)
    pcol = lax.broadcasted_iota(jnp.int32, (PAIR, PAIR), 1)
    same_head = (prow < HEAD_DIM) == (pcol < HEAD_DIM)
    eye_p = (prow == pcol).astype(F32)
    rows = [slice(ci * c, (ci + 1) * c) for ci in range(n_chunks)]
    sl = [slice(p * PAIR, (p + 1) * PAIR) for p in range(n_pairs)]
    pairs = [(ci, p) for ci in range(n_chunks) for p in range(n_pairs)]
    cut = lambda t, ci, p: t[rows[ci], sl[p]]

    rhs_dtype = BF16 if passes == 1 else F32

    def by_head(m, dtype=rhs_dtype):
        m = m.astype(dtype)
        zero = jnp.zeros_like(m)
        return jnp.concatenate([jnp.where(head0, m, zero), jnp.where(head1, m, zero)], axis=0)

    def by_head2(m, n):
        return jnp.concatenate([by_head(m), by_head(n)], axis=1)

    at_p = {k_: cut(at, *k_) for k_ in pairs}
    rt_p = {k_: cut(rt, *k_) for k_ in pairs}
    bt_p = {k_: cut(bt, *k_) for k_ in pairs}
    kt_p = {k_: cut(kt, *k_) for k_ in pairs}
    v_p = {k_: cut(v, *k_) for k_ in pairs}

    z = {k_: _mm(jnp.concatenate([at_p[k_], rt_p[k_]], axis=0),
                 jnp.concatenate([by_head(bt_p[k_], F32).T, by_head(kt_p[k_], F32).T], axis=1), NN, passes)
         for k_ in pairs}
    l_ab = {k_: jnp.where(strict, z[k_][:c, :PAIR], 0.0) for k_ in pairs}
    l_ak = {k_: jnp.where(strict, z[k_][:c, PAIR:], 0.0) for k_ in pairs}
    m_rb = {k_: jnp.where(incl, z[k_][c:, :PAIR], 0.0) for k_ in pairs}
    m_rk = {k_: jnp.where(incl, z[k_][c:, PAIR:], 0.0) for k_ in pairs}
    pw = {k_: _mm(l_ab[k_], by_head(l_ab[k_]), NN, passes) for k_ in pairs}
    t_inv = {k_: eye_c + l_ab[k_] for k_ in pairs}
    for _ in range(int(math.log2(c)) - 2):
        tp = {k_: _mm(jnp.concatenate([t_inv[k_], pw[k_]], axis=0), by_head(pw[k_]), NN, passes)
              for k_ in pairs}
        t_inv = {k_: t_inv[k_] + tp[k_][:c] for k_ in pairs}
        pw = {k_: tp[k_][c:] for k_ in pairs}
    t_inv = {k_: t_inv[k_] + _mm(t_inv[k_], by_head(pw[k_]), NN, passes) for k_ in pairs}
    lm = {k_: _mm(jnp.concatenate([l_ak[k_], m_rk[k_]], axis=0), by_head(v_p[k_]), NN, passes)
          for k_ in pairs}
    lv = {k_: lm[k_][:c] for k_ in pairs}
    mv = {k_: lm[k_][c:] for k_ in pairs}
    wu = {k_: _mm(t_inv[k_], by_head2(at_p[k_], lv[k_]), NN, passes) for k_ in pairs}
    qy = {k_: _mm(m_rb[k_], by_head2(wu[k_][:, :PAIR], wu[k_][:, PAIR:]), NN, passes) for k_ in pairs}

    qeff, y1, phi, psi = {}, {}, {}, {}
    for ci, p in pairs:
        k_ = (ci, p)
        w, u0 = wu[k_][:, :PAIR], wu[k_][:, PAIR:]
        qeff[k_] = rt_p[k_] + qy[k_][:, :PAIR]
        y1[k_] = qy[k_][:, PAIR:] + mv[k_]
        d_p = d_incl[(ci + 1) * c - 1:(ci + 1) * c, sl[p]]
        phi[k_] = jnp.where(same_head, (eye_p + _mm(w.T, bt_p[k_], NN, passes)) * d_p, 0.0)
        uv_t = jnp.concatenate([u0, v_p[k_]], axis=0).T
        bk = jnp.concatenate([bt_p[k_], kt_p[k_]], axis=0)
        psi[k_] = jnp.where(same_head, _mm(uv_t, bk, NN, passes) * d_p, 0.0)

    state = list(s_prev)
    ys = [[None] * n_pairs for _ in range(n_chunks)]
    for ci in range(n_chunks):
        for p in range(n_pairs):
            ys[ci][p] = _mm(qeff[ci, p], state[p].T, NN, state_passes) + y1[ci, p]
            state[p] = _mm(state[p], phi[ci, p], NN, state_passes) + psi[ci, p]
    y = jnp.concatenate([jnp.concatenate(ys[ci], axis=1) for ci in range(n_chunks)], axis=0)
    return y, state


def _rwkv_kernel(prkv_ref, plora_ref, vec_ref, wl_ref, al_ref, gl_ref, bd_ref, y_ref, s_ref):
    rows = prkv_ref.shape[0]
    c = RWKV_CHUNK
    width = RWKV_WIDTH

    @pl.when(pl.program_id(1) == 0)
    def _():
        s_ref[...] = jnp.zeros_like(s_ref)

    r = prkv_ref[:, 0:width]
    k_raw = prkv_ref[:, width:2 * width]
    v = prkv_ref[:, 2 * width:3 * width]
    p_wd = p_ad = p_gd = plora_ref[...]
    w0 = vec_ref[0:1, :]
    a0 = vec_ref[1:2, :]
    k_k = vec_ref[2:3, :]
    k_a = vec_ref[3:4, :]
    r_k = vec_ref[4:5, :]
    gn_w = vec_ref[5:6, :]
    gn_b = vec_ref[6:7, :]
    bd = bd_ref[...]

    def seg_sum(z):
        halves = []
        for c0 in range(0, width, bd.shape[0]):
            halves.append(_dot(z[:, c0:c0 + bd.shape[0]].astype(BF16), bd))
        return jnp.concatenate(halves, axis=1)

    log_w = -math.exp(-0.5) * _sigmoid(w0 + _mm(jnp.tanh(p_wd), wl_ref[...], NN, 3))
    a = _sigmoid(a0 + _mm(p_ad, al_ref[...], NN, RWKV_PASSES))
    g = _mm(_sigmoid(p_gd), gl_ref[...], NN, RWKV_PASSES)
    kk = k_raw * k_k
    kk = kk * lax.rsqrt(jnp.maximum(seg_sum(kk * kk), L2_EPS * L2_EPS))
    k = k_raw * (1.0 + (a - 1.0) * k_a)

    row = lax.broadcasted_iota(jnp.int32, (c, c), 0)
    col = lax.broadcasted_iota(jnp.int32, (c, c), 1)
    tri = jnp.where(row >= col, 1.0, 0.0).astype(BF16)
    cum = jnp.concatenate([_mm_exact_lhs(tri, log_w[c0:c0 + c], parts=2)
                           for c0 in range(0, rows, c)], axis=0)
    d_incl = jnp.exp(cum)
    d_inv = jnp.exp(-cum)
    d_excl = jnp.exp(cum - log_w)
    rt = r * d_incl
    kt = k * d_inv
    at = -kk * d_excl
    bt = kk * a * d_inv

    y, s_next = _rwkv_chunks(rt, kt, at, bt, v, d_incl, [s_ref[p] for p in range(N_PAIRS)],
                             RWKV_PASSES, RWKV_STATE_PASSES)
    for p in range(N_PAIRS):
        s_ref[p] = s_next[p]

    inv_n = 1.0 / HEAD_DIM
    mu = seg_sum(y) * inv_n
    yc = y - mu
    var = seg_sum(yc * yc) * inv_n
    yn = yc * lax.rsqrt(var + GN_EPS) * gn_w + gn_b
    bonus = seg_sum(r * k * r_k) * v
    y_ref[...] = ((yn + bonus) * g).astype(y_ref.dtype)


def _rwkv_call(p_rkv, p_lora, vecs, wl, al, gl, bd, batch, seq_len):
    n = p_rkv.shape[0]
    rows = RWKV_CHUNK * RWKV_CHUNKS_PER_STEP
    assert seq_len % rows == 0
    steps = seq_len // rows
    row_map = lambda b, i: (b * steps + i, 0)
    const = lambda b, i: (0, 0)
    return pl.pallas_call(
        _rwkv_kernel,
        grid=(batch, steps),
        in_specs=[
            pl.BlockSpec((rows, 3 * RWKV_WIDTH), row_map),
            pl.BlockSpec((rows, LORA_PAD), row_map),
            pl.BlockSpec((SUBLANES, RWKV_WIDTH), const),
            pl.BlockSpec((LORA_PAD, RWKV_WIDTH), const),
            pl.BlockSpec((LORA_PAD, RWKV_WIDTH), const),
            pl.BlockSpec((LORA_PAD, RWKV_WIDTH), const),
            pl.BlockSpec((2 * PAIR, 2 * PAIR), const),
        ],
        out_specs=pl.BlockSpec((rows, RWKV_WIDTH), row_map),
        out_shape=jax.ShapeDtypeStruct((n, RWKV_WIDTH), BF16),
        scratch_shapes=[pltpu.VMEM((N_PAIRS, PAIR, PAIR), F32)],
        compiler_params=pltpu.CompilerParams(
            dimension_semantics=("arbitrary", "arbitrary"), vmem_limit_bytes=VMEM_LIMIT),
        name="rwkv7_chunked",
    )(p_rkv, p_lora, vecs, wl, al, gl, bd)


def _moba_kernel(qt_ref, k_ref, vt_in_ref, kaug_ref, o_ref, kmean_ref, vt_ref,
                 m0_ref, m1_ref, acc0_ref, acc1_ref, s_even_ref, s_odd_ref,
                 smax_even_ref, smax_odd_ref, *, n_blocks):
    blk = MOBA_BLOCK
    tk = MOBA_KV_TILE
    tq = qt_ref.shape[1]
    i = pl.program_id(2)
    nb_pad = kmean_ref.shape[0]
    m_refs, acc_refs = (m0_ref, m1_ref), (acc0_ref, acc1_ref)

    @pl.when(i == 0)
    def _():
        kmean_ref[...] = jnp.zeros_like(kmean_ref)

        def mean_body(n, carry):
            off = pl.multiple_of(n * blk, blk)
            kb = k_ref[pl.ds(off, blk), :].astype(F32)
            kmean_ref[pl.ds(n, 1), :] = jnp.sum(kb, axis=0, keepdims=True) * (1.0 / blk)
            return carry
        lax.fori_loop(0, n_blocks, mean_body, 0)

        ones = jnp.ones((MOBA_V_ROWS - HEAD_DIM, tk), BF16)

        for j in range(vt_ref.shape[0]):
            for h in (0, 1):
                vt_ref[j, h] = jnp.concatenate(
                    [vt_in_ref[h * HEAD_DIM:(h + 1) * HEAD_DIM, j * tk:(j + 1) * tk], ones], axis=0)

    q_t = qt_ref[...].astype(F32)
    chan = lax.broadcasted_iota(jnp.int32, (PAIR, tq), 0)
    blk_row = lax.broadcasted_iota(jnp.int32, (nb_pad, tq), 0)
    own_blk = (i * tq + lax.broadcasted_iota(jnp.int32, (nb_pad, tq), 1)) // blk
    past = blk_row < own_blk
    aug_row = lax.broadcasted_iota(jnp.int32, (LANES, tq), 0)
    ones_rows = (aug_row >= n_blocks) & (aug_row < n_blocks + MOBA_ALIBI_PARTS)
    kmean = kmean_ref[...]

    qa_t = []
    for h in (0, 1):
        qh_t = jnp.where((chan < HEAD_DIM) == (h == 0), q_t, 0.0)
        gate = _mm_exact_rhs(kmean, qh_t.astype(BF16))
        gate = jnp.where(past, gate, F32_LOWEST)
        sel = jnp.zeros(gate.shape, jnp.bool_)
        for _ in range(MOBA_TOPK):
            mx = jnp.max(gate, axis=0, keepdims=True)
            first = jnp.min(jnp.where(gate == mx, blk_row, nb_pad), axis=0, keepdims=True)
            pick = (blk_row == first) & (mx > F32_LOWEST)
            sel = sel | pick
            gate = jnp.where(pick, F32_LOWEST, gate)
        sel_bias = jnp.where(past & jnp.logical_not(sel), NEG_INF, 0.0)
        aug_t = jnp.concatenate([sel_bias, jnp.zeros((LANES - nb_pad, tq), F32)], axis=0)
        aug_t = jnp.where(ones_rows, 1.0, aug_t)
        qa_t.append(jnp.concatenate([qh_t * (LOG2_E / math.sqrt(HEAD_DIM)), aug_t], axis=0).astype(BF16))

    def tile_scores(j, lanes=slice(None)):
        off = pl.multiple_of(j * tk, tk)
        k_t = k_ref[pl.ds(off, tk), :]
        return [_dot(jnp.concatenate([k_t, kaug_ref[h, pl.ds(off, tk), :]], axis=1), qa_t[h][:, lanes])
                for h in (0, 1)]

    def put_scores(buf, s, lanes=slice(None)):
        for h in (0, 1):
            buf[0][h, :, lanes] = s[h]
            buf[1][h, :, lanes] = jnp.max(s[h], axis=0, keepdims=True)

    def tile_update(j, buf, lanes=slice(None)):
        s_buf, smax_buf = buf
        for h in (0, 1):
            m_old = m_refs[h][:, lanes]
            m_new = jnp.maximum(m_old, smax_buf[h, :, lanes])
            p = jnp.exp2(s_buf[h, :, lanes] - m_new).astype(BF16)
            pv = _dot(vt_ref[j, h], p)
            acc_refs[h][:, lanes] = jnp.exp2(m_old - m_new) * acc_refs[h][:, lanes] + pv
            m_refs[h][:, lanes] = m_new

    for h in (0, 1):
        m_refs[h][...] = jnp.full(m_refs[h].shape, F32_LOWEST, F32)
        acc_refs[h][...] = jnp.zeros(acc_refs[h].shape, F32)

    buffers = ((s_even_ref, smax_even_ref), (s_odd_ref, smax_odd_ref))
    n_own = tq // tk
    j_first = i * n_own
    diagonal = (lax.broadcasted_iota(jnp.int32, (tk, tk), 0) <= lax.broadcasted_iota(jnp.int32, (tk, tk), 1))

    def own_scores(g):
        s = tile_scores(j_first + g, slice(g * tk, tq))
        own = [jnp.where(diagonal, s_h[:, :tk], NEG_INF) for s_h in s]
        if g == n_own - 1:
            return own
        return [jnp.concatenate([own_h, s_h[:, tk:]], axis=1) for own_h, s_h in zip(own, s)]

    put_scores(buffers[0], own_scores(n_own - 1), slice((n_own - 1) * tk, tq))
    for m in range(1, n_own):
        g = n_own - 1 - m
        put_scores(buffers[m % 2], own_scores(g), slice(g * tk, tq))
        tile_update(j_first + g + 1, buffers[(m - 1) % 2], slice((g + 1) * tk, tq))
    cur, nxt = buffers[(n_own - 1) % 2], buffers[n_own % 2]

    def previous_tile(j):
        return jnp.where(j == 0, j_first, j - 1)

    def pipelined_step(j, src, dst):
        put_scores(dst, tile_scores(j))
        tile_update(previous_tile(j), src)

    def kv_pair_step(u, carry):
        pipelined_step(2 * u, cur, nxt)
        pipelined_step(2 * u + 1, nxt, cur)
        return carry
    lax.fori_loop(0, j_first // 2, kv_pair_step, 0)

    @pl.when(j_first % 2 == 1)
    def _():
        pipelined_step(j_first - 1, cur, nxt)
        tile_update(j_first - 1, nxt)

    @pl.when(j_first % 2 == 0)
    def _():
        tile_update(previous_tile(j_first), cur)

    out_t = jnp.concatenate([acc_refs[h][0:HEAD_DIM, :] / acc_refs[h][HEAD_DIM:HEAD_DIM + 1, :]
                             for h in (0, 1)], axis=0)
    o_ref[...] = out_t.T.astype(o_ref.dtype)


def _moba_call(pk, qvt, kaug, batch, seq_len):
    n = pk.shape[0]
    blk = MOBA_BLOCK
    tq = MOBA_Q_TILE
    nb = seq_len // blk
    nq = seq_len // tq
    assert nb + MOBA_ALIBI_PARTS <= LANES and seq_len % MOBA_KV_TILE == 0
    assert tq % MOBA_KV_TILE == 0 and seq_len % tq == 0
    lane_groups = MOBA_WIDTH // LANES
    return pl.pallas_call(
        functools.partial(_moba_kernel, n_blocks=nb),
        grid=(batch, N_PAIRS, nq),
        in_specs=[
            pl.BlockSpec((PAIR, tq), lambda b, p, i: (p, b * nq + i)),
            pl.BlockSpec((seq_len, PAIR), lambda b, p, i: (b, p)),
            pl.BlockSpec((PAIR, seq_len), lambda b, p, i: (lane_groups + p, b)),
            pl.BlockSpec((2, seq_len, LANES), lambda b, p, i: (p, 0, 0)),
        ],
        out_specs=pl.BlockSpec((tq, PAIR), lambda b, p, i: (b * nq + i, p)),
        out_shape=jax.ShapeDtypeStruct((n, MOBA_WIDTH), BF16),
        scratch_shapes=[
            pltpu.VMEM((-(-nb // 8) * 8, PAIR), F32),
            pltpu.VMEM((seq_len // MOBA_KV_TILE, 2, MOBA_V_ROWS, MOBA_KV_TILE), BF16),
            pltpu.VMEM((1, tq), F32), pltpu.VMEM((1, tq), F32),
            pltpu.VMEM((MOBA_V_ROWS, tq), F32), pltpu.VMEM((MOBA_V_ROWS, tq), F32),
            pltpu.VMEM((2, MOBA_KV_TILE, tq), F32), pltpu.VMEM((2, MOBA_KV_TILE, tq), F32),
            pltpu.VMEM((2, 1, tq), F32), pltpu.VMEM((2, 1, tq), F32),
        ],
        compiler_params=pltpu.CompilerParams(
            dimension_semantics=("arbitrary", "arbitrary", "arbitrary"),
            vmem_limit_bytes=VMEM_LIMIT),
        name="moba_attention",
    )(qvt, pk, qvt, kaug)


def _moba_key_aug(seq_len):
    nb = seq_len // MOBA_BLOCK
    heads = MOBA_WIDTH // HEAD_DIM
    pos = np.arange(seq_len, dtype=np.int32)
    slopes = (2.0 ** (-8.0 * (np.arange(heads, dtype=np.float32) + 1.0) / heads)).astype(np.float32)
    aug = np.zeros((heads, seq_len, LANES), np.float32)
    aug[:, pos, pos // MOBA_BLOCK] = 1.0
    rem = (np.float32(LOG2_E) * slopes)[:, None] * pos.astype(np.float32)[None, :]
    for part in range(MOBA_ALIBI_PARTS):
        piece = (rem.view(np.uint32) & np.uint32(0xFFFF0000)).view(np.float32)
        aug[:, :, nb + part] = piece
        rem = rem - piece
    return jnp.asarray(aug.astype(BF16))


def _layer_norm(z, g, b):
    mu = jnp.mean(z, axis=-1, keepdims=True)
    zc = z - mu
    var = jnp.mean(zc * zc, axis=-1, keepdims=True)
    return zc * lax.rsqrt(var + LN_EPS) * g + b


def _outproj_kernel(ya_ref, yb_ref, x_ref, wa_ref, wb_ref, g_ref, b_ref, wr_ref, br_ref,
                    h_ref, lg_ref):
    mix = _dot(ya_ref[...], wa_ref[...]) + _dot(yb_ref[...], wb_ref[...])
    h = _layer_norm(DEEPNORM_ALPHA * x_ref[...] + mix, g_ref[...], b_ref[...])
    h_ref[...] = h
    h_hi, h_lo = _split_bf16(h, 2)
    w_hi, w_lo = _split_bf16(wr_ref[...], 2)
    hh_hl = _dot(h_hi, jnp.concatenate([w_hi, w_lo], axis=1))
    lg_ref[...] = hh_hl[:, :ROUTER_PAD] + (hh_hl[:, ROUTER_PAD:] + _dot(h_lo, w_hi)) + br_ref[...]


def _outproj_call(y_a, y_b, x2, wa, wb, ln_g, ln_b, w_router, b_router):
    n = x2.shape[0]
    tm = OUTPROJ_TM
    row = lambda i: (i, 0)
    const = lambda i: (0, 0)
    return pl.pallas_call(
        _outproj_kernel,
        grid=(n // tm,),
        in_specs=[
            pl.BlockSpec((tm, RWKV_WIDTH), row),
            pl.BlockSpec((tm, MOBA_WIDTH), row),
            pl.BlockSpec((tm, D_MODEL), row),
            pl.BlockSpec((RWKV_WIDTH, D_MODEL), const),
            pl.BlockSpec((MOBA_WIDTH, D_MODEL), const),
            pl.BlockSpec((1, D_MODEL), const),
            pl.BlockSpec((1, D_MODEL), const),
            pl.BlockSpec((D_MODEL, ROUTER_PAD), const),
            pl.BlockSpec((1, ROUTER_PAD), const),
        ],
        out_specs=[
            pl.BlockSpec((tm, D_MODEL), row),
            pl.BlockSpec((tm, ROUTER_PAD), row),
        ],
        out_shape=[
            jax.ShapeDtypeStruct((n, D_MODEL), F32),
            jax.ShapeDtypeStruct((n, ROUTER_PAD), F32),
        ],
        compiler_params=pltpu.CompilerParams(
            dimension_semantics=("arbitrary",), vmem_limit_bytes=VMEM_LIMIT),
        name="outproj_ln_router",
    )(y_a, y_b, x2, wa, wb, ln_g, ln_b, w_router, b_router)


def _route(logits_t):
    row = lax.broadcasted_iota(jnp.int32, logits_t.shape, 0)
    n_rows = logits_t.shape[0]
    is_group = (row >= GROUP_LANE0) & (row < GROUP_LANE0 + N_GROUPS)
    gl = jnp.where(is_group, logits_t, F32_LOWEST)
    g_max = jnp.max(gl, axis=0, keepdims=True)
    g_first = jnp.min(jnp.where(gl == g_max, row, n_rows), axis=0, keepdims=True)
    g_exp = jnp.where(is_group, jnp.exp(gl - g_max), 0.0)
    p_g = 1.0 / jnp.sum(g_exp, axis=0, keepdims=True)
    g_idx = g_first - GROUP_LANE0
    in_group = (row >= g_idx * EXPERTS_PER_GROUP) & (row < (g_idx + 1) * EXPERTS_PER_GROUP)
    el = jnp.where(in_group, logits_t, F32_LOWEST)
    e_max = jnp.max(el, axis=0, keepdims=True)
    e_exp = jnp.where(in_group, jnp.exp(el - e_max), 0.0)
    e_prob = e_exp / jnp.sum(e_exp, axis=0, keepdims=True)
    cand = jnp.where(in_group, e_prob, -1.0)
    v1 = jnp.max(cand, axis=0, keepdims=True)
    i1 = jnp.min(jnp.where(cand == v1, row, n_rows), axis=0, keepdims=True)
    pick1 = row == i1
    cand2 = jnp.where(pick1, -1.0, cand)
    v2 = jnp.max(cand2, axis=0, keepdims=True)
    i2 = jnp.min(jnp.where(cand2 == v2, row, n_rows), axis=0, keepdims=True)
    pick2 = row == i2
    denom = v1 + v2
    gates = jnp.where(pick1, v1 / denom * p_g, jnp.where(pick2, v2 / denom * p_g, 0.0))
    return gates, g_idx


def _route_sort_kernel(lg_ref, gates_ref, pos_ref, flags_ref, later_ref):
    tm = lg_ref.shape[0]

    @pl.when(pl.program_id(0) == 0)
    def _():
        row = lax.broadcasted_iota(jnp.int32, (tm, tm), 0)
        col = lax.broadcasted_iota(jnp.int32, (tm, tm), 1)
        later_ref[...] = jnp.where(row < col, 1.0, 0.0).astype(BF16)

    gates_t, g_idx = _route(lg_ref[...].T[:ROUTER_ROWS])
    own = lax.broadcasted_iota(jnp.int32, (SUBLANES, tm), 0) == g_idx
    onehot = jnp.where(own, 1.0, 0.0)
    rank = _dot(onehot.astype(BF16), later_ref[...])
    count = jnp.broadcast_to(jnp.sum(onehot, axis=1, keepdims=True), (SUBLANES, LANES))
    starts, running = [], jnp.zeros((1, LANES), F32)
    for g in range(SUBLANES):
        starts.append(running)
        running = running + count[g:g + 1]
    start = jnp.concatenate(starts, axis=0)
    pos = jnp.sum(jnp.where(own, rank + start[:, 0:1], 0.0), axis=0, keepdims=True)
    gates_ref[...] = jnp.concatenate([gates_t, jnp.zeros((LANES - ROUTER_ROWS, tm), F32)], axis=0).T
    pos_ref[...] = jnp.broadcast_to(pos, (LANES, tm)).T
    lane = lax.broadcasted_iota(jnp.int32, (SUBLANES, LANES), 1)
    sub_lo = lane * MOE_SUB
    hit = (count > 0.0) & (start < (sub_lo + MOE_SUB).astype(F32)) & (start + count > sub_lo.astype(F32))
    start_i = start.astype(jnp.int32)
    count_i = count.astype(jnp.int32)
    win = jnp.minimum((start_i // MOE_WINDOW_ALIGN) * MOE_WINDOW_ALIGN, tm - MOE_WINDOW)
    fits = (count_i > 0) & (start_i + count_i <= win + MOE_WINDOW)
    meta = jnp.where(lane == MOE_META_FITS, jnp.where(fits, 1, 0),
                     jnp.where(lane == MOE_META_WINDOW, win, jnp.where(hit, 1, 0)))
    flags_ref[0] = meta.astype(jnp.int32)


def _route_sort_call(logits):
    n = logits.shape[0]
    tm = MOE_TM
    row = lambda t: (t, 0)
    return pl.pallas_call(
        _route_sort_kernel,
        grid=(n // tm,),
        in_specs=[pl.BlockSpec((tm, ROUTER_PAD), row)],
        out_specs=[
            pl.BlockSpec((tm, ROUTER_PAD), row),
            pl.BlockSpec((tm, LANES), row),
            pl.BlockSpec((1, SUBLANES, LANES), lambda t: (t, 0, 0)),
        ],
        out_shape=[
            jax.ShapeDtypeStruct((n, ROUTER_PAD), F32),
            jax.ShapeDtypeStruct((n, LANES), F32),
            jax.ShapeDtypeStruct((n // tm, SUBLANES, LANES), jnp.int32),
        ],
        scratch_shapes=[pltpu.VMEM((tm, tm), BF16)],
        compiler_params=pltpu.CompilerParams(
            dimension_semantics=("arbitrary",), vmem_limit_bytes=VMEM_LIMIT),
        name="route_sort",
    )(logits)


def _moe_kernel(flags_ref, h_ref, gates_ref, pos_ref, w1_ref, w3_ref, w2_ref, g_ref, b_ref,
                o_ref, xs_ref, gs_ref, acc_ref, pt_ref):
    tile = pl.program_id(0)
    step = pl.program_id(1)
    tm = h_ref.shape[0]
    sub = MOE_SUB
    n_sub = tm // sub
    eps = MOE_EXPERTS_PER_STEP

    @pl.when(step == 0)
    def _():
        pos_b = pos_ref[...]
        pos_row = pos_b.T[0:1, :]
        g_hi, g_lo = _split_bf16(gates_ref[...], 2)
        src = jnp.concatenate([h_ref[...].astype(BF16), g_hi, g_lo], axis=1)
        for c0 in range(0, tm, sub):
            slot = (lax.broadcasted_iota(jnp.int32, (sub, tm), 0) + c0).astype(F32)
            p_c = jnp.where(slot == pos_row, 1.0, 0.0).astype(BF16)
            moved = _dot(p_c, src)
            xs_ref[c0:c0 + sub, :] = moved[:, :D_MODEL].astype(BF16)
            gs_ref[c0:c0 + sub, :] = moved[:, D_MODEL:D_MODEL + LANES] + moved[:, D_MODEL + LANES:]
        for c0 in range(0, tm, LANES):
            slot = (lax.broadcasted_iota(jnp.int32, (tm, LANES), 1) + c0).astype(F32)
            pt_ref[:, c0:c0 + LANES] = jnp.where(pos_b == slot, 1.0, 0.0).astype(BF16)
        acc_ref[...] = jnp.zeros_like(acc_ref)

    group = step // (EXPERTS_PER_GROUP // eps)

    def visit(rows):
        x_r = xs_ref[rows, :]
        g_r = gs_ref[rows, :]
        lane = lax.broadcasted_iota(jnp.int32, g_r.shape, 1)
        acc = acc_ref[rows, :]
        for e in range(eps):
            gate_e = jnp.sum(jnp.where(lane == step * eps + e, g_r, 0.0), axis=-1, keepdims=True)
            a1 = _dot(x_r, w1_ref[e])
            a3 = _dot(x_r, w3_ref[e])
            hid = (a1 * _sigmoid(a1)) * a3 * gate_e
            acc = acc + _dot(hid.astype(BF16), w2_ref[e])
        acc_ref[rows, :] = acc

    base = (tile * N_GROUPS + group) * (n_sub + 2)
    fits = flags_ref[base + n_sub] != 0

    @pl.when(fits)
    def _():
        first = pl.multiple_of(flags_ref[base + n_sub + 1], MOE_WINDOW_ALIGN)
        visit(pl.ds(first, MOE_WINDOW))

    for r in range(n_sub):
        pl.when(jnp.logical_not(fits) & (flags_ref[base + r] != 0))(
            functools.partial(visit, slice(r * sub, (r + 1) * sub)))

    @pl.when(step == N_EXPERTS // eps - 1)
    def _():
        ffn = _dot(pt_ref[...], acc_ref[...].astype(BF16))
        o_ref[...] = _layer_norm(DEEPNORM_ALPHA * h_ref[...] + ffn, g_ref[...], b_ref[...])


def _moe_call(flags, h_f32, gates, pos, w1, w3, w2, ln_g, ln_b):
    n = h_f32.shape[0]
    tm = MOE_TM
    row = lambda t, s, f: (t, 0)
    const = lambda t, s, f: (0, 0)
    wmap = lambda t, s, f: (s, 0, 0)
    eps = MOE_EXPERTS_PER_STEP
    assert EXPERTS_PER_GROUP % eps == 0 and tm % MOE_SUB == 0
    grid_spec = pltpu.PrefetchScalarGridSpec(
        num_scalar_prefetch=1,
        grid=(n // tm, N_EXPERTS // eps),
        in_specs=[
            pl.BlockSpec((tm, D_MODEL), row),
            pl.BlockSpec((tm, ROUTER_PAD), row),
            pl.BlockSpec((tm, LANES), row),
            pl.BlockSpec((eps, D_MODEL, D_EXPERT), wmap),
            pl.BlockSpec((eps, D_MODEL, D_EXPERT), wmap),
            pl.BlockSpec((eps, D_EXPERT, D_MODEL), wmap),
            pl.BlockSpec((1, D_MODEL), const),
            pl.BlockSpec((1, D_MODEL), const),
        ],
        out_specs=pl.BlockSpec((tm, D_MODEL), row),
        scratch_shapes=[
            pltpu.VMEM((tm, D_MODEL), BF16),
            pltpu.VMEM((tm, ROUTER_PAD), F32),
            pltpu.VMEM((tm, D_MODEL), F32),
            pltpu.VMEM((tm, tm), BF16),
        ],
    )
    return pl.pallas_call(
        _moe_kernel,
        grid_spec=grid_spec,
        out_shape=jax.ShapeDtypeStruct((n, D_MODEL), F32),
        compiler_params=pltpu.CompilerParams(
            dimension_semantics=("arbitrary", "arbitrary"), vmem_limit_bytes=VMEM_LIMIT),
        name="hier_moe_ln",
    )(flags, h_f32, gates, pos, w1, w3, w2, ln_g, ln_b)


def _pad_cols(w, width):
    return jnp.pad(w, ((0, 0), (0, width - w.shape[1])))


def kernel(x, w_in, mu_shift, w0, w_lora_up, a0, a_lora_up, g_lora_up, k_k, k_a, r_k, gn_w, gn_b, w_out, ln1_g, ln1_b, w_group, b_group, w_expert, b_expert, w1_exp, w3_exp, w2_exp, ln2_g, ln2_b):
    batch, seq_len, d = x.shape
    assert d == D_MODEL
    n = batch * seq_len
    x2 = x.reshape(n, d)

    c_rkv = 3 * RWKV_WIDTH
    c_wd = c_rkv + DECAY_RANK
    c_ad = c_wd + AAA_RANK
    c_gd = c_ad + GATE_RANK
    w_cat = jnp.concatenate([
        w_in[:, :c_rkv], _pad_cols(w_in[:, c_rkv:c_gd], LORA_PAD), w_in[:, c_gd:],
    ], axis=1).astype(BF16)
    mu2 = mu_shift[None, :]
    mu_cat = jnp.concatenate([mu2[:, :c_rkv], _pad_cols(mu2[:, c_rkv:c_gd], LORA_PAD)], axis=1)
    p_rkv, p_lora, p_k, p_qvt = _inproj_call(x2, w_cat, mu_cat, seq_len)

    place = lambda w, first: jnp.pad(w, ((first, LORA_PAD - first - w.shape[0]), (0, 0)))
    lora_up = (place(w_lora_up, 0), place(a_lora_up, DECAY_RANK), place(g_lora_up, DECAY_RANK + AAA_RANK))

    vecs = jnp.stack([w0, a0, k_k, k_a, r_k.reshape(-1), gn_w, gn_b, jnp.zeros_like(w0)], axis=0)
    head_id = jnp.arange(2 * PAIR) // HEAD_DIM
    bd = (head_id[:, None] == head_id[None, :]).astype(BF16)
    y_a = _rwkv_call(p_rkv, p_lora, vecs, *lora_up, bd, batch, seq_len)

    y_b = _moba_call(p_k, p_qvt, _moba_key_aug(seq_len), batch, seq_len)

    w_out_b = w_out.astype(BF16)
    w_router = _pad_cols(jnp.concatenate([w_expert, w_group], axis=1), ROUTER_PAD)
    b_router = _pad_cols(jnp.concatenate([b_expert, b_group])[None, :], ROUTER_PAD)
    h1, logits = _outproj_call(y_a, y_b, x2, w_out_b[:RWKV_WIDTH], w_out_b[RWKV_WIDTH:],
                               ln1_g[None, :], ln1_b[None, :], w_router, b_router)

    flat = lambda w: w.astype(BF16).reshape((N_EXPERTS,) + w.shape[2:])
    gates, pos, flags = _route_sort_call(logits)
    flags = jnp.concatenate([flags[:, :N_GROUPS, :MOE_TM // MOE_SUB],
                             flags[:, :N_GROUPS, MOE_META_FITS:MOE_META_WINDOW + 1]], axis=-1).reshape(-1)
    out = _moe_call(flags, h1, gates, pos, flat(w1_exp), flat(w3_exp), flat(w2_exp),
                    ln2_g[None, :], ln2_b[None, :])
    return out.reshape(batch, seq_len, d)
```

```python
import functools
import math

import jax
import jax.numpy as jnp
import numpy as np
from jax import lax
from jax.experimental import pallas as pl
from jax.experimental.pallas import tpu as pltpu

F32 = jnp.float32
BF16 = jnp.bfloat16

D_MODEL = 1024
HEAD_DIM = 64
RWKV_WIDTH = 512
MOBA_WIDTH = 512
DECAY_RANK = 32
AAA_RANK = 32
GATE_RANK = 96
GN_EPS = 64e-5
L2_EPS = 1e-12
MOBA_BLOCK = 256
MOBA_TOPK = 3
N_GROUPS = 4
EXPERTS_PER_GROUP = 8
N_EXPERTS = N_GROUPS * EXPERTS_PER_GROUP
D_EXPERT = 256
LN_EPS = 1e-5
DEEPNORM_ALPHA = float(2.0 ** 0.25)
NEG_INF = -1e30
F32_LOWEST = -3.0e38

LANES = 128
SUBLANES = 8
BF16_SUBLANES = 16
MXU_TILE = 256
V7X_VMEM_BYTES = 64 * 1024 * 1024
PAIR = 2 * HEAD_DIM
N_PAIRS = RWKV_WIDTH // PAIR
LORA_PAD = MXU_TILE
RWKV_COLS_PAD = 3 * RWKV_WIDTH + LORA_PAD
IN_COLS_PAD = RWKV_COLS_PAD + 3 * MOBA_WIDTH
VMEM_LIMIT = V7X_VMEM_BYTES * 7 // 8

INPROJ_TM = 1024
INPROJ_TN = MXU_TILE
RWKV_CHUNK = 64
RWKV_CHUNKS_PER_STEP = 8
RWKV_PASSES = 1
RWKV_STATE_PASSES = 1
OUTPROJ_TM = 1024
MOE_TM = 1024
MOE_EXPERTS_PER_STEP = 8
MOE_SUB = 256
MOE_WINDOWS = (272, 320)
MOE_WINDOW_ALIGN = BF16_SUBLANES
MOE_META_FITS = 16
MOE_META_WINDOW = 17
MOBA_KV_TILE = 512
MOBA_Q_TILE = 2048
MOBA_V_ROWS = HEAD_DIM + BF16_SUBLANES
MOBA_ALIBI_PARTS = 3
LOG2_E = 1.4426950408889634
ROUTER_PAD = LANES
GROUP_LANE0 = N_EXPERTS
ROUTER_ROWS = 40

NN = (((1,), (0,)), ((), ()))


def _dot(a, b, dims=NN):
    return lax.dot_general(a, b, dims, preferred_element_type=F32)


def _split_bf16(x, parts):
    out = []
    rem = x
    for i in range(parts):
        p = rem.astype(BF16)
        out.append(p)
        if i + 1 < parts:
            rem = rem - p.astype(F32)
    return out


def _mm(a, b, dims=NN, passes=3):
    if passes == 1:
        return _dot(a.astype(BF16), b.astype(BF16), dims)
    assert passes == 3
    a_hi, a_lo = _split_bf16(a, 2)
    b_hi, b_lo = _split_bf16(b, 2)
    return _dot(a_hi, b_hi, dims) + (_dot(a_hi, b_lo, dims) + _dot(a_lo, b_hi, dims))


def _mm_exact_lhs(a_bf16, b, dims=NN, parts=3):
    out = None
    for piece in reversed(_split_bf16(b, parts)):
        term = _dot(a_bf16, piece, dims)
        out = term if out is None else term + out
    return out


def _mm_exact_rhs(a, b_bf16, dims=NN):
    a1, a2, a3 = _split_bf16(a, 3)
    return _dot(a1, b_bf16, dims) + (_dot(a2, b_bf16, dims) + _dot(a3, b_bf16, dims))


def _inproj_kernel(x_ref, w_ref, mu_ref, prkv_ref, plora_ref, pk_ref, qvt_ref, carry_ref, *, tiles_per_seq):
    tm = x_ref.shape[0]
    xb = x_ref[...].astype(BF16)
    seq_start = (pl.program_id(0) % tiles_per_seq) == 0
    row0 = lax.broadcasted_iota(jnp.int32, (tm, INPROJ_TN), 0) == 0
    n_shift_tiles = RWKV_COLS_PAD // INPROJ_TN
    for j in range(n_shift_tiles):
        c0 = j * INPROJ_TN
        acc = _dot(xb, w_ref[:, c0:c0 + INPROJ_TN])
        prev_last = jnp.where(seq_start, 0.0, carry_ref[0:1, c0:c0 + INPROJ_TN])
        shifted = jnp.where(row0, prev_last, pltpu.roll(acc, 1, 0))
        carry_ref[0:1, c0:c0 + INPROJ_TN] = acc[tm - 1:tm, :]
        out = acc + (shifted - acc) * mu_ref[:, c0:c0 + INPROJ_TN]
        if c0 < 3 * RWKV_WIDTH:
            prkv_ref[:, c0:c0 + INPROJ_TN] = out
        else:
            plora_ref[:, c0 - 3 * RWKV_WIDTH:c0 - 3 * RWKV_WIDTH + INPROJ_TN] = out
    tiles_per_part = MOBA_WIDTH // INPROJ_TN
    for j in range(3 * tiles_per_part):
        c0 = j * INPROJ_TN
        acc = _dot(xb, w_ref[:, RWKV_COLS_PAD + c0:RWKV_COLS_PAD + c0 + INPROJ_TN])
        part, r0 = divmod(c0, MOBA_WIDTH)
        if part == 1:
            pk_ref[:, r0:r0 + INPROJ_TN] = acc.astype(BF16)
        else:
            r0 += (part // 2) * MOBA_WIDTH
            qvt_ref[r0:r0 + INPROJ_TN, :] = acc.T.astype(BF16)


def _inproj_call(x2, w_cat, mu_cat, seq_len):
    n = x2.shape[0]
    tm = INPROJ_TM
    assert seq_len % tm == 0 and (3 * RWKV_WIDTH) % INPROJ_TN == 0
    return pl.pallas_call(
        functools.partial(_inproj_kernel, tiles_per_seq=seq_len // tm),
        grid=(n // tm,),
        in_specs=[
            pl.BlockSpec((tm, D_MODEL), lambda i: (i, 0)),
            pl.BlockSpec((D_MODEL, IN_COLS_PAD), lambda i: (0, 0)),
            pl.BlockSpec((1, RWKV_COLS_PAD), lambda i: (0, 0)),
        ],
        out_specs=[
            pl.BlockSpec((tm, 3 * RWKV_WIDTH), lambda i: (i, 0)),
            pl.BlockSpec((tm, LORA_PAD), lambda i: (i, 0)),
            pl.BlockSpec((tm, MOBA_WIDTH), lambda i: (i, 0)),
            pl.BlockSpec((2 * MOBA_WIDTH, tm), lambda i: (0, i)),
        ],
        out_shape=[
            jax.ShapeDtypeStruct((n, 3 * RWKV_WIDTH), F32),
            jax.ShapeDtypeStruct((n, LORA_PAD), F32),
            jax.ShapeDtypeStruct((n, MOBA_WIDTH), BF16),
            jax.ShapeDtypeStruct((2 * MOBA_WIDTH, n), BF16),
        ],
        scratch_shapes=[pltpu.VMEM((SUBLANES, RWKV_COLS_PAD), F32)],
        compiler_params=pltpu.CompilerParams(
            dimension_semantics=("arbitrary",), vmem_limit_bytes=VMEM_LIMIT),
        name="inproj_shift",
    )(x2, w_cat, mu_cat)


def _sigmoid(z):
    return 1.0 / (1.0 + jnp.exp(-z))


def _rwkv_chunks(rt, kt, at, bt, v, d_incl, s_prev, passes, state_passes):
    c = RWKV_CHUNK
    n_chunks = rt.shape[0] // c
    n_pairs = len(s_prev)
    row = lax.broadcasted_iota(jnp.int32, (c, PAIR), 0)
    col = lax.broadcasted_iota(jnp.int32, (c, PAIR), 1) % HEAD_DIM
    strict = row > col
    incl = row >= col
    eye_c = (row == col).astype(F32)
    lane = lax.broadcasted_iota(jnp.int32, (1, PAIR), 1)
    head0 = lane < HEAD_DIM
    head1 = jnp.logical_not(head0)
    prow = lax.broadcasted_iota(jnp.int32, (PAIR, PAIR), 0)
    pcol = lax.broadcasted_iota(jnp.int32, (PAIR, PAIR), 1)
    same_head = (prow < HEAD_DIM) == (pcol < HEAD_DIM)
    eye_p = (prow == pcol).astype(F32)
    rows = [slice(ci * c, (ci + 1) * c) for ci in range(n_chunks)]
    sl = [slice(p * PAIR, (p + 1) * PAIR) for p in range(n_pairs)]
    pairs = [(ci, p) for ci in range(n_chunks) for p in range(n_pairs)]
    cut = lambda t, ci, p: t[rows[ci], sl[p]]

    rhs_dtype = BF16 if passes == 1 else F32

    def by_head(m, dtype=rhs_dtype):
        m = m.astype(dtype)
        zero = jnp.zeros_like(m)
        return jnp.concatenate([jnp.where(head0, m, zero), jnp.where(head1, m, zero)], axis=0)

    def by_head2(m, n):
        return jnp.concatenate([by_head(m), by_head(n)], axis=1)

    at_p = {k_: cut(at, *k_) for k_ in pairs}
    rt_p = {k_: cut(rt, *k_) for k_ in pairs}
    bt_p = {k_: cut(bt, *k_) for k_ in pairs}
    kt_p = {k_: cut(kt, *k_) for k_ in pairs}
    v_p = {k_: cut(v, *k_) for k_ in pairs}

    z = {k_: _mm(jnp.concatenate([at_p[k_], rt_p[k_]], axis=0),
                 jnp.concatenate([by_head(bt_p[k_], F32).T, by_head(kt_p[k_], F32).T], axis=1), NN, passes)
         for k_ in pairs}
    l_ab = {k_: jnp.where(strict, z[k_][:c, :PAIR], 0.0) for k_ in pairs}
    l_ak = {k_: jnp.where(strict, z[k_][:c, PAIR:], 0.0) for k_ in pairs}
    m_rb = {k_: jnp.where(incl, z[k_][c:, :PAIR], 0.0) for k_ in pairs}
    m_rk = {k_: jnp.where(incl, z[k_][c:, PAIR:], 0.0) for k_ in pairs}
    pw = {k_: _mm(l_ab[k_], by_head(l_ab[k_]), NN, passes) for k_ in pairs}
    t_inv = {k_: eye_c + l_ab[k_] for k_ in pairs}
    for _ in range(int(math.log2(c)) - 2):
        tp = {k_: _mm(jnp.concatenate([t_inv[k_], pw[k_]], axis=0), by_head(pw[k_]), NN, passes)
              for k_ in pairs}
        t_inv = {k_: t_inv[k_] + tp[k_][:c] for k_ in pairs}
        pw = {k_: tp[k_][c:] for k_ in pairs}
    t_inv = {k_: t_inv[k_] + _mm(t_inv[k_], by_head(pw[k_]), NN, passes) for k_ in pairs}
    lm = {k_: _mm(jnp.concatenate([l_ak[k_], m_rk[k_]], axis=0), by_head(v_p[k_]), NN, passes)
          for k_ in pairs}
    lv = {k_: lm[k_][:c] for k_ in pairs}
    mv = {k_: lm[k_][c:] for k_ in pairs}
    wu = {k_: _mm(t_inv[k_], by_head2(at_p[k_], lv[k_]), NN, passes) for k_ in pairs}
    qy = {k_: _mm(m_rb[k_], by_head2(wu[k_][:, :PAIR], wu[k_][:, PAIR:]), NN, passes) for k_ in pairs}

    qeff, y1, phi, psi = {}, {}, {}, {}
    for ci, p in pairs:
        k_ = (ci, p)
        w, u0 = wu[k_][:, :PAIR], wu[k_][:, PAIR:]
        qeff[k_] = rt_p[k_] + qy[k_][:, :PAIR]
        y1[k_] = qy[k_][:, PAIR:] + mv[k_]
        d_p = d_incl[(ci + 1) * c - 1:(ci + 1) * c, sl[p]]
        phi[k_] = jnp.where(same_head, (eye_p + _mm(w.T, bt_p[k_], NN, passes)) * d_p, 0.0)
        uv_t = jnp.concatenate([u0, v_p[k_]], axis=0).T
        bk = jnp.concatenate([bt_p[k_], kt_p[k_]], axis=0)
        psi[k_] = jnp.where(same_head, _mm(uv_t, bk, NN, passes) * d_p, 0.0)

    state = list(s_prev)
    ys = [[None] * n_pairs for _ in range(n_chunks)]
    for ci in range(n_chunks):
        for p in range(n_pairs):
            ys[ci][p] = _mm(qeff[ci, p], state[p].T, NN, state_passes) + y1[ci, p]
            state[p] = _mm(state[p], phi[ci, p], NN, state_passes) + psi[ci, p]
    y = jnp.concatenate([jnp.concatenate(ys[ci], axis=1) for ci in range(n_chunks)], axis=0)
    return y, state


def _rwkv_kernel(prkv_ref, plora_ref, vec_ref, wl_ref, al_ref, gl_ref, bd_ref, y_ref, s_ref):
    rows = prkv_ref.shape[0]
    c = RWKV_CHUNK
    width = RWKV_WIDTH

    @pl.when(pl.program_id(1) == 0)
    def _():
        s_ref[...] = jnp.zeros_like(s_ref)

    r = prkv_ref[:, 0:width]
    k_raw = prkv_ref[:, width:2 * width]
    v = prkv_ref[:, 2 * width:3 * width]
    p_wd = p_ad = p_gd = plora_ref[...]
    w0 = vec_ref[0:1, :]
    a0 = vec_ref[1:2, :]
    k_k = vec_ref[2:3, :]
    k_a = vec_ref[3:4, :]
    r_k = vec_ref[4:5, :]
    gn_w = vec_ref[5:6, :]
    gn_b = vec_ref[6:7, :]
    bd = bd_ref[...]

    def seg_sum(z):
        halves = []
        for c0 in range(0, width, bd.shape[0]):
            halves.append(_dot(z[:, c0:c0 + bd.shape[0]].astype(BF16), bd))
        return jnp.concatenate(halves, axis=1)

    log_w = -math.exp(-0.5) * _sigmoid(w0 + _mm(jnp.tanh(p_wd), wl_ref[...], NN, 3))
    a = _sigmoid(a0 + _mm(p_ad, al_ref[...], NN, RWKV_PASSES))
    g = _mm(_sigmoid(p_gd), gl_ref[...], NN, RWKV_PASSES)
    kk = k_raw * k_k
    kk = kk * lax.rsqrt(jnp.maximum(seg_sum(kk * kk), L2_EPS * L2_EPS))
    k = k_raw * (1.0 + (a - 1.0) * k_a)

    row = lax.broadcasted_iota(jnp.int32, (c, c), 0)
    col = lax.broadcasted_iota(jnp.int32, (c, c), 1)
    tri = jnp.where(row >= col, 1.0, 0.0).astype(BF16)
    cum = jnp.concatenate([_mm_exact_lhs(tri, log_w[c0:c0 + c], parts=2)
                           for c0 in range(0, rows, c)], axis=0)
    d_incl = jnp.exp(cum)
    d_inv = jnp.exp(-cum)
    d_excl = jnp.exp(cum - log_w)
    rt = r * d_incl
    kt = k * d_inv
    at = -kk * d_excl
    bt = kk * a * d_inv

    y, s_next = _rwkv_chunks(rt, kt, at, bt, v, d_incl, [s_ref[p] for p in range(N_PAIRS)],
                             RWKV_PASSES, RWKV_STATE_PASSES)
    for p in range(N_PAIRS):
        s_ref[p] = s_next[p]

    inv_n = 1.0 / HEAD_DIM
    mu = seg_sum(y) * inv_n
    yc = y - mu
    var = seg_sum(yc * yc) * inv_n
    yn = yc * lax.rsqrt(var + GN_EPS) * gn_w + gn_b
    bonus = seg_sum(r * k * r_k) * v
    y_ref[...] = ((yn + bonus) * g).astype(y_ref.dtype)


def _rwkv_call(p_rkv, p_lora, vecs, wl, al, gl, bd, batch, seq_len):
    n = p_rkv.shape[0]
    rows = RWKV_CHUNK * RWKV_CHUNKS_PER_STEP
    assert seq_len % rows == 0
    steps = seq_len // rows
    row_map = lambda b, i: (b * steps + i, 0)
    const = lambda b, i: (0, 0)
    return pl.pallas_call(
        _rwkv_kernel,
        grid=(batch, steps),
        in_specs=[
            pl.BlockSpec((rows, 3 * RWKV_WIDTH), row_map),
            pl.BlockSpec((rows, LORA_PAD), row_map),
            pl.BlockSpec((SUBLANES, RWKV_WIDTH), const),
            pl.BlockSpec((LORA_PAD, RWKV_WIDTH), const),
            pl.BlockSpec((LORA_PAD, RWKV_WIDTH), const),
            pl.BlockSpec((LORA_PAD, RWKV_WIDTH), const),
            pl.BlockSpec((2 * PAIR, 2 * PAIR), const),
        ],
        out_specs=pl.BlockSpec((rows, RWKV_WIDTH), row_map),
        out_shape=jax.ShapeDtypeStruct((n, RWKV_WIDTH), BF16),
        scratch_shapes=[pltpu.VMEM((N_PAIRS, PAIR, PAIR), F32)],
        compiler_params=pltpu.CompilerParams(
            dimension_semantics=("arbitrary", "arbitrary"), vmem_limit_bytes=VMEM_LIMIT),
        name="rwkv7_chunked",
    )(p_rkv, p_lora, vecs, wl, al, gl, bd)


def _moba_kernel(qt_ref, k_ref, vt_in_ref, kaug_ref, o_ref, kmean_ref, vt_ref,
                 m0_ref, m1_ref, acc0_ref, acc1_ref, s_even_ref, s_odd_ref,
                 smax_even_ref, smax_odd_ref, *, n_blocks):
    blk = MOBA_BLOCK
    tk = MOBA_KV_TILE
    tq = qt_ref.shape[1]
    i = pl.program_id(2)
    nb_pad = kmean_ref.shape[0]
    m_refs, acc_refs = (m0_ref, m1_ref), (acc0_ref, acc1_ref)

    @pl.when(i == 0)
    def _():
        kmean_ref[...] = jnp.zeros_like(kmean_ref)

        def mean_body(n, carry):
            off = pl.multiple_of(n * blk, blk)
            kb = k_ref[pl.ds(off, blk), :].astype(F32)
            kmean_ref[pl.ds(n, 1), :] = jnp.sum(kb, axis=0, keepdims=True) * (1.0 / blk)
            return carry
        lax.fori_loop(0, n_blocks, mean_body, 0)

        ones = jnp.ones((MOBA_V_ROWS - HEAD_DIM, tk), BF16)

        for j in range(vt_ref.shape[0]):
            for h in (0, 1):
                vt_ref[j, h] = jnp.concatenate(
                    [vt_in_ref[h * HEAD_DIM:(h + 1) * HEAD_DIM, j * tk:(j + 1) * tk], ones], axis=0)

    q_t = qt_ref[...].astype(F32)
    chan = lax.broadcasted_iota(jnp.int32, (PAIR, tq), 0)
    blk_row = lax.broadcasted_iota(jnp.int32, (nb_pad, tq), 0)
    own_blk = (i * tq + lax.broadcasted_iota(jnp.int32, (nb_pad, tq), 1)) // blk
    past = blk_row < own_blk
    aug_row = lax.broadcasted_iota(jnp.int32, (LANES, tq), 0)
    ones_rows = (aug_row >= n_blocks) & (aug_row < n_blocks + MOBA_ALIBI_PARTS)
    kmean = kmean_ref[...]

    qa_t = []
    for h in (0, 1):
        qh_t = jnp.where((chan < HEAD_DIM) == (h == 0), q_t, 0.0)
        gate = _mm_exact_rhs(kmean, qh_t.astype(BF16))
        gate = jnp.where(past, gate, F32_LOWEST)
        sel = jnp.zeros(gate.shape, jnp.bool_)
        for _ in range(MOBA_TOPK):
            mx = jnp.max(gate, axis=0, keepdims=True)
            first = jnp.min(jnp.where(gate == mx, blk_row, nb_pad), axis=0, keepdims=True)
            pick = (blk_row == first) & (mx > F32_LOWEST)
            sel = sel | pick
            gate = jnp.where(pick, F32_LOWEST, gate)
        sel_bias = jnp.where(past & jnp.logical_not(sel), NEG_INF, 0.0)
        aug_t = jnp.concatenate([sel_bias, jnp.zeros((LANES - nb_pad, tq), F32)], axis=0)
        aug_t = jnp.where(ones_rows, 1.0, aug_t)
        qa_t.append(jnp.concatenate([qh_t * (LOG2_E / math.sqrt(HEAD_DIM)), aug_t], axis=0).astype(BF16))

    def tile_scores(j, lanes=slice(None)):
        off = pl.multiple_of(j * tk, tk)
        k_t = k_ref[pl.ds(off, tk), :]
        return [_dot(jnp.concatenate([k_t, kaug_ref[h, pl.ds(off, tk), :]], axis=1), qa_t[h][:, lanes])
                for h in (0, 1)]

    def put_scores(buf, s, lanes=slice(None)):
        for h in (0, 1):
            buf[0][h, :, lanes] = s[h]
            buf[1][h, :, lanes] = jnp.max(s[h], axis=0, keepdims=True)

    def tile_update(j, buf, lanes=slice(None)):
        s_buf, smax_buf = buf
        for h in (0, 1):
            m_old = m_refs[h][:, lanes]
            m_new = jnp.maximum(m_old, smax_buf[h, :, lanes])
            p = jnp.exp2(s_buf[h, :, lanes] - m_new).astype(BF16)
            pv = _dot(vt_ref[j, h], p)
            acc_refs[h][:, lanes] = jnp.exp2(m_old - m_new) * acc_refs[h][:, lanes] + pv
            m_refs[h][:, lanes] = m_new

    for h in (0, 1):
        m_refs[h][...] = jnp.full(m_refs[h].shape, F32_LOWEST, F32)
        acc_refs[h][...] = jnp.zeros(acc_refs[h].shape, F32)

    buffers = ((s_even_ref, smax_even_ref), (s_odd_ref, smax_odd_ref))
    n_own = tq // tk
    j_first = i * n_own
    diagonal = (lax.broadcasted_iota(jnp.int32, (tk, tk), 0) <= lax.broadcasted_iota(jnp.int32, (tk, tk), 1))

    def own_scores(g):
        s = tile_scores(j_first + g, slice(g * tk, tq))
        own = [jnp.where(diagonal, s_h[:, :tk], NEG_INF) for s_h in s]
        if g == n_own - 1:
            return own
        return [jnp.concatenate([own_h, s_h[:, tk:]], axis=1) for own_h, s_h in zip(own, s)]

    put_scores(buffers[0], own_scores(n_own - 1), slice((n_own - 1) * tk, tq))
    for m in range(1, n_own):
        g = n_own - 1 - m
        put_scores(buffers[m % 2], own_scores(g), slice(g * tk, tq))
        tile_update(j_first + g + 1, buffers[(m - 1) % 2], slice((g + 1) * tk, tq))
    cur, nxt = buffers[(n_own - 1) % 2], buffers[n_own % 2]

    def previous_tile(j):
        return jnp.where(j == 0, j_first, j - 1)

    def pipelined_step(j, src, dst):
        put_scores(dst, tile_scores(j))
        tile_update(previous_tile(j), src)

    def kv_pair_step(u, carry):
        pipelined_step(2 * u, cur, nxt)
        pipelined_step(2 * u + 1, nxt, cur)
        return carry
    lax.fori_loop(0, j_first // 2, kv_pair_step, 0)

    @pl.when(j_first % 2 == 1)
    def _():
        pipelined_step(j_first - 1, cur, nxt)
        tile_update(j_first - 1, nxt)

    @pl.when(j_first % 2 == 0)
    def _():
        tile_update(previous_tile(j_first), cur)

    out_t = jnp.concatenate([acc_refs[h][0:HEAD_DIM, :] / acc_refs[h][HEAD_DIM:HEAD_DIM + 1, :]
                             for h in (0, 1)], axis=0)
    o_ref[...] = out_t.T.astype(o_ref.dtype)


def _moba_call(pk, qvt, kaug, batch, seq_len):
    n = pk.shape[0]
    blk = MOBA_BLOCK
    tq = MOBA_Q_TILE
    nb = seq_len // blk
    nq = seq_len // tq
    assert nb + MOBA_ALIBI_PARTS <= LANES and seq_len % MOBA_KV_TILE == 0
    assert tq % MOBA_KV_TILE == 0 and seq_len % tq == 0
    lane_groups = MOBA_WIDTH // LANES
    return pl.pallas_call(
        functools.partial(_moba_kernel, n_blocks=nb),
        grid=(batch, N_PAIRS, nq),
        in_specs=[
            pl.BlockSpec((PAIR, tq), lambda b, p, i: (p, b * nq + i)),
            pl.BlockSpec((seq_len, PAIR), lambda b, p, i: (b, p)),
            pl.BlockSpec((PAIR, seq_len), lambda b, p, i: (lane_groups + p, b)),
            pl.BlockSpec((2, seq_len, LANES), lambda b, p, i: (p, 0, 0)),
        ],
        out_specs=pl.BlockSpec((tq, PAIR), lambda b, p, i: (b * nq + i, p)),
        out_shape=jax.ShapeDtypeStruct((n, MOBA_WIDTH), BF16),
        scratch_shapes=[
            pltpu.VMEM((-(-nb // 8) * 8, PAIR), F32),
            pltpu.VMEM((seq_len // MOBA_KV_TILE, 2, MOBA_V_ROWS, MOBA_KV_TILE), BF16),
            pltpu.VMEM((1, tq), F32), pltpu.VMEM((1, tq), F32),
            pltpu.VMEM((MOBA_V_ROWS, tq), F32), pltpu.VMEM((MOBA_V_ROWS, tq), F32),
            pltpu.VMEM((2, MOBA_KV_TILE, tq), F32), pltpu.VMEM((2, MOBA_KV_TILE, tq), F32),
            pltpu.VMEM((2, 1, tq), F32), pltpu.VMEM((2, 1, tq), F32),
        ],
        compiler_params=pltpu.CompilerParams(
            dimension_semantics=("arbitrary", "arbitrary", "arbitrary"),
            vmem_limit_bytes=VMEM_LIMIT),
        name="moba_attention",
    )(qvt, pk, qvt, kaug)


def _moba_key_aug(seq_len):
    nb = seq_len // MOBA_BLOCK
    heads = MOBA_WIDTH // HEAD_DIM
    pos = np.arange(seq_len, dtype=np.int32)
    slopes = (2.0 ** (-8.0 * (np.arange(heads, dtype=np.float32) + 1.0) / heads)).astype(np.float32)
    aug = np.zeros((heads, seq_len, LANES), np.float32)
    aug[:, pos, pos // MOBA_BLOCK] = 1.0
    rem = (np.float32(LOG2_E) * slopes)[:, None] * pos.astype(np.float32)[None, :]
    for part in range(MOBA_ALIBI_PARTS):
        piece = (rem.view(np.uint32) & np.uint32(0xFFFF0000)).view(np.float32)
        aug[:, :, nb + part] = piece
        rem = rem - piece
    return jnp.asarray(aug.astype(BF16))


def _layer_norm(z, g, b):
    mu = jnp.mean(z, axis=-1, keepdims=True)
    zc = z - mu
    var = jnp.mean(zc * zc, axis=-1, keepdims=True)
    return zc * lax.rsqrt(var + LN_EPS) * g + b


def _outproj_kernel(ya_ref, yb_ref, x_ref, wa_ref, wb_ref, g_ref, b_ref, wr_ref, br_ref,
                    h_ref, lg_ref):
    mix = _dot(ya_ref[...], wa_ref[...]) + _dot(yb_ref[...], wb_ref[...])
    h = _layer_norm(DEEPNORM_ALPHA * x_ref[...] + mix, g_ref[...], b_ref[...])
    h_ref[...] = h
    h_hi, h_lo = _split_bf16(h, 2)
    w_hi, w_lo = _split_bf16(wr_ref[...], 2)
    hh_hl = _dot(h_hi, jnp.concatenate([w_hi, w_lo], axis=1))
    lg_ref[...] = hh_hl[:, :ROUTER_PAD] + (hh_hl[:, ROUTER_PAD:] + _dot(h_lo, w_hi)) + br_ref[...]


def _outproj_call(y_a, y_b, x2, wa, wb, ln_g, ln_b, w_router, b_router):
    n = x2.shape[0]
    tm = OUTPROJ_TM
    row = lambda i: (i, 0)
    const = lambda i: (0, 0)
    return pl.pallas_call(
        _outproj_kernel,
        grid=(n // tm,),
        in_specs=[
            pl.BlockSpec((tm, RWKV_WIDTH), row),
            pl.BlockSpec((tm, MOBA_WIDTH), row),
            pl.BlockSpec((tm, D_MODEL), row),
            pl.BlockSpec((RWKV_WIDTH, D_MODEL), const),
            pl.BlockSpec((MOBA_WIDTH, D_MODEL), const),
            pl.BlockSpec((1, D_MODEL), const),
            pl.BlockSpec((1, D_MODEL), const),
            pl.BlockSpec((D_MODEL, ROUTER_PAD), const),
            pl.BlockSpec((1, ROUTER_PAD), const),
        ],
        out_specs=[
            pl.BlockSpec((tm, D_MODEL), row),
            pl.BlockSpec((tm, ROUTER_PAD), row),
        ],
        out_shape=[
            jax.ShapeDtypeStruct((n, D_MODEL), F32),
            jax.ShapeDtypeStruct((n, ROUTER_PAD), F32),
        ],
        compiler_params=pltpu.CompilerParams(
            dimension_semantics=("arbitrary",), vmem_limit_bytes=VMEM_LIMIT),
        name="outproj_ln_router",
    )(y_a, y_b, x2, wa, wb, ln_g, ln_b, w_router, b_router)


def _route(logits_t):
    row = lax.broadcasted_iota(jnp.int32, logits_t.shape, 0)
    n_rows = logits_t.shape[0]
    is_group = (row >= GROUP_LANE0) & (row < GROUP_LANE0 + N_GROUPS)
    gl = jnp.where(is_group, logits_t, F32_LOWEST)
    g_max = jnp.max(gl, axis=0, keepdims=True)
    g_first = jnp.min(jnp.where(gl == g_max, row, n_rows), axis=0, keepdims=True)
    g_exp = jnp.where(is_group, jnp.exp(gl - g_max), 0.0)
    p_g = 1.0 / jnp.sum(g_exp, axis=0, keepdims=True)
    g_idx = g_first - GROUP_LANE0
    in_group = (row >= g_idx * EXPERTS_PER_GROUP) & (row < (g_idx + 1) * EXPERTS_PER_GROUP)
    el = jnp.where(in_group, logits_t, F32_LOWEST)
    e_max = jnp.max(el, axis=0, keepdims=True)
    e_exp = jnp.where(in_group, jnp.exp(el - e_max), 0.0)
    e_prob = e_exp / jnp.sum(e_exp, axis=0, keepdims=True)
    cand = jnp.where(in_group, e_prob, -1.0)
    v1 = jnp.max(cand, axis=0, keepdims=True)
    i1 = jnp.min(jnp.where(cand == v1, row, n_rows), axis=0, keepdims=True)
    pick1 = row == i1
    cand2 = jnp.where(pick1, -1.0, cand)
    v2 = jnp.max(cand2, axis=0, keepdims=True)
    i2 = jnp.min(jnp.where(cand2 == v2, row, n_rows), axis=0, keepdims=True)
    pick2 = row == i2
    denom = v1 + v2
    gates = jnp.where(pick1, v1 / denom * p_g, jnp.where(pick2, v2 / denom * p_g, 0.0))
    return gates, g_idx


def _route_sort_kernel(lg_ref, gates_ref, pos_ref, flags_ref, later_ref):
    tm = lg_ref.shape[0]

    @pl.when(pl.program_id(0) == 0)
    def _():
        row = lax.broadcasted_iota(jnp.int32, (tm, tm), 0)
        col = lax.broadcasted_iota(jnp.int32, (tm, tm), 1)
        later_ref[...] = jnp.where(row < col, 1.0, 0.0).astype(BF16)

    gates_t, g_idx = _route(lg_ref[...].T[:ROUTER_ROWS])
    own = lax.broadcasted_iota(jnp.int32, (SUBLANES, tm), 0) == g_idx
    onehot = jnp.where(own, 1.0, 0.0)
    rank = _dot(onehot.astype(BF16), later_ref[...])
    count = jnp.broadcast_to(jnp.sum(onehot, axis=1, keepdims=True), (SUBLANES, LANES))
    starts, running = [], jnp.zeros((1, LANES), F32)
    for g in range(SUBLANES):
        starts.append(running)
        running = running + count[g:g + 1]
    start = jnp.concatenate(starts, axis=0)
    pos = jnp.sum(jnp.where(own, rank + start[:, 0:1], 0.0), axis=0, keepdims=True)
    gates_ref[...] = jnp.concatenate([gates_t, jnp.zeros((LANES - ROUTER_ROWS, tm), F32)], axis=0).T
    pos_ref[...] = jnp.broadcast_to(pos, (LANES, tm)).T
    lane = lax.broadcasted_iota(jnp.int32, (SUBLANES, LANES), 1)
    sub_lo = lane * MOE_SUB
    hit = (count > 0.0) & (start < (sub_lo + MOE_SUB).astype(F32)) & (start + count > sub_lo.astype(F32))
    start_i = start.astype(jnp.int32)
    count_i = count.astype(jnp.int32)
    choice = jnp.zeros_like(start_i)
    first = jnp.zeros_like(start_i)
    for k in reversed(range(len(MOE_WINDOWS))):
        win = jnp.minimum((start_i // MOE_WINDOW_ALIGN) * MOE_WINDOW_ALIGN, tm - MOE_WINDOWS[k])
        fits = (count_i > 0) & (start_i + count_i <= win + MOE_WINDOWS[k])
        choice = jnp.where(fits, k + 1, choice)
        first = jnp.where(fits, win, first)
    meta = jnp.where(lane == MOE_META_FITS, choice,
                     jnp.where(lane == MOE_META_WINDOW, first, jnp.where(hit, 1, 0)))
    flags_ref[0] = meta.astype(jnp.int32)


def _route_sort_call(logits):
    n = logits.shape[0]
    tm = MOE_TM
    row = lambda t: (t, 0)
    return pl.pallas_call(
        _route_sort_kernel,
        grid=(n // tm,),
        in_specs=[pl.BlockSpec((tm, ROUTER_PAD), row)],
        out_specs=[
            pl.BlockSpec((tm, ROUTER_PAD), row),
            pl.BlockSpec((tm, LANES), row),
            pl.BlockSpec((1, SUBLANES, LANES), lambda t: (t, 0, 0)),
        ],
        out_shape=[
            jax.ShapeDtypeStruct((n, ROUTER_PAD), F32),
            jax.ShapeDtypeStruct((n, LANES), F32),
            jax.ShapeDtypeStruct((n // tm, SUBLANES, LANES), jnp.int32),
        ],
        scratch_shapes=[pltpu.VMEM((tm, tm), BF16)],
        compiler_params=pltpu.CompilerParams(
            dimension_semantics=("arbitrary",), vmem_limit_bytes=VMEM_LIMIT),
        name="route_sort",
    )(logits)


def _moe_kernel(flags_ref, h_ref, gates_ref, pos_ref, w1_ref, w3_ref, w2_ref, g_ref, b_ref,
                o_ref, xs_ref, gs_ref, acc_ref, pt_ref):
    tile = pl.program_id(0)
    step = pl.program_id(1)
    tm = h_ref.shape[0]
    sub = MOE_SUB
    n_sub = tm // sub
    eps = MOE_EXPERTS_PER_STEP

    @pl.when(step == 0)
    def _():
        pos_b = pos_ref[...]
        pos_row = pos_b.T[0:1, :]
        g_hi, g_lo = _split_bf16(gates_ref[...], 2)
        src = jnp.concatenate([h_ref[...].astype(BF16), g_hi, g_lo], axis=1)
        for c0 in range(0, tm, sub):
            slot = (lax.broadcasted_iota(jnp.int32, (sub, tm), 0) + c0).astype(F32)
            p_c = jnp.where(slot == pos_row, 1.0, 0.0).astype(BF16)
            moved = _dot(p_c, src)
            xs_ref[c0:c0 + sub, :] = moved[:, :D_MODEL].astype(BF16)
            gs_ref[c0:c0 + sub, :] = moved[:, D_MODEL:D_MODEL + LANES] + moved[:, D_MODEL + LANES:]
        for c0 in range(0, tm, LANES):
            slot = (lax.broadcasted_iota(jnp.int32, (tm, LANES), 1) + c0).astype(F32)
            pt_ref[:, c0:c0 + LANES] = jnp.where(pos_b == slot, 1.0, 0.0).astype(BF16)
        acc_ref[...] = jnp.zeros_like(acc_ref)

    group = step // (EXPERTS_PER_GROUP // eps)

    def visit(rows):
        x_r = xs_ref[rows, :]
        g_r = gs_ref[rows, :]
        lane = lax.broadcasted_iota(jnp.int32, g_r.shape, 1)
        acc = acc_ref[rows, :]
        for e in range(eps):
            gate_e = jnp.sum(jnp.where(lane == step * eps + e, g_r, 0.0), axis=-1, keepdims=True)
            a1 = _dot(x_r, w1_ref[e])
            a3 = _dot(x_r, w3_ref[e])
            hid = (a1 * _sigmoid(a1)) * a3 * gate_e
            acc = acc + _dot(hid.astype(BF16), w2_ref[e])
        acc_ref[rows, :] = acc

    base = (tile * N_GROUPS + group) * (n_sub + 2)
    choice = flags_ref[base + n_sub]

    def visit_window(size):
        first = pl.multiple_of(flags_ref[base + n_sub + 1], MOE_WINDOW_ALIGN)
        visit(pl.ds(first, size))

    for k, size in enumerate(MOE_WINDOWS):
        pl.when(choice == k + 1)(functools.partial(visit_window, size))

    for r in range(n_sub):
        pl.when((choice == 0) & (flags_ref[base + r] != 0))(
            functools.partial(visit, slice(r * sub, (r + 1) * sub)))

    @pl.when(step == N_EXPERTS // eps - 1)
    def _():
        ffn = _dot(pt_ref[...], acc_ref[...].astype(BF16))
        o_ref[...] = _layer_norm(DEEPNORM_ALPHA * h_ref[...] + ffn, g_ref[...], b_ref[...])


def _moe_call(flags, h_f32, gates, pos, w1, w3, w2, ln_g, ln_b):
    n = h_f32.shape[0]
    tm = MOE_TM
    row = lambda t, s, f: (t, 0)
    const = lambda t, s, f: (0, 0)
    wmap = lambda t, s, f: (s, 0, 0)
    eps = MOE_EXPERTS_PER_STEP
    assert EXPERTS_PER_GROUP % eps == 0 and tm % MOE_SUB == 0
    grid_spec = pltpu.PrefetchScalarGridSpec(
        num_scalar_prefetch=1,
        grid=(n // tm, N_EXPERTS // eps),
        in_specs=[
            pl.BlockSpec((tm, D_MODEL), row),
            pl.BlockSpec((tm, ROUTER_PAD), row),
            pl.BlockSpec((tm, LANES), row),
            pl.BlockSpec((eps, D_MODEL, D_EXPERT), wmap),
            pl.BlockSpec((eps, D_MODEL, D_EXPERT), wmap),
            pl.BlockSpec((eps, D_EXPERT, D_MODEL), wmap),
            pl.BlockSpec((1, D_MODEL), const),
            pl.BlockSpec((1, D_MODEL), const),
        ],
        out_specs=pl.BlockSpec((tm, D_MODEL), row),
        scratch_shapes=[
            pltpu.VMEM((tm, D_MODEL), BF16),
            pltpu.VMEM((tm, ROUTER_PAD), F32),
            pltpu.VMEM((tm, D_MODEL), F32),
            pltpu.VMEM((tm, tm), BF16),
        ],
    )
    return pl.pallas_call(
        _moe_kernel,
        grid_spec=grid_spec,
        out_shape=jax.ShapeDtypeStruct((n, D_MODEL), F32),
        compiler_params=pltpu.CompilerParams(
            dimension_semantics=("arbitrary", "arbitrary"), vmem_limit_bytes=VMEM_LIMIT),
        name="hier_moe_ln",
    )(flags, h_f32, gates, pos, w1, w3, w2, ln_g, ln_b)


def _pad_cols(w, width):
    return jnp.pad(w, ((0, 0), (0, width - w.shape[1])))


def kernel(x, w_in, mu_shift, w0, w_lora_up, a0, a_lora_up, g_lora_up, k_k, k_a, r_k, gn_w, gn_b, w_out, ln1_g, ln1_b, w_group, b_group, w_expert, b_expert, w1_exp, w3_exp, w2_exp, ln2_g, ln2_b):
    batch, seq_len, d = x.shape
    assert d == D_MODEL
    n = batch * seq_len
    x2 = x.reshape(n, d)

    c_rkv = 3 * RWKV_WIDTH
    c_wd = c_rkv + DECAY_RANK
    c_ad = c_wd + AAA_RANK
    c_gd = c_ad + GATE_RANK
    w_cat = jnp.concatenate([
        w_in[:, :c_rkv], _pad_cols(w_in[:, c_rkv:c_gd], LORA_PAD), w_in[:, c_gd:],
    ], axis=1).astype(BF16)
    mu2 = mu_shift[None, :]
    mu_cat = jnp.concatenate([mu2[:, :c_rkv], _pad_cols(mu2[:, c_rkv:c_gd], LORA_PAD)], axis=1)
    p_rkv, p_lora, p_k, p_qvt = _inproj_call(x2, w_cat, mu_cat, seq_len)

    place = lambda w, first: jnp.pad(w, ((first, LORA_PAD - first - w.shape[0]), (0, 0)))
    lora_up = (place(w_lora_up, 0), place(a_lora_up, DECAY_RANK), place(g_lora_up, DECAY_RANK + AAA_RANK))

    vecs = jnp.stack([w0, a0, k_k, k_a, r_k.reshape(-1), gn_w, gn_b, jnp.zeros_like(w0)], axis=0)
    head_id = jnp.arange(2 * PAIR) // HEAD_DIM
    bd = (head_id[:, None] == head_id[None, :]).astype(BF16)
    y_a = _rwkv_call(p_rkv, p_lora, vecs, *lora_up, bd, batch, seq_len)

    y_b = _moba_call(p_k, p_qvt, _moba_key_aug(seq_len), batch, seq_len)

    w_out_b = w_out.astype(BF16)
    w_router = _pad_cols(jnp.concatenate([w_expert, w_group], axis=1), ROUTER_PAD)
    b_router = _pad_cols(jnp.concatenate([b_expert, b_group])[None, :], ROUTER_PAD)
    h1, logits = _outproj_call(y_a, y_b, x2, w_out_b[:RWKV_WIDTH], w_out_b[RWKV_WIDTH:],
                               ln1_g[None, :], ln1_b[None, :], w_router, b_router)

    flat = lambda w: w.astype(BF16).reshape((N_EXPERTS,) + w.shape[2:])
    gates, pos, flags = _route_sort_call(logits)
    flags = jnp.concatenate([flags[:, :N_GROUPS, :MOE_TM // MOE_SUB],
                             flags[:, :N_GROUPS, MOE_META_FITS:MOE_META_WINDOW + 1]], axis=-1).reshape(-1)
    out = _moe_call(flags, h1, gates, pos, flat(w1_exp), flat(w3_exp), flat(w2_exp),
                    ln2_g[None, :], ln2_b[None, :])
    return out.reshape(batch, seq_len, d)
```

```python
import functools
import math

import jax
import jax.numpy as jnp
import numpy as np
from jax import lax
from jax.experimental import pallas as pl
from jax.experimental.pallas import tpu as pltpu

F32 = jnp.float32
BF16 = jnp.bfloat16

D_MODEL = 1024
HEAD_DIM = 64
RWKV_WIDTH = 512
MOBA_WIDTH = 512
DECAY_RANK = 32
AAA_RANK = 32
GATE_RANK = 96
GN_EPS = 64e-5
L2_EPS = 1e-12
MOBA_BLOCK = 256
MOBA_TOPK = 3
N_GROUPS = 4
EXPERTS_PER_GROUP = 8
N_EXPERTS = N_GROUPS * EXPERTS_PER_GROUP
D_EXPERT = 256
LN_EPS = 1e-5
DEEPNORM_ALPHA = float(2.0 ** 0.25)
NEG_INF = -1e30
F32_LOWEST = -3.0e38

LANES = 128
SUBLANES = 8
BF16_SUBLANES = 16
MXU_TILE = 256
V7X_VMEM_BYTES = 64 * 1024 * 1024
PAIR = 2 * HEAD_DIM
N_PAIRS = RWKV_WIDTH // PAIR
LORA_PAD = MXU_TILE
RWKV_COLS_PAD = 3 * RWKV_WIDTH + LORA_PAD
IN_COLS_PAD = RWKV_COLS_PAD + 3 * MOBA_WIDTH
VMEM_LIMIT = V7X_VMEM_BYTES * 7 // 8

INPROJ_TM = 1024
INPROJ_TN = MXU_TILE
RWKV_CHUNK = 64
RWKV_CHUNKS_PER_STEP = 8
RWKV_PASSES = 1
RWKV_STATE_PASSES = 1
OUTPROJ_TM = 1024
MOE_TM = 1024
MOE_EXPERTS_PER_STEP = 8
MOE_SUB = 256
MOE_WINDOWS = (272, 320)
MOE_WINDOW_ALIGN = BF16_SUBLANES
MOE_META_FITS = 16
MOE_META_WINDOW = 17
MOBA_KV_TILE = 512
MOBA_Q_TILE = 2048
MOBA_V_ROWS = HEAD_DIM + BF16_SUBLANES
MOBA_ALIBI_PARTS = 3
LOG2_E = 1.4426950408889634
ROUTER_PAD = LANES
GROUP_LANE0 = N_EXPERTS
ROUTER_ROWS = 40

NN = (((1,), (0,)), ((), ()))


def _dot(a, b, dims=NN):
    return lax.dot_general(a, b, dims, preferred_element_type=F32)


def _split_bf16(x, parts):
    out = []
    rem = x
    for i in range(parts):
        p = rem.astype(BF16)
        out.append(p)
        if i + 1 < parts:
            rem = rem - p.astype(F32)
    return out


def _mm(a, b, dims=NN, passes=3):
    if passes == 1:
        return _dot(a.astype(BF16), b.astype(BF16), dims)
    assert passes == 3
    a_hi, a_lo = _split_bf16(a, 2)
    b_hi, b_lo = _split_bf16(b, 2)
    return _dot(a_hi, b_hi, dims) + (_dot(a_hi, b_lo, dims) + _dot(a_lo, b_hi, dims))


def _mm_exact_lhs(a_bf16, b, dims=NN, parts=3):
    out = None
    for piece in reversed(_split_bf16(b, parts)):
        term = _dot(a_bf16, piece, dims)
        out = term if out is None else term + out
    return out


def _mm_exact_rhs(a, b_bf16, dims=NN):
    a1, a2, a3 = _split_bf16(a, 3)
    return _dot(a1, b_bf16, dims) + (_dot(a2, b_bf16, dims) + _dot(a3, b_bf16, dims))


def _inproj_kernel(x_ref, w_ref, mu_ref, prkv_ref, plora_ref, pk_ref, qvt_ref, carry_ref, *, tiles_per_seq):
    tm = x_ref.shape[0]
    xb = x_ref[...].astype(BF16)
    seq_start = (pl.program_id(0) % tiles_per_seq) == 0
    row0 = lax.broadcasted_iota(jnp.int32, (tm, INPROJ_TN), 0) == 0
    n_shift_tiles = RWKV_COLS_PAD // INPROJ_TN
    for j in range(n_shift_tiles):
        c0 = j * INPROJ_TN
        acc = _dot(xb, w_ref[:, c0:c0 + INPROJ_TN])
        prev_last = jnp.where(seq_start, 0.0, carry_ref[0:1, c0:c0 + INPROJ_TN])
        shifted = jnp.where(row0, prev_last, pltpu.roll(acc, 1, 0))
        carry_ref[0:1, c0:c0 + INPROJ_TN] = acc[tm - 1:tm, :]
        out = acc + (shifted - acc) * mu_ref[:, c0:c0 + INPROJ_TN]
        if c0 < 3 * RWKV_WIDTH:
            prkv_ref[:, c0:c0 + INPROJ_TN] = out
        else:
            plora_ref[:, c0 - 3 * RWKV_WIDTH:c0 - 3 * RWKV_WIDTH + INPROJ_TN] = out
    tiles_per_part = MOBA_WIDTH // INPROJ_TN
    for j in range(3 * tiles_per_part):
        c0 = j * INPROJ_TN
        acc = _dot(xb, w_ref[:, RWKV_COLS_PAD + c0:RWKV_COLS_PAD + c0 + INPROJ_TN])
        part, r0 = divmod(c0, MOBA_WIDTH)
        if part == 1:
            pk_ref[:, r0:r0 + INPROJ_TN] = acc.astype(BF16)
        else:
            r0 += (part // 2) * MOBA_WIDTH
            qvt_ref[r0:r0 + INPROJ_TN, :] = acc.T.astype(BF16)


def _inproj_call(x2, w_cat, mu_cat, seq_len):
    n = x2.shape[0]
    tm = INPROJ_TM
    assert seq_len % tm == 0 and (3 * RWKV_WIDTH) % INPROJ_TN == 0
    return pl.pallas_call(
        functools.partial(_inproj_kernel, tiles_per_seq=seq_len // tm),
        grid=(n // tm,),
        in_specs=[
            pl.BlockSpec((tm, D_MODEL), lambda i: (i, 0)),
            pl.BlockSpec((D_MODEL, IN_COLS_PAD), lambda i: (0, 0)),
            pl.BlockSpec((1, RWKV_COLS_PAD), lambda i: (0, 0)),
        ],
        out_specs=[
            pl.BlockSpec((tm, 3 * RWKV_WIDTH), lambda i: (i, 0)),
            pl.BlockSpec((tm, LORA_PAD), lambda i: (i, 0)),
            pl.BlockSpec((tm, MOBA_WIDTH), lambda i: (i, 0)),
            pl.BlockSpec((2 * MOBA_WIDTH, tm), lambda i: (0, i)),
        ],
        out_shape=[
            jax.ShapeDtypeStruct((n, 3 * RWKV_WIDTH), F32),
            jax.ShapeDtypeStruct((n, LORA_PAD), F32),
            jax.ShapeDtypeStruct((n, MOBA_WIDTH), BF16),
            jax.ShapeDtypeStruct((2 * MOBA_WIDTH, n), BF16),
        ],
        scratch_shapes=[pltpu.VMEM((SUBLANES, RWKV_COLS_PAD), F32)],
        compiler_params=pltpu.CompilerParams(
            dimension_semantics=("arbitrary",), vmem_limit_bytes=VMEM_LIMIT),
        name="inproj_shift",
    )(x2, w_cat, mu_cat)


def _sigmoid(z):
    return 1.0 / (1.0 + jnp.exp(-z))


def _rwkv_chunks(rt, kt, at, bt, v, d_incl, s_prev, passes, state_passes):
    c = RWKV_CHUNK
    n_chunks = rt.shape[0] // c
    n_pairs = len(s_prev)
    row = lax.broadcasted_iota(jnp.int32, (c, PAIR), 0)
    col = lax.broadcasted_iota(jnp.int32, (c, PAIR), 1) % HEAD_DIM
    strict = row > col
    incl = row >= col
    eye_c = (row == col).astype(F32)
    lane = lax.broadcasted_iota(jnp.int32, (1, PAIR), 1)
    head0 = lane < HEAD_DIM
    head1 = jnp.logical_not(head0)
    prow = lax.broadcasted_iota(jnp.int32, (PAIR, PAIR), 0)
    pcol = lax.broadcasted_iota(jnp.int32, (PAIR, PAIR), 1)
    same_head = (prow < HEAD_DIM) == (pcol < HEAD_DIM)
    eye_p = (prow == pcol).astype(F32)
    rows = [slice(ci * c, (ci + 1) * c) for ci in range(n_chunks)]
    sl = [slice(p * PAIR, (p + 1) * PAIR) for p in range(n_pairs)]
    pairs = [(ci, p) for ci in range(n_chunks) for p in range(n_pairs)]
    cut = lambda t, ci, p: t[rows[ci], sl[p]]

    rhs_dtype = BF16 if passes == 1 else F32

    def by_head(m, dtype=rhs_dtype):
        m = m.astype(dtype)
        zero = jnp.zeros_like(m)
        return jnp.concatenate([jnp.where(head0, m, zero), jnp.where(head1, m, zero)], axis=0)

    def by_head2(m, n):
        return jnp.concatenate([by_head(m), by_head(n)], axis=1)

    at_p = {k_: cut(at, *k_) for k_ in pairs}
    rt_p = {k_: cut(rt, *k_) for k_ in pairs}
    bt_p = {k_: cut(bt, *k_) for k_ in pairs}
    kt_p = {k_: cut(kt, *k_) for k_ in pairs}
    v_p = {k_: cut(v, *k_) for k_ in pairs}

    z = {k_: _mm(jnp.concatenate([at_p[k_], rt_p[k_]], axis=0),
                 jnp.concatenate([by_head(bt_p[k_], F32).T, by_head(kt_p[k_], F32).T], axis=1), NN, passes)
         for k_ in pairs}
    l_ab = {k_: jnp.where(strict, z[k_][:c, :PAIR], 0.0) for k_ in pairs}
    l_ak = {k_: jnp.where(strict, z[k_][:c, PAIR:], 0.0) for k_ in pairs}
    m_rb = {k_: jnp.where(incl, z[k_][c:, :PAIR], 0.0) for k_ in pairs}
    m_rk = {k_: jnp.where(incl, z[k_][c:, PAIR:], 0.0) for k_ in pairs}
    pw = {k_: _mm(l_ab[k_], by_head(l_ab[k_]), NN, passes) for k_ in pairs}
    t_inv = {k_: eye_c + l_ab[k_] for k_ in pairs}
    for _ in range(int(math.log2(c)) - 2):
        tp = {k_: _mm(jnp.concatenate([t_inv[k_], pw[k_]], axis=0), by_head(pw[k_]), NN, passes)
              for k_ in pairs}
        t_inv = {k_: t_inv[k_] + tp[k_][:c] for k_ in pairs}
        pw = {k_: tp[k_][c:] for k_ in pairs}
    t_inv = {k_: t_inv[k_] + _mm(t_inv[k_], by_head(pw[k_]), NN, passes) for k_ in pairs}
    lm = {k_: _mm(jnp.concatenate([l_ak[k_], m_rk[k_]], axis=0), by_head(v_p[k_]), NN, passes)
          for k_ in pairs}
    lv = {k_: lm[k_][:c] for k_ in pairs}
    mv = {k_: lm[k_][c:] for k_ in pairs}
    wu = {k_: _mm(t_inv[k_], by_head2(at_p[k_], lv[k_]), NN, passes) for k_ in pairs}
    qy = {k_: _mm(m_rb[k_], by_head2(wu[k_][:, :PAIR], wu[k_][:, PAIR:]), NN, passes) for k_ in pairs}

    qeff, y1, phi, psi = {}, {}, {}, {}
    for ci, p in pairs:
        k_ = (ci, p)
        w, u0 = wu[k_][:, :PAIR], wu[k_][:, PAIR:]
        qeff[k_] = rt_p[k_] + qy[k_][:, :PAIR]
        y1[k_] = qy[k_][:, PAIR:] + mv[k_]
        d_p = d_incl[(ci + 1) * c - 1:(ci + 1) * c, sl[p]]
        phi[k_] = jnp.where(same_head, (eye_p + _mm(w.T, bt_p[k_], NN, passes)) * d_p, 0.0)
        uv_t = jnp.concatenate([u0, v_p[k_]], axis=0).T
        bk = jnp.concatenate([bt_p[k_], kt_p[k_]], axis=0)
        psi[k_] = jnp.where(same_head, _mm(uv_t, bk, NN, passes) * d_p, 0.0)

    state = list(s_prev)
    ys = [[None] * n_pairs for _ in range(n_chunks)]
    for ci in range(n_chunks):
        for p in range(n_pairs):
            ys[ci][p] = _mm(qeff[ci, p], state[p].T, NN, state_passes) + y1[ci, p]
            state[p] = _mm(state[p], phi[ci, p], NN, state_passes) + psi[ci, p]
    y = jnp.concatenate([jnp.concatenate(ys[ci], axis=1) for ci in range(n_chunks)], axis=0)
    return y, state


def _rwkv_kernel(prkv_ref, plora_ref, vec_ref, wl_ref, al_ref, gl_ref, bd_ref, y_ref, s_ref):
    rows = prkv_ref.shape[0]
    c = RWKV_CHUNK
    width = RWKV_WIDTH

    @pl.when(pl.program_id(1) == 0)
    def _():
        s_ref[...] = jnp.zeros_like(s_ref)

    r = prkv_ref[:, 0:width]
    k_raw = prkv_ref[:, width:2 * width]
    v = prkv_ref[:, 2 * width:3 * width]
    p_wd = p_ad = p_gd = plora_ref[...]
    w0 = vec_ref[0:1, :]
    a0 = vec_ref[1:2, :]
    k_k = vec_ref[2:3, :]
    k_a = vec_ref[3:4, :]
    r_k = vec_ref[4:5, :]
    gn_w = vec_ref[5:6, :]
    gn_b = vec_ref[6:7, :]
    bd = bd_ref[...]

    def seg_sum(z):
        halves = []
        for c0 in range(0, width, bd.shape[0]):
            halves.append(_dot(z[:, c0:c0 + bd.shape[0]].astype(BF16), bd))
        return jnp.concatenate(halves, axis=1)

    log_w = -math.exp(-0.5) * _sigmoid(w0 + _mm(jnp.tanh(p_wd), wl_ref[...], NN, 3))
    a = _sigmoid(a0 + _mm(p_ad, al_ref[...], NN, RWKV_PASSES))
    g = _mm(_sigmoid(p_gd), gl_ref[...], NN, RWKV_PASSES)
    kk = k_raw * k_k
    kk = kk * lax.rsqrt(jnp.maximum(seg_sum(kk * kk), L2_EPS * L2_EPS))
    k = k_raw * (1.0 + (a - 1.0) * k_a)

    row = lax.broadcasted_iota(jnp.int32, (c, c), 0)
    col = lax.broadcasted_iota(jnp.int32, (c, c), 1)
    tri = jnp.where(row >= col, 1.0, 0.0).astype(BF16)
    cum = jnp.concatenate([_mm_exact_lhs(tri, log_w[c0:c0 + c], parts=2)
                           for c0 in range(0, rows, c)], axis=0)
    d_incl = jnp.exp(cum)
    d_inv = jnp.exp(-cum)
    d_excl = jnp.exp(cum - log_w)
    rt = r * d_incl
    kt = k * d_inv
    at = -kk * d_excl
    bt = kk * a * d_inv

    y, s_next = _rwkv_chunks(rt, kt, at, bt, v, d_incl, [s_ref[p] for p in range(N_PAIRS)],
                             RWKV_PASSES, RWKV_STATE_PASSES)
    for p in range(N_PAIRS):
        s_ref[p] = s_next[p]

    inv_n = 1.0 / HEAD_DIM
    mu = seg_sum(y) * inv_n
    yc = y - mu
    var = seg_sum(yc * yc) * inv_n
    yn = yc * lax.rsqrt(var + GN_EPS) * gn_w + gn_b
    bonus = seg_sum(r * k * r_k) * v
    y_ref[...] = ((yn + bonus) * g).astype(y_ref.dtype)


def _rwkv_call(p_rkv, p_lora, vecs, wl, al, gl, bd, batch, seq_len):
    n = p_rkv.shape[0]
    rows = RWKV_CHUNK * RWKV_CHUNKS_PER_STEP
    assert seq_len % rows == 0
    steps = seq_len // rows
    row_map = lambda b, i: (b * steps + i, 0)
    const = lambda b, i: (0, 0)
    return pl.pallas_call(
        _rwkv_kernel,
        grid=(batch, steps),
        in_specs=[
            pl.BlockSpec((rows, 3 * RWKV_WIDTH), row_map),
            pl.BlockSpec((rows, LORA_PAD), row_map),
            pl.BlockSpec((SUBLANES, RWKV_WIDTH), const),
            pl.BlockSpec((LORA_PAD, RWKV_WIDTH), const),
            pl.BlockSpec((LORA_PAD, RWKV_WIDTH), const),
            pl.BlockSpec((LORA_PAD, RWKV_WIDTH), const),
            pl.BlockSpec((2 * PAIR, 2 * PAIR), const),
        ],
        out_specs=pl.BlockSpec((rows, RWKV_WIDTH), row_map),
        out_shape=jax.ShapeDtypeStruct((n, RWKV_WIDTH), BF16),
        scratch_shapes=[pltpu.VMEM((N_PAIRS, PAIR, PAIR), F32)],
        compiler_params=pltpu.CompilerParams(
            dimension_semantics=("arbitrary", "arbitrary"), vmem_limit_bytes=VMEM_LIMIT),
        name="rwkv7_chunked",
    )(p_rkv, p_lora, vecs, wl, al, gl, bd)


def _moba_kernel(qt_ref, k_ref, vt_in_ref, kaug_ref, o_ref, kmean_ref, vt_ref,
                 m0_ref, m1_ref, acc0_ref, acc1_ref, s_even_ref, s_odd_ref,
                 smax_even_ref, smax_odd_ref, *, n_blocks):
    blk = MOBA_BLOCK
    tk = MOBA_KV_TILE
    tq = qt_ref.shape[1]
    i = pl.program_id(2)
    nb_pad = kmean_ref.shape[0]
    m_refs, acc_refs = (m0_ref, m1_ref), (acc0_ref, acc1_ref)

    @pl.when(i == 0)
    def _():
        kmean_ref[...] = jnp.zeros_like(kmean_ref)

        def mean_body(n, carry):
            off = pl.multiple_of(n * blk, blk)
            kb = k_ref[pl.ds(off, blk), :].astype(F32)
            kmean_ref[pl.ds(n, 1), :] = jnp.sum(kb, axis=0, keepdims=True) * (1.0 / blk)
            return carry
        lax.fori_loop(0, n_blocks, mean_body, 0)

        ones = jnp.ones((MOBA_V_ROWS - HEAD_DIM, tk), BF16)

        for j in range(vt_ref.shape[0]):
            for h in (0, 1):
                vt_ref[j, h] = jnp.concatenate(
                    [vt_in_ref[h * HEAD_DIM:(h + 1) * HEAD_DIM, j * tk:(j + 1) * tk], ones], axis=0)

    q_t = qt_ref[...].astype(F32)
    chan = lax.broadcasted_iota(jnp.int32, (PAIR, tq), 0)
    blk_row = lax.broadcasted_iota(jnp.int32, (nb_pad, tq), 0)
    own_blk = (i * tq + lax.broadcasted_iota(jnp.int32, (nb_pad, tq), 1)) // blk
    past = blk_row < own_blk
    aug_row = lax.broadcasted_iota(jnp.int32, (LANES, tq), 0)
    ones_rows = (aug_row >= n_blocks) & (aug_row < n_blocks + MOBA_ALIBI_PARTS)
    kmean = kmean_ref[...]

    qa_t = []
    for h in (0, 1):
        qh_t = jnp.where((chan < HEAD_DIM) == (h == 0), q_t, 0.0)
        gate = _mm_exact_rhs(kmean, qh_t.astype(BF16))
        gate = jnp.where(past, gate, F32_LOWEST)
        sel = jnp.zeros(gate.shape, jnp.bool_)
        for _ in range(MOBA_TOPK):
            mx = jnp.max(gate, axis=0, keepdims=True)
            first = jnp.min(jnp.where(gate == mx, blk_row, nb_pad), axis=0, keepdims=True)
            pick = (blk_row == first) & (mx > F32_LOWEST)
            sel = sel | pick
            gate = jnp.where(pick, F32_LOWEST, gate)
        sel_bias = jnp.where(past & jnp.logical_not(sel), NEG_INF, 0.0)
        aug_t = jnp.concatenate([sel_bias, jnp.zeros((LANES - nb_pad, tq), F32)], axis=0)
        aug_t = jnp.where(ones_rows, 1.0, aug_t)
        qa_t.append(jnp.concatenate([qh_t * (LOG2_E / math.sqrt(HEAD_DIM)), aug_t], axis=0).astype(BF16))

    def tile_scores(j, lanes=slice(None)):
        off = pl.multiple_of(j * tk, tk)
        k_t = k_ref[pl.ds(off, tk), :]
        return [_dot(jnp.concatenate([k_t, kaug_ref[h, pl.ds(off, tk), :]], axis=1), qa_t[h][:, lanes])
                for h in (0, 1)]

    def put_scores(buf, s, lanes=slice(None)):
        for h in (0, 1):
            buf[0][h, :, lanes] = s[h]
            buf[1][h, :, lanes] = jnp.max(s[h], axis=0, keepdims=True)

    def tile_update(j, buf, lanes=slice(None)):
        s_buf, smax_buf = buf
        for h in (0, 1):
            m_old = m_refs[h][:, lanes]
            m_new = jnp.maximum(m_old, smax_buf[h, :, lanes])
            p = jnp.exp2(s_buf[h, :, lanes] - m_new).astype(BF16)
            pv = _dot(vt_ref[j, h], p)
            acc_refs[h][:, lanes] = jnp.exp2(m_old - m_new) * acc_refs[h][:, lanes] + pv
            m_refs[h][:, lanes] = m_new

    for h in (0, 1):
        m_refs[h][...] = jnp.full(m_refs[h].shape, F32_LOWEST, F32)
        acc_refs[h][...] = jnp.zeros(acc_refs[h].shape, F32)

    buffers = ((s_even_ref, smax_even_ref), (s_odd_ref, smax_odd_ref))
    n_own = tq // tk
    j_first = i * n_own
    diagonal = (lax.broadcasted_iota(jnp.int32, (tk, tk), 0) <= lax.broadcasted_iota(jnp.int32, (tk, tk), 1))

    def own_scores(g):
        s = tile_scores(j_first + g, slice(g * tk, tq))
        own = [jnp.where(diagonal, s_h[:, :tk], NEG_INF) for s_h in s]
        if g == n_own - 1:
            return own
        return [jnp.concatenate([own_h, s_h[:, tk:]], axis=1) for own_h, s_h in zip(own, s)]

    put_scores(buffers[0], own_scores(n_own - 1), slice((n_own - 1) * tk, tq))
    for m in range(1, n_own):
        g = n_own - 1 - m
        put_scores(buffers[m % 2], own_scores(g), slice(g * tk, tq))
        tile_update(j_first + g + 1, buffers[(m - 1) % 2], slice((g + 1) * tk, tq))
    cur, nxt = buffers[(n_own - 1) % 2], buffers[n_own % 2]

    def previous_tile(j):
        return jnp.where(j == 0, j_first, j - 1)

    def pipelined_step(j, src, dst):
        put_scores(dst, tile_scores(j))
        tile_update(previous_tile(j), src)

    def kv_pair_step(u, carry):
        pipelined_step(2 * u, cur, nxt)
        pipelined_step(2 * u + 1, nxt, cur)
        return carry
    lax.fori_loop(0, j_first // 2, kv_pair_step, 0)

    @pl.when(j_first % 2 == 1)
    def _():
        pipelined_step(j_first - 1, cur, nxt)
        tile_update(j_first - 1, nxt)

    @pl.when(j_first % 2 == 0)
    def _():
        tile_update(previous_tile(j_first), cur)

    out_t = jnp.concatenate([acc_refs[h][0:HEAD_DIM, :] / acc_refs[h][HEAD_DIM:HEAD_DIM + 1, :]
                             for h in (0, 1)], axis=0)
    o_ref[...] = out_t.T.astype(o_ref.dtype)


def _moba_call(pk, qvt, kaug, batch, seq_len):
    n = pk.shape[0]
    blk = MOBA_BLOCK
    tq = MOBA_Q_TILE
    nb = seq_len // blk
    nq = seq_len // tq
    assert nb + MOBA_ALIBI_PARTS <= LANES and seq_len % MOBA_KV_TILE == 0
    assert tq % MOBA_KV_TILE == 0 and seq_len % tq == 0
    lane_groups = MOBA_WIDTH // LANES
    return pl.pallas_call(
        functools.partial(_moba_kernel, n_blocks=nb),
        grid=(batch, N_PAIRS, nq),
        in_specs=[
            pl.BlockSpec((PAIR, tq), lambda b, p, i: (p, b * nq + i)),
            pl.BlockSpec((seq_len, PAIR), lambda b, p, i: (b, p)),
            pl.BlockSpec((PAIR, seq_len), lambda b, p, i: (lane_groups + p, b)),
            pl.BlockSpec((2, seq_len, LANES), lambda b, p, i: (p, 0, 0)),
        ],
        out_specs=pl.BlockSpec((tq, PAIR), lambda b, p, i: (b * nq + i, p)),
        out_shape=jax.ShapeDtypeStruct((n, MOBA_WIDTH), BF16),
        scratch_shapes=[
            pltpu.VMEM((-(-nb // 8) * 8, PAIR), F32),
            pltpu.VMEM((seq_len // MOBA_KV_TILE, 2, MOBA_V_ROWS, MOBA_KV_TILE), BF16),
            pltpu.VMEM((1, tq), F32), pltpu.VMEM((1, tq), F32),
            pltpu.VMEM((MOBA_V_ROWS, tq), F32), pltpu.VMEM((MOBA_V_ROWS, tq), F32),
            pltpu.VMEM((2, MOBA_KV_TILE, tq), F32), pltpu.VMEM((2, MOBA_KV_TILE, tq), F32),
            pltpu.VMEM((2, 1, tq), F32), pltpu.VMEM((2, 1, tq), F32),
        ],
        compiler_params=pltpu.CompilerParams(
            dimension_semantics=("arbitrary", "arbitrary", "arbitrary"),
            vmem_limit_bytes=VMEM_LIMIT),
        name="moba_attention",
    )(qvt, pk, qvt, kaug)


def _moba_key_aug(seq_len):
    nb = seq_len // MOBA_BLOCK
    heads = MOBA_WIDTH // HEAD_DIM
    pos = np.arange(seq_len, dtype=np.int32)
    slopes = (2.0 ** (-8.0 * (np.arange(heads, dtype=np.float32) + 1.0) / heads)).astype(np.float32)
    aug = np.zeros((heads, seq_len, LANES), np.float32)
    aug[:, pos, pos // MOBA_BLOCK] = 1.0
    rem = (np.float32(LOG2_E) * slopes)[:, None] * pos.astype(np.float32)[None, :]
    for part in range(MOBA_ALIBI_PARTS):
        piece = (rem.view(np.uint32) & np.uint32(0xFFFF0000)).view(np.float32)
        aug[:, :, nb + part] = piece
        rem = rem - piece
    return jnp.asarray(aug.astype(BF16))


def _layer_norm(z, g, b):
    mu = jnp.mean(z, axis=-1, keepdims=True)
    zc = z - mu
    var = jnp.mean(zc * zc, axis=-1, keepdims=True)
    return zc * lax.rsqrt(var + LN_EPS) * g + b


def _outproj_kernel(ya_ref, yb_ref, x_ref, wa_ref, wb_ref, g_ref, b_ref, wr_ref, br_ref,
                    h_ref, gates_ref, pos_ref, flags_ref, later_ref):
    mix = _dot(ya_ref[...], wa_ref[...]) + _dot(yb_ref[...], wb_ref[...])
    h = _layer_norm(DEEPNORM_ALPHA * x_ref[...] + mix, g_ref[...], b_ref[...])
    h_ref[...] = h
    h_hi, h_lo = _split_bf16(h, 2)
    w_hi, w_lo = _split_bf16(wr_ref[...], 2)
    hh_hl = _dot(h_hi, jnp.concatenate([w_hi, w_lo], axis=1))
    logits = hh_hl[:, :ROUTER_PAD] + (hh_hl[:, ROUTER_PAD:] + _dot(h_lo, w_hi)) + br_ref[...]
    _route_sort_tile(logits, gates_ref, pos_ref, flags_ref, later_ref)


def _outproj_call(y_a, y_b, x2, wa, wb, ln_g, ln_b, w_router, b_router):
    n = x2.shape[0]
    tm = OUTPROJ_TM
    assert tm == MOE_TM
    row = lambda i: (i, 0)
    const = lambda i: (0, 0)
    return pl.pallas_call(
        _outproj_kernel,
        grid=(n // tm,),
        in_specs=[
            pl.BlockSpec((tm, RWKV_WIDTH), row),
            pl.BlockSpec((tm, MOBA_WIDTH), row),
            pl.BlockSpec((tm, D_MODEL), row),
            pl.BlockSpec((RWKV_WIDTH, D_MODEL), const),
            pl.BlockSpec((MOBA_WIDTH, D_MODEL), const),
            pl.BlockSpec((1, D_MODEL), const),
            pl.BlockSpec((1, D_MODEL), const),
            pl.BlockSpec((D_MODEL, ROUTER_PAD), const),
            pl.BlockSpec((1, ROUTER_PAD), const),
        ],
        out_specs=[
            pl.BlockSpec((tm, D_MODEL), row),
            pl.BlockSpec((tm, ROUTER_PAD), row),
            pl.BlockSpec((tm, LANES), row),
            pl.BlockSpec((1, SUBLANES, LANES), lambda i: (i, 0, 0)),
        ],
        out_shape=[
            jax.ShapeDtypeStruct((n, D_MODEL), F32),
            jax.ShapeDtypeStruct((n, ROUTER_PAD), F32),
            jax.ShapeDtypeStruct((n, LANES), F32),
            jax.ShapeDtypeStruct((n // tm, SUBLANES, LANES), jnp.int32),
        ],
        scratch_shapes=[pltpu.VMEM((tm, tm), BF16)],
        compiler_params=pltpu.CompilerParams(
            dimension_semantics=("arbitrary",), vmem_limit_bytes=VMEM_LIMIT),
        name="outproj_ln_route_sort",
    )(y_a, y_b, x2, wa, wb, ln_g, ln_b, w_router, b_router)


def _route(logits_t):
    row = lax.broadcasted_iota(jnp.int32, logits_t.shape, 0)
    n_rows = logits_t.shape[0]
    is_group = (row >= GROUP_LANE0) & (row < GROUP_LANE0 + N_GROUPS)
    gl = jnp.where(is_group, logits_t, F32_LOWEST)
    g_max = jnp.max(gl, axis=0, keepdims=True)
    g_first = jnp.min(jnp.where(gl == g_max, row, n_rows), axis=0, keepdims=True)
    g_exp = jnp.where(is_group, jnp.exp(gl - g_max), 0.0)
    p_g = 1.0 / jnp.sum(g_exp, axis=0, keepdims=True)
    g_idx = g_first - GROUP_LANE0
    in_group = (row >= g_idx * EXPERTS_PER_GROUP) & (row < (g_idx + 1) * EXPERTS_PER_GROUP)
    el = jnp.where(in_group, logits_t, F32_LOWEST)
    e_max = jnp.max(el, axis=0, keepdims=True)
    e_exp = jnp.where(in_group, jnp.exp(el - e_max), 0.0)
    e_prob = e_exp / jnp.sum(e_exp, axis=0, keepdims=True)
    cand = jnp.where(in_group, e_prob, -1.0)
    v1 = jnp.max(cand, axis=0, keepdims=True)
    i1 = jnp.min(jnp.where(cand == v1, row, n_rows), axis=0, keepdims=True)
    pick1 = row == i1
    cand2 = jnp.where(pick1, -1.0, cand)
    v2 = jnp.max(cand2, axis=0, keepdims=True)
    i2 = jnp.min(jnp.where(cand2 == v2, row, n_rows), axis=0, keepdims=True)
    pick2 = row == i2
    denom = v1 + v2
    gates = jnp.where(pick1, v1 / denom * p_g, jnp.where(pick2, v2 / denom * p_g, 0.0))
    return gates, g_idx


def _route_sort_tile(logits, gates_ref, pos_ref, flags_ref, later_ref):
    tm = logits.shape[0]

    @pl.when(pl.program_id(0) == 0)
    def _():
        row = lax.broadcasted_iota(jnp.int32, (tm, tm), 0)
        col = lax.broadcasted_iota(jnp.int32, (tm, tm), 1)
        later_ref[...] = jnp.where(row < col, 1.0, 0.0).astype(BF16)

    gates_t, g_idx = _route(logits.T[:ROUTER_ROWS])
    own = lax.broadcasted_iota(jnp.int32, (SUBLANES, tm), 0) == g_idx
    onehot = jnp.where(own, 1.0, 0.0)
    rank = _dot(onehot.astype(BF16), later_ref[...])
    count = jnp.broadcast_to(jnp.sum(onehot, axis=1, keepdims=True), (SUBLANES, LANES))
    starts, running = [], jnp.zeros((1, LANES), F32)
    for g in range(SUBLANES):
        starts.append(running)
        running = running + count[g:g + 1]
    start = jnp.concatenate(starts, axis=0)
    pos = jnp.sum(jnp.where(own, rank + start[:, 0:1], 0.0), axis=0, keepdims=True)
    gates_ref[...] = jnp.concatenate([gates_t, jnp.zeros((LANES - ROUTER_ROWS, tm), F32)], axis=0).T
    pos_ref[...] = jnp.broadcast_to(pos, (LANES, tm)).T
    lane = lax.broadcasted_iota(jnp.int32, (SUBLANES, LANES), 1)
    sub_lo = lane * MOE_SUB
    hit = (count > 0.0) & (start < (sub_lo + MOE_SUB).astype(F32)) & (start + count > sub_lo.astype(F32))
    start_i = start.astype(jnp.int32)
    count_i = count.astype(jnp.int32)
    choice = jnp.zeros_like(start_i)
    first = jnp.zeros_like(start_i)
    for k in reversed(range(len(MOE_WINDOWS))):
        win = jnp.minimum((start_i // MOE_WINDOW_ALIGN) * MOE_WINDOW_ALIGN, tm - MOE_WINDOWS[k])
        fits = (count_i > 0) & (start_i + count_i <= win + MOE_WINDOWS[k])
        choice = jnp.where(fits, k + 1, choice)
        first = jnp.where(fits, win, first)
    meta = jnp.where(lane == MOE_META_FITS, choice,
                     jnp.where(lane == MOE_META_WINDOW, first, jnp.where(hit, 1, 0)))
    flags_ref[0] = meta.astype(jnp.int32)


def _moe_kernel(flags_ref, h_ref, gates_ref, pos_ref, w1_ref, w3_ref, w2_ref, g_ref, b_ref,
                o_ref, xs_ref, gs_ref, acc_ref, pt_ref):
    tile = pl.program_id(0)
    step = pl.program_id(1)
    tm = h_ref.shape[0]
    sub = MOE_SUB
    n_sub = tm // sub
    eps = MOE_EXPERTS_PER_STEP

    @pl.when(step == 0)
    def _():
        pos_b = pos_ref[...]
        pos_row = pos_b.T[0:1, :]
        g_hi, g_lo = _split_bf16(gates_ref[...], 2)
        src = jnp.concatenate([h_ref[...].astype(BF16), g_hi, g_lo], axis=1)
        for c0 in range(0, tm, sub):
            slot = (lax.broadcasted_iota(jnp.int32, (sub, tm), 0) + c0).astype(F32)
            p_c = jnp.where(slot == pos_row, 1.0, 0.0).astype(BF16)
            moved = _dot(p_c, src)
            xs_ref[c0:c0 + sub, :] = moved[:, :D_MODEL].astype(BF16)
            gs_ref[c0:c0 + sub, :] = moved[:, D_MODEL:D_MODEL + LANES] + moved[:, D_MODEL + LANES:]
        for c0 in range(0, tm, LANES):
            slot = (lax.broadcasted_iota(jnp.int32, (tm, LANES), 1) + c0).astype(F32)
            pt_ref[:, c0:c0 + LANES] = jnp.where(pos_b == slot, 1.0, 0.0).astype(BF16)
        acc_ref[...] = jnp.zeros_like(acc_ref)

    group = step // (EXPERTS_PER_GROUP // eps)

    def visit(rows):
        x_r = xs_ref[rows, :]
        g_r = gs_ref[rows, :]
        lane = lax.broadcasted_iota(jnp.int32, g_r.shape, 1)
        acc = acc_ref[rows, :]
        for e in range(eps):
            gate_e = jnp.sum(jnp.where(lane == step * eps + e, g_r, 0.0), axis=-1, keepdims=True)
            a1 = _dot(x_r, w1_ref[e])
            a3 = _dot(x_r, w3_ref[e])
            hid = (a1 * _sigmoid(a1)) * a3 * gate_e
            acc = acc + _dot(hid.astype(BF16), w2_ref[e])
        acc_ref[rows, :] = acc

    base = (tile * N_GROUPS + group) * (n_sub + 2)
    choice = flags_ref[base + n_sub]

    def visit_window(size):
        first = pl.multiple_of(flags_ref[base + n_sub + 1], MOE_WINDOW_ALIGN)
        visit(pl.ds(first, size))

    for k, size in enumerate(MOE_WINDOWS):
        pl.when(choice == k + 1)(functools.partial(visit_window, size))

    for r in range(n_sub):
        pl.when((choice == 0) & (flags_ref[base + r] != 0))(
            functools.partial(visit, slice(r * sub, (r + 1) * sub)))

    @pl.when(step == N_EXPERTS // eps - 1)
    def _():
        ffn = _dot(pt_ref[...], acc_ref[...].astype(BF16))
        o_ref[...] = _layer_norm(DEEPNORM_ALPHA * h_ref[...] + ffn, g_ref[...], b_ref[...])


def _moe_call(flags, h_f32, gates, pos, w1, w3, w2, ln_g, ln_b):
    n = h_f32.shape[0]
    tm = MOE_TM
    row = lambda t, s, f: (t, 0)
    const = lambda t, s, f: (0, 0)
    wmap = lambda t, s, f: (s, 0, 0)
    eps = MOE_EXPERTS_PER_STEP
    assert EXPERTS_PER_GROUP % eps == 0 and tm % MOE_SUB == 0
    grid_spec = pltpu.PrefetchScalarGridSpec(
        num_scalar_prefetch=1,
        grid=(n // tm, N_EXPERTS // eps),
        in_specs=[
            pl.BlockSpec((tm, D_MODEL), row),
            pl.BlockSpec((tm, ROUTER_PAD), row),
            pl.BlockSpec((tm, LANES), row),
            pl.BlockSpec((eps, D_MODEL, D_EXPERT), wmap),
            pl.BlockSpec((eps, D_MODEL, D_EXPERT), wmap),
            pl.BlockSpec((eps, D_EXPERT, D_MODEL), wmap),
            pl.BlockSpec((1, D_MODEL), const),
            pl.BlockSpec((1, D_MODEL), const),
        ],
        out_specs=pl.BlockSpec((tm, D_MODEL), row),
        scratch_shapes=[
            pltpu.VMEM((tm, D_MODEL), BF16),
            pltpu.VMEM((tm, ROUTER_PAD), F32),
            pltpu.VMEM((tm, D_MODEL), F32),
            pltpu.VMEM((tm, tm), BF16),
        ],
    )
    return pl.pallas_call(
        _moe_kernel,
        grid_spec=grid_spec,
        out_shape=jax.ShapeDtypeStruct((n, D_MODEL), F32),
        compiler_params=pltpu.CompilerParams(
            dimension_semantics=("arbitrary", "arbitrary"), vmem_limit_bytes=VMEM_LIMIT),
        name="hier_moe_ln",
    )(flags, h_f32, gates, pos, w1, w3, w2, ln_g, ln_b)


def _pad_cols(w, width):
    return jnp.pad(w, ((0, 0), (0, width - w.shape[1])))


def kernel(x, w_in, mu_shift, w0, w_lora_up, a0, a_lora_up, g_lora_up, k_k, k_a, r_k, gn_w, gn_b, w_out, ln1_g, ln1_b, w_group, b_group, w_expert, b_expert, w1_exp, w3_exp, w2_exp, ln2_g, ln2_b):
    batch, seq_len, d = x.shape
    assert d == D_MODEL
    n = batch * seq_len
    x2 = x.reshape(n, d)

    c_rkv = 3 * RWKV_WIDTH
    c_wd = c_rkv + DECAY_RANK
    c_ad = c_wd + AAA_RANK
    c_gd = c_ad + GATE_RANK
    w_cat = jnp.concatenate([
        w_in[:, :c_rkv], _pad_cols(w_in[:, c_rkv:c_gd], LORA_PAD), w_in[:, c_gd:],
    ], axis=1).astype(BF16)
    mu2 = mu_shift[None, :]
    mu_cat = jnp.concatenate([mu2[:, :c_rkv], _pad_cols(mu2[:, c_rkv:c_gd], LORA_PAD)], axis=1)
    p_rkv, p_lora, p_k, p_qvt = _inproj_call(x2, w_cat, mu_cat, seq_len)

    place = lambda w, first: jnp.pad(w, ((first, LORA_PAD - first - w.shape[0]), (0, 0)))
    lora_up = (place(w_lora_up, 0), place(a_lora_up, DECAY_RANK), place(g_lora_up, DECAY_RANK + AAA_RANK))

    vecs = jnp.stack([w0, a0, k_k, k_a, r_k.reshape(-1), gn_w, gn_b, jnp.zeros_like(w0)], axis=0)
    head_id = jnp.arange(2 * PAIR) // HEAD_DIM
    bd = (head_id[:, None] == head_id[None, :]).astype(BF16)
    y_a = _rwkv_call(p_rkv, p_lora, vecs, *lora_up, bd, batch, seq_len)

    y_b = _moba_call(p_k, p_qvt, _moba_key_aug(seq_len), batch, seq_len)

    w_out_b = w_out.astype(BF16)
    w_router = _pad_cols(jnp.concatenate([w_expert, w_group], axis=1), ROUTER_PAD)
    b_router = _pad_cols(jnp.concatenate([b_expert, b_group])[None, :], ROUTER_PAD)
    h1, gates, pos, flags = _outproj_call(y_a, y_b, x2, w_out_b[:RWKV_WIDTH], w_out_b[RWKV_WIDTH:],
                               ln1_g[None, :], ln1_b[None, :], w_router, b_router)

    flat = lambda w: w.astype(BF16).reshape((N_EXPERTS,) + w.shape[2:])
    flags = jnp.concatenate([flags[:, :N_GROUPS, :MOE_TM // MOE_SUB],
                             flags[:, :N_GROUPS, MOE_META_FITS:MOE_META_WINDOW + 1]], axis=-1).reshape(-1)
    out = _moe_call(flags, h1, gates, pos, flat(w1_exp), flat(w3_exp), flat(w2_exp),
                    ln2_g[None, :], ln2_b[None, :])
    return out.reshape(batch, seq_len, d)
```

```python
import functools
import math

import jax
import jax.numpy as jnp
import numpy as np
from jax import lax
from jax.experimental import pallas as pl
from jax.experimental.pallas import tpu as pltpu

F32 = jnp.float32
BF16 = jnp.bfloat16

D_MODEL = 1024
HEAD_DIM = 64
RWKV_WIDTH = 512
MOBA_WIDTH = 512
DECAY_RANK = 32
AAA_RANK = 32
GATE_RANK = 96
GN_EPS = 64e-5
L2_EPS = 1e-12
MOBA_BLOCK = 256
MOBA_TOPK = 3
N_GROUPS = 4
EXPERTS_PER_GROUP = 8
N_EXPERTS = N_GROUPS * EXPERTS_PER_GROUP
D_EXPERT = 256
LN_EPS = 1e-5
DEEPNORM_ALPHA = float(2.0 ** 0.25)
NEG_INF = -1e30
F32_LOWEST = -3.0e38

LANES = 128
SUBLANES = 8
BF16_SUBLANES = 16
MXU_TILE = 256
V7X_VMEM_BYTES = 64 * 1024 * 1024
PAIR = 2 * HEAD_DIM
N_PAIRS = RWKV_WIDTH // PAIR
LORA_PAD = MXU_TILE
RWKV_COLS_PAD = 3 * RWKV_WIDTH + LORA_PAD
IN_COLS_PAD = RWKV_COLS_PAD + 3 * MOBA_WIDTH
VMEM_LIMIT = V7X_VMEM_BYTES * 7 // 8

INPROJ_TM = 1024
INPROJ_TN = MXU_TILE
RWKV_CHUNK = 64
RWKV_CHUNKS_PER_STEP = 8
RWKV_PASSES = 1
RWKV_STATE_PASSES = 1
OUTPROJ_TM = 1024
MOE_TM = 1024
MOE_EXPERTS_PER_STEP = 8
MOE_SUB = 256
MOE_WINDOWS = (272, 320)
MOE_WINDOW_ALIGN = BF16_SUBLANES
MOE_META_FITS = MOE_TM // MOE_SUB
MOE_META_WINDOW = MOE_META_FITS + 1
MOBA_KV_TILE = 512
MOBA_Q_TILE = 2048
MOBA_V_ROWS = HEAD_DIM + BF16_SUBLANES
MOBA_ALIBI_PARTS = 3
LOG2_E = 1.4426950408889634
ROUTER_PAD = LANES
GROUP_LANE0 = N_EXPERTS
ROUTER_ROWS = 40

NN = (((1,), (0,)), ((), ()))


def _dot(a, b, dims=NN):
    return lax.dot_general(a, b, dims, preferred_element_type=F32)


def _split_bf16(x, parts):
    out = []
    rem = x
    for i in range(parts):
        p = rem.astype(BF16)
        out.append(p)
        if i + 1 < parts:
            rem = rem - p.astype(F32)
    return out


def _mm(a, b, dims=NN, passes=3):
    if passes == 1:
        return _dot(a.astype(BF16), b.astype(BF16), dims)
    assert passes == 3
    a_hi, a_lo = _split_bf16(a, 2)
    b_hi, b_lo = _split_bf16(b, 2)
    return _dot(a_hi, b_hi, dims) + (_dot(a_hi, b_lo, dims) + _dot(a_lo, b_hi, dims))


def _mm_exact_lhs(a_bf16, b, dims=NN, parts=3):
    out = None
    for piece in reversed(_split_bf16(b, parts)):
        term = _dot(a_bf16, piece, dims)
        out = term if out is None else term + out
    return out


def _mm_exact_rhs(a, b_bf16, dims=NN):
    a1, a2, a3 = _split_bf16(a, 3)
    return _dot(a1, b_bf16, dims) + (_dot(a2, b_bf16, dims) + _dot(a3, b_bf16, dims))


def _inproj_kernel(x_ref, w_ref, mu_ref, prkv_ref, plora_ref, pk_ref, qvt_ref, carry_ref, *, tiles_per_seq):
    tm = x_ref.shape[0]
    xb = x_ref[...].astype(BF16)
    seq_start = (pl.program_id(0) % tiles_per_seq) == 0
    row0 = lax.broadcasted_iota(jnp.int32, (tm, INPROJ_TN), 0) == 0
    n_shift_tiles = RWKV_COLS_PAD // INPROJ_TN
    for j in range(n_shift_tiles):
        c0 = j * INPROJ_TN
        acc = _dot(xb, w_ref[:, c0:c0 + INPROJ_TN])
        prev_last = jnp.where(seq_start, 0.0, carry_ref[0:1, c0:c0 + INPROJ_TN])
        shifted = jnp.where(row0, prev_last, pltpu.roll(acc, 1, 0))
        carry_ref[0:1, c0:c0 + INPROJ_TN] = acc[tm - 1:tm, :]
        out = acc + (shifted - acc) * mu_ref[:, c0:c0 + INPROJ_TN]
        if c0 < 3 * RWKV_WIDTH:
            prkv_ref[:, c0:c0 + INPROJ_TN] = out
        else:
            plora_ref[:, c0 - 3 * RWKV_WIDTH:c0 - 3 * RWKV_WIDTH + INPROJ_TN] = out
    tiles_per_part = MOBA_WIDTH // INPROJ_TN
    for j in range(3 * tiles_per_part):
        c0 = j * INPROJ_TN
        acc = _dot(xb, w_ref[:, RWKV_COLS_PAD + c0:RWKV_COLS_PAD + c0 + INPROJ_TN])
        part, r0 = divmod(c0, MOBA_WIDTH)
        if part == 1:
            pk_ref[:, r0:r0 + INPROJ_TN] = acc.astype(BF16)
        else:
            r0 += (part // 2) * MOBA_WIDTH
            qvt_ref[r0:r0 + INPROJ_TN, :] = acc.T.astype(BF16)


def _inproj_call(x2, w_cat, mu_cat, seq_len):
    n = x2.shape[0]
    tm = INPROJ_TM
    assert seq_len % tm == 0 and (3 * RWKV_WIDTH) % INPROJ_TN == 0
    return pl.pallas_call(
        functools.partial(_inproj_kernel, tiles_per_seq=seq_len // tm),
        grid=(n // tm,),
        in_specs=[
            pl.BlockSpec((tm, D_MODEL), lambda i: (i, 0)),
            pl.BlockSpec((D_MODEL, IN_COLS_PAD), lambda i: (0, 0)),
            pl.BlockSpec((1, RWKV_COLS_PAD), lambda i: (0, 0)),
        ],
        out_specs=[
            pl.BlockSpec((tm, 3 * RWKV_WIDTH), lambda i: (i, 0)),
            pl.BlockSpec((tm, LORA_PAD), lambda i: (i, 0)),
            pl.BlockSpec((tm, MOBA_WIDTH), lambda i: (i, 0)),
            pl.BlockSpec((2 * MOBA_WIDTH, tm), lambda i: (0, i)),
        ],
        out_shape=[
            jax.ShapeDtypeStruct((n, 3 * RWKV_WIDTH), F32),
            jax.ShapeDtypeStruct((n, LORA_PAD), F32),
            jax.ShapeDtypeStruct((n, MOBA_WIDTH), BF16),
            jax.ShapeDtypeStruct((2 * MOBA_WIDTH, n), BF16),
        ],
        scratch_shapes=[pltpu.VMEM((SUBLANES, RWKV_COLS_PAD), F32)],
        compiler_params=pltpu.CompilerParams(
            dimension_semantics=("arbitrary",), vmem_limit_bytes=VMEM_LIMIT),
        name="inproj_shift",
    )(x2, w_cat, mu_cat)


def _sigmoid(z):
    return 1.0 / (1.0 + jnp.exp(-z))


def _rwkv_chunks(rt, kt, at, bt, v, d_incl, s_prev, passes, state_passes):
    c = RWKV_CHUNK
    n_chunks = rt.shape[0] // c
    n_pairs = len(s_prev)
    row = lax.broadcasted_iota(jnp.int32, (c, PAIR), 0)
    col = lax.broadcasted_iota(jnp.int32, (c, PAIR), 1) % HEAD_DIM
    strict = row > col
    incl = row >= col
    eye_c = (row == col).astype(F32)
    lane = lax.broadcasted_iota(jnp.int32, (1, PAIR), 1)
    head0 = lane < HEAD_DIM
    head1 = jnp.logical_not(head0)
    prow = lax.broadcasted_iota(jnp.int32, (PAIR, PAIR), 0)
    pcol = lax.broadcasted_iota(jnp.int32, (PAIR, PAIR), 1)
    same_head = (prow < HEAD_DIM) == (pcol < HEAD_DIM)
    eye_p = (prow == pcol).astype(F32)
    rows = [slice(ci * c, (ci + 1) * c) for ci in range(n_chunks)]
    sl = [slice(p * PAIR, (p + 1) * PAIR) for p in range(n_pairs)]
    pairs = [(ci, p) for ci in range(n_chunks) for p in range(n_pairs)]
    cut = lambda t, ci, p: t[rows[ci], sl[p]]

    rhs_dtype = BF16 if passes == 1 else F32

    def by_head(m, dtype=rhs_dtype):
        m = m.astype(dtype)
        zero = jnp.zeros_like(m)
        return jnp.concatenate([jnp.where(head0, m, zero), jnp.where(head1, m, zero)], axis=0)

    def by_head2(m, n):
        return jnp.concatenate([by_head(m), by_head(n)], axis=1)

    at_p = {k_: cut(at, *k_) for k_ in pairs}
    rt_p = {k_: cut(rt, *k_) for k_ in pairs}
    bt_p = {k_: cut(bt, *k_) for k_ in pairs}
    kt_p = {k_: cut(kt, *k_) for k_ in pairs}
    v_p = {k_: cut(v, *k_) for k_ in pairs}

    z = {k_: _mm(jnp.concatenate([at_p[k_], rt_p[k_]], axis=0),
                 jnp.concatenate([by_head(bt_p[k_], F32).T, by_head(kt_p[k_], F32).T], axis=1), NN, passes)
         for k_ in pairs}
    l_ab = {k_: jnp.where(strict, z[k_][:c, :PAIR], 0.0) for k_ in pairs}
    l_ak = {k_: jnp.where(strict, z[k_][:c, PAIR:], 0.0) for k_ in pairs}
    m_rb = {k_: jnp.where(incl, z[k_][c:, :PAIR], 0.0) for k_ in pairs}
    m_rk = {k_: jnp.where(incl, z[k_][c:, PAIR:], 0.0) for k_ in pairs}
    pw = {k_: _mm(l_ab[k_], by_head(l_ab[k_]), NN, passes) for k_ in pairs}
    t_inv = {k_: eye_c + l_ab[k_] for k_ in pairs}
    for _ in range(int(math.log2(c)) - 2):
        tp = {k_: _mm(jnp.concatenate([t_inv[k_], pw[k_]], axis=0), by_head(pw[k_]), NN, passes)
              for k_ in pairs}
        t_inv = {k_: t_inv[k_] + tp[k_][:c] for k_ in pairs}
        pw = {k_: tp[k_][c:] for k_ in pairs}
    t_inv = {k_: t_inv[k_] + _mm(t_inv[k_], by_head(pw[k_]), NN, passes) for k_ in pairs}
    lm = {k_: _mm(jnp.concatenate([l_ak[k_], m_rk[k_]], axis=0), by_head(v_p[k_]), NN, passes)
          for k_ in pairs}
    lv = {k_: lm[k_][:c] for k_ in pairs}
    mv = {k_: lm[k_][c:] for k_ in pairs}
    wu = {k_: _mm(t_inv[k_], by_head2(at_p[k_], lv[k_]), NN, passes) for k_ in pairs}
    qy = {k_: _mm(m_rb[k_], by_head2(wu[k_][:, :PAIR], wu[k_][:, PAIR:]), NN, passes) for k_ in pairs}

    qeff, y1, phi, psi = {}, {}, {}, {}
    for ci, p in pairs:
        k_ = (ci, p)
        w, u0 = wu[k_][:, :PAIR], wu[k_][:, PAIR:]
        qeff[k_] = rt_p[k_] + qy[k_][:, :PAIR]
        y1[k_] = qy[k_][:, PAIR:] + mv[k_]
        d_p = d_incl[(ci + 1) * c - 1:(ci + 1) * c, sl[p]]
        phi[k_] = jnp.where(same_head, (eye_p + _mm(w.T, bt_p[k_], NN, passes)) * d_p, 0.0)
        uv_t = jnp.concatenate([u0, v_p[k_]], axis=0).T
        bk = jnp.concatenate([bt_p[k_], kt_p[k_]], axis=0)
        psi[k_] = jnp.where(same_head, _mm(uv_t, bk, NN, passes) * d_p, 0.0)

    state = list(s_prev)
    ys = [[None] * n_pairs for _ in range(n_chunks)]
    for ci in range(n_chunks):
        for p in range(n_pairs):
            ys[ci][p] = _mm(qeff[ci, p], state[p].T, NN, state_passes) + y1[ci, p]
            state[p] = _mm(state[p], phi[ci, p], NN, state_passes) + psi[ci, p]
    y = jnp.concatenate([jnp.concatenate(ys[ci], axis=1) for ci in range(n_chunks)], axis=0)
    return y, state


def _rwkv_kernel(prkv_ref, plora_ref, vec_ref, wl_ref, al_ref, gl_ref, bd_ref, y_ref, s_ref):
    rows = prkv_ref.shape[0]
    c = RWKV_CHUNK
    width = RWKV_WIDTH

    @pl.when(pl.program_id(1) == 0)
    def _():
        s_ref[...] = jnp.zeros_like(s_ref)

    r = prkv_ref[:, 0:width]
    k_raw = prkv_ref[:, width:2 * width]
    v = prkv_ref[:, 2 * width:3 * width]
    p_wd = p_ad = p_gd = plora_ref[...]
    w0 = vec_ref[0:1, :]
    a0 = vec_ref[1:2, :]
    k_k = vec_ref[2:3, :]
    k_a = vec_ref[3:4, :]
    r_k = vec_ref[4:5, :]
    gn_w = vec_ref[5:6, :]
    gn_b = vec_ref[6:7, :]
    bd = bd_ref[...]

    def seg_sum(z):
        halves = []
        for c0 in range(0, width, bd.shape[0]):
            halves.append(_dot(z[:, c0:c0 + bd.shape[0]].astype(BF16), bd))
        return jnp.concatenate(halves, axis=1)

    log_w = -math.exp(-0.5) * _sigmoid(w0 + _mm(jnp.tanh(p_wd), wl_ref[...], NN, 3))
    a = _sigmoid(a0 + _mm(p_ad, al_ref[...], NN, RWKV_PASSES))
    g = _mm(_sigmoid(p_gd), gl_ref[...], NN, RWKV_PASSES)
    kk = k_raw * k_k
    kk = kk * lax.rsqrt(jnp.maximum(seg_sum(kk * kk), L2_EPS * L2_EPS))
    k = k_raw * (1.0 + (a - 1.0) * k_a)

    row = lax.broadcasted_iota(jnp.int32, (c, c), 0)
    col = lax.broadcasted_iota(jnp.int32, (c, c), 1)
    tri = jnp.where(row >= col, 1.0, 0.0).astype(BF16)
    cum = jnp.concatenate([_mm_exact_lhs(tri, log_w[c0:c0 + c], parts=2)
                           for c0 in range(0, rows, c)], axis=0)
    d_incl = jnp.exp(cum)
    d_inv = jnp.exp(-cum)
    d_excl = jnp.exp(cum - log_w)
    rt = r * d_incl
    kt = k * d_inv
    at = -kk * d_excl
    bt = kk * a * d_inv

    y, s_next = _rwkv_chunks(rt, kt, at, bt, v, d_incl, [s_ref[p] for p in range(N_PAIRS)],
                             RWKV_PASSES, RWKV_STATE_PASSES)
    for p in range(N_PAIRS):
        s_ref[p] = s_next[p]

    inv_n = 1.0 / HEAD_DIM
    mu = seg_sum(y) * inv_n
    yc = y - mu
    var = seg_sum(yc * yc) * inv_n
    yn = yc * lax.rsqrt(var + GN_EPS) * gn_w + gn_b
    bonus = seg_sum(r * k * r_k) * v
    y_ref[...] = ((yn + bonus) * g).astype(y_ref.dtype)


def _rwkv_call(p_rkv, p_lora, vecs, wl, al, gl, bd, batch, seq_len):
    n = p_rkv.shape[0]
    rows = RWKV_CHUNK * RWKV_CHUNKS_PER_STEP
    assert seq_len % rows == 0
    steps = seq_len // rows
    row_map = lambda b, i: (b * steps + i, 0)
    const = lambda b, i: (0, 0)
    return pl.pallas_call(
        _rwkv_kernel,
        grid=(batch, steps),
        in_specs=[
            pl.BlockSpec((rows, 3 * RWKV_WIDTH), row_map),
            pl.BlockSpec((rows, LORA_PAD), row_map),
            pl.BlockSpec((SUBLANES, RWKV_WIDTH), const),
            pl.BlockSpec((LORA_PAD, RWKV_WIDTH), const),
            pl.BlockSpec((LORA_PAD, RWKV_WIDTH), const),
            pl.BlockSpec((LORA_PAD, RWKV_WIDTH), const),
            pl.BlockSpec((2 * PAIR, 2 * PAIR), const),
        ],
        out_specs=pl.BlockSpec((rows, RWKV_WIDTH), row_map),
        out_shape=jax.ShapeDtypeStruct((n, RWKV_WIDTH), BF16),
        scratch_shapes=[pltpu.VMEM((N_PAIRS, PAIR, PAIR), F32)],
        compiler_params=pltpu.CompilerParams(
            dimension_semantics=("arbitrary", "arbitrary"), vmem_limit_bytes=VMEM_LIMIT),
        name="rwkv7_chunked",
    )(p_rkv, p_lora, vecs, wl, al, gl, bd)


def _moba_kernel(qt_ref, k_ref, vt_in_ref, kaug_ref, o_ref, kmean_ref, vt_ref,
                 m0_ref, m1_ref, acc0_ref, acc1_ref, s_even_ref, s_odd_ref,
                 smax_even_ref, smax_odd_ref, *, n_blocks):
    blk = MOBA_BLOCK
    tk = MOBA_KV_TILE
    tq = qt_ref.shape[1]
    i = pl.program_id(2)
    nb_pad = kmean_ref.shape[0]
    m_refs, acc_refs = (m0_ref, m1_ref), (acc0_ref, acc1_ref)

    @pl.when(i == 0)
    def _():
        kmean_ref[...] = jnp.zeros_like(kmean_ref)

        def mean_body(n, carry):
            off = pl.multiple_of(n * blk, blk)
            kb = k_ref[pl.ds(off, blk), :].astype(F32)
            kmean_ref[pl.ds(n, 1), :] = jnp.sum(kb, axis=0, keepdims=True) * (1.0 / blk)
            return carry
        lax.fori_loop(0, n_blocks, mean_body, 0)

        ones = jnp.ones((MOBA_V_ROWS - HEAD_DIM, tk), BF16)

        for j in range(vt_ref.shape[0]):
            for h in (0, 1):
                vt_ref[j, h] = jnp.concatenate(
                    [vt_in_ref[h * HEAD_DIM:(h + 1) * HEAD_DIM, j * tk:(j + 1) * tk], ones], axis=0)

    q_t = qt_ref[...].astype(F32)
    chan = lax.broadcasted_iota(jnp.int32, (PAIR, tq), 0)
    blk_row = lax.broadcasted_iota(jnp.int32, (nb_pad, tq), 0)
    own_blk = (i * tq + lax.broadcasted_iota(jnp.int32, (nb_pad, tq), 1)) // blk
    past = blk_row < own_blk
    aug_row = lax.broadcasted_iota(jnp.int32, (LANES, tq), 0)
    ones_rows = (aug_row >= n_blocks) & (aug_row < n_blocks + MOBA_ALIBI_PARTS)
    kmean = kmean_ref[...]

    qa_t = []
    for h in (0, 1):
        qh_t = jnp.where((chan < HEAD_DIM) == (h == 0), q_t, 0.0)
        gate = _mm_exact_rhs(kmean, qh_t.astype(BF16))
        gate = jnp.where(past, gate, F32_LOWEST)
        sel = jnp.zeros(gate.shape, jnp.bool_)
        for _ in range(MOBA_TOPK):
            mx = jnp.max(gate, axis=0, keepdims=True)
            first = jnp.min(jnp.where(gate == mx, blk_row, nb_pad), axis=0, keepdims=True)
            pick = (blk_row == first) & (mx > F32_LOWEST)
            sel = sel | pick
            gate = jnp.where(pick, F32_LOWEST, gate)
        sel_bias = jnp.where(past & jnp.logical_not(sel), NEG_INF, 0.0)
        aug_t = jnp.concatenate([sel_bias, jnp.zeros((LANES - nb_pad, tq), F32)], axis=0)
        aug_t = jnp.where(ones_rows, 1.0, aug_t)
        qa_t.append(jnp.concatenate([qh_t * (LOG2_E / math.sqrt(HEAD_DIM)), aug_t], axis=0).astype(BF16))

    def tile_scores(j, lanes=slice(None)):
        off = pl.multiple_of(j * tk, tk)
        k_t = k_ref[pl.ds(off, tk), :]
        return [_dot(jnp.concatenate([k_t, kaug_ref[h, pl.ds(off, tk), :]], axis=1), qa_t[h][:, lanes])
                for h in (0, 1)]

    def put_scores(buf, s, lanes=slice(None)):
        for h in (0, 1):
            buf[0][h, :, lanes] = s[h]
            buf[1][h, :, lanes] = jnp.max(s[h], axis=0, keepdims=True)

    def tile_update(j, buf, lanes=slice(None)):
        s_buf, smax_buf = buf
        for h in (0, 1):
            m_old = m_refs[h][:, lanes]
            m_new = jnp.maximum(m_old, smax_buf[h, :, lanes])
            p = jnp.exp2(s_buf[h, :, lanes] - m_new).astype(BF16)
            pv = _dot(vt_ref[j, h], p)
            acc_refs[h][:, lanes] = jnp.exp2(m_old - m_new) * acc_refs[h][:, lanes] + pv
            m_refs[h][:, lanes] = m_new

    for h in (0, 1):
        m_refs[h][...] = jnp.full(m_refs[h].shape, F32_LOWEST, F32)
        acc_refs[h][...] = jnp.zeros(acc_refs[h].shape, F32)

    buffers = ((s_even_ref, smax_even_ref), (s_odd_ref, smax_odd_ref))
    n_own = tq // tk
    j_first = i * n_own
    diagonal = (lax.broadcasted_iota(jnp.int32, (tk, tk), 0) <= lax.broadcasted_iota(jnp.int32, (tk, tk), 1))

    def own_scores(g):
        s = tile_scores(j_first + g, slice(g * tk, tq))
        own = [jnp.where(diagonal, s_h[:, :tk], NEG_INF) for s_h in s]
        if g == n_own - 1:
            return own
        return [jnp.concatenate([own_h, s_h[:, tk:]], axis=1) for own_h, s_h in zip(own, s)]

    put_scores(buffers[0], own_scores(n_own - 1), slice((n_own - 1) * tk, tq))
    for m in range(1, n_own):
        g = n_own - 1 - m
        put_scores(buffers[m % 2], own_scores(g), slice(g * tk, tq))
        tile_update(j_first + g + 1, buffers[(m - 1) % 2], slice((g + 1) * tk, tq))
    cur, nxt = buffers[(n_own - 1) % 2], buffers[n_own % 2]

    def previous_tile(j):
        return jnp.where(j == 0, j_first, j - 1)

    def pipelined_step(j, src, dst):
        put_scores(dst, tile_scores(j))
        tile_update(previous_tile(j), src)

    def kv_pair_step(u, carry):
        pipelined_step(2 * u, cur, nxt)
        pipelined_step(2 * u + 1, nxt, cur)
        return carry
    lax.fori_loop(0, j_first // 2, kv_pair_step, 0)

    @pl.when(j_first % 2 == 1)
    def _():
        pipelined_step(j_first - 1, cur, nxt)
        tile_update(j_first - 1, nxt)

    @pl.when(j_first % 2 == 0)
    def _():
        tile_update(previous_tile(j_first), cur)

    out_t = jnp.concatenate([acc_refs[h][0:HEAD_DIM, :] / acc_refs[h][HEAD_DIM:HEAD_DIM + 1, :]
                             for h in (0, 1)], axis=0)
    o_ref[...] = out_t.T.astype(o_ref.dtype)


def _moba_call(pk, qvt, kaug, batch, seq_len):
    n = pk.shape[0]
    blk = MOBA_BLOCK
    tq = MOBA_Q_TILE
    nb = seq_len // blk
    nq = seq_len // tq
    assert nb + MOBA_ALIBI_PARTS <= LANES and seq_len % MOBA_KV_TILE == 0
    assert tq % MOBA_KV_TILE == 0 and seq_len % tq == 0
    lane_groups = MOBA_WIDTH // LANES
    return pl.pallas_call(
        functools.partial(_moba_kernel, n_blocks=nb),
        grid=(batch, N_PAIRS, nq),
        in_specs=[
            pl.BlockSpec((PAIR, tq), lambda b, p, i: (p, b * nq + i)),
            pl.BlockSpec((seq_len, PAIR), lambda b, p, i: (b, p)),
            pl.BlockSpec((PAIR, seq_len), lambda b, p, i: (lane_groups + p, b)),
            pl.BlockSpec((2, seq_len, LANES), lambda b, p, i: (p, 0, 0)),
        ],
        out_specs=pl.BlockSpec((tq, PAIR), lambda b, p, i: (b * nq + i, p)),
        out_shape=jax.ShapeDtypeStruct((n, MOBA_WIDTH), BF16),
        scratch_shapes=[
            pltpu.VMEM((-(-nb // 8) * 8, PAIR), F32),
            pltpu.VMEM((seq_len // MOBA_KV_TILE, 2, MOBA_V_ROWS, MOBA_KV_TILE), BF16),
            pltpu.VMEM((1, tq), F32), pltpu.VMEM((1, tq), F32),
            pltpu.VMEM((MOBA_V_ROWS, tq), F32), pltpu.VMEM((MOBA_V_ROWS, tq), F32),
            pltpu.VMEM((2, MOBA_KV_TILE, tq), F32), pltpu.VMEM((2, MOBA_KV_TILE, tq), F32),
            pltpu.VMEM((2, 1, tq), F32), pltpu.VMEM((2, 1, tq), F32),
        ],
        compiler_params=pltpu.CompilerParams(
            dimension_semantics=("arbitrary", "arbitrary", "arbitrary"),
            vmem_limit_bytes=VMEM_LIMIT),
        name="moba_attention",
    )(qvt, pk, qvt, kaug)


def _moba_key_aug(seq_len):
    nb = seq_len // MOBA_BLOCK
    heads = MOBA_WIDTH // HEAD_DIM
    pos = np.arange(seq_len, dtype=np.int32)
    slopes = (2.0 ** (-8.0 * (np.arange(heads, dtype=np.float32) + 1.0) / heads)).astype(np.float32)
    aug = np.zeros((heads, seq_len, LANES), np.float32)
    aug[:, pos, pos // MOBA_BLOCK] = 1.0
    rem = (np.float32(LOG2_E) * slopes)[:, None] * pos.astype(np.float32)[None, :]
    for part in range(MOBA_ALIBI_PARTS):
        piece = (rem.view(np.uint32) & np.uint32(0xFFFF0000)).view(np.float32)
        aug[:, :, nb + part] = piece
        rem = rem - piece
    return jnp.asarray(aug.astype(BF16))


def _layer_norm(z, g, b):
    mu = jnp.mean(z, axis=-1, keepdims=True)
    zc = z - mu
    var = jnp.mean(zc * zc, axis=-1, keepdims=True)
    return zc * lax.rsqrt(var + LN_EPS) * g + b


def _outproj_kernel(ya_ref, yb_ref, x_ref, wa_ref, wb_ref, g_ref, b_ref, wr_ref, br_ref,
                    h_ref, gates_ref, pos_ref, flags_ref, later_ref):
    mix = _dot(ya_ref[...], wa_ref[...]) + _dot(yb_ref[...], wb_ref[...])
    h = _layer_norm(DEEPNORM_ALPHA * x_ref[...] + mix, g_ref[...], b_ref[...])
    h_ref[...] = h
    h_hi, h_lo = _split_bf16(h, 2)
    w_hi, w_lo = _split_bf16(wr_ref[...], 2)
    hh_hl = _dot(h_hi, jnp.concatenate([w_hi, w_lo], axis=1))
    logits = hh_hl[:, :ROUTER_PAD] + (hh_hl[:, ROUTER_PAD:] + _dot(h_lo, w_hi)) + br_ref[...]
    _route_sort_tile(logits, gates_ref, pos_ref, flags_ref, later_ref)


def _outproj_call(y_a, y_b, x2, wa, wb, ln_g, ln_b, w_router, b_router):
    n = x2.shape[0]
    tm = OUTPROJ_TM
    assert tm == MOE_TM
    row = lambda i: (i, 0)
    const = lambda i: (0, 0)
    return pl.pallas_call(
        _outproj_kernel,
        grid=(n // tm,),
        in_specs=[
            pl.BlockSpec((tm, RWKV_WIDTH), row),
            pl.BlockSpec((tm, MOBA_WIDTH), row),
            pl.BlockSpec((tm, D_MODEL), row),
            pl.BlockSpec((RWKV_WIDTH, D_MODEL), const),
            pl.BlockSpec((MOBA_WIDTH, D_MODEL), const),
            pl.BlockSpec((1, D_MODEL), const),
            pl.BlockSpec((1, D_MODEL), const),
            pl.BlockSpec((D_MODEL, ROUTER_PAD), const),
            pl.BlockSpec((1, ROUTER_PAD), const),
        ],
        out_specs=[
            pl.BlockSpec((tm, D_MODEL), row),
            pl.BlockSpec((tm, ROUTER_PAD), row),
            pl.BlockSpec((tm, LANES), row),
            pl.BlockSpec((1, SUBLANES, LANES), lambda i: (i, 0, 0)),
        ],
        out_shape=[
            jax.ShapeDtypeStruct((n, D_MODEL), F32),
            jax.ShapeDtypeStruct((n, ROUTER_PAD), F32),
            jax.ShapeDtypeStruct((n, LANES), F32),
            jax.ShapeDtypeStruct((n // tm, SUBLANES, LANES), jnp.int32),
        ],
        scratch_shapes=[pltpu.VMEM((tm, tm), BF16)],
        compiler_params=pltpu.CompilerParams(
            dimension_semantics=("arbitrary",), vmem_limit_bytes=VMEM_LIMIT),
        name="outproj_ln_route_sort",
    )(y_a, y_b, x2, wa, wb, ln_g, ln_b, w_router, b_router)


def _route(logits_t):
    row = lax.broadcasted_iota(jnp.int32, logits_t.shape, 0)
    n_rows = logits_t.shape[0]
    is_group = (row >= GROUP_LANE0) & (row < GROUP_LANE0 + N_GROUPS)
    gl = jnp.where(is_group, logits_t, F32_LOWEST)
    g_max = jnp.max(gl, axis=0, keepdims=True)
    g_first = jnp.min(jnp.where(gl == g_max, row, n_rows), axis=0, keepdims=True)
    g_exp = jnp.where(is_group, jnp.exp(gl - g_max), 0.0)
    p_g = 1.0 / jnp.sum(g_exp, axis=0, keepdims=True)
    g_idx = g_first - GROUP_LANE0
    in_group = (row >= g_idx * EXPERTS_PER_GROUP) & (row < (g_idx + 1) * EXPERTS_PER_GROUP)
    el = jnp.where(in_group, logits_t, F32_LOWEST)
    e_max = jnp.max(el, axis=0, keepdims=True)
    e_exp = jnp.where(in_group, jnp.exp(el - e_max), 0.0)
    e_prob = e_exp / jnp.sum(e_exp, axis=0, keepdims=True)
    cand = jnp.where(in_group, e_prob, -1.0)
    v1 = jnp.max(cand, axis=0, keepdims=True)
    i1 = jnp.min(jnp.where(cand == v1, row, n_rows), axis=0, keepdims=True)
    pick1 = row == i1
    cand2 = jnp.where(pick1, -1.0, cand)
    v2 = jnp.max(cand2, axis=0, keepdims=True)
    i2 = jnp.min(jnp.where(cand2 == v2, row, n_rows), axis=0, keepdims=True)
    pick2 = row == i2
    denom = v1 + v2
    gates = jnp.where(pick1, v1 / denom * p_g, jnp.where(pick2, v2 / denom * p_g, 0.0))
    return gates, g_idx


def _route_sort_tile(logits, gates_ref, pos_ref, flags_ref, later_ref):
    tm = logits.shape[0]

    @pl.when(pl.program_id(0) == 0)
    def _():
        row = lax.broadcasted_iota(jnp.int32, (tm, tm), 0)
        col = lax.broadcasted_iota(jnp.int32, (tm, tm), 1)
        later_ref[...] = jnp.where(row < col, 1.0, 0.0).astype(BF16)

    gates_t, g_idx = _route(logits.T[:ROUTER_ROWS])
    own = lax.broadcasted_iota(jnp.int32, (SUBLANES, tm), 0) == g_idx
    onehot = jnp.where(own, 1.0, 0.0)
    rank = _dot(onehot.astype(BF16), later_ref[...])
    count = jnp.broadcast_to(jnp.sum(onehot, axis=1, keepdims=True), (SUBLANES, LANES))
    starts, running = [], jnp.zeros((1, LANES), F32)
    for g in range(SUBLANES):
        starts.append(running)
        running = running + count[g:g + 1]
    start = jnp.concatenate(starts, axis=0)
    pos = jnp.sum(jnp.where(own, rank + start[:, 0:1], 0.0), axis=0, keepdims=True)
    gates_ref[...] = jnp.concatenate([gates_t, jnp.zeros((LANES - ROUTER_ROWS, tm), F32)], axis=0).T
    pos_ref[...] = jnp.broadcast_to(pos, (LANES, tm)).T
    lane = lax.broadcasted_iota(jnp.int32, (SUBLANES, LANES), 1)
    sub_lo = lane * MOE_SUB
    hit = (count > 0.0) & (start < (sub_lo + MOE_SUB).astype(F32)) & (start + count > sub_lo.astype(F32))
    start_i = start.astype(jnp.int32)
    count_i = count.astype(jnp.int32)
    choice = jnp.zeros_like(start_i)
    first = jnp.zeros_like(start_i)
    for k in reversed(range(len(MOE_WINDOWS))):
        win = jnp.minimum((start_i // MOE_WINDOW_ALIGN) * MOE_WINDOW_ALIGN, tm - MOE_WINDOWS[k])
        fits = (count_i > 0) & (start_i + count_i <= win + MOE_WINDOWS[k])
        choice = jnp.where(fits, k + 1, choice)
        first = jnp.where(fits, win, first)
    meta = jnp.where(lane == MOE_META_FITS, choice,
                     jnp.where(lane == MOE_META_WINDOW, first, jnp.where(hit, 1, 0)))
    flags_ref[0] = meta.astype(jnp.int32)


def _moe_kernel(flags_ref, h_ref, gates_ref, pos_ref, w1_ref, w3_ref, w2_ref, g_ref, b_ref,
                o_ref, xs_ref, gs_ref, acc_ref, pt_ref):
    tile = pl.program_id(0)
    step = pl.program_id(1)
    tm = h_ref.shape[0]
    sub = MOE_SUB
    n_sub = tm // sub
    eps = MOE_EXPERTS_PER_STEP

    @pl.when(step == 0)
    def _():
        pos_b = pos_ref[...]
        pos_row = pos_b.T[0:1, :]
        g_hi, g_lo = _split_bf16(gates_ref[...], 2)
        src = jnp.concatenate([h_ref[...].astype(BF16), g_hi, g_lo], axis=1)
        for c0 in range(0, tm, sub):
            slot = (lax.broadcasted_iota(jnp.int32, (sub, tm), 0) + c0).astype(F32)
            p_c = jnp.where(slot == pos_row, 1.0, 0.0).astype(BF16)
            moved = _dot(p_c, src)
            xs_ref[c0:c0 + sub, :] = moved[:, :D_MODEL].astype(BF16)
            gs_ref[c0:c0 + sub, :] = moved[:, D_MODEL:D_MODEL + LANES] + moved[:, D_MODEL + LANES:]
        for c0 in range(0, tm, LANES):
            slot = (lax.broadcasted_iota(jnp.int32, (tm, LANES), 1) + c0).astype(F32)
            pt_ref[:, c0:c0 + LANES] = jnp.where(pos_b == slot, 1.0, 0.0).astype(BF16)
        acc_ref[...] = jnp.zeros_like(acc_ref)

    group = step // (EXPERTS_PER_GROUP // eps)

    def visit(rows):
        x_r = xs_ref[rows, :]
        g_r = gs_ref[rows, :]
        lane = lax.broadcasted_iota(jnp.int32, g_r.shape, 1)
        acc = acc_ref[rows, :]
        for e in range(eps):
            gate_e = jnp.sum(jnp.where(lane == step * eps + e, g_r, 0.0), axis=-1, keepdims=True)
            a1 = _dot(x_r, w1_ref[e])
            a3 = _dot(x_r, w3_ref[e])
            hid = (a1 * _sigmoid(a1)) * a3 * gate_e
            acc = acc + _dot(hid.astype(BF16), w2_ref[e])
        acc_ref[rows, :] = acc

    base = (tile * N_GROUPS + group) * (n_sub + 2)
    choice = flags_ref[base + n_sub]

    def visit_window(size):
        first = pl.multiple_of(flags_ref[base + n_sub + 1], MOE_WINDOW_ALIGN)
        visit(pl.ds(first, size))

    for k, size in enumerate(MOE_WINDOWS):
        pl.when(choice == k + 1)(functools.partial(visit_window, size))

    for r in range(n_sub):
        pl.when((choice == 0) & (flags_ref[base + r] != 0))(
            functools.partial(visit, slice(r * sub, (r + 1) * sub)))

    @pl.when(step == N_EXPERTS // eps - 1)
    def _():
        ffn = _dot(pt_ref[...], acc_ref[...].astype(BF16))
        o_ref[...] = _layer_norm(DEEPNORM_ALPHA * h_ref[...] + ffn, g_ref[...], b_ref[...])


def _moe_call(flags, h_f32, gates, pos, w1, w3, w2, ln_g, ln_b):
    n = h_f32.shape[0]
    tm = MOE_TM
    row = lambda t, s, f: (t, 0)
    const = lambda t, s, f: (0, 0)
    wmap = lambda t, s, f: (s, 0, 0)
    eps = MOE_EXPERTS_PER_STEP
    assert EXPERTS_PER_GROUP % eps == 0 and tm % MOE_SUB == 0
    grid_spec = pltpu.PrefetchScalarGridSpec(
        num_scalar_prefetch=1,
        grid=(n // tm, N_EXPERTS // eps),
        in_specs=[
            pl.BlockSpec((tm, D_MODEL), row),
            pl.BlockSpec((tm, ROUTER_PAD), row),
            pl.BlockSpec((tm, LANES), row),
            pl.BlockSpec((eps, D_MODEL, D_EXPERT), wmap),
            pl.BlockSpec((eps, D_MODEL, D_EXPERT), wmap),
            pl.BlockSpec((eps, D_EXPERT, D_MODEL), wmap),
            pl.BlockSpec((1, D_MODEL), const),
            pl.BlockSpec((1, D_MODEL), const),
        ],
        out_specs=pl.BlockSpec((tm, D_MODEL), row),
        scratch_shapes=[
            pltpu.VMEM((tm, D_MODEL), BF16),
            pltpu.VMEM((tm, ROUTER_PAD), F32),
            pltpu.VMEM((tm, D_MODEL), F32),
            pltpu.VMEM((tm, tm), BF16),
        ],
    )
    return pl.pallas_call(
        _moe_kernel,
        grid_spec=grid_spec,
        out_shape=jax.ShapeDtypeStruct((n, D_MODEL), F32),
        compiler_params=pltpu.CompilerParams(
            dimension_semantics=("arbitrary", "arbitrary"), vmem_limit_bytes=VMEM_LIMIT),
        name="hier_moe_ln",
    )(flags, h_f32, gates, pos, w1, w3, w2, ln_g, ln_b)


def _pad_cols(w, width):
    return jnp.pad(w, ((0, 0), (0, width - w.shape[1])))


def kernel(x, w_in, mu_shift, w0, w_lora_up, a0, a_lora_up, g_lora_up, k_k, k_a, r_k, gn_w, gn_b, w_out, ln1_g, ln1_b, w_group, b_group, w_expert, b_expert, w1_exp, w3_exp, w2_exp, ln2_g, ln2_b):
    batch, seq_len, d = x.shape
    assert d == D_MODEL
    n = batch * seq_len
    x2 = x.reshape(n, d)

    c_rkv = 3 * RWKV_WIDTH
    c_wd = c_rkv + DECAY_RANK
    c_ad = c_wd + AAA_RANK
    c_gd = c_ad + GATE_RANK
    w_cat = jnp.concatenate([
        w_in[:, :c_rkv], _pad_cols(w_in[:, c_rkv:c_gd], LORA_PAD), w_in[:, c_gd:],
    ], axis=1).astype(BF16)
    mu2 = mu_shift[None, :]
    mu_cat = jnp.concatenate([mu2[:, :c_rkv], _pad_cols(mu2[:, c_rkv:c_gd], LORA_PAD)], axis=1)
    p_rkv, p_lora, p_k, p_qvt = _inproj_call(x2, w_cat, mu_cat, seq_len)

    place = lambda w, first: jnp.pad(w, ((first, LORA_PAD - first - w.shape[0]), (0, 0)))
    lora_up = (place(w_lora_up, 0), place(a_lora_up, DECAY_RANK), place(g_lora_up, DECAY_RANK + AAA_RANK))

    vecs = jnp.stack([w0, a0, k_k, k_a, r_k.reshape(-1), gn_w, gn_b, jnp.zeros_like(w0)], axis=0)
    head_id = jnp.arange(2 * PAIR) // HEAD_DIM
    bd = (head_id[:, None] == head_id[None, :]).astype(BF16)
    y_a = _rwkv_call(p_rkv, p_lora, vecs, *lora_up, bd, batch, seq_len)

    y_b = _moba_call(p_k, p_qvt, _moba_key_aug(seq_len), batch, seq_len)

    w_out_b = w_out.astype(BF16)
    w_router = _pad_cols(jnp.concatenate([w_expert, w_group], axis=1), ROUTER_PAD)
    b_router = _pad_cols(jnp.concatenate([b_expert, b_group])[None, :], ROUTER_PAD)
    h1, gates, pos, flags = _outproj_call(y_a, y_b, x2, w_out_b[:RWKV_WIDTH], w_out_b[RWKV_WIDTH:],
                               ln1_g[None, :], ln1_b[None, :], w_router, b_router)

    flat = lambda w: w.astype(BF16).reshape((N_EXPERTS,) + w.shape[2:])
    flags = flags[:, :N_GROUPS, :MOE_META_WINDOW + 1].reshape(-1)
    out = _moe_call(flags, h1, gates, pos, flat(w1_exp), flat(w3_exp), flat(w2_exp),
                    ln2_g[None, :], ln2_b[None, :])
    return out.reshape(batch, seq_len, d)
```
